```python
import jax, jax.numpy as jnp
from jax import lax
import numpy as np

D_MODEL = 1024
BATCH = 32
SEQ = 2048
DEPTH = 4

GRID_W = 64
CTX_LEN = 256
N_MIXERS = 2
N_ATTN_LAYERS = (DEPTH + N_MIXERS - 1) // N_MIXERS
N_GLA_LAYERS = DEPTH // N_MIXERS
RMS_EPS = 1e-6
ATTN_HEAD_DIM = 64
ATTN_HEADS = D_MODEL // ATTN_HEAD_DIM
ATTN_KV_HEADS = 4
ATTN_GROUP = ATTN_HEADS // ATTN_KV_HEADS
WINDOW = 128
ATTN_BLOCK = 128
ATTN_HALO = -(-WINDOW // ATTN_BLOCK) * ATTN_BLOCK
ROPE_THETA = 10000.0
ROPE_AXIS_DIM = ATTN_HEAD_DIM // 2
GLA_HEADS = 4
GLA_KEY_DIM = D_MODEL // 2 // GLA_HEADS
GLA_VAL_DIM = D_MODEL // GLA_HEADS
GLA_GATE_RANK = 16
GLA_GATE_NORM = 16.0
GLA_CHUNK = 64
N_EXPERTS = 16
N_GROUPS = 4
EXPERTS_PER_GROUP = N_EXPERTS // N_GROUPS
TOP_K = 2
D_EXPERT = D_MODEL // 2
MOE_BLOCK = 256

kernel_name = 'hybrid_swa_gla_grouped_moe_dit'

F32 = jnp.float32


def rmsnorm(x, gain):
    xf = x.astype(F32)
    y = xf * lax.rsqrt(jnp.mean(xf * xf, axis=-1, keepdims=True) + RMS_EPS) * gain.astype(F32)
    return y.astype(x.dtype)


def modulate(h, shift, scale):
    return h * (1 + scale) + shift


def axial_rope_tables(rows):
    row = jnp.repeat(jnp.arange(rows, dtype=F32), GRID_W)
    col = jnp.tile(jnp.arange(GRID_W, dtype=F32), rows)
    half = ROPE_AXIS_DIM // 2
    inv_freq = ROPE_THETA ** (-jnp.arange(half, dtype=F32) / half)
    ang_r = row[:, None] * inv_freq[None, :]
    ang_c = col[:, None] * inv_freq[None, :]
    return (jnp.cos(ang_r), jnp.sin(ang_r), jnp.cos(ang_c), jnp.sin(ang_c))


def rope_2d(x, rope):
    cos_r, sin_r, cos_c, sin_c = rope
    shp = (x.shape[1],) + (1,) * (x.ndim - 3) + (ROPE_AXIS_DIM // 2,)
    xf = x.astype(F32)

    def rot(y, cos, sin):
        y1, y2 = jnp.split(y, 2, axis=-1)
        cos, sin = cos.reshape(shp), sin.reshape(shp)
        return jnp.concatenate([y1 * cos - y2 * sin, y2 * cos + y1 * sin], axis=-1)

    x_row, x_col = jnp.split(xf, 2, axis=-1)
    return jnp.concatenate([rot(x_row, cos_r, sin_r), rot(x_col, cos_c, sin_c)], axis=-1).astype(x.dtype)


def sink_attend(scores, values, sink):
    B, KH, G, Q, _ = scores.shape
    sink_col = jnp.broadcast_to(sink.astype(F32)[None, :, :, None, None], (B, KH, G, Q, 1))
    p = jax.nn.softmax(jnp.concatenate([scores, sink_col], axis=-1), axis=-1)[..., :-1]
    return jnp.einsum('bkgqs,bskd->bqkgd', p.astype(values.dtype), values)


def window_gqa_mixer(h_lat, h_ctx, w_qkv, w_o, sinks, rope, need_ctx_out):
    B, S, _ = h_lat.shape
    q_dim = ATTN_HEADS * ATTN_HEAD_DIM
    kv_dim = ATTN_KV_HEADS * ATTN_HEAD_DIM
    scale = ATTN_HEAD_DIM ** -0.5
    sinks_g = sinks.reshape(ATTN_KV_HEADS, ATTN_GROUP)

    def project(h):
        b, t, _ = h.shape
        z = h @ w_qkv
        q = z[..., :q_dim].reshape(b, t, ATTN_KV_HEADS, ATTN_GROUP, ATTN_HEAD_DIM)
        k = z[..., q_dim:q_dim + kv_dim].reshape(b, t, ATTN_KV_HEADS, ATTN_HEAD_DIM)
        v = z[..., q_dim + kv_dim:].reshape(b, t, ATTN_KV_HEADS, ATTN_HEAD_DIM)
        return q, k, v

    q_l, k_l, v_l = project(h_lat)
    q_l, k_l = rope_2d(q_l, rope), rope_2d(k_l, rope)
    q_c, k_c, v_c = project(h_ctx)

    pad = ((0, 0), (ATTN_HALO, ATTN_HALO), (0, 0), (0, 0))
    k_pad, v_pad = jnp.pad(k_l, pad), jnp.pad(v_l, pad)
    span = ATTN_BLOCK + 2 * ATTN_HALO
    q_off = jnp.arange(ATTN_BLOCK)
    k_off = jnp.arange(span) - ATTN_HALO

    def block(start):
        qb = lax.dynamic_slice_in_dim(q_l, start, ATTN_BLOCK, axis=1)
        kb = lax.dynamic_slice_in_dim(k_pad, start, span, axis=1)
        vb = lax.dynamic_slice_in_dim(v_pad, start, span, axis=1)
        qpos, kpos = start + q_off, start + k_off
        valid = (jnp.abs(qpos[:, None] - kpos[None, :]) <= WINDOW) & (kpos[None, :] >= 0) & (kpos[None, :] < S)
        s_win = jnp.einsum('bqkgd,bskd->bkgqs', qb, kb).astype(F32) * scale
        s_win = jnp.where(valid, s_win, -jnp.inf)
        s_cx = jnp.einsum('bqkgd,bckd->bkgqc', qb, k_c).astype(F32) * scale
        return sink_attend(jnp.concatenate([s_win, s_cx], axis=-1),
                           jnp.concatenate([vb, v_c], axis=1), sinks_g)

    o = lax.map(block, jnp.arange(S // ATTN_BLOCK) * ATTN_BLOCK)
    y_lat = o.transpose(1, 0, 2, 3, 4, 5).reshape(B, S, q_dim) @ w_o
    y_ctx = None
    if need_ctx_out:
        s_cc = jnp.einsum('bqkgd,bckd->bkgqc', q_c, k_c).astype(F32) * scale
        o_c = sink_attend(s_cc, v_c, sinks_g)
        y_ctx = o_c.reshape(B, h_ctx.shape[1], q_dim) @ w_o
    return y_lat, y_ctx


def gla_chunked(q, k, v, log_a, s0):
    B, T, H, DK = q.shape
    DV = v.shape[-1]
    nc = T // GLA_CHUNK

    def chunks(a):
        return a.astype(F32).reshape(B, nc, GLA_CHUNK, H, a.shape[-1]).transpose(1, 0, 3, 2, 4)

    lower_tri = jnp.tril(jnp.ones((GLA_CHUNK, GLA_CHUNK), dtype=bool))[:, :, None]

    def step(state, inp):
        qi, ki, vi, gi = inp
        b = jnp.cumsum(gi, axis=2)
        b_end = b[:, :, -1:, :]
        o_inter = jnp.einsum('bhtk,bhkv->bhtv', qi * jnp.exp(b), state)
        diff = jnp.where(lower_tri, b[:, :, :, None, :] - b[:, :, None, :, :], -jnp.inf)
        scores = jnp.einsum('bhtk,bhsk,bhtsk->bhts', qi, ki, jnp.exp(diff))
        o_intra = jnp.einsum('bhts,bhsv->bhtv', scores, vi)
        new_state = state * jnp.exp(b_end[:, :, 0, :, None]) + jnp.einsum(
            'bhsk,bhsv->bhkv', ki * jnp.exp(b_end - b), vi)
        return new_state, o_inter + o_intra

    s_final, o = lax.scan(step, s0, (chunks(q), chunks(k), chunks(v), chunks(log_a)))
    o = o.transpose(1, 0, 3, 2, 4).reshape(B, T, H, DV).astype(v.dtype)
    return o, s_final


def gla_mixer(h_lat, h_ctx, w_in, w_a1, w_a2, b_a, norm_g, w_o, need_ctx_out):
    kd = GLA_HEADS * GLA_KEY_DIM
    vd = GLA_HEADS * GLA_VAL_DIM

    def project(h):
        b, t, _ = h.shape
        z = h @ w_in
        q = z[..., :kd].reshape(b, t, GLA_HEADS, GLA_KEY_DIM) * (GLA_KEY_DIM ** -0.5)
        k = z[..., kd:2 * kd].reshape(b, t, GLA_HEADS, GLA_KEY_DIM)
        v = z[..., 2 * kd:2 * kd + vd].reshape(b, t, GLA_HEADS, GLA_VAL_DIM)
        g = z[..., 2 * kd + vd:]
        log_a = [(jax.nn.log_sigmoid((h @ w_a1[d] @ w_a2[d] + b_a[d]).astype(F32)) / GLA_GATE_NORM)
                 .reshape(b, t, GLA_HEADS, GLA_KEY_DIM) for d in range(2)]
        return q, k, v, g, log_a

    def output(o, g):
        b, t = o.shape[:2]
        o = rmsnorm(o, norm_g).reshape(b, t, vd)
        return (o * jax.nn.silu(g)) @ w_o

    flip = lambda a: jnp.flip(a, axis=1)
    q_l, k_l, v_l, g_l, la_l = project(h_lat)
    q_c, k_c, v_c, g_c, la_c = project(h_ctx)
    B = h_lat.shape[0]
    s0 = jnp.zeros((B, GLA_HEADS, GLA_KEY_DIM, GLA_VAL_DIM), F32)
    o_cf, s_cf = gla_chunked(q_c, k_c, v_c, la_c[0], s0)
    o_lf, _ = gla_chunked(q_l, k_l, v_l, la_l[0], s_cf)
    o_cb, s_cb = gla_chunked(flip(q_c), flip(k_c), flip(v_c), flip(la_c[1]), s0)
    o_lb, _ = gla_chunked(flip(q_l), flip(k_l), flip(v_l), flip(la_l[1]), s_cb)
    y_lat = output(o_lf + flip(o_lb), g_l)
    y_ctx = output(o_cf + flip(o_cb), g_c) if need_ctx_out else None
    return y_lat, y_ctx


def grouped_moe(h, router_w, router_b, w_gate, w_up, w_down):
    N, D = h.shape
    scores = jax.nn.sigmoid(h.astype(F32) @ router_w.astype(F32))
    sel = (scores + router_b.astype(F32)).reshape(N, N_GROUPS, EXPERTS_PER_GROUP)
    group_score = lax.top_k(sel, TOP_K)[0].sum(-1)
    grp = jnp.argmax(group_score, axis=-1)
    sel_in = jnp.take_along_axis(sel, grp[:, None, None], axis=1)[:, 0]
    _, local = lax.top_k(sel_in, TOP_K)
    eidx = grp[:, None] * EXPERTS_PER_GROUP + local
    gate = jnp.take_along_axis(scores, eidx, axis=1)
    gate = gate / jnp.sum(gate, axis=-1, keepdims=True)
    A = N * TOP_K
    e_flat = eidx.reshape(A)
    order = jnp.argsort(e_flat)
    e_sorted = e_flat[order]
    tok_sorted = order // TOP_K
    gate_sorted = gate.reshape(A)[order].astype(h.dtype)
    counts = jnp.bincount(e_flat, length=N_EXPERTS)
    starts = jnp.cumsum(counts) - counts
    padded = (counts + MOE_BLOCK - 1) // MOE_BLOCK * MOE_BLOCK
    pends = jnp.cumsum(padded)
    pstarts = pends - padded
    slot = pstarts[e_sorted] + (jnp.arange(A) - starts[e_sorted])
    n_blocks = -(-A // MOE_BLOCK) + N_EXPERTS
    buf = jnp.zeros((n_blocks * MOE_BLOCK, D), h.dtype).at[slot].set(h[tok_sorted])
    block_expert = jnp.minimum(jnp.searchsorted(pends, jnp.arange(n_blocks) * MOE_BLOCK, side='right'),
                               N_EXPERTS - 1)

    def expert_block(args):
        xb, e = args
        hid = jax.nn.silu(xb @ w_gate[e]) * (xb @ w_up[e])
        return hid @ w_down[e]

    y_buf = lax.map(expert_block, (buf.reshape(n_blocks, MOE_BLOCK, D), block_expert)).reshape(-1, D)
    y = y_buf[slot] * gate_sorted[:, None]
    return jnp.zeros((N, D), h.dtype).at[tok_sorted].add(y)


def setup_inputs(seed: int = 0) -> dict:
    key = jax.random.key(seed)
    ks = jax.random.split(key, 23)

    def nrm(k, shape, scale):
        return jax.random.normal(k, shape, F32) * scale

    D = D_MODEL
    qkv_w = ATTN_HEADS * ATTN_HEAD_DIM + 2 * ATTN_KV_HEADS * ATTN_HEAD_DIM
    kd = GLA_HEADS * GLA_KEY_DIM
    vd = GLA_HEADS * GLA_VAL_DIM
    return {
        'x': nrm(ks[0], (BATCH, SEQ, D), 1.0),
        'c': nrm(ks[1], (BATCH, D), 1.0),
        'ctx': nrm(ks[2], (BATCH, CTX_LEN, D), 1.0),
        'c_ctx': nrm(ks[3], (D,), 1.0),
        'ada_w': nrm(ks[4], (DEPTH, D, 6 * D), 0.5 * D ** -0.5),
        'ada_b': nrm(ks[5], (DEPTH, 6 * D), 0.02),
        'norm_mix_g': 1.0 + nrm(ks[6], (DEPTH, D), 0.05),
        'norm_ffn_g': 1.0 + nrm(ks[7], (DEPTH, D), 0.05),
        'final_g': 1.0 + nrm(ks[8], (D,), 0.05),
        'attn_w_qkv': nrm(ks[9], (N_ATTN_LAYERS, D, qkv_w), D ** -0.5),
        'attn_w_o': nrm(ks[10], (N_ATTN_LAYERS, ATTN_HEADS * ATTN_HEAD_DIM, D), (ATTN_HEADS * ATTN_HEAD_DIM) ** -0.5),
        'attn_sinks': nrm(ks[11], (N_ATTN_LAYERS, ATTN_HEADS), 0.5),
        'gla_w_in': nrm(ks[12], (N_GLA_LAYERS, D, 2 * kd + 2 * vd), D ** -0.5),
        'gla_w_a1': nrm(ks[13], (N_GLA_LAYERS, 2, D, GLA_GATE_RANK), D ** -0.5),
        'gla_w_a2': nrm(ks[14], (N_GLA_LAYERS, 2, GLA_GATE_RANK, kd), GLA_GATE_RANK ** -0.5),
        'gla_b_a': nrm(ks[15], (N_GLA_LAYERS, 2, kd), 0.1),
        'gla_norm_g': 1.0 + nrm(ks[16], (N_GLA_LAYERS, GLA_VAL_DIM), 0.05),
        'gla_w_o': nrm(ks[17], (N_GLA_LAYERS, vd, D), vd ** -0.5),
        'router_w': nrm(ks[18], (D, N_EXPERTS), D ** -0.5),
        'router_b': nrm(ks[19], (N_EXPERTS,), 0.01),
        'moe_w_gate': nrm(ks[20], (DEPTH, N_EXPERTS, D, D_EXPERT), D ** -0.5),
        'moe_w_up': nrm(ks[21], (DEPTH, N_EXPERTS, D, D_EXPERT), D ** -0.5),
        'moe_w_down': nrm(ks[22], (DEPTH, N_EXPERTS, D_EXPERT, D), D_EXPERT ** -0.5),
    }


def reference(x, c, ctx, c_ctx, ada_w, ada_b, norm_mix_g, norm_ffn_g, final_g, attn_w_qkv, attn_w_o, attn_sinks,
              gla_w_in, gla_w_a1, gla_w_a2, gla_b_a, gla_norm_g, gla_w_o, router_w, router_b,
              moe_w_gate, moe_w_up, moe_w_down):
    B, S, D = x.shape
    C = ctx.shape[1]
    ROWS = S // GRID_W
    rope = axial_rope_tables(ROWS)
    silu_c = jax.nn.silu(c)
    silu_cc = jax.nn.silu(c_ctx)
    h_ctx_stream = ctx
    for i in range(DEPTH):
        last = i == DEPTH - 1
        mod_l = silu_c @ ada_w[i] + ada_b[i]
        mod_c = silu_cc @ ada_w[i] + ada_b[i]
        sh_m, sc_m, g_m, sh_f, sc_f, g_f = [m[:, None, :] for m in jnp.split(mod_l, 6, axis=-1)]
        csh_m, csc_m, cg_m, csh_f, csc_f, cg_f = jnp.split(mod_c, 6, axis=-1)
        h_l = modulate(rmsnorm(x, norm_mix_g[i]), sh_m, sc_m)
        h_c = modulate(rmsnorm(h_ctx_stream, norm_mix_g[i]), csh_m, csc_m)
        j = i // N_MIXERS
        if i % N_MIXERS == 0:
            y_l, y_c = window_gqa_mixer(h_l, h_c, attn_w_qkv[j], attn_w_o[j], attn_sinks[j], rope, not last)
        else:
            y_l, y_c = gla_mixer(h_l, h_c, gla_w_in[j], gla_w_a1[j], gla_w_a2[j], gla_b_a[j],
                                 gla_norm_g[j], gla_w_o[j], not last)
        x = x + g_m * y_l
        f_l = modulate(rmsnorm(x, norm_ffn_g[i]), sh_f, sc_f).reshape(B * S, D)
        if last:
            x = x + g_f * grouped_moe(f_l, router_w, router_b, moe_w_gate[i], moe_w_up[i],
                                      moe_w_down[i]).reshape(B, S, D)
        else:
            h_ctx_stream = h_ctx_stream + cg_m * y_c
            f_c = modulate(rmsnorm(h_ctx_stream, norm_ffn_g[i]), csh_f, csc_f).reshape(B * C, D)
            y_f = grouped_moe(jnp.concatenate([f_l, f_c], axis=0), router_w, router_b,
                              moe_w_gate[i], moe_w_up[i], moe_w_down[i])
            x = x + g_f * y_f[:B * S].reshape(B, S, D)
            h_ctx_stream = h_ctx_stream + cg_f * y_f[B * S:].reshape(B, C, D)
    return rmsnorm(x, final_g)
```

```python
import functools

import numpy as np
import jax
import jax.numpy as jnp
from jax import lax
from jax.experimental import pallas as pl
from jax.experimental.pallas import tpu as pltpu

F32 = jnp.float32
BF16 = jnp.bfloat16
I32 = jnp.int32

LANES = 128
VMEM_LIMIT_BYTES = 56 * 1024 * 1024

RMS_EPS = 1e-6
GRID_W = 64
ROPE_THETA = 10000.0
ATTN_HEAD_DIM = 64
ATTN_KV_HEADS = 4
ATTN_GROUP = 4
ATTN_BLOCK = 128
GLA_HEADS = 4
GLA_KEY_DIM = 128
GLA_VAL_DIM = 256
GLA_GATE_RANK = 16
GLA_GATE_NORM = 16.0
GLA_CHUNK = 64
N_EXPERTS = 16
N_GROUPS = 4
EXPERTS_PER_GROUP = 4
EXPERT_BLOCK = 256
TOKEN_TILE = 512


def _cparams(*sem):
    return pltpu.CompilerParams(dimension_semantics=sem, vmem_limit_bytes=VMEM_LIMIT_BYTES)


def _split3(a):
    hi = a.astype(BF16)
    r1 = a - hi.astype(F32)
    mid = r1.astype(BF16)
    lo = (r1 - mid.astype(F32)).astype(BF16)
    return hi, mid, lo


def _norm_mod(x, gain, shift, scale):
    h = x * lax.rsqrt(jnp.mean(x * x, axis=-1, keepdims=True) + RMS_EPS) * gain
    return h * (1.0 + scale) + shift


def _ada_kernel(c_ref, w_ref, b_ref, o_ref):
    c = c_ref[...]
    s = (c * jax.nn.sigmoid(c)).astype(BF16)
    o_ref[0] = jnp.dot(s, w_ref[0].astype(BF16), preferred_element_type=F32) + b_ref[0]


def _ada_table(cc, ada_w, ada_b):
    L, D, D6 = ada_w.shape
    R = cc.shape[0]
    tn = 1536
    return pl.pallas_call(
        _ada_kernel,
        grid=(L, D6 // tn),
        in_specs=[pl.BlockSpec((R, D), lambda l, j: (0, 0)),
                  pl.BlockSpec((1, D, tn), lambda l, j: (l, 0, j)),
                  pl.BlockSpec((1, 1, tn), lambda l, j: (l, 0, j))],
        out_specs=pl.BlockSpec((1, R, tn), lambda l, j: (l, 0, j)),
        out_shape=jax.ShapeDtypeStruct((L, R, D6), F32),
        compiler_params=_cparams("parallel", "parallel"),
        name="ada_table",
    )(cc, ada_w, ada_b.reshape(L, 1, D6))


def _qkv_kernel(x_ref, g_ref, sh_ref, sc_ref, w_ref, cos_ref, s1_ref, s2_ref, q_ref, k_ref, v_ref):
    h = _norm_mod(x_ref[...], g_ref[...], sh_ref[0, 0], sc_ref[0, 0])
    z = jnp.dot(h.astype(BF16), w_ref[...], preferred_element_type=F32)
    cos, s1, s2 = cos_ref[...], s1_ref[...], s2_ref[...]
    nq = q_ref.shape[1] // LANES
    nk = k_ref.shape[1] // LANES
    for j in range(nq + nk):
        zc = z[:, j * LANES:(j + 1) * LANES]
        r = zc * cos + pltpu.roll(zc, LANES - 16, 1) * s1 + pltpu.roll(zc, 16, 1) * s2
        if j < nq:
            q_ref[:, j * LANES:(j + 1) * LANES] = (r * (ATTN_HEAD_DIM ** -0.5)).astype(BF16)
        else:
            k_ref[:, (j - nq) * LANES:(j - nq + 1) * LANES] = r.astype(BF16)
    v_ref[...] = z[:, (nq + nk) * LANES:].astype(BF16)


def _gla_in_kernel(x_ref, g_ref, sh_ref, sc_ref, w_ref, w2_ref, ba_ref, qk_ref, v_ref, og_ref, la_ref):
    h = _norm_mod(x_ref[...], g_ref[...], sh_ref[0, 0], sc_ref[0, 0])
    z = jnp.dot(h.astype(BF16), w_ref[...], preferred_element_type=F32)
    kd = GLA_HEADS * GLA_KEY_DIM
    vd = GLA_HEADS * GLA_VAL_DIM
    qk_ref[:, :kd] = (z[:, :kd] * (GLA_KEY_DIM ** -0.5)).astype(BF16)
    qk_ref[:, kd:] = z[:, kd:2 * kd].astype(BF16)
    v_ref[...] = z[:, 2 * kd:2 * kd + vd].astype(BF16)
    og_ref[...] = z[:, 2 * kd + vd:2 * kd + 2 * vd].astype(BF16)
    a1 = z[:, 2 * kd + 2 * vd:].astype(BF16)
    pre = jnp.dot(a1, w2_ref[...], preferred_element_type=F32) + ba_ref[...]
    la_ref[...] = (jnp.minimum(pre, 0.0) - jnp.log1p(jnp.exp(-jnp.abs(pre)))) * (1.0 / GLA_GATE_NORM)


def _tile_specs(D, tm, n_lat_tiles, tiles_per_seq, n_mod_rows):
    def mod_idx(which):
        return lambda t: (which, jnp.minimum(t // tiles_per_seq, n_mod_rows - 1), 0, 0)
    return mod_idx, [pl.BlockSpec((tm, D), lambda t: (t, 0)),
                     pl.BlockSpec((1, D), lambda t: (0, 0))]


def _qkv_proj(xs, gain, mod, which, w, rope, tm, n_lat_tiles, tiles_per_seq):
    N, D = xs.shape
    nmod = mod.shape[1]
    mod_idx, specs = _tile_specs(D, tm, n_lat_tiles, tiles_per_seq, nmod)
    qd = ATTN_KV_HEADS * ATTN_GROUP * ATTN_HEAD_DIM
    kd = ATTN_KV_HEADS * LANES
    rope_idx = lambda t: (jnp.where(t < n_lat_tiles, t % tiles_per_seq, tiles_per_seq), 0)
    in_specs = specs + [
        pl.BlockSpec((1, 1, 1, D), mod_idx(which)),
        pl.BlockSpec((1, 1, 1, D), mod_idx(which + 1)),
        pl.BlockSpec(w.shape, lambda t: (0, 0)),
        pl.BlockSpec((tm, LANES), rope_idx),
        pl.BlockSpec((tm, LANES), rope_idx),
        pl.BlockSpec((tm, LANES), rope_idx),
    ]
    return pl.pallas_call(
        _qkv_kernel,
        grid=(N // tm,),
        in_specs=in_specs,
        out_specs=[pl.BlockSpec((tm, qd), lambda t: (t, 0)),
                   pl.BlockSpec((tm, kd), lambda t: (t, 0)),
                   pl.BlockSpec((tm, kd), lambda t: (t, 0))],
        out_shape=[jax.ShapeDtypeStruct((N, qd), BF16),
                   jax.ShapeDtypeStruct((N, kd), BF16),
                   jax.ShapeDtypeStruct((N, kd), BF16)],
        compiler_params=_cparams("parallel"),
        name="attn_qkv_proj",
    )(xs, gain.reshape(1, D), mod, mod, w, *rope)


def _gla_in_proj(xs, gain, mod, which, w, w2, ba, tm, n_lat_tiles, tiles_per_seq):
    N, D = xs.shape
    nmod = mod.shape[1]
    mod_idx, specs = _tile_specs(D, tm, n_lat_tiles, tiles_per_seq, nmod)
    kd = GLA_HEADS * GLA_KEY_DIM
    vd = GLA_HEADS * GLA_VAL_DIM
    in_specs = specs + [
        pl.BlockSpec((1, 1, 1, D), mod_idx(which)),
        pl.BlockSpec((1, 1, 1, D), mod_idx(which + 1)),
        pl.BlockSpec(w.shape, lambda t: (0, 0)),
        pl.BlockSpec(w2.shape, lambda t: (0, 0)),
        pl.BlockSpec((1, 2 * kd), lambda t: (0, 0)),
    ]
    row = lambda width: pl.BlockSpec((tm, width), lambda t: (t, 0))
    return pl.pallas_call(
        _gla_in_kernel,
        grid=(N // tm,),
        in_specs=in_specs,
        out_specs=[row(2 * kd), row(vd), row(vd), row(2 * kd)],
        out_shape=[jax.ShapeDtypeStruct((N, 2 * kd), BF16),
                   jax.ShapeDtypeStruct((N, vd), BF16),
                   jax.ShapeDtypeStruct((N, vd), BF16),
                   jax.ShapeDtypeStruct((N, 2 * kd), F32)],
        compiler_params=_cparams("parallel"),
        name="gla_in_proj",
    )(xs, gain.reshape(1, D), mod, mod, w, w2, ba)


def _attn_kernel(*refs, window, nq):
    if window:
        q_ref, kp, kc, kn, kx, vp, vc, vn, vx, sink_ref, o_ref = refs
        k_parts, v_parts = (kp, kc, kn, kx), (vp, vc, vn, vx)
    else:
        q_ref, kx, vx, sink_ref, _, o_ref = refs
        k_parts, v_parts = (kx,), (vx,)
    tq = q_ref.shape[0]
    rows = ATTN_GROUP * tq
    nkeys = sum(p.shape[0] for p in k_parts)
    lane = lax.broadcasted_iota(I32, (tq, LANES), 1)
    first_head = lane < ATTN_HEAD_DIM
    if window:
        j = pl.program_id(1)
        r = lax.broadcasted_iota(I32, (rows, nkeys), 0) & (tq - 1)
        c = lax.broadcasted_iota(I32, (rows, nkeys), 1)
        valid = ((c >= tq) & (c < 2 * tq)) | (c >= 3 * tq)
        valid = valid | ((c < tq) & (c >= r) & (j > 0))
        valid = valid | ((c >= 2 * tq) & (c < 3 * tq) & (c - 2 * tq <= r) & (j < nq - 1))
        bias = jnp.where(valid, 0.0, -jnp.inf).astype(F32)
    for kh in range(ATTN_KV_HEADS):
        ks = slice(kh * LANES, (kh + 1) * LANES)
        kk = jnp.concatenate([p[:, ks] for p in k_parts], axis=0) if window else kx[:, ks]
        vv = jnp.concatenate([p[:, ks] for p in v_parts], axis=0) if window else vx[:, ks]
        qa = q_ref[:, (2 * kh) * LANES:(2 * kh + 1) * LANES]
        qb = q_ref[:, (2 * kh + 1) * LANES:(2 * kh + 2) * LANES]
        zero = jnp.zeros_like(qa)
        qs = jnp.concatenate([jnp.where(first_head, qa, zero), jnp.where(first_head, zero, qa),
                              jnp.where(first_head, qb, zero), jnp.where(first_head, zero, qb)], axis=0)
        s = lax.dot_general(qs, kk, (((1,), (1,)), ((), ())), preferred_element_type=F32)
        if window:
            s = s + bias
        sink = sink_ref[kh]
        m = jnp.maximum(jnp.max(s, axis=1, keepdims=True), sink)
        p = jnp.exp(s - m)
        l = jnp.sum(p, axis=1, keepdims=True) + jnp.exp(sink - m)
        o = jnp.dot(p.astype(BF16), vv, preferred_element_type=F32) / l
        oa = jnp.where(first_head, o[0:tq], o[tq:2 * tq])
        ob = jnp.where(first_head, o[2 * tq:3 * tq], o[3 * tq:4 * tq])
        o_ref[:, (2 * kh) * LANES:(2 * kh + 1) * LANES] = oa.astype(BF16)
        o_ref[:, (2 * kh + 1) * LANES:(2 * kh + 2) * LANES] = ob.astype(BF16)


def _attention(q, k, v, sink_col, B, S, C, need_ctx):
    N, qd = q.shape
    kd = k.shape[1]
    tq = ATTN_BLOCK
    nq = S // tq
    ctx_blk0 = (B * S) // C
    qmap = lambda b, j: (b * nq + j, 0)
    prev = lambda b, j: (b * nq + jnp.maximum(j - 1, 0), 0)
    nxt = lambda b, j: (b * nq + jnp.minimum(j + 1, nq - 1), 0)
    cmap = lambda b, j: (ctx_blk0 + b, 0)
    kblk = lambda m: pl.BlockSpec((tq, kd), m)
    cblk = pl.BlockSpec((C, kd), cmap)
    sink_spec = pl.BlockSpec(sink_col.shape, lambda b, j: (0, 0, 0))
    o_lat = pl.pallas_call(
        functools.partial(_attn_kernel, window=True, nq=nq),
        grid=(B, nq),
        in_specs=[pl.BlockSpec((tq, qd), qmap), kblk(prev), kblk(qmap), kblk(nxt), cblk,
                  kblk(prev), kblk(qmap), kblk(nxt), cblk, sink_spec],
        out_specs=pl.BlockSpec((tq, qd), qmap),
        out_shape=jax.ShapeDtypeStruct((N, qd), BF16),
        compiler_params=_cparams("parallel", "parallel"),
        name="attn_window",
    )(q, k, k, k, k, v, v, v, v, sink_col)
    if not need_ctx:
        return o_lat
    ncq = C // tq
    lat_blks = (B * S) // tq
    cq = lambda b, j: (lat_blks + b * ncq + j, 0)
    return pl.pallas_call(
        functools.partial(_attn_kernel, window=False, nq=ncq),
        grid=(B, ncq),
        in_specs=[pl.BlockSpec((tq, qd), cq), cblk, cblk, sink_spec,
                  pl.BlockSpec(memory_space=pl.ANY)],
        out_specs=pl.BlockSpec((tq, qd), cq),
        out_shape=jax.ShapeDtypeStruct((N, qd), BF16),
        input_output_aliases={4: 0},
        compiler_params=_cparams("parallel", "parallel"),
        name="attn_context",
    )(q, k, v, sink_col, o_lat)


def _gla_constants(C):
    levels = []
    m = 1
    while m < C:
        levels.append(m)
        m *= 2
    t = np.arange(C)[:, None]
    u = np.arange(C)[None, :]
    secs = [(u <= t), (u > t)]
    masks = []
    for m in levels:
        base = (t // (2 * m)) * (2 * m)
        ref = base + m - 1
        second = t >= base + m
        secs.append(np.where(second, (u > ref) & (u <= t), (u > t) & (u <= ref)))
        masks.append((t // (2 * m) == u // (2 * m)) & second & (u < (u // (2 * m)) * (2 * m) + m))
    masks.append(t == u)
    mf = np.concatenate([s.astype(np.float32) for s in secs], axis=0)
    kf = np.stack([mk.astype(np.float32) for mk in masks], axis=0)
    mb = np.concatenate([s.astype(np.float32)[::-1, ::-1] for s in secs], axis=0)
    kb = np.stack([mk.astype(np.float32)[::-1, ::-1] for mk in masks], axis=0)
    return np.stack([mf, mb]), np.stack([kf, kb]), len(levels)


def _gla_chunk(q, k, v, g, st_ref, mmat, masks, nl):
    C = q.shape[0]
    g3 = jnp.concatenate(_split3(g), axis=1)
    e3 = jnp.dot(mmat, g3, preferred_element_type=F32)
    dk = q.shape[1]
    ex = jnp.exp(e3[:, :dk] + e3[:, dk:2 * dk] + e3[:, 2 * dk:])
    qf, kf = q.astype(F32), k.astype(F32)
    qe = (qf * ex[:C]).astype(BF16)
    ke = (kf * ex[C:2 * C]).astype(BF16)
    st = st_ref[...]
    o = lax.dot_general(qe, st.astype(BF16), (((1,), (1,)), ((), ())), preferred_element_type=F32)
    a = masks[nl] * lax.dot_general(q, k, (((1,), (1,)), ((), ())), preferred_element_type=F32)
    for i in range(nl):
        xl = ex[(2 + i) * C:(3 + i) * C]
        ql = (qf * xl).astype(BF16)
        kl = (kf * xl).astype(BF16)
        a = a + masks[i] * lax.dot_general(ql, kl, (((1,), (1,)), ((), ())), preferred_element_type=F32)
    o = o + jnp.dot(a.astype(BF16), v, preferred_element_type=F32)
    decay = jnp.exp(jnp.sum(g, axis=0, keepdims=True))
    st_ref[...] = st * decay + lax.dot_general(v, ke, (((0,), (0,)), ((), ())), preferred_element_type=F32)
    return o


def _gla_kernel(qc, kc, vc, lfc, lbc, ql, kl, vl, lfl, lbl, mm_ref, mk_ref, oc_ref, ol_ref, sf_ref, sb_ref, *, nl):
    C = GLA_CHUNK
    sf_ref[...] = jnp.zeros_like(sf_ref)
    sb_ref[...] = jnp.zeros_like(sb_ref)
    oc_ref[...] = jnp.zeros_like(oc_ref)
    ol_ref[...] = jnp.zeros_like(ol_ref)

    def phase(q_ref, k_ref, v_ref, lf_ref, lb_ref, o_ref):
        n = q_ref.shape[0] // C

        def body(i, carry):
            rf = pl.ds(pl.multiple_of(i * C, C), C)
            rb = pl.ds(pl.multiple_of((n - 1 - i) * C, C), C)
            of = _gla_chunk(q_ref[rf, :], k_ref[rf, :], v_ref[rf, :], lf_ref[rf, :], sf_ref,
                            mm_ref[0], [mk_ref[0, m] for m in range(nl + 1)], nl)
            o_ref[rf, :] += of
            ob = _gla_chunk(q_ref[rb, :], k_ref[rb, :], v_ref[rb, :], lb_ref[rb, :], sb_ref,
                            mm_ref[1], [mk_ref[1, m] for m in range(nl + 1)], nl)
            o_ref[rb, :] += ob
            return carry

        lax.fori_loop(0, n, body, 0)

    phase(qc, kc, vc, lfc, lbc, oc_ref)
    phase(ql, kl, vl, lfl, lbl, ol_ref)


def _gla_scan(qk, v, la, B, S, C):
    N = qk.shape[0]
    H, DK, DV = GLA_HEADS, GLA_KEY_DIM, GLA_VAL_DIM
    mm, mk, nl = _gla_constants(GLA_CHUNK)
    mm = jnp.asarray(mm, BF16)
    mk = jnp.asarray(mk, F32)
    cb0 = (B * S) // C
    lat = lambda off: (lambda b, h: (b, off + h))
    ctx = lambda off: (lambda b, h: (cb0 + b, off + h))
    in_specs = [
        pl.BlockSpec((C, DK), ctx(0)), pl.BlockSpec((C, DK), ctx(H)), pl.BlockSpec((C, DV), ctx(0)),
        pl.BlockSpec((C, DK), ctx(0)), pl.BlockSpec((C, DK), ctx(H)),
        pl.BlockSpec((S, DK), lat(0)), pl.BlockSpec((S, DK), lat(H)), pl.BlockSpec((S, DV), lat(0)),
        pl.BlockSpec((S, DK), lat(0)), pl.BlockSpec((S, DK), lat(H)),
        pl.BlockSpec(mm.shape, lambda b, h: (0, 0, 0)),
        pl.BlockSpec(mk.shape, lambda b, h: (0, 0, 0, 0)),
    ]
    o_ctx, o_lat = pl.pallas_call(
        functools.partial(_gla_kernel, nl=nl),
        grid=(B, H),
        in_specs=in_specs,
        out_specs=[pl.BlockSpec((C, DV), lambda b, h: (b, h)), pl.BlockSpec((S, DV), lambda b, h: (b, h))],
        out_shape=[jax.ShapeDtypeStruct((B * C, H * DV), F32), jax.ShapeDtypeStruct((B * S, H * DV), F32)],
        scratch_shapes=[pltpu.VMEM((DV, DK), F32), pltpu.VMEM((DV, DK), F32)],
        compiler_params=_cparams("parallel", "parallel"),
        name="gla_scan",
    )(qk, qk, v, la, la, qk, qk, v, la, la, mm, mk)
    return o_lat, o_ctx


def _post_kernel(*refs, gla):
    if gla:
        o_ref, og_ref, ng_ref, w_ref, x_ref, gate_ref, fg_ref, fsh_ref, fsc_ref, xo_ref, f_ref = refs
        o = o_ref[...]
        g = og_ref[...].astype(F32)
        parts = []
        for h in range(GLA_HEADS):
            oh = o[:, h * GLA_VAL_DIM:(h + 1) * GLA_VAL_DIM]
            parts.append(oh * lax.rsqrt(jnp.mean(oh * oh, axis=-1, keepdims=True) + RMS_EPS) * ng_ref[...])
        mix = (jnp.concatenate(parts, axis=1) * (g * jax.nn.sigmoid(g))).astype(BF16)
    else:
        o_ref, w_ref, x_ref, gate_ref, fg_ref, fsh_ref, fsc_ref, xo_ref, f_ref = refs
        mix = o_ref[...]
    y = jnp.dot(mix, w_ref[...], preferred_element_type=F32)
    xn = x_ref[...] + gate_ref[0, 0] * y
    xo_ref[...] = xn
    f_ref[...] = _norm_mod(xn, fg_ref[...], fsh_ref[0, 0], fsc_ref[0, 0])


def _post_mixer(o, w_o, xs, mod, ffn_gain, tm, tiles_per_seq, gla_extra=None):
    N, D = xs.shape
    nmod = mod.shape[1]
    mod_idx = lambda which: (lambda t: (which, jnp.minimum(t // tiles_per_seq, nmod - 1), 0, 0))
    row = lambda width: pl.BlockSpec((tm, width), lambda t: (t, 0))
    const = lambda a: pl.BlockSpec(a.shape, lambda t: (0,) * a.ndim)
    args, specs = [o], [row(o.shape[1])]
    if gla_extra is not None:
        og, ng = gla_extra
        args += [og, ng]
        specs += [row(og.shape[1]), const(ng)]
    x_index = len(args) + 1
    args += [w_o, xs, mod, ffn_gain.reshape(1, D), mod, mod]
    specs += [const(w_o), row(D), pl.BlockSpec((1, 1, 1, D), mod_idx(2)), pl.BlockSpec((1, D), lambda t: (0, 0)),
              pl.BlockSpec((1, 1, 1, D), mod_idx(3)), pl.BlockSpec((1, 1, 1, D), mod_idx(4))]
    return pl.pallas_call(
        functools.partial(_post_kernel, gla=gla_extra is not None),
        grid=(N // tm,),
        in_specs=specs,
        out_specs=[row(D), row(D)],
        out_shape=[jax.ShapeDtypeStruct((N, D), F32), jax.ShapeDtypeStruct((N, D), F32)],
        input_output_aliases={x_index: 0},
        compiler_params=_cparams("parallel"),
        name="post_mixer_gla" if gla_extra is not None else "post_mixer_attn",
    )(*args)


def _router_kernel(f_ref, rw_ref, rb_ref, tri_ref, e_ref, gate_ref, rank_ref, cnt_ref, carry_ref):
    @pl.when(pl.program_id(0) == 0)
    def _():
        carry_ref[...] = jnp.zeros_like(carry_ref)

    f = f_ref[...]
    fh = f.astype(BF16)
    fl = (f - fh.astype(F32)).astype(BF16)
    rw = rw_ref[...]
    wh = rw.astype(BF16)
    wl = (rw - wh.astype(F32)).astype(BF16)
    logits = (jnp.dot(fh, wh, preferred_element_type=F32) + jnp.dot(fh, wl, preferred_element_type=F32)
              + jnp.dot(fl, wh, preferred_element_type=F32))
    lt = logits.T[:N_EXPERTS]
    scores = jax.nn.sigmoid(lt)
    sel = scores + rb_ref[...]
    srow = [sel[e:e + 1] for e in range(N_EXPERTS)]
    prow = [scores[e:e + 1] for e in range(N_EXPERTS)]
    gscore = []
    for g in range(N_GROUPS):
        a, b, c, d = srow[4 * g:4 * g + 4]
        hi1, lo1, hi2, lo2 = jnp.maximum(a, b), jnp.minimum(a, b), jnp.maximum(c, d), jnp.minimum(c, d)
        gscore.append(jnp.maximum(hi1, hi2) + jnp.maximum(jnp.minimum(hi1, hi2), jnp.maximum(lo1, lo2)))
    best, grp = gscore[0], jnp.zeros_like(gscore[0], dtype=I32)
    for g in range(1, N_GROUPS):
        better = gscore[g] > best
        grp = jnp.where(better, g, grp)
        best = jnp.where(better, gscore[g], best)
    s_in, p_in = [], []
    for k in range(EXPERTS_PER_GROUP):
        sv, pv = srow[k], prow[k]
        for g in range(1, N_GROUPS):
            sv = jnp.where(grp == g, srow[4 * g + k], sv)
            pv = jnp.where(grp == g, prow[4 * g + k], pv)
        s_in.append(sv)
        p_in.append(pv)
    i1, v1, g1 = jnp.zeros_like(grp), s_in[0], p_in[0]
    for k in range(1, EXPERTS_PER_GROUP):
        better = s_in[k] > v1
        i1 = jnp.where(better, k, i1)
        g1 = jnp.where(better, p_in[k], g1)
        v1 = jnp.where(better, s_in[k], v1)
    i2, v2, g2 = jnp.zeros_like(grp), jnp.full_like(v1, -jnp.inf), jnp.zeros_like(v1)
    for k in range(EXPERTS_PER_GROUP):
        better = (i1 != k) & (s_in[k] > v2)
        i2 = jnp.where(better, k, i2)
        g2 = jnp.where(better, p_in[k], g2)
        v2 = jnp.where(better, s_in[k], v2)
    e1 = grp * EXPERTS_PER_GROUP + i1
    e2 = grp * EXPERTS_PER_GROUP + i2
    tot = g1 + g2
    e_ref[0:1, :] = e1
    e_ref[1:2, :] = e2
    gate_ref[0:1, :] = g1 / tot
    gate_ref[1:2, :] = g2 / tot
    eid = lax.broadcasted_iota(I32, scores.shape, 0)
    hot1 = eid == e1
    hot2 = eid == e2
    onehot = jnp.where(hot1 | hot2, 1.0, 0.0).astype(BF16)
    cum = jnp.dot(onehot, tri_ref[...], preferred_element_type=F32) + carry_ref[...]
    rank_ref[0:1, :] = jnp.sum(jnp.where(hot1, cum, 0.0), axis=0, keepdims=True).astype(I32) - 1
    rank_ref[1:2, :] = jnp.sum(jnp.where(hot2, cum, 0.0), axis=0, keepdims=True).astype(I32) - 1
    new_carry = cum[:, cum.shape[1] - 1:]
    carry_ref[...] = new_carry
    cnt_ref[...] = jnp.broadcast_to(new_carry, cnt_ref.shape)


def _router(f, router_w, router_b, n_tiles, tm):
    D = f.shape[1]
    N = n_tiles * tm
    rw = jnp.zeros((D, LANES), F32).at[:, :N_EXPERTS].set(router_w.astype(F32))
    rb = router_b.astype(F32).reshape(N_EXPERTS, 1)
    tri = jnp.asarray(np.triu(np.ones((tm, tm), np.float32)), BF16)
    lane_row = lambda rows, dt: (pl.BlockSpec((rows, tm), lambda t: (0, t)), jax.ShapeDtypeStruct((rows, N), dt))
    outs = [lane_row(2, I32), lane_row(2, F32), lane_row(2, I32),
            (pl.BlockSpec((N_EXPERTS, LANES), lambda t: (0, 0)), jax.ShapeDtypeStruct((N_EXPERTS, LANES), F32))]
    return pl.pallas_call(
        _router_kernel,
        grid=(N // tm,),
        in_specs=[pl.BlockSpec((tm, D), lambda t: (t, 0)), pl.BlockSpec((D, LANES), lambda t: (0, 0)),
                  pl.BlockSpec((N_EXPERTS, 1), lambda t: (0, 0)), pl.BlockSpec((tm, tm), lambda t: (0, 0))],
        out_specs=[o[0] for o in outs],
        out_shape=[o[1] for o in outs],
        scratch_shapes=[pltpu.VMEM((N_EXPERTS, 1), F32)],
        compiler_params=_cparams("arbitrary"),
        name="moe_router",
    )(f, rw, rb, tri)


def _dispatch_kernel(slot_hbm, f_ref, zeros_hbm, buf_hbm, slot_smem, sem_idx, sem_rows):
    del zeros_hbm
    t = pl.program_id(0)
    tm = f_ref.shape[0]
    idx_copy = pltpu.make_async_copy(slot_hbm.at[t], slot_smem, sem_idx)
    idx_copy.start()
    idx_copy.wait()

    def row_copy(r, k):
        return pltpu.make_async_copy(f_ref.at[pl.ds(r, 1)], buf_hbm.at[pl.ds(slot_smem[k * tm + r], 1)], sem_rows)

    def issue(r, c):
        row_copy(r, 0).start()
        row_copy(r, 1).start()
        return c

    def drain(r, c):
        row_copy(r, 0).wait()
        row_copy(r, 1).wait()
        return c

    lax.fori_loop(0, tm, issue, 0)
    lax.fori_loop(0, tm, drain, 0)


def _dispatch(f, slot_tiles, n_rows, tm):
    D = f.shape[1]
    zeros = jnp.zeros((n_rows, D), F32)
    return pl.pallas_call(
        _dispatch_kernel,
        grid=(slot_tiles.shape[0],),
        in_specs=[pl.BlockSpec(memory_space=pl.ANY), pl.BlockSpec((tm, D), lambda t: (t, 0)),
                  pl.BlockSpec(memory_space=pl.ANY)],
        out_specs=pl.BlockSpec(memory_space=pl.ANY),
        out_shape=jax.ShapeDtypeStruct((n_rows, D), F32),
        scratch_shapes=[pltpu.SMEM((2 * tm,), I32), pltpu.SemaphoreType.DMA, pltpu.SemaphoreType.DMA],
        input_output_aliases={2: 0},
        compiler_params=_cparams("arbitrary"),
        name="moe_dispatch",
    )(slot_tiles, f, zeros)


def _expert_kernel(be_ref, bc_ref, x_ref, wgu_ref, wd_ref, y_ref):
    i = pl.program_id(0)

    @pl.when(bc_ref[i] > 0)
    def _():
        x = x_ref[...].astype(BF16)
        hu = jnp.dot(x, wgu_ref[0], preferred_element_type=F32)
        de = hu.shape[1] // 2
        gate, up = hu[:, :de], hu[:, de:]
        hid = (gate * jax.nn.sigmoid(gate) * up).astype(BF16)
        y_ref[...] = jnp.dot(hid, wd_ref[0], preferred_element_type=F32)

    @pl.when(bc_ref[i] == 0)
    def _():
        y_ref[...] = jnp.zeros_like(y_ref)


def _experts(buf, block_expert, block_count, wgu, wd):
    n_rows, D = buf.shape
    nb = n_rows // EXPERT_BLOCK
    grid_spec = pltpu.PrefetchScalarGridSpec(
        num_scalar_prefetch=2,
        grid=(nb,),
        in_specs=[pl.BlockSpec((EXPERT_BLOCK, D), lambda i, be, bc: (i, 0)),
                  pl.BlockSpec((1,) + wgu.shape[1:], lambda i, be, bc: (be[i], 0, 0)),
                  pl.BlockSpec((1,) + wd.shape[1:], lambda i, be, bc: (be[i], 0, 0))],
        out_specs=pl.BlockSpec((EXPERT_BLOCK, D), lambda i, be, bc: (i, 0)),
    )
    return pl.pallas_call(
        _expert_kernel,
        grid_spec=grid_spec,
        out_shape=jax.ShapeDtypeStruct((n_rows, D), F32),
        compiler_params=_cparams("arbitrary"),
        name="moe_experts",
    )(block_expert, block_count, buf, wgu, wd)


def _combine_kernel(slot_hbm, y_hbm, x_ref, gates_ref, gate_ref, fin_ref, xo_ref, slot_smem, y0_ref, y1_ref,
                    sem_idx, sem_rows, *, final):
    t = pl.program_id(0)
    tm = x_ref.shape[0]
    idx_copy = pltpu.make_async_copy(slot_hbm.at[t], slot_smem, sem_idx)
    idx_copy.start()
    idx_copy.wait()

    def row_copy(r, k):
        dst = y0_ref if k == 0 else y1_ref
        return pltpu.make_async_copy(y_hbm.at[pl.ds(slot_smem[k * tm + r], 1)], dst.at[pl.ds(r, 1)], sem_rows)

    def issue(r, c):
        row_copy(r, 0).start()
        row_copy(r, 1).start()
        return c

    def drain(r, c):
        row_copy(r, 0).wait()
        row_copy(r, 1).wait()
        return c

    lax.fori_loop(0, tm, issue, 0)
    lax.fori_loop(0, tm, drain, 0)
    gw = gates_ref[...]
    y = y0_ref[...] * gw[:, 0:1] + y1_ref[...] * gw[:, 1:2]
    xn = x_ref[...] + gate_ref[0, 0] * y
    if final:
        xn = xn * lax.rsqrt(jnp.mean(xn * xn, axis=-1, keepdims=True) + RMS_EPS) * fin_ref[...]
    xo_ref[...] = xn


def _combine(y_buf, slot_tiles, gates_t, xs, mod, final_gain, n_tiles, tm, tiles_per_seq, final):
    N, D = xs.shape
    nmod = mod.shape[1]
    out_rows = n_tiles * tm if final else N
    kwargs = {} if final else {"input_output_aliases": {2: 0}}
    return pl.pallas_call(
        functools.partial(_combine_kernel, final=final),
        grid=(n_tiles,),
        in_specs=[pl.BlockSpec(memory_space=pl.ANY), pl.BlockSpec(memory_space=pl.ANY),
                  pl.BlockSpec((tm, D), lambda t: (t, 0)),
                  pl.BlockSpec((tm, 2), lambda t: (t, 0)),
                  pl.BlockSpec((1, 1, 1, D), lambda t: (5, jnp.minimum(t // tiles_per_seq, nmod - 1), 0, 0)),
                  pl.BlockSpec((1, D), lambda t: (0, 0))],
        out_specs=pl.BlockSpec((tm, D), lambda t: (t, 0)),
        out_shape=jax.ShapeDtypeStruct((out_rows, D), F32),
        scratch_shapes=[pltpu.SMEM((2 * tm,), I32), pltpu.VMEM((tm, D), F32), pltpu.VMEM((tm, D), F32),
                        pltpu.SemaphoreType.DMA, pltpu.SemaphoreType.DMA],
        compiler_params=_cparams("arbitrary"),
        name="moe_combine_final" if final else "moe_combine",
        **kwargs,
    )(slot_tiles, y_buf, xs, gates_t, mod, final_gain.reshape(1, D))


def _moe(f, xs, mod, router_w, router_b, wgu, wd, final_gain, n_tok, tm, tiles_per_seq, final):
    nt = n_tok // tm
    e, gates, rank, cnt = _router(f, router_w, router_b, nt, tm)
    counts = cnt[:, 0].astype(I32)
    padded = (counts + EXPERT_BLOCK - 1) // EXPERT_BLOCK * EXPERT_BLOCK
    pends = jnp.cumsum(padded)
    pstarts = pends - padded
    slot = pstarts[e] + rank
    n_blocks = -(-(2 * n_tok) // EXPERT_BLOCK) + N_EXPERTS
    blk0 = jnp.arange(n_blocks, dtype=I32) * EXPERT_BLOCK
    block_expert = jnp.minimum(jnp.searchsorted(pends, blk0, side='right'), N_EXPERTS - 1).astype(I32)
    block_count = jnp.clip(pstarts[block_expert] + counts[block_expert] - blk0, 0, EXPERT_BLOCK).astype(I32)
    slot_tiles = slot.reshape(2, nt, tm).transpose(1, 0, 2).reshape(nt, 2 * tm)
    buf = _dispatch(f, slot_tiles, n_blocks * EXPERT_BLOCK, tm)
    y_buf = _experts(buf, block_expert, block_count, wgu, wd)
    return _combine(y_buf, slot_tiles, gates.T, xs, mod, final_gain, nt, tm, tiles_per_seq, final)


def _rope_tables(S, tm):
    rows = S // GRID_W
    row = jnp.repeat(jnp.arange(rows, dtype=F32), GRID_W)
    col = jnp.tile(jnp.arange(GRID_W, dtype=F32), rows)
    half = ATTN_HEAD_DIM // 4
    inv_freq = ROPE_THETA ** (-jnp.arange(half, dtype=F32) / half)
    ang_r = row[:, None] * inv_freq[None, :]
    ang_c = col[:, None] * inv_freq[None, :]
    zeros = jnp.zeros_like(ang_r)
    cos = jnp.concatenate([jnp.cos(ang_r)] * 2 + [jnp.cos(ang_c)] * 2, axis=1)
    s1 = jnp.concatenate([-jnp.sin(ang_r), zeros, -jnp.sin(ang_c), zeros], axis=1)
    s2 = jnp.concatenate([zeros, jnp.sin(ang_r), zeros, jnp.sin(ang_c)], axis=1)
    def finish(tab, fill):
        tab = jnp.tile(tab, (1, LANES // ATTN_HEAD_DIM))
        return jnp.concatenate([tab, jnp.full((tm, LANES), fill, F32)], axis=0)
    return finish(cos, 1.0), finish(s1, 0.0), finish(s2, 0.0)


def kernel(x, c, ctx, c_ctx, ada_w, ada_b, norm_mix_g, norm_ffn_g, final_g, attn_w_qkv, attn_w_o, attn_sinks,
           gla_w_in, gla_w_a1, gla_w_a2, gla_b_a, gla_norm_g, gla_w_o, router_w, router_b,
           moe_w_gate, moe_w_up, moe_w_down):
    B, S, D = x.shape
    C = ctx.shape[1]
    depth = ada_w.shape[0]
    tm = TOKEN_TILE
    assert S % tm == 0 and (B * C) % tm == 0 and S % ATTN_BLOCK == 0 and C % ATTN_BLOCK == 0
    assert (B * S) % C == 0 and S % GLA_CHUNK == 0 and C % GLA_CHUNK == 0
    n_lat = B * S
    n_lat_tiles = n_lat // tm
    tiles_per_seq = S // tm

    rpad = -(-(B + 1) // 8) * 8
    cc = jnp.zeros((rpad, D), F32).at[:B].set(c).at[B].set(c_ctx)
    mods = _ada_table(cc, ada_w, ada_b)
    mods = mods[:, :B + 1].reshape(depth, B + 1, 6, 1, D).transpose(0, 2, 1, 3, 4)

    xs = jnp.concatenate([x.reshape(n_lat, D), ctx.reshape(B * C, D)], axis=0)
    rope = _rope_tables(S, tm)
    q_dim = ATTN_KV_HEADS * ATTN_GROUP * ATTN_HEAD_DIM
    kv_dim = ATTN_KV_HEADS * ATTN_HEAD_DIM
    kd = GLA_HEADS * GLA_KEY_DIM
    vd = GLA_HEADS * GLA_VAL_DIM

    def dup_heads(w):
        w = w.reshape(D, ATTN_KV_HEADS, 1, ATTN_HEAD_DIM)
        return jnp.broadcast_to(w, (D, ATTN_KV_HEADS, LANES // ATTN_HEAD_DIM, ATTN_HEAD_DIM)).reshape(D, -1)

    for i in range(depth):
        last = i == depth - 1
        mod = mods[i]
        j = i // 2
        if i % 2 == 0:
            wqkv = attn_w_qkv[j]
            w = jnp.concatenate([wqkv[:, :q_dim], dup_heads(wqkv[:, q_dim:q_dim + kv_dim]),
                                 dup_heads(wqkv[:, q_dim + kv_dim:])], axis=1).astype(BF16)
            q, k, v = _qkv_proj(xs, norm_mix_g[i], mod, 0, w, rope, tm, n_lat_tiles, tiles_per_seq)
            sink_col = jnp.repeat(attn_sinks[j].astype(F32).reshape(ATTN_KV_HEADS, ATTN_GROUP), ATTN_BLOCK, axis=1)
            o = _attention(q, k, v, sink_col[:, :, None], B, S, C, not last)
            xs, f = _post_mixer(o, attn_w_o[j].astype(BF16), xs, mod, norm_ffn_g[i], tm, tiles_per_seq)
        else:
            a1 = jnp.zeros((D, LANES), F32).at[:, :2 * GLA_GATE_RANK].set(
                jnp.concatenate([gla_w_a1[j, 0], gla_w_a1[j, 1]], axis=1))
            w = jnp.concatenate([gla_w_in[j], a1], axis=1).astype(BF16)
            w2 = jnp.zeros((LANES, 2 * kd), F32)
            w2 = w2.at[:GLA_GATE_RANK, :kd].set(gla_w_a2[j, 0]).at[GLA_GATE_RANK:2 * GLA_GATE_RANK, kd:].set(gla_w_a2[j, 1])
            ba = gla_b_a[j].reshape(1, 2 * kd).astype(F32)
            qk, v, og, la = _gla_in_proj(xs, norm_mix_g[i], mod, 0, w, w2.astype(BF16), ba, tm, n_lat_tiles, tiles_per_seq)
            o_lat, o_ctx = _gla_scan(qk, v, la, B, S, C)
            o = jnp.concatenate([o_lat, o_ctx], axis=0)
            xs, f = _post_mixer(o, gla_w_o[j].astype(BF16), xs, mod, norm_ffn_g[i], tm, tiles_per_seq,
                                gla_extra=(og, gla_norm_g[j].reshape(1, GLA_VAL_DIM).astype(F32)))
        wgu = jnp.concatenate([moe_w_gate[i], moe_w_up[i]], axis=2).astype(BF16)
        wd = moe_w_down[i].astype(BF16)
        n_tok = n_lat if last else n_lat + B * C
        xs = _moe(f, xs, mod, router_w, router_b, wgu, wd, final_g, n_tok, tm, tiles_per_seq, last)
    return xs.reshape(B, S, D)
```

```python
import functools

import numpy as np
import jax
import jax.numpy as jnp
from jax import lax
from jax.experimental import pallas as pl
from jax.experimental.pallas import tpu as pltpu

F32 = jnp.float32
BF16 = jnp.bfloat16
I32 = jnp.int32

LANES = 128
VMEM_LIMIT_BYTES = 56 * 1024 * 1024

RMS_EPS = 1e-6
GRID_W = 64
ROPE_THETA = 10000.0
ATTN_HEAD_DIM = 64
ATTN_KV_HEADS = 4
ATTN_GROUP = 4
ATTN_BLOCK = 128
GLA_HEADS = 4
GLA_KEY_DIM = 128
GLA_VAL_DIM = 256
GLA_GATE_RANK = 16
GLA_GATE_NORM = 16.0
GLA_CHUNK = 64
N_EXPERTS = 16
N_GROUPS = 4
EXPERTS_PER_GROUP = 4
EXPERT_BLOCK = 256
RUN_ALIGN = 16
TOKEN_TILE = 512


def _cparams(*sem):
    return pltpu.CompilerParams(dimension_semantics=sem, vmem_limit_bytes=VMEM_LIMIT_BYTES)


def _norm_mod(x, gain, shift, scale):
    h = x * lax.rsqrt(jnp.mean(x * x, axis=-1, keepdims=True) + RMS_EPS) * gain
    return h * (1.0 + scale) + shift


def _ada_kernel(c_ref, w_ref, b_ref, o_ref):
    c = c_ref[...]
    s = (c * jax.nn.sigmoid(c)).astype(BF16)
    o_ref[0] = jnp.dot(s, w_ref[0].astype(BF16), preferred_element_type=F32) + b_ref[0]


def _ada_table(cc, ada_w, ada_b):
    L, D, D6 = ada_w.shape
    R = cc.shape[0]
    tn = 1536
    return pl.pallas_call(
        _ada_kernel,
        grid=(L, D6 // tn),
        in_specs=[pl.BlockSpec((R, D), lambda l, j: (0, 0)),
                  pl.BlockSpec((1, D, tn), lambda l, j: (l, 0, j)),
                  pl.BlockSpec((1, 1, tn), lambda l, j: (l, 0, j))],
        out_specs=pl.BlockSpec((1, R, tn), lambda l, j: (l, 0, j)),
        out_shape=jax.ShapeDtypeStruct((L, R, D6), F32),
        compiler_params=_cparams("parallel", "parallel"),
        name="ada_table",
    )(cc, ada_w, ada_b.reshape(L, 1, D6))


def _qkv_kernel(x_ref, g_ref, sh_ref, sc_ref, w_ref, cos_ref, s1_ref, s2_ref, q_ref, k_ref, v_ref):
    h = _norm_mod(x_ref[...], g_ref[...], sh_ref[0, 0], sc_ref[0, 0])
    z = jnp.dot(h.astype(BF16), w_ref[...], preferred_element_type=F32)
    cos, s1, s2 = cos_ref[...], s1_ref[...], s2_ref[...]
    nq = q_ref.shape[1] // LANES
    nk = k_ref.shape[1] // LANES
    for j in range(nq + nk):
        zc = z[:, j * LANES:(j + 1) * LANES]
        r = zc * cos + pltpu.roll(zc, LANES - 16, 1) * s1 + pltpu.roll(zc, 16, 1) * s2
        if j < nq:
            q_ref[:, j * LANES:(j + 1) * LANES] = (r * (ATTN_HEAD_DIM ** -0.5)).astype(BF16)
        else:
            k_ref[:, (j - nq) * LANES:(j - nq + 1) * LANES] = r.astype(BF16)
    v_ref[...] = z[:, (nq + nk) * LANES:].astype(BF16)


def _gla_in_kernel(x_ref, g_ref, sh_ref, sc_ref, w_ref, w2_ref, ba_ref, qk_ref, v_ref, og_ref, la_ref):
    h = _norm_mod(x_ref[...], g_ref[...], sh_ref[0, 0], sc_ref[0, 0])
    z = jnp.dot(h.astype(BF16), w_ref[...], preferred_element_type=F32)
    kd = GLA_HEADS * GLA_KEY_DIM
    vd = GLA_HEADS * GLA_VAL_DIM
    qk_ref[:, :kd] = (z[:, :kd] * (GLA_KEY_DIM ** -0.5)).astype(BF16)
    qk_ref[:, kd:] = z[:, kd:2 * kd].astype(BF16)
    v_ref[...] = z[:, 2 * kd:2 * kd + vd].astype(BF16)
    og_ref[...] = z[:, 2 * kd + vd:2 * kd + 2 * vd].astype(BF16)
    a1 = z[:, 2 * kd + 2 * vd:].astype(BF16)
    pre = jnp.dot(a1, w2_ref[...], preferred_element_type=F32) + ba_ref[...]
    la_ref[...] = (jnp.minimum(pre, 0.0) - jnp.log1p(jnp.exp(-jnp.abs(pre)))) * (1.0 / GLA_GATE_NORM)


def _tile_specs(D, tm, n_lat_tiles, tiles_per_seq, n_mod_rows):
    def mod_idx(which):
        return lambda t: (which, jnp.minimum(t // tiles_per_seq, n_mod_rows - 1), 0, 0)
    return mod_idx, [pl.BlockSpec((tm, D), lambda t: (t, 0)),
                     pl.BlockSpec((1, D), lambda t: (0, 0))]


def _qkv_proj(xs, gain, mod, which, w, rope, tm, n_lat_tiles, tiles_per_seq):
    N, D = xs.shape
    nmod = mod.shape[1]
    mod_idx, specs = _tile_specs(D, tm, n_lat_tiles, tiles_per_seq, nmod)
    qd = ATTN_KV_HEADS * ATTN_GROUP * ATTN_HEAD_DIM
    kd = ATTN_KV_HEADS * LANES
    rope_idx = lambda t: (jnp.where(t < n_lat_tiles, t % tiles_per_seq, tiles_per_seq), 0)
    in_specs = specs + [
        pl.BlockSpec((1, 1, 1, D), mod_idx(which)),
        pl.BlockSpec((1, 1, 1, D), mod_idx(which + 1)),
        pl.BlockSpec(w.shape, lambda t: (0, 0)),
        pl.BlockSpec((tm, LANES), rope_idx),
        pl.BlockSpec((tm, LANES), rope_idx),
        pl.BlockSpec((tm, LANES), rope_idx),
    ]
    return pl.pallas_call(
        _qkv_kernel,
        grid=(N // tm,),
        in_specs=in_specs,
        out_specs=[pl.BlockSpec((tm, qd), lambda t: (t, 0)),
                   pl.BlockSpec((tm, kd), lambda t: (t, 0)),
                   pl.BlockSpec((tm, kd), lambda t: (t, 0))],
        out_shape=[jax.ShapeDtypeStruct((N, qd), BF16),
                   jax.ShapeDtypeStruct((N, kd), BF16),
                   jax.ShapeDtypeStruct((N, kd), BF16)],
        compiler_params=_cparams("parallel"),
        name="attn_qkv_proj",
    )(xs, gain.reshape(1, D), mod, mod, w, *rope)


def _gla_in_proj(xs, gain, mod, which, w, w2, ba, tm, n_lat_tiles, tiles_per_seq):
    N, D = xs.shape
    nmod = mod.shape[1]
    mod_idx, specs = _tile_specs(D, tm, n_lat_tiles, tiles_per_seq, nmod)
    kd = GLA_HEADS * GLA_KEY_DIM
    vd = GLA_HEADS * GLA_VAL_DIM
    in_specs = specs + [
        pl.BlockSpec((1, 1, 1, D), mod_idx(which)),
        pl.BlockSpec((1, 1, 1, D), mod_idx(which + 1)),
        pl.BlockSpec(w.shape, lambda t: (0, 0)),
        pl.BlockSpec(w2.shape, lambda t: (0, 0)),
        pl.BlockSpec((1, 2 * kd), lambda t: (0, 0)),
    ]
    row = lambda width: pl.BlockSpec((tm, width), lambda t: (t, 0))
    return pl.pallas_call(
        _gla_in_kernel,
        grid=(N // tm,),
        in_specs=in_specs,
        out_specs=[row(2 * kd), row(vd), row(vd), row(2 * kd)],
        out_shape=[jax.ShapeDtypeStruct((N, 2 * kd), BF16),
                   jax.ShapeDtypeStruct((N, vd), BF16),
                   jax.ShapeDtypeStruct((N, vd), BF16),
                   jax.ShapeDtypeStruct((N, 2 * kd), F32)],
        compiler_params=_cparams("parallel"),
        name="gla_in_proj",
    )(xs, gain.reshape(1, D), mod, mod, w, w2, ba)


def _attn_kernel(*refs, window, nq):
    if window:
        q_ref, kp, kc, kn, kx, vp, vc, vn, vx, sink_ref, o_ref = refs
        k_parts, v_parts = (kp, kc, kn, kx), (vp, vc, vn, vx)
    else:
        q_ref, kx, vx, sink_ref, _, o_ref = refs
        k_parts, v_parts = (kx,), (vx,)
    tq = q_ref.shape[0]
    rows = ATTN_GROUP * tq
    nkeys = sum(p.shape[0] for p in k_parts)
    lane = lax.broadcasted_iota(I32, (tq, LANES), 1)
    first_head = lane < ATTN_HEAD_DIM
    if window:
        j = pl.program_id(1)
        r = lax.broadcasted_iota(I32, (rows, nkeys), 0) & (tq - 1)
        c = lax.broadcasted_iota(I32, (rows, nkeys), 1)
        valid = ((c >= tq) & (c < 2 * tq)) | (c >= 3 * tq)
        valid = valid | ((c < tq) & (c >= r) & (j > 0))
        valid = valid | ((c >= 2 * tq) & (c < 3 * tq) & (c - 2 * tq <= r) & (j < nq - 1))
        bias = jnp.where(valid, 0.0, -jnp.inf).astype(F32)
    for kh in range(ATTN_KV_HEADS):
        ks = slice(kh * LANES, (kh + 1) * LANES)
        kk = jnp.concatenate([p[:, ks] for p in k_parts], axis=0) if window else kx[:, ks]
        vv = jnp.concatenate([p[:, ks] for p in v_parts], axis=0) if window else vx[:, ks]
        qa = q_ref[:, (2 * kh) * LANES:(2 * kh + 1) * LANES]
        qb = q_ref[:, (2 * kh + 1) * LANES:(2 * kh + 2) * LANES]
        zero = jnp.zeros_like(qa)
        qs = jnp.concatenate([jnp.where(first_head, qa, zero), jnp.where(first_head, zero, qa),
                              jnp.where(first_head, qb, zero), jnp.where(first_head, zero, qb)], axis=0)
        s = lax.dot_general(qs, kk, (((1,), (1,)), ((), ())), preferred_element_type=F32)
        if window:
            s = s + bias
        sink = sink_ref[kh]
        m = jnp.maximum(jnp.max(s, axis=1, keepdims=True), sink)
        p = jnp.exp(s - m)
        l = jnp.sum(p, axis=1, keepdims=True) + jnp.exp(sink - m)
        o = jnp.dot(p.astype(BF16), vv, preferred_element_type=F32) / l
        oa = jnp.where(first_head, o[0:tq], o[tq:2 * tq])
        ob = jnp.where(first_head, o[2 * tq:3 * tq], o[3 * tq:4 * tq])
        o_ref[:, (2 * kh) * LANES:(2 * kh + 1) * LANES] = oa.astype(BF16)
        o_ref[:, (2 * kh + 1) * LANES:(2 * kh + 2) * LANES] = ob.astype(BF16)


def _attention(q, k, v, sink_col, B, S, C, need_ctx):
    N, qd = q.shape
    kd = k.shape[1]
    tq = ATTN_BLOCK
    nq = S // tq
    ctx_blk0 = (B * S) // C
    qmap = lambda b, j: (b * nq + j, 0)
    prev = lambda b, j: (b * nq + jnp.maximum(j - 1, 0), 0)
    nxt = lambda b, j: (b * nq + jnp.minimum(j + 1, nq - 1), 0)
    cmap = lambda b, j: (ctx_blk0 + b, 0)
    kblk = lambda m: pl.BlockSpec((tq, kd), m)
    cblk = pl.BlockSpec((C, kd), cmap)
    sink_spec = pl.BlockSpec(sink_col.shape, lambda b, j: (0, 0, 0))
    o_lat = pl.pallas_call(
        functools.partial(_attn_kernel, window=True, nq=nq),
        grid=(B, nq),
        in_specs=[pl.BlockSpec((tq, qd), qmap), kblk(prev), kblk(qmap), kblk(nxt), cblk,
                  kblk(prev), kblk(qmap), kblk(nxt), cblk, sink_spec],
        out_specs=pl.BlockSpec((tq, qd), qmap),
        out_shape=jax.ShapeDtypeStruct((N, qd), BF16),
        compiler_params=_cparams("parallel", "parallel"),
        name="attn_window",
    )(q, k, k, k, k, v, v, v, v, sink_col)
    if not need_ctx:
        return o_lat
    ncq = C // tq
    lat_blks = (B * S) // tq
    cq = lambda b, j: (lat_blks + b * ncq + j, 0)
    return pl.pallas_call(
        functools.partial(_attn_kernel, window=False, nq=ncq),
        grid=(B, ncq),
        in_specs=[pl.BlockSpec((tq, qd), cq), cblk, cblk, sink_spec,
                  pl.BlockSpec(memory_space=pl.ANY)],
        out_specs=pl.BlockSpec((tq, qd), cq),
        out_shape=jax.ShapeDtypeStruct((N, qd), BF16),
        input_output_aliases={4: 0},
        compiler_params=_cparams("parallel", "parallel"),
        name="attn_context",
    )(q, k, v, sink_col, o_lat)


def _gla_constants(C):
    levels = []
    m = 1
    while m < C:
        levels.append(m)
        m *= 2
    t = np.arange(C)[:, None]
    u = np.arange(C)[None, :]
    secs = [(u <= t), (u > t)]
    masks = []
    for m in levels:
        base = (t // (2 * m)) * (2 * m)
        ref = base + m - 1
        second = t >= base + m
        secs.append(np.where(second, (u > ref) & (u <= t), (u > t) & (u <= ref)))
        masks.append((t // (2 * m) == u // (2 * m)) & second & (u < (u // (2 * m)) * (2 * m) + m))
    masks.append(t == u)
    mf = np.concatenate([s.astype(np.float32) for s in secs], axis=0)
    kf = np.stack([mk.astype(np.float32) for mk in masks], axis=0)
    mb = np.concatenate([s.astype(np.float32)[::-1, ::-1] for s in secs], axis=0)
    kb = np.stack([mk.astype(np.float32)[::-1, ::-1] for mk in masks], axis=0)
    return np.stack([mf, mb]), np.stack([kf, kb]), len(levels)


def _gla_chunk(q, k, v, g, st_ref, mmat, masks, nl):
    C = q.shape[0]
    g_hi = g.astype(BF16)
    g_lo = (g - g_hi.astype(F32)).astype(BF16)
    e2 = jnp.dot(mmat, jnp.concatenate([g_hi, g_lo], axis=1), preferred_element_type=F32)
    dk = q.shape[1]
    ex = jnp.exp(e2[:, :dk] + e2[:, dk:])
    qf, kf = q.astype(F32), k.astype(F32)
    qe = (qf * ex[:C]).astype(BF16)
    ke = (kf * ex[C:2 * C]).astype(BF16)
    st = st_ref[...]
    o = lax.dot_general(qe, st.astype(BF16), (((1,), (1,)), ((), ())), preferred_element_type=F32)
    a = masks[nl] * lax.dot_general(q, k, (((1,), (1,)), ((), ())), preferred_element_type=F32)
    for i in range(nl):
        xl = ex[(2 + i) * C:(3 + i) * C]
        ql = (qf * xl).astype(BF16)
        kl = (kf * xl).astype(BF16)
        a = a + masks[i] * lax.dot_general(ql, kl, (((1,), (1,)), ((), ())), preferred_element_type=F32)
    o = o + jnp.dot(a.astype(BF16), v, preferred_element_type=F32)
    decay = jnp.exp(jnp.sum(g, axis=0, keepdims=True))
    st_ref[...] = st * decay + lax.dot_general(v, ke, (((0,), (0,)), ((), ())), preferred_element_type=F32)
    return o


def _gla_kernel(qc, kc, vc, lfc, lbc, ql, kl, vl, lfl, lbl, mm_ref, mk_ref, oc_ref, ol_ref, sf_ref, sb_ref, *, nl):
    C = GLA_CHUNK
    sf_ref[...] = jnp.zeros_like(sf_ref)
    sb_ref[...] = jnp.zeros_like(sb_ref)
    oc_ref[...] = jnp.zeros_like(oc_ref)
    ol_ref[...] = jnp.zeros_like(ol_ref)

    def phase(q_ref, k_ref, v_ref, lf_ref, lb_ref, o_ref):
        n = q_ref.shape[0] // C

        def body(i, carry):
            rf = pl.ds(pl.multiple_of(i * C, C), C)
            rb = pl.ds(pl.multiple_of((n - 1 - i) * C, C), C)
            of = _gla_chunk(q_ref[rf, :], k_ref[rf, :], v_ref[rf, :], lf_ref[rf, :], sf_ref,
                            mm_ref[0], [mk_ref[0, m] for m in range(nl + 1)], nl)
            o_ref[rf, :] += of
            ob = _gla_chunk(q_ref[rb, :], k_ref[rb, :], v_ref[rb, :], lb_ref[rb, :], sb_ref,
                            mm_ref[1], [mk_ref[1, m] for m in range(nl + 1)], nl)
            o_ref[rb, :] += ob
            return carry

        lax.fori_loop(0, n, body, 0)

    phase(qc, kc, vc, lfc, lbc, oc_ref)
    phase(ql, kl, vl, lfl, lbl, ol_ref)


def _gla_scan(qk, v, la, B, S, C):
    N = qk.shape[0]
    H, DK, DV = GLA_HEADS, GLA_KEY_DIM, GLA_VAL_DIM
    mm, mk, nl = _gla_constants(GLA_CHUNK)
    mm = jnp.asarray(mm, BF16)
    mk = jnp.asarray(mk, F32)
    cb0 = (B * S) // C
    lat = lambda off: (lambda b, h: (b, off + h))
    ctx = lambda off: (lambda b, h: (cb0 + b, off + h))
    in_specs = [
        pl.BlockSpec((C, DK), ctx(0)), pl.BlockSpec((C, DK), ctx(H)), pl.BlockSpec((C, DV), ctx(0)),
        pl.BlockSpec((C, DK), ctx(0)), pl.BlockSpec((C, DK), ctx(H)),
        pl.BlockSpec((S, DK), lat(0)), pl.BlockSpec((S, DK), lat(H)), pl.BlockSpec((S, DV), lat(0)),
        pl.BlockSpec((S, DK), lat(0)), pl.BlockSpec((S, DK), lat(H)),
        pl.BlockSpec(mm.shape, lambda b, h: (0, 0, 0)),
        pl.BlockSpec(mk.shape, lambda b, h: (0, 0, 0, 0)),
    ]
    o_ctx, o_lat = pl.pallas_call(
        functools.partial(_gla_kernel, nl=nl),
        grid=(B, H),
        in_specs=in_specs,
        out_specs=[pl.BlockSpec((C, DV), lambda b, h: (b, h)), pl.BlockSpec((S, DV), lambda b, h: (b, h))],
        out_shape=[jax.ShapeDtypeStruct((B * C, H * DV), F32), jax.ShapeDtypeStruct((B * S, H * DV), F32)],
        scratch_shapes=[pltpu.VMEM((DV, DK), F32), pltpu.VMEM((DV, DK), F32)],
        compiler_params=_cparams("parallel", "parallel"),
        name="gla_scan",
    )(qk, qk, v, la, la, qk, qk, v, la, la, mm, mk)
    return o_lat, o_ctx


def _post_kernel(*refs, gla):
    if gla:
        o_ref, og_ref, ng_ref, w_ref, x_ref, gate_ref, fg_ref, fsh_ref, fsc_ref, xo_ref, f_ref = refs
        o = o_ref[...]
        g = og_ref[...].astype(F32)
        parts = []
        for h in range(GLA_HEADS):
            oh = o[:, h * GLA_VAL_DIM:(h + 1) * GLA_VAL_DIM]
            parts.append(oh * lax.rsqrt(jnp.mean(oh * oh, axis=-1, keepdims=True) + RMS_EPS) * ng_ref[...])
        mix = (jnp.concatenate(parts, axis=1) * (g * jax.nn.sigmoid(g))).astype(BF16)
    else:
        o_ref, w_ref, x_ref, gate_ref, fg_ref, fsh_ref, fsc_ref, xo_ref, f_ref = refs
        mix = o_ref[...]
    y = jnp.dot(mix, w_ref[...], preferred_element_type=F32)
    xn = x_ref[...] + gate_ref[0, 0] * y
    xo_ref[...] = xn
    f_ref[...] = _norm_mod(xn, fg_ref[...], fsh_ref[0, 0], fsc_ref[0, 0])


def _post_mixer(o, w_o, xs, mod, ffn_gain, tm, tiles_per_seq, gla_extra=None):
    N, D = xs.shape
    nmod = mod.shape[1]
    mod_idx = lambda which: (lambda t: (which, jnp.minimum(t // tiles_per_seq, nmod - 1), 0, 0))
    row = lambda width: pl.BlockSpec((tm, width), lambda t: (t, 0))
    const = lambda a: pl.BlockSpec(a.shape, lambda t: (0,) * a.ndim)
    args, specs = [o], [row(o.shape[1])]
    if gla_extra is not None:
        og, ng = gla_extra
        args += [og, ng]
        specs += [row(og.shape[1]), const(ng)]
    x_index = len(args) + 1
    args += [w_o, xs, mod, ffn_gain.reshape(1, D), mod, mod]
    specs += [const(w_o), row(D), pl.BlockSpec((1, 1, 1, D), mod_idx(2)), pl.BlockSpec((1, D), lambda t: (0, 0)),
              pl.BlockSpec((1, 1, 1, D), mod_idx(3)), pl.BlockSpec((1, 1, 1, D), mod_idx(4))]
    return pl.pallas_call(
        functools.partial(_post_kernel, gla=gla_extra is not None),
        grid=(N // tm,),
        in_specs=specs,
        out_specs=[row(D), row(D)],
        out_shape=[jax.ShapeDtypeStruct((N, D), F32), jax.ShapeDtypeStruct((N, D), F32)],
        input_output_aliases={x_index: 0},
        compiler_params=_cparams("parallel"),
        name="post_mixer_gla" if gla_extra is not None else "post_mixer_attn",
    )(*args)


def _router_kernel(f_ref, rw_ref, rb_ref, tri_ref, e_ref, gate_ref, rank_ref, cnt_ref):
    f = f_ref[...]
    fh = f.astype(BF16)
    fl = (f - fh.astype(F32)).astype(BF16)
    rw = rw_ref[...]
    wh = rw.astype(BF16)
    wl = (rw - wh.astype(F32)).astype(BF16)
    logits = (jnp.dot(fh, wh, preferred_element_type=F32) + jnp.dot(fh, wl, preferred_element_type=F32)
              + jnp.dot(fl, wh, preferred_element_type=F32))
    lt = logits.T[:N_EXPERTS]
    scores = jax.nn.sigmoid(lt)
    sel = scores + rb_ref[...]
    srow = [sel[e:e + 1] for e in range(N_EXPERTS)]
    prow = [scores[e:e + 1] for e in range(N_EXPERTS)]
    gscore = []
    for g in range(N_GROUPS):
        a, b, c, d = srow[4 * g:4 * g + 4]
        hi1, lo1, hi2, lo2 = jnp.maximum(a, b), jnp.minimum(a, b), jnp.maximum(c, d), jnp.minimum(c, d)
        gscore.append(jnp.maximum(hi1, hi2) + jnp.maximum(jnp.minimum(hi1, hi2), jnp.maximum(lo1, lo2)))
    best, grp = gscore[0], jnp.zeros_like(gscore[0], dtype=I32)
    for g in range(1, N_GROUPS):
        better = gscore[g] > best
        grp = jnp.where(better, g, grp)
        best = jnp.where(better, gscore[g], best)
    s_in, p_in = [], []
    for k in range(EXPERTS_PER_GROUP):
        sv, pv = srow[k], prow[k]
        for g in range(1, N_GROUPS):
            sv = jnp.where(grp == g, srow[4 * g + k], sv)
            pv = jnp.where(grp == g, prow[4 * g + k], pv)
        s_in.append(sv)
        p_in.append(pv)
    i1, v1, g1 = jnp.zeros_like(grp), s_in[0], p_in[0]
    for k in range(1, EXPERTS_PER_GROUP):
        better = s_in[k] > v1
        i1 = jnp.where(better, k, i1)
        g1 = jnp.where(better, p_in[k], g1)
        v1 = jnp.where(better, s_in[k], v1)
    i2, v2, g2 = jnp.zeros_like(grp), jnp.full_like(v1, -jnp.inf), jnp.zeros_like(v1)
    for k in range(EXPERTS_PER_GROUP):
        better = (i1 != k) & (s_in[k] > v2)
        i2 = jnp.where(better, k, i2)
        g2 = jnp.where(better, p_in[k], g2)
        v2 = jnp.where(better, s_in[k], v2)
    e1 = grp * EXPERTS_PER_GROUP + i1
    e2 = grp * EXPERTS_PER_GROUP + i2
    tot = g1 + g2
    e_ref[0:1, :] = e1
    e_ref[1:2, :] = e2
    gate_ref[0:1, :] = g1 / tot
    gate_ref[1:2, :] = g2 / tot
    eid = lax.broadcasted_iota(I32, scores.shape, 0)
    hot1 = eid == e1
    hot2 = eid == e2
    onehot = jnp.where(hot1 | hot2, 1.0, 0.0).astype(BF16)
    cum = jnp.dot(onehot, tri_ref[...], preferred_element_type=F32)
    rank_ref[0:1, :] = jnp.sum(jnp.where(hot1, cum, 0.0), axis=0, keepdims=True).astype(I32) - 1
    rank_ref[1:2, :] = jnp.sum(jnp.where(hot2, cum, 0.0), axis=0, keepdims=True).astype(I32) - 1
    cnt_ref[0] = jnp.broadcast_to(cum[:, cum.shape[1] - 1:], cnt_ref.shape[1:])


def _router(f, router_w, router_b, n_tiles, tm):
    D = f.shape[1]
    N = n_tiles * tm
    rw = jnp.zeros((D, LANES), F32).at[:, :N_EXPERTS].set(router_w.astype(F32))
    rb = router_b.astype(F32).reshape(N_EXPERTS, 1)
    tri = jnp.asarray(np.triu(np.ones((tm, tm), np.float32)), BF16)
    lane_row = lambda rows, dt: (pl.BlockSpec((rows, tm), lambda t: (0, t)), jax.ShapeDtypeStruct((rows, N), dt))
    outs = [lane_row(2, I32), lane_row(2, F32), lane_row(2, I32),
            (pl.BlockSpec((1, N_EXPERTS, LANES), lambda t: (t, 0, 0)),
             jax.ShapeDtypeStruct((n_tiles, N_EXPERTS, LANES), F32))]
    return pl.pallas_call(
        _router_kernel,
        grid=(n_tiles,),
        in_specs=[pl.BlockSpec((tm, D), lambda t: (t, 0)), pl.BlockSpec((D, LANES), lambda t: (0, 0)),
                  pl.BlockSpec((N_EXPERTS, 1), lambda t: (0, 0)), pl.BlockSpec((tm, tm), lambda t: (0, 0))],
        out_specs=[o[0] for o in outs],
        out_shape=[o[1] for o in outs],
        compiler_params=_cparams("parallel"),
        name="moe_router",
    )(f, rw, rb, tri)


def _run_chunks(lo_ref, go_ref, len_ref, t, max_len):
    for e in range(N_EXPERTS):
        idx = t * N_EXPERTS + e
        units = len_ref[idx] // RUN_ALIGN
        lo, go = lo_ref[idx], go_ref[idx]
        done = jnp.zeros((), I32)
        bit = max_len // RUN_ALIGN
        while bit >= 1:
            rows = bit * RUN_ALIGN
            flag = (units & bit) != 0
            yield flag, pl.multiple_of(lo + done, RUN_ALIGN), pl.multiple_of(go + done, RUN_ALIGN), rows
            done = done + jnp.where(flag, rows, 0)
            bit //= 2


def _dispatch_kernel(lo_ref, go_ref, len_ref, f_ref, pos_ref, zeros_hbm, buf_hbm, srt_ref, sem):
    del zeros_hbm
    t = pl.program_id(0)
    tm = f_ref.shape[0]
    ls = srt_ref.shape[0]
    j = lax.broadcasted_iota(I32, (ls, tm), 0)
    perm = jnp.where((pos_ref[0:1, :] == j) | (pos_ref[1:2, :] == j), 1.0, 0.0).astype(BF16)
    srt_ref[...] = jnp.dot(perm, f_ref[...].astype(BF16), preferred_element_type=F32).astype(BF16)

    def chunk_copy(lo, go, rows):
        return pltpu.make_async_copy(srt_ref.at[pl.ds(lo, rows)], buf_hbm.at[pl.ds(go, rows)], sem)

    for flag, lo, go, rows in _run_chunks(lo_ref, go_ref, len_ref, t, tm):
        @pl.when(flag)
        def _():
            chunk_copy(lo, go, rows).start()
    for flag, lo, go, rows in _run_chunks(lo_ref, go_ref, len_ref, t, tm):
        @pl.when(flag)
        def _():
            chunk_copy(lo, go, rows).wait()


def _dispatch(f, pos, lo, go, rlen, n_rows, n_tiles, tm):
    D = f.shape[1]
    ls = 2 * tm + N_EXPERTS * RUN_ALIGN
    zeros = jnp.zeros((n_rows, D), BF16)
    grid_spec = pltpu.PrefetchScalarGridSpec(
        num_scalar_prefetch=3,
        grid=(n_tiles,),
        in_specs=[pl.BlockSpec((tm, D), lambda t, *_: (t, 0)), pl.BlockSpec((2, tm), lambda t, *_: (0, t)),
                  pl.BlockSpec(memory_space=pl.ANY)],
        out_specs=pl.BlockSpec(memory_space=pl.ANY),
        scratch_shapes=[pltpu.VMEM((ls, D), BF16), pltpu.SemaphoreType.DMA],
    )
    return pl.pallas_call(
        _dispatch_kernel,
        grid_spec=grid_spec,
        out_shape=jax.ShapeDtypeStruct((n_rows, D), BF16),
        input_output_aliases={5: 0},
        compiler_params=_cparams("arbitrary"),
        name="moe_dispatch",
    )(lo, go, rlen, f, pos, zeros)


def _expert_kernel(be_ref, bc_ref, x_ref, wgu_ref, wd_ref, y_ref):
    i = pl.program_id(0)

    @pl.when(bc_ref[i] > 0)
    def _():
        hu = jnp.dot(x_ref[...], wgu_ref[0], preferred_element_type=F32)
        de = hu.shape[1] // 2
        gate, up = hu[:, :de], hu[:, de:]
        hid = (gate * jax.nn.sigmoid(gate) * up).astype(BF16)
        y_ref[...] = jnp.dot(hid, wd_ref[0], preferred_element_type=F32).astype(BF16)

    @pl.when(bc_ref[i] == 0)
    def _():
        y_ref[...] = jnp.zeros_like(y_ref)


def _experts(buf, block_expert, block_count, wgu, wd):
    n_rows, D = buf.shape
    nb = n_rows // EXPERT_BLOCK
    grid_spec = pltpu.PrefetchScalarGridSpec(
        num_scalar_prefetch=2,
        grid=(nb,),
        in_specs=[pl.BlockSpec((EXPERT_BLOCK, D), lambda i, be, bc: (i, 0)),
                  pl.BlockSpec((1,) + wgu.shape[1:], lambda i, be, bc: (be[i], 0, 0)),
                  pl.BlockSpec((1,) + wd.shape[1:], lambda i, be, bc: (be[i], 0, 0))],
        out_specs=pl.BlockSpec((EXPERT_BLOCK, D), lambda i, be, bc: (i, 0)),
    )
    return pl.pallas_call(
        _expert_kernel,
        grid_spec=grid_spec,
        out_shape=jax.ShapeDtypeStruct((n_rows, D), BF16),
        compiler_params=_cparams("arbitrary"),
        name="moe_experts",
    )(block_expert, block_count, buf, wgu, wd)


def _combine_kernel(lo_ref, go_ref, len_ref, y_hbm, x_ref, pos_ref, gates_ref, gate_ref, fin_ref, xo_ref,
                    srt_ref, sem, *, final):
    t = pl.program_id(0)
    tm = x_ref.shape[0]
    ls = srt_ref.shape[0]

    @pl.when(t == 0)
    def _():
        srt_ref[...] = jnp.zeros_like(srt_ref)

    def chunk_copy(lo, go, rows):
        return pltpu.make_async_copy(y_hbm.at[pl.ds(go, rows)], srt_ref.at[pl.ds(lo, rows)], sem)

    for flag, lo, go, rows in _run_chunks(lo_ref, go_ref, len_ref, t, tm):
        @pl.when(flag)
        def _():
            chunk_copy(lo, go, rows).start()
    for flag, lo, go, rows in _run_chunks(lo_ref, go_ref, len_ref, t, tm):
        @pl.when(flag)
        def _():
            chunk_copy(lo, go, rows).wait()

    j = lax.broadcasted_iota(I32, (tm, ls), 1)
    ys = srt_ref[...]
    gw = gates_ref[...]
    y = jnp.zeros(x_ref.shape, F32)
    for k in range(2):
        pick = jnp.where(pos_ref[:, k:k + 1] == j, 1.0, 0.0).astype(BF16)
        y = y + gw[:, k:k + 1] * jnp.dot(pick, ys, preferred_element_type=F32)
    xn = x_ref[...] + gate_ref[0, 0] * y
    if final:
        xn = xn * lax.rsqrt(jnp.mean(xn * xn, axis=-1, keepdims=True) + RMS_EPS) * fin_ref[...]
    xo_ref[...] = xn


def _combine(y_buf, pos_t, gates_t, lo, go, rlen, xs, mod, final_gain, n_tiles, tm, tiles_per_seq, final):
    N, D = xs.shape
    nmod = mod.shape[1]
    ls = 2 * tm + N_EXPERTS * RUN_ALIGN
    out_rows = n_tiles * tm if final else N
    kwargs = {} if final else {"input_output_aliases": {4: 0}}
    grid_spec = pltpu.PrefetchScalarGridSpec(
        num_scalar_prefetch=3,
        grid=(n_tiles,),
        in_specs=[pl.BlockSpec(memory_space=pl.ANY),
                  pl.BlockSpec((tm, D), lambda t, *_: (t, 0)),
                  pl.BlockSpec((tm, 2), lambda t, *_: (t, 0)),
                  pl.BlockSpec((tm, 2), lambda t, *_: (t, 0)),
                  pl.BlockSpec((1, 1, 1, D), lambda t, *_: (5, jnp.minimum(t // tiles_per_seq, nmod - 1), 0, 0)),
                  pl.BlockSpec((1, D), lambda t, *_: (0, 0))],
        out_specs=pl.BlockSpec((tm, D), lambda t, *_: (t, 0)),
        scratch_shapes=[pltpu.VMEM((ls, D), BF16), pltpu.SemaphoreType.DMA],
    )
    return pl.pallas_call(
        functools.partial(_combine_kernel, final=final),
        grid_spec=grid_spec,
        out_shape=jax.ShapeDtypeStruct((out_rows, D), F32),
        compiler_params=_cparams("arbitrary"),
        name="moe_combine_final" if final else "moe_combine",
        **kwargs,
    )(lo, go, rlen, y_buf, xs, pos_t, gates_t, mod, final_gain.reshape(1, D))


def _moe(f, xs, mod, router_w, router_b, wgu, wd, final_gain, n_tok, tm, tiles_per_seq, final):
    nt = n_tok // tm
    e, gates, rank, cnt = _router(f, router_w, router_b, nt, tm)
    n = cnt[:, :, 0].astype(I32)
    rlen = (n + RUN_ALIGN - 1) // RUN_ALIGN * RUN_ALIGN
    lo = jnp.cumsum(rlen, axis=1) - rlen
    region = jnp.sum(rlen, axis=0)
    region_pad = (region + EXPERT_BLOCK - 1) // EXPERT_BLOCK * EXPERT_BLOCK
    pends = jnp.cumsum(region_pad)
    pstarts = pends - region_pad
    go = pstarts[None, :] + jnp.cumsum(rlen, axis=0) - rlen
    n_blocks = -(-(2 * n_tok + nt * N_EXPERTS * RUN_ALIGN) // EXPERT_BLOCK) + N_EXPERTS
    blk0 = jnp.arange(n_blocks, dtype=I32) * EXPERT_BLOCK
    block_expert = jnp.minimum(jnp.sum((blk0[:, None] >= pends[None, :]).astype(I32), axis=1), N_EXPERTS - 1)
    block_used = (blk0 < (pstarts + region)[block_expert]).astype(I32)
    hot = e.reshape(2, nt, tm, 1) == jnp.arange(N_EXPERTS, dtype=I32)
    pos = jnp.sum(jnp.where(hot, lo[None, :, None, :], 0), axis=-1).reshape(2, n_tok) + rank
    flat = lambda a: a.reshape(-1).astype(I32)
    buf = _dispatch(f, pos, flat(lo), flat(go), flat(rlen), n_blocks * EXPERT_BLOCK, nt, tm)
    y_buf = _experts(buf, block_expert.astype(I32), block_used, wgu, wd)
    return _combine(y_buf, pos.T, gates.T, flat(lo), flat(go), flat(rlen), xs, mod, final_gain, nt, tm,
                    tiles_per_seq, final)


def _rope_tables(S, tm):
    rows = S // GRID_W
    row = jnp.repeat(jnp.arange(rows, dtype=F32), GRID_W)
    col = jnp.tile(jnp.arange(GRID_W, dtype=F32), rows)
    half = ATTN_HEAD_DIM // 4
    inv_freq = ROPE_THETA ** (-jnp.arange(half, dtype=F32) / half)
    ang_r = row[:, None] * inv_freq[None, :]
    ang_c = col[:, None] * inv_freq[None, :]
    zeros = jnp.zeros_like(ang_r)
    cos = jnp.concatenate([jnp.cos(ang_r)] * 2 + [jnp.cos(ang_c)] * 2, axis=1)
    s1 = jnp.concatenate([-jnp.sin(ang_r), zeros, -jnp.sin(ang_c), zeros], axis=1)
    s2 = jnp.concatenate([zeros, jnp.sin(ang_r), zeros, jnp.sin(ang_c)], axis=1)
    def finish(tab, fill):
        tab = jnp.tile(tab, (1, LANES // ATTN_HEAD_DIM))
        return jnp.concatenate([tab, jnp.full((tm, LANES), fill, F32)], axis=0)
    return finish(cos, 1.0), finish(s1, 0.0), finish(s2, 0.0)


def kernel(x, c, ctx, c_ctx, ada_w, ada_b, norm_mix_g, norm_ffn_g, final_g, attn_w_qkv, attn_w_o, attn_sinks,
           gla_w_in, gla_w_a1, gla_w_a2, gla_b_a, gla_norm_g, gla_w_o, router_w, router_b,
           moe_w_gate, moe_w_up, moe_w_down):
    B, S, D = x.shape
    C = ctx.shape[1]
    depth = ada_w.shape[0]
    tm = TOKEN_TILE
    assert S % tm == 0 and (B * C) % tm == 0 and S % ATTN_BLOCK == 0 and C % ATTN_BLOCK == 0
    assert (B * S) % C == 0 and S % GLA_CHUNK == 0 and C % GLA_CHUNK == 0
    n_lat = B * S
    n_lat_tiles = n_lat // tm
    tiles_per_seq = S // tm

    rpad = -(-(B + 1) // 8) * 8
    cc = jnp.zeros((rpad, D), F32).at[:B].set(c).at[B].set(c_ctx)
    mods = _ada_table(cc, ada_w, ada_b)
    mods = mods[:, :B + 1].reshape(depth, B + 1, 6, 1, D).transpose(0, 2, 1, 3, 4)

    xs = jnp.concatenate([x.reshape(n_lat, D), ctx.reshape(B * C, D)], axis=0)
    rope = _rope_tables(S, tm)
    q_dim = ATTN_KV_HEADS * ATTN_GROUP * ATTN_HEAD_DIM
    kv_dim = ATTN_KV_HEADS * ATTN_HEAD_DIM
    kd = GLA_HEADS * GLA_KEY_DIM
    vd = GLA_HEADS * GLA_VAL_DIM

    def dup_heads(w):
        w = w.reshape(D, ATTN_KV_HEADS, 1, ATTN_HEAD_DIM)
        return jnp.broadcast_to(w, (D, ATTN_KV_HEADS, LANES // ATTN_HEAD_DIM, ATTN_HEAD_DIM)).reshape(D, -1)

    for i in range(depth):
        last = i == depth - 1
        mod = mods[i]
        j = i // 2
        if i % 2 == 0:
            wqkv = attn_w_qkv[j]
            w = jnp.concatenate([wqkv[:, :q_dim], dup_heads(wqkv[:, q_dim:q_dim + kv_dim]),
                                 dup_heads(wqkv[:, q_dim + kv_dim:])], axis=1).astype(BF16)
            q, k, v = _qkv_proj(xs, norm_mix_g[i], mod, 0, w, rope, tm, n_lat_tiles, tiles_per_seq)
            sink_col = jnp.repeat(attn_sinks[j].astype(F32).reshape(ATTN_KV_HEADS, ATTN_GROUP), ATTN_BLOCK, axis=1)
            o = _attention(q, k, v, sink_col[:, :, None], B, S, C, not last)
            xs, f = _post_mixer(o, attn_w_o[j].astype(BF16), xs, mod, norm_ffn_g[i], tm, tiles_per_seq)
        else:
            a1 = jnp.zeros((D, LANES), F32).at[:, :2 * GLA_GATE_RANK].set(
                jnp.concatenate([gla_w_a1[j, 0], gla_w_a1[j, 1]], axis=1))
            w = jnp.concatenate([gla_w_in[j], a1], axis=1).astype(BF16)
            w2 = jnp.zeros((LANES, 2 * kd), F32)
            w2 = w2.at[:GLA_GATE_RANK, :kd].set(gla_w_a2[j, 0]).at[GLA_GATE_RANK:2 * GLA_GATE_RANK, kd:].set(gla_w_a2[j, 1])
            ba = gla_b_a[j].reshape(1, 2 * kd).astype(F32)
            qk, v, og, la = _gla_in_proj(xs, norm_mix_g[i], mod, 0, w, w2.astype(BF16), ba, tm, n_lat_tiles, tiles_per_seq)
            o_lat, o_ctx = _gla_scan(qk, v, la, B, S, C)
            o = jnp.concatenate([o_lat, o_ctx], axis=0)
            xs, f = _post_mixer(o, gla_w_o[j].astype(BF16), xs, mod, norm_ffn_g[i], tm, tiles_per_seq,
                                gla_extra=(og, gla_norm_g[j].reshape(1, GLA_VAL_DIM).astype(F32)))
        wgu = jnp.concatenate([moe_w_gate[i], moe_w_up[i]], axis=2).astype(BF16)
        wd = moe_w_down[i].astype(BF16)
        n_tok = n_lat if last else n_lat + B * C
        xs = _moe(f, xs, mod, router_w, router_b, wgu, wd, final_g, n_tok, tm, tiles_per_seq, last)
    return xs.reshape(B, S, D)
```

```python
import functools

import numpy as np
import jax
import jax.numpy as jnp
from jax import lax
from jax.experimental import pallas as pl
from jax.experimental.pallas import tpu as pltpu

F32 = jnp.float32
BF16 = jnp.bfloat16
I32 = jnp.int32

LANES = 128
VMEM_LIMIT_BYTES = 56 * 1024 * 1024

RMS_EPS = 1e-6
GRID_W = 64
ROPE_THETA = 10000.0
ATTN_HEAD_DIM = 64
ATTN_KV_HEADS = 4
ATTN_GROUP = 4
ATTN_BLOCK = 128
LOG2_E = 1.4426950408889634
ATTN_Q_SCALE = ATTN_HEAD_DIM ** -0.5 * LOG2_E
GLA_HEADS = 4
GLA_KEY_DIM = 128
GLA_VAL_DIM = 256
GLA_GATE_RANK = 16
GLA_GATE_NORM = 16.0
GLA_CHUNK = 64
GLA_BOUNDED_TOTAL = 40.0
N_EXPERTS = 16
N_GROUPS = 4
EXPERTS_PER_GROUP = 4
EXPERT_BLOCK = 512
RUN_ALIGN = 16
TOKEN_TILE = 512


def _cparams(*sem):
    return pltpu.CompilerParams(dimension_semantics=sem, vmem_limit_bytes=VMEM_LIMIT_BYTES)


def _norm_mod(x, gain, shift, scale):
    h = x * lax.rsqrt(jnp.mean(x * x, axis=-1, keepdims=True) + RMS_EPS) * gain
    return h * (1.0 + scale) + shift


def _ada_kernel(c_ref, w_ref, b_ref, o_ref):
    c = c_ref[...]
    s = (c * jax.nn.sigmoid(c)).astype(BF16)
    o_ref[0] = jnp.dot(s, w_ref[0].astype(BF16), preferred_element_type=F32) + b_ref[0]


def _ada_table(cc, ada_w, ada_b):
    L, D, D6 = ada_w.shape
    R = cc.shape[0]
    tn = 1536
    return pl.pallas_call(
        _ada_kernel,
        grid=(L, D6 // tn),
        in_specs=[pl.BlockSpec((R, D), lambda l, j: (0, 0)),
                  pl.BlockSpec((1, D, tn), lambda l, j: (l, 0, j)),
                  pl.BlockSpec((1, 1, tn), lambda l, j: (l, 0, j))],
        out_specs=pl.BlockSpec((1, R, tn), lambda l, j: (l, 0, j)),
        out_shape=jax.ShapeDtypeStruct((L, R, D6), F32),
        compiler_params=_cparams("parallel", "parallel"),
        name="ada_table",
    )(cc, ada_w, ada_b.reshape(L, 1, D6))


def _qkv_kernel(x_ref, g_ref, sh_ref, sc_ref, w_ref, cos_ref, s1_ref, s2_ref, q_ref, k_ref, v_ref):
    h = _norm_mod(x_ref[...], g_ref[...], sh_ref[0, 0], sc_ref[0, 0])
    z = jnp.dot(h.astype(BF16), w_ref[...], preferred_element_type=F32)
    cos, s1, s2 = cos_ref[...], s1_ref[...], s2_ref[...]
    nq = q_ref.shape[1] // LANES
    nk = k_ref.shape[1] // LANES
    for j in range(nq + nk):
        zc = z[:, j * LANES:(j + 1) * LANES]
        r = zc * cos + pltpu.roll(zc, LANES - 16, 1) * s1 + pltpu.roll(zc, 16, 1) * s2
        if j < nq:
            q_ref[:, j * LANES:(j + 1) * LANES] = (r * ATTN_Q_SCALE).astype(BF16)
        else:
            k_ref[:, (j - nq) * LANES:(j - nq + 1) * LANES] = r.astype(BF16)
    v_ref[...] = z[:, (nq + nk) * LANES:].astype(BF16)


def _gla_in_kernel(x_ref, g_ref, sh_ref, sc_ref, w_ref, w2_ref, ba_ref, qk_ref, v_ref, og_ref, la_ref, tot_ref):
    h = _norm_mod(x_ref[...], g_ref[...], sh_ref[0, 0], sc_ref[0, 0])
    z = jnp.dot(h.astype(BF16), w_ref[...], preferred_element_type=F32)
    kd = GLA_HEADS * GLA_KEY_DIM
    vd = GLA_HEADS * GLA_VAL_DIM
    qk_ref[:, :kd] = (z[:, :kd] * (GLA_KEY_DIM ** -0.5)).astype(BF16)
    qk_ref[:, kd:] = z[:, kd:2 * kd].astype(BF16)
    v_ref[...] = z[:, 2 * kd:2 * kd + vd].astype(BF16)
    og_ref[...] = z[:, 2 * kd + vd:2 * kd + 2 * vd].astype(BF16)
    a1 = z[:, 2 * kd + 2 * vd:].astype(BF16)
    pre = jnp.dot(a1, w2_ref[...], preferred_element_type=F32) + ba_ref[...]
    la = (jnp.minimum(pre, 0.0) - jnp.log1p(jnp.exp(-jnp.abs(pre)))) * (1.0 / GLA_GATE_NORM)
    la_ref[...] = la
    nc = la.shape[0] // GLA_CHUNK
    tot = jnp.sum(la.reshape(nc, GLA_CHUNK, la.shape[1]), axis=1)
    lane = lax.broadcasted_iota(I32, (nc, LANES), 1)
    acc = jnp.zeros((nc, LANES), F32)
    for hd in range(2 * GLA_HEADS):
        worst = jnp.min(tot[:, hd * GLA_KEY_DIM:(hd + 1) * GLA_KEY_DIM], axis=1, keepdims=True)
        acc = jnp.where(lane == hd, worst, acc)
    tot_ref[...] = acc


def _tile_specs(D, tm, n_lat_tiles, tiles_per_seq, n_mod_rows):
    def mod_idx(which):
        return lambda t: (which, jnp.minimum(t // tiles_per_seq, n_mod_rows - 1), 0, 0)
    return mod_idx, [pl.BlockSpec((tm, D), lambda t: (t, 0)),
                     pl.BlockSpec((1, D), lambda t: (0, 0))]


def _qkv_proj(xs, gain, mod, which, w, rope, tm, n_lat_tiles, tiles_per_seq):
    N, D = xs.shape
    nmod = mod.shape[1]
    mod_idx, specs = _tile_specs(D, tm, n_lat_tiles, tiles_per_seq, nmod)
    qd = ATTN_KV_HEADS * ATTN_GROUP * ATTN_HEAD_DIM
    kd = ATTN_KV_HEADS * LANES
    rope_idx = lambda t: (jnp.where(t < n_lat_tiles, t % tiles_per_seq, tiles_per_seq), 0)
    in_specs = specs + [
        pl.BlockSpec((1, 1, 1, D), mod_idx(which)),
        pl.BlockSpec((1, 1, 1, D), mod_idx(which + 1)),
        pl.BlockSpec(w.shape, lambda t: (0, 0)),
        pl.BlockSpec((tm, LANES), rope_idx),
        pl.BlockSpec((tm, LANES), rope_idx),
        pl.BlockSpec((tm, LANES), rope_idx),
    ]
    return pl.pallas_call(
        _qkv_kernel,
        grid=(N // tm,),
        in_specs=in_specs,
        out_specs=[pl.BlockSpec((tm, qd), lambda t: (t, 0)),
                   pl.BlockSpec((tm, kd), lambda t: (t, 0)),
                   pl.BlockSpec((tm, kd), lambda t: (t, 0))],
        out_shape=[jax.ShapeDtypeStruct((N, qd), BF16),
                   jax.ShapeDtypeStruct((N, kd), BF16),
                   jax.ShapeDtypeStruct((N, kd), BF16)],
        compiler_params=_cparams("parallel"),
        name="attn_qkv_proj",
    )(xs, gain.reshape(1, D), mod, mod, w, *rope)


def _gla_in_proj(xs, gain, mod, which, w, w2, ba, tm, n_lat_tiles, tiles_per_seq):
    N, D = xs.shape
    nmod = mod.shape[1]
    mod_idx, specs = _tile_specs(D, tm, n_lat_tiles, tiles_per_seq, nmod)
    kd = GLA_HEADS * GLA_KEY_DIM
    vd = GLA_HEADS * GLA_VAL_DIM
    in_specs = specs + [
        pl.BlockSpec((1, 1, 1, D), mod_idx(which)),
        pl.BlockSpec((1, 1, 1, D), mod_idx(which + 1)),
        pl.BlockSpec(w.shape, lambda t: (0, 0)),
        pl.BlockSpec(w2.shape, lambda t: (0, 0)),
        pl.BlockSpec((1, 2 * kd), lambda t: (0, 0)),
    ]
    row = lambda width: pl.BlockSpec((tm, width), lambda t: (t, 0))
    return pl.pallas_call(
        _gla_in_kernel,
        grid=(N // tm,),
        in_specs=in_specs,
        out_specs=[row(2 * kd), row(vd), row(vd), row(2 * kd),
                   pl.BlockSpec((tm // GLA_CHUNK, LANES), lambda t: (t, 0))],
        out_shape=[jax.ShapeDtypeStruct((N, 2 * kd), BF16),
                   jax.ShapeDtypeStruct((N, vd), BF16),
                   jax.ShapeDtypeStruct((N, vd), BF16),
                   jax.ShapeDtypeStruct((N, 2 * kd), F32),
                   jax.ShapeDtypeStruct((N // GLA_CHUNK, LANES), F32)],
        compiler_params=_cparams("parallel"),
        name="gla_in_proj",
    )(xs, gain.reshape(1, D), mod, mod, w, w2, ba)


def _attn_kernel(*refs, window, nq):
    if window:
        q_ref, kp, kc, kn, kx, vp, vc, vn, vx, sink_ref, tri_ref, o_ref = refs
        k_parts, v_parts = (kp, kc, kn, kx), (vp, vc, vn, vx)
    else:
        q_ref, kx, vx, sink_ref, _, o_ref = refs
        k_parts, v_parts = (kx,), (vx,)
    tq = q_ref.shape[0]
    lane = lax.broadcasted_iota(I32, (tq, LANES), 1)
    first_head = lane < ATTN_HEAD_DIM
    if window:
        j = pl.program_id(1)
        bias_prev = jnp.where(j > 0, tri_ref[0], -jnp.inf)
        bias_next = jnp.where(j < nq - 1, tri_ref[1], -jnp.inf)
        bias_prev = jnp.concatenate([bias_prev] * ATTN_GROUP, axis=0)
        bias_next = jnp.concatenate([bias_next] * ATTN_GROUP, axis=0)
    def scores(kh):
        ks = slice(kh * LANES, (kh + 1) * LANES)
        kk = jnp.concatenate([p[:, ks] for p in k_parts], axis=0) if window else kx[:, ks]
        qa = q_ref[:, (2 * kh) * LANES:(2 * kh + 1) * LANES]
        qb = q_ref[:, (2 * kh + 1) * LANES:(2 * kh + 2) * LANES]
        zero = jnp.zeros_like(qa)
        qs = jnp.concatenate([jnp.where(first_head, qa, zero), jnp.where(first_head, zero, qa),
                              jnp.where(first_head, qb, zero), jnp.where(first_head, zero, qb)], axis=0)
        return lax.dot_general(qs, kk, (((1,), (1,)), ((), ())), preferred_element_type=F32)

    def softmax(s, kh):
        if window:
            s = jnp.concatenate([s[:, :tq] + bias_prev, s[:, tq:2 * tq], s[:, 2 * tq:3 * tq] + bias_next,
                                 s[:, 3 * tq:]], axis=1)
        sink = sink_ref[kh]
        m = jnp.maximum(jnp.max(s, axis=1, keepdims=True), sink)
        p = jnp.exp2(s - m)
        l = jnp.sum(p, axis=1, keepdims=True) + jnp.exp2(sink - m)
        return p.astype(BF16), l

    def values(p, l, kh):
        ks = slice(kh * LANES, (kh + 1) * LANES)
        vv = jnp.concatenate([part[:, ks] for part in v_parts], axis=0) if window else vx[:, ks]
        o = jnp.dot(p, vv, preferred_element_type=F32) / l
        oa = jnp.where(first_head, o[0:tq], o[tq:2 * tq])
        ob = jnp.where(first_head, o[2 * tq:3 * tq], o[3 * tq:4 * tq])
        o_ref[:, (2 * kh) * LANES:(2 * kh + 1) * LANES] = oa.astype(BF16)
        o_ref[:, (2 * kh + 1) * LANES:(2 * kh + 2) * LANES] = ob.astype(BF16)

    s_next = scores(0)
    for kh in range(ATTN_KV_HEADS):
        s_cur = s_next
        if kh + 1 < ATTN_KV_HEADS:
            s_next = scores(kh + 1)
        p, l = softmax(s_cur, kh)
        values(p, l, kh)


def _attention(q, k, v, sink_col, B, S, C, need_ctx):
    N, qd = q.shape
    kd = k.shape[1]
    tq = ATTN_BLOCK
    nq = S // tq
    ctx_blk0 = (B * S) // C
    qmap = lambda b, j: (b * nq + j, 0)
    prev = lambda b, j: (b * nq + jnp.maximum(j - 1, 0), 0)
    nxt = lambda b, j: (b * nq + jnp.minimum(j + 1, nq - 1), 0)
    cmap = lambda b, j: (ctx_blk0 + b, 0)
    kblk = lambda m: pl.BlockSpec((tq, kd), m)
    cblk = pl.BlockSpec((C, kd), cmap)
    sink_spec = pl.BlockSpec(sink_col.shape, lambda b, j: (0, 0, 0))
    r = np.arange(tq)[:, None]
    c = np.arange(tq)[None, :]
    tri = jnp.asarray(np.stack([np.where(c >= r, 0.0, -np.inf), np.where(c <= r, 0.0, -np.inf)]), F32)
    o_lat = pl.pallas_call(
        functools.partial(_attn_kernel, window=True, nq=nq),
        grid=(B, nq),
        in_specs=[pl.BlockSpec((tq, qd), qmap), kblk(prev), kblk(qmap), kblk(nxt), cblk,
                  kblk(prev), kblk(qmap), kblk(nxt), cblk, sink_spec,
                  pl.BlockSpec(tri.shape, lambda b, j: (0, 0, 0))],
        out_specs=pl.BlockSpec((tq, qd), qmap),
        out_shape=jax.ShapeDtypeStruct((N, qd), BF16),
        compiler_params=_cparams("parallel", "parallel"),
        name="attn_window",
    )(q, k, k, k, k, v, v, v, v, sink_col, tri)
    if not need_ctx:
        return o_lat
    ncq = C // tq
    lat_blks = (B * S) // tq
    cq = lambda b, j: (lat_blks + b * ncq + j, 0)
    return pl.pallas_call(
        functools.partial(_attn_kernel, window=False, nq=ncq),
        grid=(B, ncq),
        in_specs=[pl.BlockSpec((tq, qd), cq), cblk, cblk, sink_spec,
                  pl.BlockSpec(memory_space=pl.ANY)],
        out_specs=pl.BlockSpec((tq, qd), cq),
        out_shape=jax.ShapeDtypeStruct((N, qd), BF16),
        input_output_aliases={4: 0},
        compiler_params=_cparams("parallel", "parallel"),
        name="attn_context",
    )(q, k, v, sink_col, o_lat)


def _gla_constants(C):
    levels = []
    m = 1
    while m < C:
        levels.append(m)
        m *= 2
    t = np.arange(C)[:, None]
    u = np.arange(C)[None, :]
    secs = [(u <= t), (u > t)]
    masks = []
    for m in levels:
        base = (t // (2 * m)) * (2 * m)
        ref = base + m - 1
        second = t >= base + m
        secs.append(np.where(second, (u > ref) & (u <= t), (u > t) & (u <= ref)))
        masks.append((t // (2 * m) == u // (2 * m)) & second & (u < (u // (2 * m)) * (2 * m) + m))
    masks.append(t == u)
    masks.append(u <= t)
    mf = np.concatenate([s.astype(np.float32) for s in secs], axis=0)
    kf = np.stack([mk.astype(np.float32) for mk in masks], axis=0)
    mb = np.concatenate([s.astype(np.float32)[::-1, ::-1] for s in secs], axis=0)
    kb = np.stack([mk.astype(np.float32)[::-1, ::-1] for mk in masks], axis=0)
    return np.stack([mf, mb]), np.stack([kf, kb]), len(levels)


def _gla_chunk(q, k, v, g, st_ref, mm_ref, mk_ref, d, nl, bounded):
    C, dk = q.shape
    nt = (((1,), (1,)), ((), ()))
    g_hi = g.astype(BF16)
    g_lo = (g - g_hi.astype(F32)).astype(BF16)
    mmat = mm_ref[d, :2 * C] if bounded else mm_ref[d]
    e2 = jnp.dot(mmat, jnp.concatenate([g_hi, g_lo], axis=1), preferred_element_type=F32)
    ee = e2[:, :dk] + e2[:, dk:]
    ex = jnp.exp(ee)
    qf, kf = q.astype(F32), k.astype(F32)
    qe = (qf * ex[:C]).astype(BF16)
    ke = (kf * ex[C:2 * C]).astype(BF16)
    st = st_ref[...]
    o = lax.dot_general(qe, st.astype(BF16), nt, preferred_element_type=F32)
    if bounded:
        ki = (kf * jnp.exp(-ee[:C])).astype(BF16)
        a = mk_ref[d, nl + 1] * lax.dot_general(qe, ki, nt, preferred_element_type=F32)
    else:
        a = mk_ref[d, nl] * lax.dot_general(q, k, nt, preferred_element_type=F32)
        for i in range(nl):
            xl = ex[(2 + i) * C:(3 + i) * C]
            ql = (qf * xl).astype(BF16)
            kl = (kf * xl).astype(BF16)
            a = a + mk_ref[d, i] * lax.dot_general(ql, kl, nt, preferred_element_type=F32)
    o = o + jnp.dot(a.astype(BF16), v, preferred_element_type=F32)
    decay = jnp.exp(jnp.sum(g, axis=0, keepdims=True))
    st_ref[...] = st * decay + lax.dot_general(v, ke, (((0,), (0,)), ((), ())), preferred_element_type=F32)
    return o


def _gla_kernel(okf_ref, okb_ref, qf, kf, vf, lf, qb, kb, vb, lb, mm_ref, mk_ref, of_ref, ob_ref, st_ref,
                *, nl, seg_chunks, lat_segs, ctx_seg0):
    C = GLA_CHUNK
    H, DK, DV = GLA_HEADS, GLA_KEY_DIM, GLA_VAL_DIM
    b, j = pl.program_id(0), pl.program_id(1)

    @pl.when(j == 0)
    def _():
        st_ref[...] = jnp.zeros_like(st_ref)

    seg_f = jnp.where(j == 0, ctx_seg0 + b, b * lat_segs + j - 1)
    seg_b = jnp.where(j == 0, ctx_seg0 + b, b * lat_segs + lat_segs - j)

    def body(i, carry):
        cf, cb = i, seg_chunks - 1 - i
        rf = pl.ds(pl.multiple_of(cf * C, C), C)
        rb = pl.ds(pl.multiple_of(cb * C, C), C)
        bounded = (okf_ref[seg_f * seg_chunks + cf] != 0) & (okb_ref[seg_b * seg_chunks + cb] != 0)

        def advance_general():
            for h in range(H):
                ks, vs = slice(h * DK, (h + 1) * DK), slice(h * DV, (h + 1) * DV)
                o = _gla_chunk(qf[rf, ks], kf[rf, ks], vf[rf, vs], lf[rf, ks], st_ref.at[h],
                               mm_ref, mk_ref, 0, nl, False)
                of_ref[rf, vs] = o.astype(BF16)
                o = _gla_chunk(qb[rb, ks], kb[rb, ks], vb[rb, vs], lb[rb, ks], st_ref.at[H + h],
                               mm_ref, mk_ref, 1, nl, False)
                ob_ref[rb, vs] = o.astype(BF16)

        def advance_bounded():
            nt = (((1,), (1,)), ((), ()))
            kw = H * DK
            sides = ((qf, kf, vf, lf, rf, of_ref, 0), (qb, kb, vb, lb, rb, ob_ref, 1))
            pre = []
            for q_r, k_r, v_r, l_r, rows, o_r, d in sides:
                g = l_r[rows, :]
                g_hi = g.astype(BF16)
                g_lo = (g - g_hi.astype(F32)).astype(BF16)
                e2 = jnp.dot(mm_ref[d, :2 * C], jnp.concatenate([g_hi, g_lo], axis=1),
                             preferred_element_type=F32)
                pre.append((e2[:, :kw] + e2[:, kw:], g))
            units = []
            for (ee, g), (q_r, k_r, v_r, l_r, rows, o_r, d) in zip(pre, sides):
                ex = jnp.exp(ee)
                kf32 = k_r[rows, :].astype(F32)
                qe = (q_r[rows, :].astype(F32) * ex[:C]).astype(BF16)
                ke = (kf32 * ex[C:]).astype(BF16)
                ki = (kf32 * jnp.exp(-ee[:C])).astype(BF16)
                decay = jnp.exp(jnp.sum(g, axis=0, keepdims=True))
                for h in range(H):
                    ks = slice(h * DK, (h + 1) * DK)
                    units.append((d, h, qe[:, ks], ke[:, ks], ki[:, ks], decay[:, ks], v_r, rows, o_r))
            inter, score = [], []
            for d, h, qe, ke, ki, decay, v_r, rows, o_r in units:
                st = st_ref[d * H + h]
                inter.append(lax.dot_general(qe, st.astype(BF16), nt, preferred_element_type=F32))
                score.append(lax.dot_general(qe, ki, nt, preferred_element_type=F32))
            for n, (d, h, qe, ke, ki, decay, v_r, rows, o_r) in enumerate(units):
                vs = slice(h * DV, (h + 1) * DV)
                v = v_r[rows, vs]
                a = (mk_ref[d, nl + 1] * score[n]).astype(BF16)
                o = inter[n] + jnp.dot(a, v, preferred_element_type=F32)
                upd = lax.dot_general(v, ke, (((0,), (0,)), ((), ())), preferred_element_type=F32)
                o_r[rows, vs] = o.astype(BF16)
                st_ref[d * H + h] = st_ref[d * H + h] * decay + upd

        @pl.when(bounded)
        def _():
            advance_bounded()

        @pl.when(jnp.logical_not(bounded))
        def _():
            advance_general()

        return carry

    lax.fori_loop(0, seg_chunks, body, 0)


def _gla_scan(qk, v, la, chunk_tot, B, S, C):
    N = qk.shape[0]
    H, DK, DV = GLA_HEADS, GLA_KEY_DIM, GLA_VAL_DIM
    seg = C
    assert S % seg == 0 and seg % GLA_CHUNK == 0
    lat_segs = S // seg
    ctx_seg0 = (B * S) // seg
    mm, mk, nl = _gla_constants(GLA_CHUNK)
    mm = jnp.asarray(mm, BF16)
    mk = jnp.asarray(mk, F32)
    ok = chunk_tot[:, :2 * H] >= -GLA_BOUNDED_TOTAL
    okf = jnp.all(ok[:, :H], axis=1).astype(I32)
    okb = jnp.all(ok[:, H:], axis=1).astype(I32)
    fwd = lambda col: (lambda b, j, *_: (jnp.where(j == 0, ctx_seg0 + b, b * lat_segs + j - 1), col))
    bwd = lambda col: (lambda b, j, *_: (jnp.where(j == 0, ctx_seg0 + b, b * lat_segs + lat_segs - j), col))
    kw, vw = H * DK, H * DV
    in_specs = [
        pl.BlockSpec((seg, kw), fwd(0)), pl.BlockSpec((seg, kw), fwd(1)), pl.BlockSpec((seg, vw), fwd(0)),
        pl.BlockSpec((seg, kw), fwd(0)),
        pl.BlockSpec((seg, kw), bwd(0)), pl.BlockSpec((seg, kw), bwd(1)), pl.BlockSpec((seg, vw), bwd(0)),
        pl.BlockSpec((seg, kw), bwd(1)),
        pl.BlockSpec(mm.shape, lambda b, j, *_: (0, 0, 0)),
        pl.BlockSpec(mk.shape, lambda b, j, *_: (0, 0, 0, 0)),
    ]
    grid_spec = pltpu.PrefetchScalarGridSpec(
        num_scalar_prefetch=2,
        grid=(B, lat_segs + 1),
        in_specs=in_specs,
        out_specs=[pl.BlockSpec((seg, vw), fwd(0)), pl.BlockSpec((seg, vw), bwd(0))],
        scratch_shapes=[pltpu.VMEM((2 * H, DV, DK), F32)],
    )
    return pl.pallas_call(
        functools.partial(_gla_kernel, nl=nl, seg_chunks=seg // GLA_CHUNK, lat_segs=lat_segs, ctx_seg0=ctx_seg0),
        grid_spec=grid_spec,
        out_shape=[jax.ShapeDtypeStruct((N, vw), BF16), jax.ShapeDtypeStruct((N, vw), BF16)],
        compiler_params=_cparams("parallel", "arbitrary"),
        name="gla_scan",
    )(okf, okb, qk, qk, v, la, qk, qk, v, la, mm, mk)


def _post_kernel(*refs, gla):
    if gla:
        o_ref, ob_ref, og_ref, ng_ref, w_ref, x_ref, gate_ref, fg_ref, fsh_ref, fsc_ref, xo_ref, f_ref = refs
        o = o_ref[...].astype(F32) + ob_ref[...].astype(F32)
        g = og_ref[...].astype(F32)
        parts = []
        for h in range(GLA_HEADS):
            oh = o[:, h * GLA_VAL_DIM:(h + 1) * GLA_VAL_DIM]
            parts.append(oh * lax.rsqrt(jnp.mean(oh * oh, axis=-1, keepdims=True) + RMS_EPS) * ng_ref[...])
        mix = (jnp.concatenate(parts, axis=1) * (g * jax.nn.sigmoid(g))).astype(BF16)
    else:
        o_ref, w_ref, x_ref, gate_ref, fg_ref, fsh_ref, fsc_ref, xo_ref, f_ref = refs
        mix = o_ref[...]
    y = jnp.dot(mix, w_ref[...], preferred_element_type=F32)
    xn = x_ref[...] + gate_ref[0, 0] * y
    xo_ref[...] = xn
    f_ref[...] = _norm_mod(xn, fg_ref[...], fsh_ref[0, 0], fsc_ref[0, 0])


def _post_mixer(o, w_o, xs, mod, ffn_gain, tm, tiles_per_seq, gla_extra=None):
    N, D = xs.shape
    nmod = mod.shape[1]
    mod_idx = lambda which: (lambda t: (which, jnp.minimum(t // tiles_per_seq, nmod - 1), 0, 0))
    row = lambda width: pl.BlockSpec((tm, width), lambda t: (t, 0))
    const = lambda a: pl.BlockSpec(a.shape, lambda t: (0,) * a.ndim)
    args, specs = [o], [row(o.shape[1])]
    if gla_extra is not None:
        o_bwd, og, ng = gla_extra
        args += [o_bwd, og, ng]
        specs += [row(o_bwd.shape[1]), row(og.shape[1]), const(ng)]
    x_index = len(args) + 1
    args += [w_o, xs, mod, ffn_gain.reshape(1, D), mod, mod]
    specs += [const(w_o), row(D), pl.BlockSpec((1, 1, 1, D), mod_idx(2)), pl.BlockSpec((1, D), lambda t: (0, 0)),
              pl.BlockSpec((1, 1, 1, D), mod_idx(3)), pl.BlockSpec((1, 1, 1, D), mod_idx(4))]
    return pl.pallas_call(
        functools.partial(_post_kernel, gla=gla_extra is not None),
        grid=(N // tm,),
        in_specs=specs,
        out_specs=[row(D), row(D)],
        out_shape=[jax.ShapeDtypeStruct((N, D), F32), jax.ShapeDtypeStruct((N, D), F32)],
        input_output_aliases={x_index: 0},
        compiler_params=_cparams("parallel"),
        name="post_mixer_gla" if gla_extra is not None else "post_mixer_attn",
    )(*args)


def _router_kernel(f_ref, rw_ref, rb_ref, tri_ref, e_ref, gate_ref, rank_ref, cnt_ref):
    f = f_ref[...]
    fh = f.astype(BF16)
    fl = (f - fh.astype(F32)).astype(BF16)
    rw = rw_ref[...]
    wh = rw.astype(BF16)
    wl = (rw - wh.astype(F32)).astype(BF16)
    logits = (jnp.dot(fh, wh, preferred_element_type=F32) + jnp.dot(fh, wl, preferred_element_type=F32)
              + jnp.dot(fl, wh, preferred_element_type=F32))
    lt = logits.T[:N_EXPERTS]
    scores = jax.nn.sigmoid(lt)
    sel = scores + rb_ref[...]
    srow = [sel[e:e + 1] for e in range(N_EXPERTS)]
    prow = [scores[e:e + 1] for e in range(N_EXPERTS)]
    gscore = []
    for g in range(N_GROUPS):
        a, b, c, d = srow[4 * g:4 * g + 4]
        hi1, lo1, hi2, lo2 = jnp.maximum(a, b), jnp.minimum(a, b), jnp.maximum(c, d), jnp.minimum(c, d)
        gscore.append(jnp.maximum(hi1, hi2) + jnp.maximum(jnp.minimum(hi1, hi2), jnp.maximum(lo1, lo2)))
    best, grp = gscore[0], jnp.zeros_like(gscore[0], dtype=I32)
    for g in range(1, N_GROUPS):
        better = gscore[g] > best
        grp = jnp.where(better, g, grp)
        best = jnp.where(better, gscore[g], best)
    s_in, p_in = [], []
    for k in range(EXPERTS_PER_GROUP):
        sv, pv = srow[k], prow[k]
        for g in range(1, N_GROUPS):
            sv = jnp.where(grp == g, srow[4 * g + k], sv)
            pv = jnp.where(grp == g, prow[4 * g + k], pv)
        s_in.append(sv)
        p_in.append(pv)
    i1, v1, g1 = jnp.zeros_like(grp), s_in[0], p_in[0]
    for k in range(1, EXPERTS_PER_GROUP):
        better = s_in[k] > v1
        i1 = jnp.where(better, k, i1)
        g1 = jnp.where(better, p_in[k], g1)
        v1 = jnp.where(better, s_in[k], v1)
    i2, v2, g2 = jnp.zeros_like(grp), jnp.full_like(v1, -jnp.inf), jnp.zeros_like(v1)
    for k in range(EXPERTS_PER_GROUP):
        better = (i1 != k) & (s_in[k] > v2)
        i2 = jnp.where(better, k, i2)
        g2 = jnp.where(better, p_in[k], g2)
        v2 = jnp.where(better, s_in[k], v2)
    e1 = grp * EXPERTS_PER_GROUP + i1
    e2 = grp * EXPERTS_PER_GROUP + i2
    tot = g1 + g2
    e_ref[0:1, :] = e1
    e_ref[1:2, :] = e2
    gate_ref[0:1, :] = g1 / tot
    gate_ref[1:2, :] = g2 / tot
    eid = lax.broadcasted_iota(I32, scores.shape, 0)
    hot1 = eid == e1
    hot2 = eid == e2
    onehot = jnp.where(hot1 | hot2, 1.0, 0.0).astype(BF16)
    cum = jnp.dot(onehot, tri_ref[...], preferred_element_type=F32)
    rank_ref[0:1, :] = jnp.sum(jnp.where(hot1, cum, 0.0), axis=0, keepdims=True).astype(I32) - 1
    rank_ref[1:2, :] = jnp.sum(jnp.where(hot2, cum, 0.0), axis=0, keepdims=True).astype(I32) - 1
    cnt_ref[0] = jnp.broadcast_to(cum[:, cum.shape[1] - 1:], cnt_ref.shape[1:])


def _router(f, router_w, router_b, n_tiles, tm):
    D = f.shape[1]
    N = n_tiles * tm
    rw = jnp.zeros((D, LANES), F32).at[:, :N_EXPERTS].set(router_w.astype(F32))
    rb = router_b.astype(F32).reshape(N_EXPERTS, 1)
    tri = jnp.asarray(np.triu(np.ones((tm, tm), np.float32)), BF16)
    lane_row = lambda rows, dt: (pl.BlockSpec((rows, tm), lambda t: (0, t)), jax.ShapeDtypeStruct((rows, N), dt))
    outs = [lane_row(2, I32), lane_row(2, F32), lane_row(2, I32),
            (pl.BlockSpec((1, N_EXPERTS, LANES), lambda t: (t, 0, 0)),
             jax.ShapeDtypeStruct((n_tiles, N_EXPERTS, LANES), F32))]
    return pl.pallas_call(
        _router_kernel,
        grid=(n_tiles,),
        in_specs=[pl.BlockSpec((tm, D), lambda t: (t, 0)), pl.BlockSpec((D, LANES), lambda t: (0, 0)),
                  pl.BlockSpec((N_EXPERTS, 1), lambda t: (0, 0)), pl.BlockSpec((tm, tm), lambda t: (0, 0))],
        out_specs=[o[0] for o in outs],
        out_shape=[o[1] for o in outs],
        compiler_params=_cparams("parallel"),
        name="moe_router",
    )(f, rw, rb, tri)


def _run_chunks(lo_ref, go_ref, len_ref, t, max_len):
    for e in range(N_EXPERTS):
        idx = t * N_EXPERTS + e
        units = len_ref[idx] // RUN_ALIGN
        lo, go = lo_ref[idx], go_ref[idx]
        done = jnp.zeros((), I32)
        bit = max_len // RUN_ALIGN
        while bit >= 1:
            rows = bit * RUN_ALIGN
            flag = (units & bit) != 0
            yield flag, pl.multiple_of(lo + done, RUN_ALIGN), pl.multiple_of(go + done, RUN_ALIGN), rows
            done = done + jnp.where(flag, rows, 0)
            bit //= 2


def _dispatch_kernel(lo_ref, go_ref, len_ref, f_ref, pos_ref, zeros_hbm, buf_hbm, srt_ref, sem):
    del zeros_hbm
    t = pl.program_id(0)
    tm = f_ref.shape[0]
    ls = srt_ref.shape[0]
    j = lax.broadcasted_iota(I32, (ls, tm), 0)
    perm = jnp.where((pos_ref[0:1, :] == j) | (pos_ref[1:2, :] == j), 1.0, 0.0).astype(BF16)
    srt_ref[...] = jnp.dot(perm, f_ref[...].astype(BF16), preferred_element_type=F32).astype(BF16)

    def chunk_copy(lo, go, rows):
        return pltpu.make_async_copy(srt_ref.at[pl.ds(lo, rows)], buf_hbm.at[pl.ds(go, rows)], sem)

    for flag, lo, go, rows in _run_chunks(lo_ref, go_ref, len_ref, t, tm):
        @pl.when(flag)
        def _():
            chunk_copy(lo, go, rows).start()
    for flag, lo, go, rows in _run_chunks(lo_ref, go_ref, len_ref, t, tm):
        @pl.when(flag)
        def _():
            chunk_copy(lo, go, rows).wait()


def _dispatch(f, pos, lo, go, rlen, n_rows, n_tiles, tm):
    D = f.shape[1]
    ls = 2 * tm + N_EXPERTS * RUN_ALIGN
    zeros = jnp.zeros((n_rows, D), BF16)
    grid_spec = pltpu.PrefetchScalarGridSpec(
        num_scalar_prefetch=3,
        grid=(n_tiles,),
        in_specs=[pl.BlockSpec((tm, D), lambda t, *_: (t, 0)), pl.BlockSpec((2, tm), lambda t, *_: (0, t)),
                  pl.BlockSpec(memory_space=pl.ANY)],
        out_specs=pl.BlockSpec(memory_space=pl.ANY),
        scratch_shapes=[pltpu.VMEM((ls, D), BF16), pltpu.SemaphoreType.DMA],
    )
    return pl.pallas_call(
        _dispatch_kernel,
        grid_spec=grid_spec,
        out_shape=jax.ShapeDtypeStruct((n_rows, D), BF16),
        input_output_aliases={5: 0},
        compiler_params=_cparams("arbitrary"),
        name="moe_dispatch",
    )(lo, go, rlen, f, pos, zeros)


def _expert_kernel(be_ref, bc_ref, x_ref, wgu_ref, wd_ref, y_ref):
    i = pl.program_id(0)

    @pl.when(bc_ref[i] > 0)
    def _():
        hu = jnp.dot(x_ref[...], wgu_ref[0], preferred_element_type=F32)
        de = hu.shape[1] // 2
        gate, up = hu[:, :de], hu[:, de:]
        hid = (gate * jax.nn.sigmoid(gate) * up).astype(BF16)
        y_ref[...] = jnp.dot(hid, wd_ref[0], preferred_element_type=F32).astype(BF16)

    @pl.when(bc_ref[i] == 0)
    def _():
        y_ref[...] = jnp.zeros_like(y_ref)


def _experts(buf, block_expert, block_count, wgu, wd):
    n_rows, D = buf.shape
    nb = n_rows // EXPERT_BLOCK
    grid_spec = pltpu.PrefetchScalarGridSpec(
        num_scalar_prefetch=2,
        grid=(nb,),
        in_specs=[pl.BlockSpec((EXPERT_BLOCK, D), lambda i, be, bc: (i, 0)),
                  pl.BlockSpec((1,) + wgu.shape[1:], lambda i, be, bc: (be[i], 0, 0)),
                  pl.BlockSpec((1,) + wd.shape[1:], lambda i, be, bc: (be[i], 0, 0))],
        out_specs=pl.BlockSpec((EXPERT_BLOCK, D), lambda i, be, bc: (i, 0)),
    )
    return pl.pallas_call(
        _expert_kernel,
        grid_spec=grid_spec,
        out_shape=jax.ShapeDtypeStruct((n_rows, D), BF16),
        compiler_params=_cparams("arbitrary"),
        name="moe_experts",
    )(block_expert, block_count, buf, wgu, wd)


def _combine_kernel(lo_ref, go_ref, len_ref, y_hbm, x_ref, pos_ref, gates_ref, gate_ref, fin_ref, xo_ref,
                    srt_ref, sem, *, final):
    t = pl.program_id(0)
    tm = x_ref.shape[0]
    ls = srt_ref.shape[0]

    @pl.when(t == 0)
    def _():
        srt_ref[...] = jnp.zeros_like(srt_ref)

    def chunk_copy(lo, go, rows):
        return pltpu.make_async_copy(y_hbm.at[pl.ds(go, rows)], srt_ref.at[pl.ds(lo, rows)], sem)

    for flag, lo, go, rows in _run_chunks(lo_ref, go_ref, len_ref, t, tm):
        @pl.when(flag)
        def _():
            chunk_copy(lo, go, rows).start()
    for flag, lo, go, rows in _run_chunks(lo_ref, go_ref, len_ref, t, tm):
        @pl.when(flag)
        def _():
            chunk_copy(lo, go, rows).wait()

    j = lax.broadcasted_iota(I32, (tm, ls), 1)
    ys = srt_ref[...]
    gw = gates_ref[...]
    y = jnp.zeros(x_ref.shape, F32)
    for k in range(2):
        pick = jnp.where(pos_ref[:, k:k + 1] == j, 1.0, 0.0).astype(BF16)
        y = y + gw[:, k:k + 1] * jnp.dot(pick, ys, preferred_element_type=F32)
    xn = x_ref[...] + gate_ref[0, 0] * y
    if final:
        xn = xn * lax.rsqrt(jnp.mean(xn * xn, axis=-1, keepdims=True) + RMS_EPS) * fin_ref[...]
    xo_ref[...] = xn


def _combine(y_buf, pos_t, gates_t, lo, go, rlen, xs, mod, final_gain, n_tiles, tm, tiles_per_seq, final):
    N, D = xs.shape
    nmod = mod.shape[1]
    ls = 2 * tm + N_EXPERTS * RUN_ALIGN
    out_rows = n_tiles * tm if final else N
    kwargs = {} if final else {"input_output_aliases": {4: 0}}
    grid_spec = pltpu.PrefetchScalarGridSpec(
        num_scalar_prefetch=3,
        grid=(n_tiles,),
        in_specs=[pl.BlockSpec(memory_space=pl.ANY),
                  pl.BlockSpec((tm, D), lambda t, *_: (t, 0)),
                  pl.BlockSpec((tm, 2), lambda t, *_: (t, 0)),
                  pl.BlockSpec((tm, 2), lambda t, *_: (t, 0)),
                  pl.BlockSpec((1, 1, 1, D), lambda t, *_: (5, jnp.minimum(t // tiles_per_seq, nmod - 1), 0, 0)),
                  pl.BlockSpec((1, D), lambda t, *_: (0, 0))],
        out_specs=pl.BlockSpec((tm, D), lambda t, *_: (t, 0)),
        scratch_shapes=[pltpu.VMEM((ls, D), BF16), pltpu.SemaphoreType.DMA],
    )
    return pl.pallas_call(
        functools.partial(_combine_kernel, final=final),
        grid_spec=grid_spec,
        out_shape=jax.ShapeDtypeStruct((out_rows, D), F32),
        compiler_params=_cparams("arbitrary"),
        name="moe_combine_final" if final else "moe_combine",
        **kwargs,
    )(lo, go, rlen, y_buf, xs, pos_t, gates_t, mod, final_gain.reshape(1, D))


def _moe(f, xs, mod, router_w, router_b, wgu, wd, final_gain, n_tok, tm, tiles_per_seq, final):
    nt = n_tok // tm
    e, gates, rank, cnt = _router(f, router_w, router_b, nt, tm)
    n = cnt[:, :, 0].astype(I32)
    rlen = (n + RUN_ALIGN - 1) // RUN_ALIGN * RUN_ALIGN
    lo = jnp.cumsum(rlen, axis=1) - rlen
    region = jnp.sum(rlen, axis=0)
    region_pad = (region + EXPERT_BLOCK - 1) // EXPERT_BLOCK * EXPERT_BLOCK
    pends = jnp.cumsum(region_pad)
    pstarts = pends - region_pad
    go = pstarts[None, :] + jnp.cumsum(rlen, axis=0) - rlen
    n_blocks = -(-(2 * n_tok + nt * N_EXPERTS * RUN_ALIGN) // EXPERT_BLOCK) + N_EXPERTS
    blk0 = jnp.arange(n_blocks, dtype=I32) * EXPERT_BLOCK
    block_expert = jnp.minimum(jnp.sum((blk0[:, None] >= pends[None, :]).astype(I32), axis=1), N_EXPERTS - 1)
    block_used = (blk0 < (pstarts + region)[block_expert]).astype(I32)
    hot = e.reshape(2, nt, tm, 1) == jnp.arange(N_EXPERTS, dtype=I32)
    pos = jnp.sum(jnp.where(hot, lo[None, :, None, :], 0), axis=-1).reshape(2, n_tok) + rank
    flat = lambda a: a.reshape(-1).astype(I32)
    buf = _dispatch(f, pos, flat(lo), flat(go), flat(rlen), n_blocks * EXPERT_BLOCK, nt, tm)
    y_buf = _experts(buf, block_expert.astype(I32), block_used, wgu, wd)
    return _combine(y_buf, pos.T, gates.T, flat(lo), flat(go), flat(rlen), xs, mod, final_gain, nt, tm,
                    tiles_per_seq, final)


def _rope_tables(S, tm):
    rows = S // GRID_W
    row = jnp.repeat(jnp.arange(rows, dtype=F32), GRID_W)
    col = jnp.tile(jnp.arange(GRID_W, dtype=F32), rows)
    half = ATTN_HEAD_DIM // 4
    inv_freq = ROPE_THETA ** (-jnp.arange(half, dtype=F32) / half)
    ang_r = row[:, None] * inv_freq[None, :]
    ang_c = col[:, None] * inv_freq[None, :]
    zeros = jnp.zeros_like(ang_r)
    cos = jnp.concatenate([jnp.cos(ang_r)] * 2 + [jnp.cos(ang_c)] * 2, axis=1)
    s1 = jnp.concatenate([-jnp.sin(ang_r), zeros, -jnp.sin(ang_c), zeros], axis=1)
    s2 = jnp.concatenate([zeros, jnp.sin(ang_r), zeros, jnp.sin(ang_c)], axis=1)
    def finish(tab, fill):
        tab = jnp.tile(tab, (1, LANES // ATTN_HEAD_DIM))
        return jnp.concatenate([tab, jnp.full((tm, LANES), fill, F32)], axis=0)
    return finish(cos, 1.0), finish(s1, 0.0), finish(s2, 0.0)


def kernel(x, c, ctx, c_ctx, ada_w, ada_b, norm_mix_g, norm_ffn_g, final_g, attn_w_qkv, attn_w_o, attn_sinks,
           gla_w_in, gla_w_a1, gla_w_a2, gla_b_a, gla_norm_g, gla_w_o, router_w, router_b,
           moe_w_gate, moe_w_up, moe_w_down):
    B, S, D = x.shape
    C = ctx.shape[1]
    depth = ada_w.shape[0]
    tm = TOKEN_TILE
    assert S % tm == 0 and (B * C) % tm == 0 and S % ATTN_BLOCK == 0 and C % ATTN_BLOCK == 0
    assert (B * S) % C == 0 and S % GLA_CHUNK == 0 and C % GLA_CHUNK == 0
    n_lat = B * S
    n_lat_tiles = n_lat // tm
    tiles_per_seq = S // tm

    rpad = -(-(B + 1) // 8) * 8
    cc = jnp.zeros((rpad, D), F32).at[:B].set(c).at[B].set(c_ctx)
    mods = _ada_table(cc, ada_w, ada_b)
    mods = mods[:, :B + 1].reshape(depth, B + 1, 6, 1, D).transpose(0, 2, 1, 3, 4)

    xs = jnp.concatenate([x.reshape(n_lat, D), ctx.reshape(B * C, D)], axis=0)
    rope = _rope_tables(S, tm)
    q_dim = ATTN_KV_HEADS * ATTN_GROUP * ATTN_HEAD_DIM
    kv_dim = ATTN_KV_HEADS * ATTN_HEAD_DIM
    kd = GLA_HEADS * GLA_KEY_DIM
    vd = GLA_HEADS * GLA_VAL_DIM

    def dup_heads(w):
        w = w.reshape(D, ATTN_KV_HEADS, 1, ATTN_HEAD_DIM)
        return jnp.broadcast_to(w, (D, ATTN_KV_HEADS, LANES // ATTN_HEAD_DIM, ATTN_HEAD_DIM)).reshape(D, -1)

    for i in range(depth):
        last = i == depth - 1
        mod = mods[i]
        j = i // 2
        if i % 2 == 0:
            wqkv = attn_w_qkv[j]
            w = jnp.concatenate([wqkv[:, :q_dim], dup_heads(wqkv[:, q_dim:q_dim + kv_dim]),
                                 dup_heads(wqkv[:, q_dim + kv_dim:])], axis=1).astype(BF16)
            q, k, v = _qkv_proj(xs, norm_mix_g[i], mod, 0, w, rope, tm, n_lat_tiles, tiles_per_seq)
            sink_col = jnp.repeat(attn_sinks[j].astype(F32).reshape(ATTN_KV_HEADS, ATTN_GROUP) * LOG2_E,
                                  ATTN_BLOCK, axis=1)
            o = _attention(q, k, v, sink_col[:, :, None], B, S, C, not last)
            xs, f = _post_mixer(o, attn_w_o[j].astype(BF16), xs, mod, norm_ffn_g[i], tm, tiles_per_seq)
        else:
            a1 = jnp.zeros((D, LANES), F32).at[:, :2 * GLA_GATE_RANK].set(
                jnp.concatenate([gla_w_a1[j, 0], gla_w_a1[j, 1]], axis=1))
            w = jnp.concatenate([gla_w_in[j], a1], axis=1).astype(BF16)
            w2 = jnp.zeros((LANES, 2 * kd), F32)
            w2 = w2.at[:GLA_GATE_RANK, :kd].set(gla_w_a2[j, 0]).at[GLA_GATE_RANK:2 * GLA_GATE_RANK, kd:].set(gla_w_a2[j, 1])
            ba = gla_b_a[j].reshape(1, 2 * kd).astype(F32)
            qk, v, og, la, chunk_tot = _gla_in_proj(xs, norm_mix_g[i], mod, 0, w, w2.astype(BF16), ba, tm,
                                                    n_lat_tiles, tiles_per_seq)
            o_fwd, o_bwd = _gla_scan(qk, v, la, chunk_tot, B, S, C)
            xs, f = _post_mixer(o_fwd, gla_w_o[j].astype(BF16), xs, mod, norm_ffn_g[i], tm, tiles_per_seq,
                                gla_extra=(o_bwd, og, gla_norm_g[j].reshape(1, GLA_VAL_DIM).astype(F32)))
        wgu = jnp.concatenate([moe_w_gate[i], moe_w_up[i]], axis=2).astype(BF16)
        wd = moe_w_down[i].astype(BF16)
        n_tok = n_lat if last else n_lat + B * C
        xs = _moe(f, xs, mod, router_w, router_b, wgu, wd, final_g, n_tok, tm, tiles_per_seq, last)
    return xs.reshape(B, S, D)
```

```python
import functools

import numpy as np
import jax
import jax.numpy as jnp
from jax import lax
from jax.experimental import pallas as pl
from jax.experimental.pallas import tpu as pltpu

F32 = jnp.float32
BF16 = jnp.bfloat16
I32 = jnp.int32

LANES = 128
VMEM_LIMIT_BYTES = 56 * 1024 * 1024

RMS_EPS = 1e-6
GRID_W = 64
ROPE_THETA = 10000.0
ATTN_HEAD_DIM = 64
ATTN_KV_HEADS = 4
ATTN_GROUP = 4
ATTN_BLOCK = 128
LOG2_E = 1.4426950408889634
ATTN_Q_SCALE = ATTN_HEAD_DIM ** -0.5 * LOG2_E
GLA_HEADS = 4
GLA_KEY_DIM = 128
GLA_VAL_DIM = 256
GLA_GATE_RANK = 16
GLA_GATE_NORM = 16.0
GLA_CHUNK = 64
GLA_BOUNDED_TOTAL = 40.0
N_EXPERTS = 16
N_GROUPS = 4
EXPERTS_PER_GROUP = 4
EXPERT_BLOCK = 512
RUN_ALIGN = 16
TOKEN_TILE = 512


def _cparams(*sem):
    return pltpu.CompilerParams(dimension_semantics=sem, vmem_limit_bytes=VMEM_LIMIT_BYTES)


def _norm_mod(x, gain, shift, scale):
    h = x * lax.rsqrt(jnp.mean(x * x, axis=-1, keepdims=True) + RMS_EPS) * gain
    return h * (1.0 + scale) + shift


def _ada_kernel(c_ref, w_ref, b_ref, o_ref):
    c = c_ref[...]
    s = (c * jax.nn.sigmoid(c)).astype(BF16)
    o_ref[0] = jnp.dot(s, w_ref[0].astype(BF16), preferred_element_type=F32) + b_ref[0]


def _ada_table(cc, ada_w, ada_b):
    L, D, D6 = ada_w.shape
    R = cc.shape[0]
    tn = 1536
    return pl.pallas_call(
        _ada_kernel,
        grid=(L, D6 // tn),
        in_specs=[pl.BlockSpec((R, D), lambda l, j: (0, 0)),
                  pl.BlockSpec((1, D, tn), lambda l, j: (l, 0, j)),
                  pl.BlockSpec((1, 1, tn), lambda l, j: (l, 0, j))],
        out_specs=pl.BlockSpec((1, R, tn), lambda l, j: (l, 0, j)),
        out_shape=jax.ShapeDtypeStruct((L, R, D6), F32),
        compiler_params=_cparams("parallel", "parallel"),
        name="ada_table",
    )(cc, ada_w, ada_b.reshape(L, 1, D6))


def _qkv_kernel(x_ref, g_ref, sh_ref, sc_ref, w_ref, cos_ref, s1_ref, s2_ref, q_ref, k_ref, v_ref):
    h = _norm_mod(x_ref[...], g_ref[...], sh_ref[0, 0], sc_ref[0, 0])
    z = jnp.dot(h.astype(BF16), w_ref[...], preferred_element_type=F32)
    cos, s1, s2 = cos_ref[...], s1_ref[...], s2_ref[...]
    nq = q_ref.shape[1] // LANES
    nk = k_ref.shape[1] // LANES
    for j in range(nq + nk):
        zc = z[:, j * LANES:(j + 1) * LANES]
        r = zc * cos + pltpu.roll(zc, LANES - 16, 1) * s1 + pltpu.roll(zc, 16, 1) * s2
        if j < nq:
            q_ref[:, j * LANES:(j + 1) * LANES] = (r * ATTN_Q_SCALE).astype(BF16)
        else:
            k_ref[:, (j - nq) * LANES:(j - nq + 1) * LANES] = r.astype(BF16)
    v_ref[...] = z[:, (nq + nk) * LANES:].astype(BF16)


def _gla_in_kernel(x_ref, g_ref, sh_ref, sc_ref, w_ref, w2_ref, ba_ref, qk_ref, v_ref, og_ref, la_ref, tot_ref):
    h = _norm_mod(x_ref[...], g_ref[...], sh_ref[0, 0], sc_ref[0, 0])
    z = jnp.dot(h.astype(BF16), w_ref[...], preferred_element_type=F32)
    kd = GLA_HEADS * GLA_KEY_DIM
    vd = GLA_HEADS * GLA_VAL_DIM
    qk_ref[:, :kd] = (z[:, :kd] * (GLA_KEY_DIM ** -0.5)).astype(BF16)
    qk_ref[:, kd:] = z[:, kd:2 * kd].astype(BF16)
    v_ref[...] = z[:, 2 * kd:2 * kd + vd].astype(BF16)
    og_ref[...] = z[:, 2 * kd + vd:2 * kd + 2 * vd].astype(BF16)
    a1 = z[:, 2 * kd + 2 * vd:].astype(BF16)
    pre = jnp.dot(a1, w2_ref[...], preferred_element_type=F32) + ba_ref[...]
    la = (jnp.minimum(pre, 0.0) - jnp.log1p(jnp.exp(-jnp.abs(pre)))) * (1.0 / GLA_GATE_NORM)
    la_ref[...] = la
    nc = la.shape[0] // GLA_CHUNK
    tot = jnp.sum(la.reshape(nc, GLA_CHUNK, la.shape[1]), axis=1)
    lane = lax.broadcasted_iota(I32, (nc, LANES), 1)
    acc = jnp.zeros((nc, LANES), F32)
    for hd in range(2 * GLA_HEADS):
        worst = jnp.min(tot[:, hd * GLA_KEY_DIM:(hd + 1) * GLA_KEY_DIM], axis=1, keepdims=True)
        acc = jnp.where(lane == hd, worst, acc)
    tot_ref[...] = acc


def _tile_specs(D, tm, n_lat_tiles, tiles_per_seq, n_mod_rows):
    def mod_idx(which):
        return lambda t: (which, jnp.minimum(t // tiles_per_seq, n_mod_rows - 1), 0, 0)
    return mod_idx, [pl.BlockSpec((tm, D), lambda t: (t, 0)),
                     pl.BlockSpec((1, D), lambda t: (0, 0))]


def _qkv_proj(xs, gain, mod, which, w, rope, tm, n_lat_tiles, tiles_per_seq):
    N, D = xs.shape
    nmod = mod.shape[1]
    mod_idx, specs = _tile_specs(D, tm, n_lat_tiles, tiles_per_seq, nmod)
    qd = ATTN_KV_HEADS * ATTN_GROUP * ATTN_HEAD_DIM
    kd = ATTN_KV_HEADS * LANES
    rope_idx = lambda t: (jnp.where(t < n_lat_tiles, t % tiles_per_seq, tiles_per_seq), 0)
    in_specs = specs + [
        pl.BlockSpec((1, 1, 1, D), mod_idx(which)),
        pl.BlockSpec((1, 1, 1, D), mod_idx(which + 1)),
        pl.BlockSpec(w.shape, lambda t: (0, 0)),
        pl.BlockSpec((tm, LANES), rope_idx),
        pl.BlockSpec((tm, LANES), rope_idx),
        pl.BlockSpec((tm, LANES), rope_idx),
    ]
    return pl.pallas_call(
        _qkv_kernel,
        grid=(N // tm,),
        in_specs=in_specs,
        out_specs=[pl.BlockSpec((tm, qd), lambda t: (t, 0)),
                   pl.BlockSpec((tm, kd), lambda t: (t, 0)),
                   pl.BlockSpec((tm, kd), lambda t: (t, 0))],
        out_shape=[jax.ShapeDtypeStruct((N, qd), BF16),
                   jax.ShapeDtypeStruct((N, kd), BF16),
                   jax.ShapeDtypeStruct((N, kd), BF16)],
        compiler_params=_cparams("parallel"),
        name="attn_qkv_proj",
    )(xs, gain.reshape(1, D), mod, mod, w, *rope)


def _gla_in_proj(xs, gain, mod, which, w, w2, ba, tm, n_lat_tiles, tiles_per_seq):
    N, D = xs.shape
    nmod = mod.shape[1]
    mod_idx, specs = _tile_specs(D, tm, n_lat_tiles, tiles_per_seq, nmod)
    kd = GLA_HEADS * GLA_KEY_DIM
    vd = GLA_HEADS * GLA_VAL_DIM
    in_specs = specs + [
        pl.BlockSpec((1, 1, 1, D), mod_idx(which)),
        pl.BlockSpec((1, 1, 1, D), mod_idx(which + 1)),
        pl.BlockSpec(w.shape, lambda t: (0, 0)),
        pl.BlockSpec(w2.shape, lambda t: (0, 0)),
        pl.BlockSpec((1, 2 * kd), lambda t: (0, 0)),
    ]
    row = lambda width: pl.BlockSpec((tm, width), lambda t: (t, 0))
    return pl.pallas_call(
        _gla_in_kernel,
        grid=(N // tm,),
        in_specs=in_specs,
        out_specs=[row(2 * kd), row(vd), row(vd), row(2 * kd),
                   pl.BlockSpec((tm // GLA_CHUNK, LANES), lambda t: (t, 0))],
        out_shape=[jax.ShapeDtypeStruct((N, 2 * kd), BF16),
                   jax.ShapeDtypeStruct((N, vd), BF16),
                   jax.ShapeDtypeStruct((N, vd), BF16),
                   jax.ShapeDtypeStruct((N, 2 * kd), F32),
                   jax.ShapeDtypeStruct((N // GLA_CHUNK, LANES), F32)],
        compiler_params=_cparams("parallel"),
        name="gla_in_proj",
    )(xs, gain.reshape(1, D), mod, mod, w, w2, ba)


def _attn_kernel(*refs, window, nq):
    if window:
        q_ref, kp, kc, kn, kx, vp, vc, vn, vx, sink_ref, tri_ref, o_ref = refs
        k_parts, v_parts = (kp, kc, kn, kx), (vp, vc, vn, vx)
    else:
        q_ref, kx, vx, sink_ref, _, o_ref = refs
        k_parts, v_parts = (kx,), (vx,)
    tq = q_ref.shape[0]
    lane = lax.broadcasted_iota(I32, (tq, LANES), 1)
    first_head = lane < ATTN_HEAD_DIM
    if window:
        j = pl.program_id(1)
        bias_prev = jnp.where(j > 0, tri_ref[0], -jnp.inf)
        bias_next = jnp.where(j < nq - 1, tri_ref[1], -jnp.inf)
        bias_prev = jnp.concatenate([bias_prev] * ATTN_GROUP, axis=0)
        bias_next = jnp.concatenate([bias_next] * ATTN_GROUP, axis=0)
    def scores(kh):
        ks = slice(kh * LANES, (kh + 1) * LANES)
        kk = jnp.concatenate([p[:, ks] for p in k_parts], axis=0) if window else kx[:, ks]
        qa = q_ref[:, (2 * kh) * LANES:(2 * kh + 1) * LANES]
        qb = q_ref[:, (2 * kh + 1) * LANES:(2 * kh + 2) * LANES]
        zero = jnp.zeros_like(qa)
        qs = jnp.concatenate([jnp.where(first_head, qa, zero), jnp.where(first_head, zero, qa),
                              jnp.where(first_head, qb, zero), jnp.where(first_head, zero, qb)], axis=0)
        return lax.dot_general(qs, kk, (((1,), (1,)), ((), ())), preferred_element_type=F32)

    def softmax(s, kh):
        if window:
            s = jnp.concatenate([s[:, :tq] + bias_prev, s[:, tq:2 * tq], s[:, 2 * tq:3 * tq] + bias_next,
                                 s[:, 3 * tq:]], axis=1)
        sink = sink_ref[kh]
        m = jnp.maximum(jnp.max(s, axis=1, keepdims=True), sink)
        p = jnp.exp2(s - m)
        l = jnp.sum(p, axis=1, keepdims=True) + jnp.exp2(sink - m)
        return p.astype(BF16), l

    def values(p, l, kh):
        ks = slice(kh * LANES, (kh + 1) * LANES)
        vv = jnp.concatenate([part[:, ks] for part in v_parts], axis=0) if window else vx[:, ks]
        o = jnp.dot(p, vv, preferred_element_type=F32) / l
        oa = jnp.where(first_head, o[0:tq], o[tq:2 * tq])
        ob = jnp.where(first_head, o[2 * tq:3 * tq], o[3 * tq:4 * tq])
        o_ref[:, (2 * kh) * LANES:(2 * kh + 1) * LANES] = oa.astype(BF16)
        o_ref[:, (2 * kh + 1) * LANES:(2 * kh + 2) * LANES] = ob.astype(BF16)

    s_next = scores(0)
    for kh in range(ATTN_KV_HEADS):
        s_cur = s_next
        if kh + 1 < ATTN_KV_HEADS:
            s_next = scores(kh + 1)
        p, l = softmax(s_cur, kh)
        values(p, l, kh)


def _attention(q, k, v, sink_col, B, S, C, need_ctx):
    N, qd = q.shape
    kd = k.shape[1]
    tq = ATTN_BLOCK
    nq = S // tq
    ctx_blk0 = (B * S) // C
    qmap = lambda b, j: (b * nq + j, 0)
    prev = lambda b, j: (b * nq + jnp.maximum(j - 1, 0), 0)
    nxt = lambda b, j: (b * nq + jnp.minimum(j + 1, nq - 1), 0)
    cmap = lambda b, j: (ctx_blk0 + b, 0)
    kblk = lambda m: pl.BlockSpec((tq, kd), m)
    cblk = pl.BlockSpec((C, kd), cmap)
    sink_spec = pl.BlockSpec(sink_col.shape, lambda b, j: (0, 0, 0))
    r = np.arange(tq)[:, None]
    c = np.arange(tq)[None, :]
    tri = jnp.asarray(np.stack([np.where(c >= r, 0.0, -np.inf), np.where(c <= r, 0.0, -np.inf)]), F32)
    o_lat = pl.pallas_call(
        functools.partial(_attn_kernel, window=True, nq=nq),
        grid=(B, nq),
        in_specs=[pl.BlockSpec((tq, qd), qmap), kblk(prev), kblk(qmap), kblk(nxt), cblk,
                  kblk(prev), kblk(qmap), kblk(nxt), cblk, sink_spec,
                  pl.BlockSpec(tri.shape, lambda b, j: (0, 0, 0))],
        out_specs=pl.BlockSpec((tq, qd), qmap),
        out_shape=jax.ShapeDtypeStruct((N, qd), BF16),
        compiler_params=_cparams("parallel", "parallel"),
        name="attn_window",
    )(q, k, k, k, k, v, v, v, v, sink_col, tri)
    if not need_ctx:
        return o_lat
    ncq = C // tq
    lat_blks = (B * S) // tq
    cq = lambda b, j: (lat_blks + b * ncq + j, 0)
    return pl.pallas_call(
        functools.partial(_attn_kernel, window=False, nq=ncq),
        grid=(B, ncq),
        in_specs=[pl.BlockSpec((tq, qd), cq), cblk, cblk, sink_spec,
                  pl.BlockSpec(memory_space=pl.ANY)],
        out_specs=pl.BlockSpec((tq, qd), cq),
        out_shape=jax.ShapeDtypeStruct((N, qd), BF16),
        input_output_aliases={4: 0},
        compiler_params=_cparams("parallel", "parallel"),
        name="attn_context",
    )(q, k, v, sink_col, o_lat)


def _gla_constants(C):
    levels = []
    m = 1
    while m < C:
        levels.append(m)
        m *= 2
    t = np.arange(C)[:, None]
    u = np.arange(C)[None, :]
    secs = [(u <= t), (u > t)]
    masks = []
    for m in levels:
        base = (t // (2 * m)) * (2 * m)
        ref = base + m - 1
        second = t >= base + m
        secs.append(np.where(second, (u > ref) & (u <= t), (u > t) & (u <= ref)))
        masks.append((t // (2 * m) == u // (2 * m)) & second & (u < (u // (2 * m)) * (2 * m) + m))
    masks.append(t == u)
    masks.append(u <= t)
    mf = np.concatenate([s.astype(np.float32) for s in secs], axis=0)
    kf = np.stack([mk.astype(np.float32) for mk in masks], axis=0)
    mb = np.concatenate([s.astype(np.float32)[::-1, ::-1] for s in secs], axis=0)
    kb = np.stack([mk.astype(np.float32)[::-1, ::-1] for mk in masks], axis=0)
    return np.stack([mf, mb]), np.stack([kf, kb]), len(levels)


def _gla_chunk(q, k, v, g, st_ref, mm_ref, mk_ref, d, nl, bounded):
    C, dk = q.shape
    nt = (((1,), (1,)), ((), ()))
    g_hi = g.astype(BF16)
    g_lo = (g - g_hi.astype(F32)).astype(BF16)
    mmat = mm_ref[d, :2 * C] if bounded else mm_ref[d]
    e2 = jnp.dot(mmat, jnp.concatenate([g_hi, g_lo], axis=1), preferred_element_type=F32)
    ee = e2[:, :dk] + e2[:, dk:]
    ex = jnp.exp(ee)
    qf, kf = q.astype(F32), k.astype(F32)
    qe = (qf * ex[:C]).astype(BF16)
    ke = (kf * ex[C:2 * C]).astype(BF16)
    st = st_ref[...]
    o = lax.dot_general(qe, st.astype(BF16), nt, preferred_element_type=F32)
    if bounded:
        ki = (kf * jnp.exp(-ee[:C])).astype(BF16)
        a = mk_ref[d, nl + 1] * lax.dot_general(qe, ki, nt, preferred_element_type=F32)
    else:
        a = mk_ref[d, nl] * lax.dot_general(q, k, nt, preferred_element_type=F32)
        for i in range(nl):
            xl = ex[(2 + i) * C:(3 + i) * C]
            ql = (qf * xl).astype(BF16)
            kl = (kf * xl).astype(BF16)
            a = a + mk_ref[d, i] * lax.dot_general(ql, kl, nt, preferred_element_type=F32)
    o = o + jnp.dot(a.astype(BF16), v, preferred_element_type=F32)
    decay = jnp.exp(jnp.sum(g, axis=0, keepdims=True))
    st_ref[...] = st * decay + lax.dot_general(v, ke, (((0,), (0,)), ((), ())), preferred_element_type=F32)
    return o


def _gla_kernel(okf_ref, okb_ref, qf, kf, vf, lf, qb, kb, vb, lb, mm_ref, mk_ref, of_ref, ob_ref, st_ref,
                *, nl, seg_chunks, lat_segs, ctx_seg0):
    C = GLA_CHUNK
    H, DK, DV = GLA_HEADS, GLA_KEY_DIM, GLA_VAL_DIM
    b, j = pl.program_id(0), pl.program_id(1)

    @pl.when(j == 0)
    def _():
        st_ref[...] = jnp.zeros_like(st_ref)

    seg_f = jnp.where(j == 0, ctx_seg0 + b, b * lat_segs + j - 1)
    seg_b = jnp.where(j == 0, ctx_seg0 + b, b * lat_segs + lat_segs - j)

    def body(i, carry):
        cf, cb = i, seg_chunks - 1 - i
        rf = pl.ds(pl.multiple_of(cf * C, C), C)
        rb = pl.ds(pl.multiple_of(cb * C, C), C)
        bounded = (okf_ref[seg_f * seg_chunks + cf] != 0) & (okb_ref[seg_b * seg_chunks + cb] != 0)

        def advance_general():
            for h in range(H):
                ks, vs = slice(h * DK, (h + 1) * DK), slice(h * DV, (h + 1) * DV)
                o = _gla_chunk(qf[rf, ks], kf[rf, ks], vf[rf, vs], lf[rf, ks], st_ref.at[h],
                               mm_ref, mk_ref, 0, nl, False)
                of_ref[rf, vs] = o.astype(BF16)
                o = _gla_chunk(qb[rb, ks], kb[rb, ks], vb[rb, vs], lb[rb, ks], st_ref.at[H + h],
                               mm_ref, mk_ref, 1, nl, False)
                ob_ref[rb, vs] = o.astype(BF16)

        def advance_bounded():
            nt = (((1,), (1,)), ((), ()))
            kw = H * DK
            sides = ((qf, kf, vf, lf, rf, of_ref, 0), (qb, kb, vb, lb, rb, ob_ref, 1))
            pre = []
            for q_r, k_r, v_r, l_r, rows, o_r, d in sides:
                g = l_r[rows, :]
                g_hi = g.astype(BF16)
                g_lo = (g - g_hi.astype(F32)).astype(BF16)
                e2 = jnp.dot(mm_ref[d, :2 * C], jnp.concatenate([g_hi, g_lo], axis=1),
                             preferred_element_type=F32)
                pre.append((e2[:, :kw] + e2[:, kw:], g))
            units = []
            for (ee, g), (q_r, k_r, v_r, l_r, rows, o_r, d) in zip(pre, sides):
                ex = jnp.exp(ee)
                kf32 = k_r[rows, :].astype(F32)
                qe = (q_r[rows, :].astype(F32) * ex[:C]).astype(BF16)
                ke = (kf32 * ex[C:]).astype(BF16)
                ki = (kf32 * jnp.exp(-ee[:C])).astype(BF16)
                decay = jnp.exp(jnp.sum(g, axis=0, keepdims=True))
                for h in range(H):
                    ks = slice(h * DK, (h + 1) * DK)
                    units.append((d, h, qe[:, ks], ke[:, ks], ki[:, ks], decay[:, ks], v_r, rows, o_r))
            inter, score = [], []
            for d, h, qe, ke, ki, decay, v_r, rows, o_r in units:
                st = st_ref[d * H + h]
                inter.append(lax.dot_general(qe, st.astype(BF16), nt, preferred_element_type=F32))
                score.append(lax.dot_general(qe, ki, nt, preferred_element_type=F32))
            for n, (d, h, qe, ke, ki, decay, v_r, rows, o_r) in enumerate(units):
                vs = slice(h * DV, (h + 1) * DV)
                v = v_r[rows, vs]
                a = (mk_ref[d, nl + 1] * score[n]).astype(BF16)
                o = inter[n] + jnp.dot(a, v, preferred_element_type=F32)
                upd = lax.dot_general(v, ke, (((0,), (0,)), ((), ())), preferred_element_type=F32)
                o_r[rows, vs] = o.astype(BF16)
                st_ref[d * H + h] = st_ref[d * H + h] * decay + upd

        @pl.when(bounded)
        def _():
            advance_bounded()

        @pl.when(jnp.logical_not(bounded))
        def _():
            advance_general()

        return carry

    lax.fori_loop(0, seg_chunks, body, 0)


def _gla_scan(qk, v, la, chunk_tot, B, S, C):
    N = qk.shape[0]
    H, DK, DV = GLA_HEADS, GLA_KEY_DIM, GLA_VAL_DIM
    seg = C
    assert S % seg == 0 and seg % GLA_CHUNK == 0
    lat_segs = S // seg
    ctx_seg0 = (B * S) // seg
    mm, mk, nl = _gla_constants(GLA_CHUNK)
    mm = jnp.asarray(mm, BF16)
    mk = jnp.asarray(mk, F32)
    ok = chunk_tot[:, :2 * H] >= -GLA_BOUNDED_TOTAL
    okf = jnp.all(ok[:, :H], axis=1).astype(I32)
    okb = jnp.all(ok[:, H:], axis=1).astype(I32)
    fwd = lambda col: (lambda b, j, *_: (jnp.where(j == 0, ctx_seg0 + b, b * lat_segs + j - 1), col))
    bwd = lambda col: (lambda b, j, *_: (jnp.where(j == 0, ctx_seg0 + b, b * lat_segs + lat_segs - j), col))
    kw, vw = H * DK, H * DV
    in_specs = [
        pl.BlockSpec((seg, kw), fwd(0)), pl.BlockSpec((seg, kw), fwd(1)), pl.BlockSpec((seg, vw), fwd(0)),
        pl.BlockSpec((seg, kw), fwd(0)),
        pl.BlockSpec((seg, kw), bwd(0)), pl.BlockSpec((seg, kw), bwd(1)), pl.BlockSpec((seg, vw), bwd(0)),
        pl.BlockSpec((seg, kw), bwd(1)),
        pl.BlockSpec(mm.shape, lambda b, j, *_: (0, 0, 0)),
        pl.BlockSpec(mk.shape, lambda b, j, *_: (0, 0, 0, 0)),
    ]
    grid_spec = pltpu.PrefetchScalarGridSpec(
        num_scalar_prefetch=2,
        grid=(B, lat_segs + 1),
        in_specs=in_specs,
        out_specs=[pl.BlockSpec((seg, vw), fwd(0)), pl.BlockSpec((seg, vw), bwd(0))],
        scratch_shapes=[pltpu.VMEM((2 * H, DV, DK), F32)],
    )
    return pl.pallas_call(
        functools.partial(_gla_kernel, nl=nl, seg_chunks=seg // GLA_CHUNK, lat_segs=lat_segs, ctx_seg0=ctx_seg0),
        grid_spec=grid_spec,
        out_shape=[jax.ShapeDtypeStruct((N, vw), BF16), jax.ShapeDtypeStruct((N, vw), BF16)],
        compiler_params=_cparams("parallel", "arbitrary"),
        name="gla_scan",
    )(okf, okb, qk, qk, v, la, qk, qk, v, la, mm, mk)


def _post_kernel(*refs, gla):
    route_refs = refs[-4:]
    route_in = refs[-9:-6]
    refs = refs[:-9] + refs[-6:-4]
    if gla:
        o_ref, ob_ref, og_ref, ng_ref, w_ref, x_ref, gate_ref, fg_ref, fsh_ref, fsc_ref, xo_ref, f_ref = refs
        o = o_ref[...].astype(F32) + ob_ref[...].astype(F32)
        g = og_ref[...].astype(F32)
        parts = []
        for h in range(GLA_HEADS):
            oh = o[:, h * GLA_VAL_DIM:(h + 1) * GLA_VAL_DIM]
            parts.append(oh * lax.rsqrt(jnp.mean(oh * oh, axis=-1, keepdims=True) + RMS_EPS) * ng_ref[...])
        mix = (jnp.concatenate(parts, axis=1) * (g * jax.nn.sigmoid(g))).astype(BF16)
    else:
        o_ref, w_ref, x_ref, gate_ref, fg_ref, fsh_ref, fsc_ref, xo_ref, f_ref = refs
        mix = o_ref[...]
    y = jnp.dot(mix, w_ref[...], preferred_element_type=F32)
    xn = x_ref[...] + gate_ref[0, 0] * y
    xo_ref[...] = xn
    f = _norm_mod(xn, fg_ref[...], fsh_ref[0, 0], fsc_ref[0, 0])
    f_ref[...] = f.astype(BF16)
    _route_tile(f, *route_in, *route_refs)


def _post_mixer(o, w_o, xs, mod, ffn_gain, router_w, router_b, tm, tiles_per_seq, gla_extra=None):
    N, D = xs.shape
    r_args, r_in, r_out, r_shapes = _route_io(router_w, router_b, N, D, tm)
    nmod = mod.shape[1]
    mod_idx = lambda which: (lambda t: (which, jnp.minimum(t // tiles_per_seq, nmod - 1), 0, 0))
    row = lambda width: pl.BlockSpec((tm, width), lambda t: (t, 0))
    const = lambda a: pl.BlockSpec(a.shape, lambda t: (0,) * a.ndim)
    args, specs = [o], [row(o.shape[1])]
    if gla_extra is not None:
        o_bwd, og, ng = gla_extra
        args += [o_bwd, og, ng]
        specs += [row(o_bwd.shape[1]), row(og.shape[1]), const(ng)]
    x_index = len(args) + 1
    args += [w_o, xs, mod, ffn_gain.reshape(1, D), mod, mod]
    specs += [const(w_o), row(D), pl.BlockSpec((1, 1, 1, D), mod_idx(2)), pl.BlockSpec((1, D), lambda t: (0, 0)),
              pl.BlockSpec((1, 1, 1, D), mod_idx(3)), pl.BlockSpec((1, 1, 1, D), mod_idx(4))]
    args += r_args
    specs += r_in
    return pl.pallas_call(
        functools.partial(_post_kernel, gla=gla_extra is not None),
        grid=(N // tm,),
        in_specs=specs,
        out_specs=[row(D), row(D)] + r_out,
        out_shape=[jax.ShapeDtypeStruct((N, D), F32), jax.ShapeDtypeStruct((N, D), BF16)] + r_shapes,
        input_output_aliases={x_index: 0},
        compiler_params=_cparams("parallel"),
        name="post_mixer_gla" if gla_extra is not None else "post_mixer_attn",
    )(*args)


def _route_tile(f, rw_ref, rb_ref, tri_ref, e_ref, gate_ref, rank_ref, cnt_ref):
    fh = f.astype(BF16)
    fl = (f - fh.astype(F32)).astype(BF16)
    rw = rw_ref[...]
    wh = rw.astype(BF16)
    wl = (rw - wh.astype(F32)).astype(BF16)
    logits = (jnp.dot(fh, wh, preferred_element_type=F32) + jnp.dot(fh, wl, preferred_element_type=F32)
              + jnp.dot(fl, wh, preferred_element_type=F32))
    lt = logits.T[:N_EXPERTS]
    scores = jax.nn.sigmoid(lt)
    sel = scores + rb_ref[...]
    srow = [sel[e:e + 1] for e in range(N_EXPERTS)]
    prow = [scores[e:e + 1] for e in range(N_EXPERTS)]
    gscore = []
    for g in range(N_GROUPS):
        a, b, c, d = srow[4 * g:4 * g + 4]
        hi1, lo1, hi2, lo2 = jnp.maximum(a, b), jnp.minimum(a, b), jnp.maximum(c, d), jnp.minimum(c, d)
        gscore.append(jnp.maximum(hi1, hi2) + jnp.maximum(jnp.minimum(hi1, hi2), jnp.maximum(lo1, lo2)))
    best, grp = gscore[0], jnp.zeros_like(gscore[0], dtype=I32)
    for g in range(1, N_GROUPS):
        better = gscore[g] > best
        grp = jnp.where(better, g, grp)
        best = jnp.where(better, gscore[g], best)
    s_in, p_in = [], []
    for k in range(EXPERTS_PER_GROUP):
        sv, pv = srow[k], prow[k]
        for g in range(1, N_GROUPS):
            sv = jnp.where(grp == g, srow[4 * g + k], sv)
            pv = jnp.where(grp == g, prow[4 * g + k], pv)
        s_in.append(sv)
        p_in.append(pv)
    i1, v1, g1 = jnp.zeros_like(grp), s_in[0], p_in[0]
    for k in range(1, EXPERTS_PER_GROUP):
        better = s_in[k] > v1
        i1 = jnp.where(better, k, i1)
        g1 = jnp.where(better, p_in[k], g1)
        v1 = jnp.where(better, s_in[k], v1)
    i2, v2, g2 = jnp.zeros_like(grp), jnp.full_like(v1, -jnp.inf), jnp.zeros_like(v1)
    for k in range(EXPERTS_PER_GROUP):
        better = (i1 != k) & (s_in[k] > v2)
        i2 = jnp.where(better, k, i2)
        g2 = jnp.where(better, p_in[k], g2)
        v2 = jnp.where(better, s_in[k], v2)
    e1 = grp * EXPERTS_PER_GROUP + i1
    e2 = grp * EXPERTS_PER_GROUP + i2
    tot = g1 + g2
    e_ref[0:1, :] = e1
    e_ref[1:2, :] = e2
    gate_ref[0:1, :] = g1 / tot
    gate_ref[1:2, :] = g2 / tot
    eid = lax.broadcasted_iota(I32, scores.shape, 0)
    hot1 = eid == e1
    hot2 = eid == e2
    onehot = jnp.where(hot1 | hot2, 1.0, 0.0).astype(BF16)
    cum = jnp.dot(onehot, tri_ref[...], preferred_element_type=F32)
    rank_ref[0:1, :] = jnp.sum(jnp.where(hot1, cum, 0.0), axis=0, keepdims=True).astype(I32) - 1
    rank_ref[1:2, :] = jnp.sum(jnp.where(hot2, cum, 0.0), axis=0, keepdims=True).astype(I32) - 1
    cnt_ref[0] = jnp.broadcast_to(cum[:, cum.shape[1] - 1:], cnt_ref.shape[1:])


def _route_io(router_w, router_b, N, D, tm):
    rw = jnp.zeros((D, LANES), F32).at[:, :N_EXPERTS].set(router_w.astype(F32))
    rb = router_b.astype(F32).reshape(N_EXPERTS, 1)
    tri = jnp.asarray(np.triu(np.ones((tm, tm), np.float32)), BF16)
    in_specs = [pl.BlockSpec((D, LANES), lambda t: (0, 0)), pl.BlockSpec((N_EXPERTS, 1), lambda t: (0, 0)),
                pl.BlockSpec((tm, tm), lambda t: (0, 0))]
    lane_row = lambda dt: (pl.BlockSpec((2, tm), lambda t: (0, t)), jax.ShapeDtypeStruct((2, N), dt))
    outs = [lane_row(I32), lane_row(F32), lane_row(I32),
            (pl.BlockSpec((1, N_EXPERTS, LANES), lambda t: (t, 0, 0)),
             jax.ShapeDtypeStruct((N // tm, N_EXPERTS, LANES), F32))]
    return [rw, rb, tri], in_specs, [o[0] for o in outs], [o[1] for o in outs]


def _chunk_tables(lo, go, rlen, tm):
    n_cls = (tm // RUN_ALIGN).bit_length()
    units = rlen // RUN_ALIGN
    cls = jnp.arange(n_cls, dtype=I32)
    flag = (units[:, :, None] >> cls) & 1
    rows = flag * (RUN_ALIGN << cls)
    above = jnp.cumsum(rows[..., ::-1], axis=-1)[..., ::-1] - rows
    src = lo[:, :, None] + above
    dst = go[:, :, None] + above
    slot = jnp.cumsum(flag, axis=1) - 1
    hit = (flag[:, None] == 1) & (slot[:, None] == jnp.arange(N_EXPERTS, dtype=I32)[None, :, None, None])
    compact = lambda a: jnp.sum(jnp.where(hit, a[:, None], 0), axis=2).transpose(0, 2, 1)
    flat = lambda a: a.reshape(-1).astype(I32)
    return (flat(compact(src)), flat(compact(dst)), flat(jnp.sum(flag, axis=1)), flat(jnp.sum(units, axis=1))), n_cls


def _start_pieces(tabs, tile, n_cls, make_copy):
    src_ref, dst_ref, cnt_ref, _ = tabs
    for b in range(n_cls):
        base = (tile * n_cls + b) * N_EXPERTS

        def body(i, carry, base=base, rows=RUN_ALIGN << b):
            make_copy(pl.multiple_of(src_ref[base + i], RUN_ALIGN), pl.multiple_of(dst_ref[base + i], RUN_ALIGN),
                      rows).start()
            return carry

        lax.fori_loop(0, cnt_ref[tile * n_cls + b], body, 0)


def _await_pieces(tabs, tile, max_rows, make_copy):
    units = tabs[3][tile]
    for b in range((max_rows // RUN_ALIGN).bit_length()):
        @pl.when(((units >> b) & 1) != 0)
        def _(rows=RUN_ALIGN << b):
            make_copy(0, 0, rows).wait()


def _dispatch_kernel(src_tab, dst_tab, cnt_tab, tot_tab, f_ref, pos_ref, gates_ref, zeros_hbm, buf_hbm, srt_ref,
                     sems, *, n_tiles, n_cls):
    del zeros_hbm
    t = pl.program_id(0)
    slot = t % 2
    tm, D = f_ref.shape
    ls = srt_ref.shape[1]
    j = lax.broadcasted_iota(I32, (ls, tm), 0)
    lane = lax.broadcasted_iota(I32, (tm, LANES), 1)
    perms, gcols = [], jnp.zeros((ls, LANES), F32)
    for k in range(2):
        perm = jnp.where(pos_ref[k:k + 1, :] == j, 1.0, 0.0).astype(BF16)
        g = gates_ref[:, k:k + 1]
        g_hi = g.astype(BF16).astype(F32)
        g_lo = (g - g_hi).astype(BF16).astype(F32)
        pieces = jnp.where(lane == 0, g_hi, jnp.where(lane == 1, g_lo, 0.0)).astype(BF16)
        gcols = gcols + jnp.dot(perm, pieces, preferred_element_type=F32)
        perms.append(perm)
    srt_ref[slot, :, :D] = jnp.dot(perms[0] + perms[1], f_ref[...], preferred_element_type=F32).astype(BF16)
    srt_ref[slot, :, D:] = gcols.astype(BF16)

    tabs = (src_tab, dst_tab, cnt_tab, tot_tab)

    def copier(buf_slot):
        return lambda lo, go, rows: pltpu.make_async_copy(
            srt_ref.at[buf_slot, pl.ds(lo, rows)], buf_hbm.at[pl.ds(go, rows)], sems.at[buf_slot])

    _start_pieces(tabs, t, n_cls, copier(slot))

    @pl.when(t > 0)
    def _():
        _await_pieces(tabs, t - 1, ls, copier(1 - slot))

    @pl.when(t == n_tiles - 1)
    def _():
        _await_pieces(tabs, t, ls, copier(slot))


def _dispatch(f, pos, gates_t, tabs, n_cls, n_rows, n_tiles, tm):
    D = f.shape[1]
    ls = 2 * tm + N_EXPERTS * RUN_ALIGN
    width = D + LANES
    zeros = jnp.zeros((n_rows, width), BF16)
    grid_spec = pltpu.PrefetchScalarGridSpec(
        num_scalar_prefetch=4,
        grid=(n_tiles,),
        in_specs=[pl.BlockSpec((tm, D), lambda t, *_: (t, 0)), pl.BlockSpec((2, tm), lambda t, *_: (0, t)),
                  pl.BlockSpec((tm, 2), lambda t, *_: (t, 0)), pl.BlockSpec(memory_space=pl.ANY)],
        out_specs=pl.BlockSpec(memory_space=pl.ANY),
        scratch_shapes=[pltpu.VMEM((2, ls, width), BF16), pltpu.SemaphoreType.DMA((2,))],
    )
    return pl.pallas_call(
        functools.partial(_dispatch_kernel, n_tiles=n_tiles, n_cls=n_cls),
        grid_spec=grid_spec,
        out_shape=jax.ShapeDtypeStruct((n_rows, width), BF16),
        input_output_aliases={7: 0},
        compiler_params=_cparams("arbitrary"),
        name="moe_dispatch",
    )(*tabs, f, pos, gates_t, zeros)


def _expert_kernel(be_ref, bc_ref, x_ref, wgu_ref, wd_ref, y_ref):
    i = pl.program_id(0)
    D = y_ref.shape[1]

    @pl.when(bc_ref[i] > 0)
    def _():
        hu = jnp.dot(x_ref[:, :D], wgu_ref[0], preferred_element_type=F32)
        de = hu.shape[1] // 2
        gate, up = hu[:, :de], hu[:, de:]
        hid = (gate * jax.nn.sigmoid(gate) * up).astype(BF16)
        pieces = x_ref[:, D:].astype(F32)
        route_gate = pieces[:, 0:1] + pieces[:, 1:2]
        y_ref[...] = (jnp.dot(hid, wd_ref[0], preferred_element_type=F32) * route_gate).astype(BF16)

    @pl.when(bc_ref[i] == 0)
    def _():
        y_ref[...] = jnp.zeros_like(y_ref)


def _experts(buf, block_expert, block_count, wgu, wd):
    n_rows = buf.shape[0]
    D = wgu.shape[1]
    nb = n_rows // EXPERT_BLOCK
    grid_spec = pltpu.PrefetchScalarGridSpec(
        num_scalar_prefetch=2,
        grid=(nb,),
        in_specs=[pl.BlockSpec((EXPERT_BLOCK, buf.shape[1]), lambda i, be, bc: (i, 0)),
                  pl.BlockSpec((1,) + wgu.shape[1:], lambda i, be, bc: (be[i], 0, 0)),
                  pl.BlockSpec((1,) + wd.shape[1:], lambda i, be, bc: (be[i], 0, 0))],
        out_specs=pl.BlockSpec((EXPERT_BLOCK, D), lambda i, be, bc: (i, 0)),
    )
    return pl.pallas_call(
        _expert_kernel,
        grid_spec=grid_spec,
        out_shape=jax.ShapeDtypeStruct((n_rows, D), BF16),
        compiler_params=_cparams("arbitrary"),
        name="moe_experts",
    )(block_expert, block_count, buf, wgu, wd)


def _combine_kernel(src_tab, dst_tab, cnt_tab, tot_tab, y_hbm, x_ref, pos_ref, gate_ref, fin_ref, xo_ref,
                    srt_ref, sems, *, final, n_tiles, n_cls):
    t = pl.program_id(0)
    slot = t % 2
    tm = x_ref.shape[0]
    ls = srt_ref.shape[1]

    tabs = (src_tab, dst_tab, cnt_tab, tot_tab)

    def copier(buf_slot):
        return lambda lo, go, rows: pltpu.make_async_copy(
            y_hbm.at[pl.ds(go, rows)], srt_ref.at[buf_slot, pl.ds(lo, rows)], sems.at[buf_slot])

    @pl.when(t == 0)
    def _():
        srt_ref[...] = jnp.zeros_like(srt_ref)
        _start_pieces(tabs, t, n_cls, copier(slot))

    @pl.when(t + 1 < n_tiles)
    def _():
        _start_pieces(tabs, t + 1, n_cls, copier(1 - slot))

    _await_pieces(tabs, t, ls, copier(slot))
    j = lax.broadcasted_iota(I32, (tm, ls), 1)
    pick = jnp.where((pos_ref[:, 0:1] == j) | (pos_ref[:, 1:2] == j), 1.0, 0.0).astype(BF16)
    y = jnp.dot(pick, srt_ref[slot], preferred_element_type=F32)
    xn = x_ref[...] + gate_ref[0, 0] * y
    if final:
        xn = xn * lax.rsqrt(jnp.mean(xn * xn, axis=-1, keepdims=True) + RMS_EPS) * fin_ref[...]
    xo_ref[...] = xn


def _combine(y_buf, pos_t, tabs, n_cls, xs, mod, final_gain, n_tiles, tm, tiles_per_seq, final):
    N, D = xs.shape
    nmod = mod.shape[1]
    ls = 2 * tm + N_EXPERTS * RUN_ALIGN
    out_rows = n_tiles * tm if final else N
    kwargs = {} if final else {"input_output_aliases": {5: 0}}
    grid_spec = pltpu.PrefetchScalarGridSpec(
        num_scalar_prefetch=4,
        grid=(n_tiles,),
        in_specs=[pl.BlockSpec(memory_space=pl.ANY),
                  pl.BlockSpec((tm, D), lambda t, *_: (t, 0)),
                  pl.BlockSpec((tm, 2), lambda t, *_: (t, 0)),
                  pl.BlockSpec((1, 1, 1, D), lambda t, *_: (5, jnp.minimum(t // tiles_per_seq, nmod - 1), 0, 0)),
                  pl.BlockSpec((1, D), lambda t, *_: (0, 0))],
        out_specs=pl.BlockSpec((tm, D), lambda t, *_: (t, 0)),
        scratch_shapes=[pltpu.VMEM((2, ls, D), BF16), pltpu.SemaphoreType.DMA((2,))],
    )
    return pl.pallas_call(
        functools.partial(_combine_kernel, final=final, n_tiles=n_tiles, n_cls=n_cls),
        grid_spec=grid_spec,
        out_shape=jax.ShapeDtypeStruct((out_rows, D), F32),
        compiler_params=_cparams("arbitrary"),
        name="moe_combine_final" if final else "moe_combine",
        **kwargs,
    )(*tabs, y_buf, xs, pos_t, mod, final_gain.reshape(1, D))


def _moe(f, routing, xs, mod, wgu, wd, final_gain, n_tok, tm, tiles_per_seq, final):
    nt = n_tok // tm
    e, gates, rank, cnt = routing
    e, gates, rank, cnt = e[:, :n_tok], gates[:, :n_tok], rank[:, :n_tok], cnt[:nt]
    n = cnt[:, :, 0].astype(I32)
    rlen = (n + RUN_ALIGN - 1) // RUN_ALIGN * RUN_ALIGN
    lo = jnp.cumsum(rlen, axis=1) - rlen
    region = jnp.sum(rlen, axis=0)
    region_pad = (region + EXPERT_BLOCK - 1) // EXPERT_BLOCK * EXPERT_BLOCK
    pends = jnp.cumsum(region_pad)
    pstarts = pends - region_pad
    go = pstarts[None, :] + jnp.cumsum(rlen, axis=0) - rlen
    n_blocks = -(-(2 * n_tok + nt * N_EXPERTS * RUN_ALIGN) // EXPERT_BLOCK) + N_EXPERTS
    blk0 = jnp.arange(n_blocks, dtype=I32) * EXPERT_BLOCK
    block_expert = jnp.minimum(jnp.sum((blk0[:, None] >= pends[None, :]).astype(I32), axis=1), N_EXPERTS - 1)
    block_used = (blk0 < (pstarts + region)[block_expert]).astype(I32)
    hot = e.reshape(2, nt, tm, 1) == jnp.arange(N_EXPERTS, dtype=I32)
    pos = jnp.sum(jnp.where(hot, lo[None, :, None, :], 0), axis=-1).reshape(2, n_tok) + rank
    tabs, n_cls = _chunk_tables(lo, go, rlen, tm)
    buf = _dispatch(f, pos, gates.T, tabs, n_cls, n_blocks * EXPERT_BLOCK, nt, tm)
    y_buf = _experts(buf, block_expert.astype(I32), block_used, wgu, wd)
    return _combine(y_buf, pos.T, tabs, n_cls, xs, mod, final_gain, nt, tm, tiles_per_seq, final)


def _rope_tables(S, tm):
    rows = S // GRID_W
    row = jnp.repeat(jnp.arange(rows, dtype=F32), GRID_W)
    col = jnp.tile(jnp.arange(GRID_W, dtype=F32), rows)
    half = ATTN_HEAD_DIM // 4
    inv_freq = ROPE_THETA ** (-jnp.arange(half, dtype=F32) / half)
    ang_r = row[:, None] * inv_freq[None, :]
    ang_c = col[:, None] * inv_freq[None, :]
    zeros = jnp.zeros_like(ang_r)
    cos = jnp.concatenate([jnp.cos(ang_r)] * 2 + [jnp.cos(ang_c)] * 2, axis=1)
    s1 = jnp.concatenate([-jnp.sin(ang_r), zeros, -jnp.sin(ang_c), zeros], axis=1)
    s2 = jnp.concatenate([zeros, jnp.sin(ang_r), zeros, jnp.sin(ang_c)], axis=1)
    def finish(tab, fill):
        tab = jnp.tile(tab, (1, LANES // ATTN_HEAD_DIM))
        return jnp.concatenate([tab, jnp.full((tm, LANES), fill, F32)], axis=0)
    return finish(cos, 1.0), finish(s1, 0.0), finish(s2, 0.0)


def kernel(x, c, ctx, c_ctx, ada_w, ada_b, norm_mix_g, norm_ffn_g, final_g, attn_w_qkv, attn_w_o, attn_sinks,
           gla_w_in, gla_w_a1, gla_w_a2, gla_b_a, gla_norm_g, gla_w_o, router_w, router_b,
           moe_w_gate, moe_w_up, moe_w_down):
    B, S, D = x.shape
    C = ctx.shape[1]
    depth = ada_w.shape[0]
    tm = TOKEN_TILE
    assert S % tm == 0 and (B * C) % tm == 0 and S % ATTN_BLOCK == 0 and C % ATTN_BLOCK == 0
    assert (B * S) % C == 0 and S % GLA_CHUNK == 0 and C % GLA_CHUNK == 0
    n_lat = B * S
    n_lat_tiles = n_lat // tm
    tiles_per_seq = S // tm

    rpad = -(-(B + 1) // 8) * 8
    cc = jnp.zeros((rpad, D), F32).at[:B].set(c).at[B].set(c_ctx)
    mods = _ada_table(cc, ada_w, ada_b)
    mods = mods[:, :B + 1].reshape(depth, B + 1, 6, 1, D).transpose(0, 2, 1, 3, 4)

    xs = jnp.concatenate([x.reshape(n_lat, D), ctx.reshape(B * C, D)], axis=0)
    rope = _rope_tables(S, tm)
    q_dim = ATTN_KV_HEADS * ATTN_GROUP * ATTN_HEAD_DIM
    kv_dim = ATTN_KV_HEADS * ATTN_HEAD_DIM
    kd = GLA_HEADS * GLA_KEY_DIM
    vd = GLA_HEADS * GLA_VAL_DIM

    def dup_heads(w):
        w = w.reshape(D, ATTN_KV_HEADS, 1, ATTN_HEAD_DIM)
        return jnp.broadcast_to(w, (D, ATTN_KV_HEADS, LANES // ATTN_HEAD_DIM, ATTN_HEAD_DIM)).reshape(D, -1)

    for i in range(depth):
        last = i == depth - 1
        mod = mods[i]
        j = i // 2
        if i % 2 == 0:
            wqkv = attn_w_qkv[j]
            w = jnp.concatenate([wqkv[:, :q_dim], dup_heads(wqkv[:, q_dim:q_dim + kv_dim]),
                                 dup_heads(wqkv[:, q_dim + kv_dim:])], axis=1).astype(BF16)
            q, k, v = _qkv_proj(xs, norm_mix_g[i], mod, 0, w, rope, tm, n_lat_tiles, tiles_per_seq)
            sink_col = jnp.repeat(attn_sinks[j].astype(F32).reshape(ATTN_KV_HEADS, ATTN_GROUP) * LOG2_E,
                                  ATTN_BLOCK, axis=1)
            o = _attention(q, k, v, sink_col[:, :, None], B, S, C, not last)
            xs, f, *routing = _post_mixer(o, attn_w_o[j].astype(BF16), xs, mod, norm_ffn_g[i], router_w, router_b,
                                          tm, tiles_per_seq)
        else:
            a1 = jnp.zeros((D, LANES), F32).at[:, :2 * GLA_GATE_RANK].set(
                jnp.concatenate([gla_w_a1[j, 0], gla_w_a1[j, 1]], axis=1))
            w = jnp.concatenate([gla_w_in[j], a1], axis=1).astype(BF16)
            w2 = jnp.zeros((LANES, 2 * kd), F32)
            w2 = w2.at[:GLA_GATE_RANK, :kd].set(gla_w_a2[j, 0]).at[GLA_GATE_RANK:2 * GLA_GATE_RANK, kd:].set(gla_w_a2[j, 1])
            ba = gla_b_a[j].reshape(1, 2 * kd).astype(F32)
            qk, v, og, la, chunk_tot = _gla_in_proj(xs, norm_mix_g[i], mod, 0, w, w2.astype(BF16), ba, tm,
                                                    n_lat_tiles, tiles_per_seq)
            o_fwd, o_bwd = _gla_scan(qk, v, la, chunk_tot, B, S, C)
            xs, f, *routing = _post_mixer(o_fwd, gla_w_o[j].astype(BF16), xs, mod, norm_ffn_g[i], router_w, router_b,
                                          tm, tiles_per_seq,
                                          gla_extra=(o_bwd, og, gla_norm_g[j].reshape(1, GLA_VAL_DIM).astype(F32)))
        wgu = jnp.concatenate([moe_w_gate[i], moe_w_up[i]], axis=2).astype(BF16)
        wd = moe_w_down[i].astype(BF16)
        n_tok = n_lat if last else n_lat + B * C
        xs = _moe(f, routing, xs, mod, wgu, wd, final_g, n_tok, tm, tiles_per_seq, last)
    return xs.reshape(B, S, D)
```

```python
import functools

import numpy as np
import jax
import jax.numpy as jnp
from jax import lax
from jax.experimental import pallas as pl
from jax.experimental.pallas import tpu as pltpu

F32 = jnp.float32
BF16 = jnp.bfloat16
I32 = jnp.int32

LANES = 128
VMEM_LIMIT_BYTES = 56 * 1024 * 1024

RMS_EPS = 1e-6
GRID_W = 64
ROPE_THETA = 10000.0
ATTN_HEAD_DIM = 64
ATTN_KV_HEADS = 4
ATTN_GROUP = 4
ATTN_BLOCK = 128
LOG2_E = 1.4426950408889634
ATTN_Q_SCALE = ATTN_HEAD_DIM ** -0.5 * LOG2_E
GLA_HEADS = 4
GLA_KEY_DIM = 128
GLA_VAL_DIM = 256
GLA_GATE_RANK = 16
GLA_GATE_NORM = 16.0
GLA_CHUNK = 64
GLA_BOUNDED_TOTAL = 40.0
N_EXPERTS = 16
N_GROUPS = 4
EXPERTS_PER_GROUP = 4
EXPERT_BLOCK = 512
RUN_ALIGN = 16
TOKEN_TILE = 512


def _cparams(*sem):
    return pltpu.CompilerParams(dimension_semantics=sem, vmem_limit_bytes=VMEM_LIMIT_BYTES)


def _norm_mod(x, gain, shift, scale):
    h = x * lax.rsqrt(jnp.mean(x * x, axis=-1, keepdims=True) + RMS_EPS) * gain
    return h * (1.0 + scale) + shift


def _ada_kernel(c_ref, w_ref, b_ref, o_ref):
    c = c_ref[...]
    s = (c * jax.nn.sigmoid(c)).astype(BF16)
    o_ref[0] = jnp.dot(s, w_ref[0].astype(BF16), preferred_element_type=F32) + b_ref[0]


def _ada_table(cc, ada_w, ada_b):
    L, D, D6 = ada_w.shape
    R = cc.shape[0]
    tn = 1536
    return pl.pallas_call(
        _ada_kernel,
        grid=(L, D6 // tn),
        in_specs=[pl.BlockSpec((R, D), lambda l, j: (0, 0)),
                  pl.BlockSpec((1, D, tn), lambda l, j: (l, 0, j)),
                  pl.BlockSpec((1, 1, tn), lambda l, j: (l, 0, j))],
        out_specs=pl.BlockSpec((1, R, tn), lambda l, j: (l, 0, j)),
        out_shape=jax.ShapeDtypeStruct((L, R, D6), F32),
        compiler_params=_cparams("parallel", "parallel"),
        name="ada_table",
    )(cc, ada_w, ada_b.reshape(L, 1, D6))


def _stream_tile(x_refs, n_lat_tiles):
    if len(x_refs) == 1:
        return x_refs[0][...]
    return jnp.where(pl.program_id(0) < n_lat_tiles, x_refs[0][...], x_refs[1][...])


def _stream_args(xs, tm, n_lat_tiles):
    if isinstance(xs, tuple):
        D = xs[0].shape[1]
        return list(xs), [pl.BlockSpec((tm, D), lambda t: (jnp.minimum(t, n_lat_tiles - 1), 0)),
                          pl.BlockSpec((tm, D), lambda t: (jnp.maximum(t - n_lat_tiles, 0), 0))]
    return [xs], [pl.BlockSpec((tm, xs.shape[1]), lambda t: (t, 0))]


def _qkv_kernel(*refs, n_x, n_lat_tiles):
    x_refs = refs[:n_x]
    g_ref, sh_ref, sc_ref, w_ref, wvt_ref, cos_ref, s1_ref, s2_ref, q_ref, k_ref, vt_ref = refs[n_x:]
    h = _norm_mod(_stream_tile(x_refs, n_lat_tiles), g_ref[...], sh_ref[0, 0], sc_ref[0, 0]).astype(BF16)
    z = jnp.dot(h, w_ref[...], preferred_element_type=F32)
    vt_ref[...] = lax.dot_general(wvt_ref[...], h, (((1,), (1,)), ((), ())), preferred_element_type=F32).astype(BF16)
    cos, s1, s2 = cos_ref[...], s1_ref[...], s2_ref[...]
    nq = q_ref.shape[1] // LANES
    nk = k_ref.shape[1] // LANES
    for j in range(nq + nk):
        zc = z[:, j * LANES:(j + 1) * LANES]
        r = zc * cos + pltpu.roll(zc, LANES - 16, 1) * s1 + pltpu.roll(zc, 16, 1) * s2
        if j < nq:
            q_ref[:, j * LANES:(j + 1) * LANES] = (r * ATTN_Q_SCALE).astype(BF16)
        else:
            k_ref[:, (j - nq) * LANES:(j - nq + 1) * LANES] = r.astype(BF16)


def _gla_in_kernel(x_ref, g_ref, sh_ref, sc_ref, w_ref, w2_ref, ba_ref, qk_ref, v_ref, og_ref, la_ref, tot_ref):
    h = _norm_mod(x_ref[...], g_ref[...], sh_ref[0, 0], sc_ref[0, 0])
    z = jnp.dot(h.astype(BF16), w_ref[...], preferred_element_type=F32)
    kd = GLA_HEADS * GLA_KEY_DIM
    vd = GLA_HEADS * GLA_VAL_DIM
    qk_ref[:, :kd] = (z[:, :kd] * (GLA_KEY_DIM ** -0.5)).astype(BF16)
    qk_ref[:, kd:] = z[:, kd:2 * kd].astype(BF16)
    v_ref[...] = z[:, 2 * kd:2 * kd + vd].astype(BF16)
    og_ref[...] = z[:, 2 * kd + vd:2 * kd + 2 * vd].astype(BF16)
    a1 = z[:, 2 * kd + 2 * vd:].astype(BF16)
    pre = jnp.dot(a1, w2_ref[...], preferred_element_type=F32) + ba_ref[...]
    la = (jnp.minimum(pre, 0.0) - jnp.log1p(jnp.exp(-jnp.abs(pre)))) * (1.0 / GLA_GATE_NORM)
    la_ref[...] = la
    nc = la.shape[0] // GLA_CHUNK
    tot = jnp.sum(la.reshape(nc, GLA_CHUNK, la.shape[1]), axis=1)
    lane = lax.broadcasted_iota(I32, (nc, LANES), 1)
    acc = jnp.zeros((nc, LANES), F32)
    for hd in range(2 * GLA_HEADS):
        worst = jnp.min(tot[:, hd * GLA_KEY_DIM:(hd + 1) * GLA_KEY_DIM], axis=1, keepdims=True)
        acc = jnp.where(lane == hd, worst, acc)
    tot_ref[...] = acc


def _tile_specs(D, tm, n_lat_tiles, tiles_per_seq, n_mod_rows):
    def mod_idx(which):
        return lambda t: (which, jnp.minimum(t // tiles_per_seq, n_mod_rows - 1), 0, 0)
    return mod_idx, [pl.BlockSpec((tm, D), lambda t: (t, 0)),
                     pl.BlockSpec((1, D), lambda t: (0, 0))]


def _qkv_proj(xs, gain, mod, which, w, wvt, rope, tm, n_lat_tiles, tiles_per_seq):
    x_args, x_specs = _stream_args(xs, tm, n_lat_tiles)
    N = sum(a.shape[0] for a in x_args)
    D = x_args[0].shape[1]
    nmod = mod.shape[1]
    mod_idx, specs = _tile_specs(D, tm, n_lat_tiles, tiles_per_seq, nmod)
    qd = ATTN_KV_HEADS * ATTN_GROUP * ATTN_HEAD_DIM
    kd = ATTN_KV_HEADS * LANES
    vd = wvt.shape[0]
    rope_idx = lambda t: (jnp.where(t < n_lat_tiles, t % tiles_per_seq, tiles_per_seq), 0)
    in_specs = x_specs + specs[1:] + [
        pl.BlockSpec((1, 1, 1, D), mod_idx(which)),
        pl.BlockSpec((1, 1, 1, D), mod_idx(which + 1)),
        pl.BlockSpec(w.shape, lambda t: (0, 0)),
        pl.BlockSpec(wvt.shape, lambda t: (0, 0)),
        pl.BlockSpec((tm, LANES), rope_idx),
        pl.BlockSpec((tm, LANES), rope_idx),
        pl.BlockSpec((tm, LANES), rope_idx),
    ]
    return pl.pallas_call(
        functools.partial(_qkv_kernel, n_x=len(x_args), n_lat_tiles=n_lat_tiles),
        grid=(N // tm,),
        in_specs=in_specs,
        out_specs=[pl.BlockSpec((tm, qd), lambda t: (t, 0)),
                   pl.BlockSpec((tm, kd), lambda t: (t, 0)),
                   pl.BlockSpec((vd, tm), lambda t: (0, t))],
        out_shape=[jax.ShapeDtypeStruct((N, qd), BF16),
                   jax.ShapeDtypeStruct((N, kd), BF16),
                   jax.ShapeDtypeStruct((vd, N), BF16)],
        compiler_params=_cparams("parallel"),
        name="attn_qkv_proj",
    )(*x_args, gain.reshape(1, D), mod, mod, w, wvt, *rope)


def _gla_in_proj(xs, gain, mod, which, w, w2, ba, tm, n_lat_tiles, tiles_per_seq):
    N, D = xs.shape
    nmod = mod.shape[1]
    mod_idx, specs = _tile_specs(D, tm, n_lat_tiles, tiles_per_seq, nmod)
    kd = GLA_HEADS * GLA_KEY_DIM
    vd = GLA_HEADS * GLA_VAL_DIM
    in_specs = specs + [
        pl.BlockSpec((1, 1, 1, D), mod_idx(which)),
        pl.BlockSpec((1, 1, 1, D), mod_idx(which + 1)),
        pl.BlockSpec(w.shape, lambda t: (0, 0)),
        pl.BlockSpec(w2.shape, lambda t: (0, 0)),
        pl.BlockSpec((1, 2 * kd), lambda t: (0, 0)),
    ]
    row = lambda width: pl.BlockSpec((tm, width), lambda t: (t, 0))
    return pl.pallas_call(
        _gla_in_kernel,
        grid=(N // tm,),
        in_specs=in_specs,
        out_specs=[row(2 * kd), row(vd), row(vd), row(2 * kd),
                   pl.BlockSpec((tm // GLA_CHUNK, LANES), lambda t: (t, 0))],
        out_shape=[jax.ShapeDtypeStruct((N, 2 * kd), BF16),
                   jax.ShapeDtypeStruct((N, vd), BF16),
                   jax.ShapeDtypeStruct((N, vd), BF16),
                   jax.ShapeDtypeStruct((N, 2 * kd), F32),
                   jax.ShapeDtypeStruct((N // GLA_CHUNK, LANES), F32)],
        compiler_params=_cparams("parallel"),
        name="gla_in_proj",
    )(xs, gain.reshape(1, D), mod, mod, w, w2, ba)


def _attn_kernel(*refs, window, nq):
    if window:
        q_ref, kp, kc, kn, kx, vp, vc, vn, vx, sink_ref, tri_ref, o_ref = refs
        k_parts, v_parts = (kp, kc, kn, kx), (vp, vc, vn, vx)
    else:
        q_ref, kx, vx, sink_ref, _, o_ref = refs
        k_parts, v_parts = (kx,), (vx,)
    tq = q_ref.shape[0]
    lane = lax.broadcasted_iota(I32, (tq, LANES), 1)
    first_head = lane < ATTN_HEAD_DIM
    if window:
        j = pl.program_id(1)
        bias_prev = jnp.where(j > 0, tri_ref[0], -jnp.inf)
        bias_next = jnp.where(j < nq - 1, tri_ref[1], -jnp.inf)
        bias_prev = jnp.concatenate([bias_prev] * ATTN_GROUP, axis=1)
        bias_next = jnp.concatenate([bias_next] * ATTN_GROUP, axis=1)

    def scores(kh):
        ks = slice(kh * LANES, (kh + 1) * LANES)
        kk = jnp.concatenate([p[:, ks] for p in k_parts], axis=0) if window else kx[:, ks]
        qa = q_ref[:, (2 * kh) * LANES:(2 * kh + 1) * LANES]
        qb = q_ref[:, (2 * kh + 1) * LANES:(2 * kh + 2) * LANES]
        zero = jnp.zeros_like(qa)
        qs = jnp.concatenate([jnp.where(first_head, qa, zero), jnp.where(first_head, zero, qa),
                              jnp.where(first_head, qb, zero), jnp.where(first_head, zero, qb)], axis=0)
        return lax.dot_general(kk, qs, (((1,), (1,)), ((), ())), preferred_element_type=F32)

    def softmax(s, kh):
        if window:
            s = jnp.concatenate([s[:tq] + bias_prev, s[tq:2 * tq], s[2 * tq:3 * tq] + bias_next, s[3 * tq:]], axis=0)
        sink = sink_ref[kh]
        m = jnp.maximum(jnp.max(s, axis=0, keepdims=True), sink)
        p = jnp.exp2(s - m)
        l = jnp.sum(p, axis=0, keepdims=True) + jnp.exp2(sink - m)
        return p.astype(BF16), l

    def values(p, l, kh):
        hs = slice(kh * ATTN_HEAD_DIM, (kh + 1) * ATTN_HEAD_DIM)
        vv = jnp.concatenate([part[hs, :] for part in v_parts], axis=1) if window else vx[hs, :]
        o = jnp.dot(vv, p, preferred_element_type=F32) / l
        for g in range(ATTN_GROUP):
            h = kh * ATTN_GROUP + g
            o_ref[h * ATTN_HEAD_DIM:(h + 1) * ATTN_HEAD_DIM, :] = o[:, g * tq:(g + 1) * tq].astype(BF16)

    s_next = scores(0)
    pending = None
    for kh in range(ATTN_KV_HEADS):
        s_cur = s_next
        if kh + 1 < ATTN_KV_HEADS:
            s_next = scores(kh + 1)
        if pending is not None:
            values(*pending)
        pending = softmax(s_cur, kh) + (kh,)
    values(*pending)


def _attention(q, k, vt, sink_row, B, S, C, need_ctx):
    N, qd = q.shape
    kd = k.shape[1]
    vd = vt.shape[0]
    tq = ATTN_BLOCK
    nq = S // tq
    ctx_blk0 = (B * S) // C
    qmap = lambda b, j: (b * nq + j, 0)
    prev = lambda b, j: (b * nq + jnp.maximum(j - 1, 0), 0)
    nxt = lambda b, j: (b * nq + jnp.minimum(j + 1, nq - 1), 0)
    cmap = lambda b, j: (ctx_blk0 + b, 0)
    cols = lambda m: (lambda b, j: m(b, j)[::-1])
    kblk = lambda m: pl.BlockSpec((tq, kd), m)
    vblk = lambda m: pl.BlockSpec((vd, tq), cols(m))
    kctx = pl.BlockSpec((C, kd), cmap)
    vctx = pl.BlockSpec((vd, C), cols(cmap))
    sink_spec = pl.BlockSpec(sink_row.shape, lambda b, j: (0, 0, 0))
    key = np.arange(tq)[:, None]
    qry = np.arange(tq)[None, :]
    tri = jnp.asarray(np.stack([np.where(key >= qry, 0.0, -np.inf), np.where(key <= qry, 0.0, -np.inf)]), F32)
    o_lat = pl.pallas_call(
        functools.partial(_attn_kernel, window=True, nq=nq),
        grid=(B, nq),
        in_specs=[pl.BlockSpec((tq, qd), qmap), kblk(prev), kblk(qmap), kblk(nxt), kctx,
                  vblk(prev), vblk(qmap), vblk(nxt), vctx, sink_spec,
                  pl.BlockSpec(tri.shape, lambda b, j: (0, 0, 0))],
        out_specs=pl.BlockSpec((qd, tq), cols(qmap)),
        out_shape=jax.ShapeDtypeStruct((qd, N), BF16),
        compiler_params=_cparams("parallel", "parallel"),
        name="attn_window",
    )(q, k, k, k, k, vt, vt, vt, vt, sink_row, tri)
    if not need_ctx:
        return o_lat
    ncq = C // tq
    lat_blks = (B * S) // tq
    cq = lambda b, j: (lat_blks + b * ncq + j, 0)
    return pl.pallas_call(
        functools.partial(_attn_kernel, window=False, nq=ncq),
        grid=(B, ncq),
        in_specs=[pl.BlockSpec((tq, qd), cq), kctx, vctx, sink_spec,
                  pl.BlockSpec(memory_space=pl.ANY)],
        out_specs=pl.BlockSpec((qd, tq), cols(cq)),
        out_shape=jax.ShapeDtypeStruct((qd, N), BF16),
        input_output_aliases={4: 0},
        compiler_params=_cparams("parallel", "parallel"),
        name="attn_context",
    )(q, k, vt, sink_row, o_lat)


def _gla_constants(C):
    levels = []
    m = 1
    while m < C:
        levels.append(m)
        m *= 2
    t = np.arange(C)[:, None]
    u = np.arange(C)[None, :]
    secs = [(u <= t), (u > t)]
    masks = []
    for m in levels:
        base = (t // (2 * m)) * (2 * m)
        ref = base + m - 1
        second = t >= base + m
        secs.append(np.where(second, (u > ref) & (u <= t), (u > t) & (u <= ref)))
        masks.append((t // (2 * m) == u // (2 * m)) & second & (u < (u // (2 * m)) * (2 * m) + m))
    masks.append(t == u)
    masks.append(u <= t)
    mf = np.concatenate([s.astype(np.float32) for s in secs], axis=0)
    kf = np.stack([mk.astype(np.float32) for mk in masks], axis=0)
    mb = np.concatenate([s.astype(np.float32)[::-1, ::-1] for s in secs], axis=0)
    kb = np.stack([mk.astype(np.float32)[::-1, ::-1] for mk in masks], axis=0)
    return np.stack([mf, mb]), np.stack([kf, kb]), len(levels)


def _gla_chunk(q, k, v, g, st_ref, mm_ref, mk_ref, d, nl, bounded):
    C, dk = q.shape
    nt = (((1,), (1,)), ((), ()))
    g_hi = g.astype(BF16)
    g_lo = (g - g_hi.astype(F32)).astype(BF16)
    mmat = mm_ref[d, :2 * C] if bounded else mm_ref[d]
    e2 = jnp.dot(mmat, jnp.concatenate([g_hi, g_lo], axis=1), preferred_element_type=F32)
    ee = e2[:, :dk] + e2[:, dk:]
    ex = jnp.exp(ee)
    qf, kf = q.astype(F32), k.astype(F32)
    qe = (qf * ex[:C]).astype(BF16)
    ke = (kf * ex[C:2 * C]).astype(BF16)
    st = st_ref[...]
    o = lax.dot_general(qe, st.astype(BF16), nt, preferred_element_type=F32)
    if bounded:
        ki = (kf * jnp.exp(-ee[:C])).astype(BF16)
        a = mk_ref[d, nl + 1] * lax.dot_general(qe, ki, nt, preferred_element_type=F32)
    else:
        a = mk_ref[d, nl] * lax.dot_general(q, k, nt, preferred_element_type=F32)
        for i in range(nl):
            xl = ex[(2 + i) * C:(3 + i) * C]
            ql = (qf * xl).astype(BF16)
            kl = (kf * xl).astype(BF16)
            a = a + mk_ref[d, i] * lax.dot_general(ql, kl, nt, preferred_element_type=F32)
    o = o + jnp.dot(a.astype(BF16), v, preferred_element_type=F32)
    decay = jnp.exp(jnp.sum(g, axis=0, keepdims=True))
    st_ref[...] = st * decay + lax.dot_general(v, ke, (((0,), (0,)), ((), ())), preferred_element_type=F32)
    return o


def _gla_kernel(okf_ref, okb_ref, qf, kf, vf, lf, qb, kb, vb, lb, mm_ref, mk_ref, of_ref, ob_ref, st_ref,
                *, nl, seg_chunks, lat_segs, ctx_seg0):
    C = GLA_CHUNK
    H, DK, DV = GLA_HEADS, GLA_KEY_DIM, GLA_VAL_DIM
    b, j = pl.program_id(0), pl.program_id(1)

    @pl.when(j == 0)
    def _():
        st_ref[...] = jnp.zeros_like(st_ref)

    seg_f = jnp.where(j == 0, ctx_seg0 + b, b * lat_segs + j - 1)
    seg_b = jnp.where(j == 0, ctx_seg0 + b, b * lat_segs + lat_segs - j)

    def body(i, carry):
        cf, cb = i, seg_chunks - 1 - i
        rf = pl.ds(pl.multiple_of(cf * C, C), C)
        rb = pl.ds(pl.multiple_of(cb * C, C), C)
        bounded = (okf_ref[seg_f * seg_chunks + cf] != 0) & (okb_ref[seg_b * seg_chunks + cb] != 0)

        def advance_general():
            for h in range(H):
                ks, vs = slice(h * DK, (h + 1) * DK), slice(h * DV, (h + 1) * DV)
                o = _gla_chunk(qf[rf, ks], kf[rf, ks], vf[rf, vs], lf[rf, ks], st_ref.at[h],
                               mm_ref, mk_ref, 0, nl, False)
                of_ref[rf, vs] = o.astype(BF16)
                o = _gla_chunk(qb[rb, ks], kb[rb, ks], vb[rb, vs], lb[rb, ks], st_ref.at[H + h],
                               mm_ref, mk_ref, 1, nl, False)
                ob_ref[rb, vs] = o.astype(BF16)

        def advance_bounded():
            nt = (((1,), (1,)), ((), ()))
            kw = H * DK
            sides = ((qf, kf, vf, lf, rf, of_ref, 0), (qb, kb, vb, lb, rb, ob_ref, 1))
            pre = []
            for q_r, k_r, v_r, l_r, rows, o_r, d in sides:
                g = l_r[rows, :]
                g_hi = g.astype(BF16)
                g_lo = (g - g_hi.astype(F32)).astype(BF16)
                e2 = jnp.dot(mm_ref[d, :2 * C], jnp.concatenate([g_hi, g_lo], axis=1),
                             preferred_element_type=F32)
                pre.append((e2[:, :kw] + e2[:, kw:], g))
            units = []
            for (ee, g), (q_r, k_r, v_r, l_r, rows, o_r, d) in zip(pre, sides):
                ex = jnp.exp(ee)
                kf32 = k_r[rows, :].astype(F32)
                qe = (q_r[rows, :].astype(F32) * ex[:C]).astype(BF16)
                ke = (kf32 * ex[C:]).astype(BF16)
                ki = (kf32 * jnp.exp(-ee[:C])).astype(BF16)
                decay = jnp.exp(jnp.sum(g, axis=0, keepdims=True))
                for h in range(H):
                    ks = slice(h * DK, (h + 1) * DK)
                    units.append((d, h, qe[:, ks], ke[:, ks], ki[:, ks], decay[:, ks], v_r, rows, o_r))
            inter, score = [], []
            for d, h, qe, ke, ki, decay, v_r, rows, o_r in units:
                st = st_ref[d * H + h]
                inter.append(lax.dot_general(qe, st.astype(BF16), nt, preferred_element_type=F32))
                score.append(lax.dot_general(qe, ki, nt, preferred_element_type=F32))
            for n, (d, h, qe, ke, ki, decay, v_r, rows, o_r) in enumerate(units):
                vs = slice(h * DV, (h + 1) * DV)
                v = v_r[rows, vs]
                a = (mk_ref[d, nl + 1] * score[n]).astype(BF16)
                o = inter[n] + jnp.dot(a, v, preferred_element_type=F32)
                upd = lax.dot_general(v, ke, (((0,), (0,)), ((), ())), preferred_element_type=F32)
                o_r[rows, vs] = o.astype(BF16)
                st_ref[d * H + h] = st_ref[d * H + h] * decay + upd

        @pl.when(bounded)
        def _():
            advance_bounded()

        @pl.when(jnp.logical_not(bounded))
        def _():
            advance_general()

        return carry

    lax.fori_loop(0, seg_chunks, body, 0)


def _gla_scan(qk, v, la, chunk_tot, B, S, C):
    N = qk.shape[0]
    H, DK, DV = GLA_HEADS, GLA_KEY_DIM, GLA_VAL_DIM
    seg = C
    assert S % seg == 0 and seg % GLA_CHUNK == 0
    lat_segs = S // seg
    ctx_seg0 = (B * S) // seg
    mm, mk, nl = _gla_constants(GLA_CHUNK)
    mm = jnp.asarray(mm, BF16)
    mk = jnp.asarray(mk, F32)
    ok = chunk_tot[:, :2 * H] >= -GLA_BOUNDED_TOTAL
    okf = jnp.all(ok[:, :H], axis=1).astype(I32)
    okb = jnp.all(ok[:, H:], axis=1).astype(I32)
    fwd = lambda col: (lambda b, j, *_: (jnp.where(j == 0, ctx_seg0 + b, b * lat_segs + j - 1), col))
    bwd = lambda col: (lambda b, j, *_: (jnp.where(j == 0, ctx_seg0 + b, b * lat_segs + lat_segs - j), col))
    kw, vw = H * DK, H * DV
    in_specs = [
        pl.BlockSpec((seg, kw), fwd(0)), pl.BlockSpec((seg, kw), fwd(1)), pl.BlockSpec((seg, vw), fwd(0)),
        pl.BlockSpec((seg, kw), fwd(0)),
        pl.BlockSpec((seg, kw), bwd(0)), pl.BlockSpec((seg, kw), bwd(1)), pl.BlockSpec((seg, vw), bwd(0)),
        pl.BlockSpec((seg, kw), bwd(1)),
        pl.BlockSpec(mm.shape, lambda b, j, *_: (0, 0, 0)),
        pl.BlockSpec(mk.shape, lambda b, j, *_: (0, 0, 0, 0)),
    ]
    grid_spec = pltpu.PrefetchScalarGridSpec(
        num_scalar_prefetch=2,
        grid=(B, lat_segs + 1),
        in_specs=in_specs,
        out_specs=[pl.BlockSpec((seg, vw), fwd(0)), pl.BlockSpec((seg, vw), bwd(0))],
        scratch_shapes=[pltpu.VMEM((2 * H, DV, DK), F32)],
    )
    return pl.pallas_call(
        functools.partial(_gla_kernel, nl=nl, seg_chunks=seg // GLA_CHUNK, lat_segs=lat_segs, ctx_seg0=ctx_seg0),
        grid_spec=grid_spec,
        out_shape=[jax.ShapeDtypeStruct((N, vw), BF16), jax.ShapeDtypeStruct((N, vw), BF16)],
        compiler_params=_cparams("parallel", "arbitrary"),
        name="gla_scan",
    )(okf, okb, qk, qk, v, la, qk, qk, v, la, mm, mk)


def _post_kernel(*refs, gla, n_x, n_lat_tiles):
    x_refs, refs = refs[:n_x], refs[n_x:]
    route_refs = refs[-4:]
    route_in = refs[-9:-6]
    refs = refs[:-9] + refs[-6:-4]
    x_tile = _stream_tile(x_refs, n_lat_tiles)
    if gla:
        o_ref, ob_ref, og_ref, ng_ref, w_ref, gate_ref, fg_ref, fsh_ref, fsc_ref, xo_ref, f_ref = refs
        o = o_ref[...].astype(F32) + ob_ref[...].astype(F32)
        g = og_ref[...].astype(F32)
        parts = []
        for h in range(GLA_HEADS):
            oh = o[:, h * GLA_VAL_DIM:(h + 1) * GLA_VAL_DIM]
            parts.append(oh * lax.rsqrt(jnp.mean(oh * oh, axis=-1, keepdims=True) + RMS_EPS) * ng_ref[...])
        mix = (jnp.concatenate(parts, axis=1) * (g * jax.nn.sigmoid(g))).astype(BF16)
        y = jnp.dot(mix, w_ref[...], preferred_element_type=F32)
    else:
        o_ref, w_ref, gate_ref, fg_ref, fsh_ref, fsc_ref, xo_ref, f_ref = refs
        y = lax.dot_general(o_ref[...], w_ref[...], (((0,), (0,)), ((), ())), preferred_element_type=F32)
    xn = x_tile + gate_ref[0, 0] * y
    xo_ref[...] = xn
    f = _norm_mod(xn, fg_ref[...], fsh_ref[0, 0], fsc_ref[0, 0])
    f_ref[...] = f.astype(BF16)
    _route_tile(f, *route_in, *route_refs)


def _post_mixer(o, w_o, xs, mod, ffn_gain, router_w, router_b, tm, n_lat_tiles, tiles_per_seq, gla_extra=None):
    args, specs = _stream_args(xs, tm, n_lat_tiles)
    n_x = len(args)
    N = sum(a.shape[0] for a in args)
    D = args[0].shape[1]
    r_args, r_in, r_out, r_shapes = _route_io(router_w, router_b, N, D, tm)
    nmod = mod.shape[1]
    mod_idx = lambda which: (lambda t: (which, jnp.minimum(t // tiles_per_seq, nmod - 1), 0, 0))
    row = lambda width: pl.BlockSpec((tm, width), lambda t: (t, 0))
    const = lambda a: pl.BlockSpec(a.shape, lambda t: (0,) * a.ndim)
    if gla_extra is None:
        args, specs = args + [o], specs + [pl.BlockSpec((o.shape[0], tm), lambda t: (0, t))]
    else:
        o_bwd, og, ng = gla_extra
        args += [o, o_bwd, og, ng]
        specs += [row(o.shape[1]), row(o_bwd.shape[1]), row(og.shape[1]), const(ng)]
    args += [w_o, mod, ffn_gain.reshape(1, D), mod, mod]
    specs += [const(w_o), pl.BlockSpec((1, 1, 1, D), mod_idx(2)), pl.BlockSpec((1, D), lambda t: (0, 0)),
              pl.BlockSpec((1, 1, 1, D), mod_idx(3)), pl.BlockSpec((1, 1, 1, D), mod_idx(4))]
    args += r_args
    specs += r_in
    return pl.pallas_call(
        functools.partial(_post_kernel, gla=gla_extra is not None, n_x=n_x, n_lat_tiles=n_lat_tiles),
        grid=(N // tm,),
        in_specs=specs,
        out_specs=[row(D), row(D)] + r_out,
        out_shape=[jax.ShapeDtypeStruct((N, D), F32), jax.ShapeDtypeStruct((N, D), BF16)] + r_shapes,
        input_output_aliases={0: 0} if n_x == 1 else {},
        compiler_params=_cparams("parallel"),
        name="post_mixer_gla" if gla_extra is not None else "post_mixer_attn",
    )(*args)


def _route_tile(f, rw_ref, rb_ref, tri_ref, e_ref, gate_ref, rank_ref, cnt_ref):
    fh = f.astype(BF16)
    fl = (f - fh.astype(F32)).astype(BF16)
    rw = rw_ref[...]
    wh = rw.astype(BF16)
    wl = (rw - wh.astype(F32)).astype(BF16)
    logits = (jnp.dot(fh, wh, preferred_element_type=F32) + jnp.dot(fh, wl, preferred_element_type=F32)
              + jnp.dot(fl, wh, preferred_element_type=F32))
    lt = logits.T[:N_EXPERTS]
    scores = jax.nn.sigmoid(lt)
    sel = scores + rb_ref[...]
    srow = [sel[e:e + 1] for e in range(N_EXPERTS)]
    prow = [scores[e:e + 1] for e in range(N_EXPERTS)]
    gscore = []
    for g in range(N_GROUPS):
        a, b, c, d = srow[4 * g:4 * g + 4]
        hi1, lo1, hi2, lo2 = jnp.maximum(a, b), jnp.minimum(a, b), jnp.maximum(c, d), jnp.minimum(c, d)
        gscore.append(jnp.maximum(hi1, hi2) + jnp.maximum(jnp.minimum(hi1, hi2), jnp.maximum(lo1, lo2)))
    best, grp = gscore[0], jnp.zeros_like(gscore[0], dtype=I32)
    for g in range(1, N_GROUPS):
        better = gscore[g] > best
        grp = jnp.where(better, g, grp)
        best = jnp.where(better, gscore[g], best)
    s_in, p_in = [], []
    for k in range(EXPERTS_PER_GROUP):
        sv, pv = srow[k], prow[k]
        for g in range(1, N_GROUPS):
            sv = jnp.where(grp == g, srow[4 * g + k], sv)
            pv = jnp.where(grp == g, prow[4 * g + k], pv)
        s_in.append(sv)
        p_in.append(pv)
    i1, v1, g1 = jnp.zeros_like(grp), s_in[0], p_in[0]
    for k in range(1, EXPERTS_PER_GROUP):
        better = s_in[k] > v1
        i1 = jnp.where(better, k, i1)
        g1 = jnp.where(better, p_in[k], g1)
        v1 = jnp.where(better, s_in[k], v1)
    i2, v2, g2 = jnp.zeros_like(grp), jnp.full_like(v1, -jnp.inf), jnp.zeros_like(v1)
    for k in range(EXPERTS_PER_GROUP):
        better = (i1 != k) & (s_in[k] > v2)
        i2 = jnp.where(better, k, i2)
        g2 = jnp.where(better, p_in[k], g2)
        v2 = jnp.where(better, s_in[k], v2)
    e1 = grp * EXPERTS_PER_GROUP + i1
    e2 = grp * EXPERTS_PER_GROUP + i2
    tot = g1 + g2
    e_ref[0:1, :] = e1
    e_ref[1:2, :] = e2
    gate_ref[0:1, :] = g1 / tot
    gate_ref[1:2, :] = g2 / tot
    eid = lax.broadcasted_iota(I32, scores.shape, 0)
    hot1 = eid == e1
    hot2 = eid == e2
    onehot = jnp.where(hot1 | hot2, 1.0, 0.0).astype(BF16)
    cum = jnp.dot(onehot, tri_ref[...], preferred_element_type=F32)
    rank_ref[0:1, :] = jnp.sum(jnp.where(hot1, cum, 0.0), axis=0, keepdims=True).astype(I32) - 1
    rank_ref[1:2, :] = jnp.sum(jnp.where(hot2, cum, 0.0), axis=0, keepdims=True).astype(I32) - 1
    cnt_ref[0] = jnp.broadcast_to(cum[:, cum.shape[1] - 1:], cnt_ref.shape[1:])


def _route_io(router_w, router_b, N, D, tm):
    rw = jnp.zeros((D, LANES), F32).at[:, :N_EXPERTS].set(router_w.astype(F32))
    rb = router_b.astype(F32).reshape(N_EXPERTS, 1)
    tri = jnp.asarray(np.triu(np.ones((tm, tm), np.float32)), BF16)
    in_specs = [pl.BlockSpec((D, LANES), lambda t: (0, 0)), pl.BlockSpec((N_EXPERTS, 1), lambda t: (0, 0)),
                pl.BlockSpec((tm, tm), lambda t: (0, 0))]
    lane_row = lambda dt: (pl.BlockSpec((2, tm), lambda t: (0, t)), jax.ShapeDtypeStruct((2, N), dt))
    outs = [lane_row(I32), lane_row(F32), lane_row(I32),
            (pl.BlockSpec((1, N_EXPERTS, LANES), lambda t: (t, 0, 0)),
             jax.ShapeDtypeStruct((N // tm, N_EXPERTS, LANES), F32))]
    return [rw, rb, tri], in_specs, [o[0] for o in outs], [o[1] for o in outs]


def _chunk_tables(lo, go, rlen, tm):
    n_cls = (tm // RUN_ALIGN).bit_length()
    units = rlen // RUN_ALIGN
    cls = jnp.arange(n_cls, dtype=I32)
    flag = (units[:, :, None] >> cls) & 1
    rows = flag * (RUN_ALIGN << cls)
    above = jnp.cumsum(rows[..., ::-1], axis=-1)[..., ::-1] - rows
    src = lo[:, :, None] + above
    dst = go[:, :, None] + above
    slot = jnp.cumsum(flag, axis=1) - 1
    hit = (flag[:, None] == 1) & (slot[:, None] == jnp.arange(N_EXPERTS, dtype=I32)[None, :, None, None])
    compact = lambda a: jnp.sum(jnp.where(hit, a[:, None], 0), axis=2).transpose(0, 2, 1)
    flat = lambda a: a.reshape(-1).astype(I32)
    return (flat(compact(src)), flat(compact(dst)), flat(jnp.sum(flag, axis=1)), flat(jnp.sum(units, axis=1))), n_cls


def _start_pieces(tabs, tile, n_cls, make_copy, wait=False):
    src_ref, dst_ref, cnt_ref = tabs[:3]
    for b in range(n_cls):
        base = (tile * n_cls + b) * N_EXPERTS

        def body(i, carry, base=base, rows=RUN_ALIGN << b):
            copy = make_copy(pl.multiple_of(src_ref[base + i], RUN_ALIGN),
                             pl.multiple_of(dst_ref[base + i], RUN_ALIGN), rows)
            copy.wait() if wait else copy.start()
            return carry

        lax.fori_loop(0, cnt_ref[tile * n_cls + b], body, 0)


def _await_pieces(tabs, tile, max_rows, make_copy):
    units = tabs[3][tile]
    for b in range((max_rows // RUN_ALIGN).bit_length()):
        @pl.when(((units >> b) & 1) != 0)
        def _(rows=RUN_ALIGN << b):
            make_copy(0, 0, rows).wait()


def _dispatch_kernel(src_tab, dst_tab, cnt_tab, tot_tab, gap_src, gap_dst, gap_cnt, f_ref, pos_ref, gates_ref,
                     buf_hbm, srt_ref, zero_ref, sems, *, n_tiles, n_cls, gap_cls):
    t = pl.program_id(0)
    gaps = (gap_src, gap_dst, gap_cnt)
    gap_copy = lambda lo, go, rows: pltpu.make_async_copy(
        zero_ref.at[pl.ds(lo, rows)], buf_hbm.at[pl.ds(go, rows)], sems.at[2])

    @pl.when(t == 0)
    def _():
        zero_ref[...] = jnp.zeros_like(zero_ref)
        _start_pieces(gaps, 0, gap_cls, gap_copy)
    slot = t % 2
    tm, D = f_ref.shape
    ls = srt_ref.shape[1]
    j = lax.broadcasted_iota(I32, (ls, tm), 0)
    lane = lax.broadcasted_iota(I32, (tm, LANES), 1)
    perms, gcols = [], jnp.zeros((ls, LANES), F32)
    for k in range(2):
        perm = jnp.where(pos_ref[k:k + 1, :] == j, 1.0, 0.0).astype(BF16)
        g = gates_ref[:, k:k + 1]
        g_hi = g.astype(BF16).astype(F32)
        g_lo = (g - g_hi).astype(BF16).astype(F32)
        pieces = jnp.where(lane == 0, g_hi, jnp.where(lane == 1, g_lo, 0.0)).astype(BF16)
        gcols = gcols + jnp.dot(perm, pieces, preferred_element_type=F32)
        perms.append(perm)
    srt_ref[slot, :, :D] = jnp.dot(perms[0] + perms[1], f_ref[...], preferred_element_type=F32).astype(BF16)
    srt_ref[slot, :, D:] = gcols.astype(BF16)

    tabs = (src_tab, dst_tab, cnt_tab, tot_tab)

    def copier(buf_slot):
        return lambda lo, go, rows: pltpu.make_async_copy(
            srt_ref.at[buf_slot, pl.ds(lo, rows)], buf_hbm.at[pl.ds(go, rows)], sems.at[buf_slot])

    _start_pieces(tabs, t, n_cls, copier(slot))

    @pl.when(t > 0)
    def _():
        _await_pieces(tabs, t - 1, ls, copier(1 - slot))

    @pl.when(t == n_tiles - 1)
    def _():
        _await_pieces(tabs, t, ls, copier(slot))
        _start_pieces(gaps, 0, gap_cls, gap_copy, wait=True)


def _dispatch(f, pos, gates_t, tabs, n_cls, gap_tabs, gap_cls, n_rows, n_tiles, tm):
    D = f.shape[1]
    ls = 2 * tm + N_EXPERTS * RUN_ALIGN
    width = D + LANES
    grid_spec = pltpu.PrefetchScalarGridSpec(
        num_scalar_prefetch=7,
        grid=(n_tiles,),
        in_specs=[pl.BlockSpec((tm, D), lambda t, *_: (t, 0)), pl.BlockSpec((2, tm), lambda t, *_: (0, t)),
                  pl.BlockSpec((tm, 2), lambda t, *_: (t, 0))],
        out_specs=pl.BlockSpec(memory_space=pl.ANY),
        scratch_shapes=[pltpu.VMEM((2, ls, width), BF16), pltpu.VMEM((EXPERT_BLOCK, width), BF16),
                        pltpu.SemaphoreType.DMA((3,))],
    )
    return pl.pallas_call(
        functools.partial(_dispatch_kernel, n_tiles=n_tiles, n_cls=n_cls, gap_cls=gap_cls),
        grid_spec=grid_spec,
        out_shape=jax.ShapeDtypeStruct((n_rows, width), BF16),
        compiler_params=_cparams("arbitrary"),
        name="moe_dispatch",
    )(*tabs, *gap_tabs[:3], f, pos, gates_t)


def _expert_kernel(be_ref, bc_ref, bs_ref, x_ref, wgu_ref, wd_ref, y_ref):
    i = pl.program_id(0)
    D = y_ref.shape[1]

    @pl.when(bc_ref[i] > 0)
    def _():
        hu = jnp.dot(x_ref[:, :D], wgu_ref[0], preferred_element_type=F32)
        de = hu.shape[1] // 2
        gate, up = hu[:, :de], hu[:, de:]
        hid = (gate * jax.nn.sigmoid(gate) * up).astype(BF16)
        pieces = x_ref[:, D:].astype(F32)
        route_gate = pieces[:, 0:1] + pieces[:, 1:2]
        y_ref[...] = (jnp.dot(hid, wd_ref[0], preferred_element_type=F32) * route_gate).astype(BF16)

    @pl.when(bc_ref[i] == 0)
    def _():
        y_ref[...] = jnp.zeros_like(y_ref)


def _experts(buf, block_expert, block_count, block_src, wgu, wd):
    n_rows = buf.shape[0]
    D = wgu.shape[1]
    nb = n_rows // EXPERT_BLOCK
    grid_spec = pltpu.PrefetchScalarGridSpec(
        num_scalar_prefetch=3,
        grid=(nb,),
        in_specs=[pl.BlockSpec((EXPERT_BLOCK, buf.shape[1]), lambda i, be, bc, bs: (bs[i], 0)),
                  pl.BlockSpec((1,) + wgu.shape[1:], lambda i, be, bc, bs: (be[i], 0, 0)),
                  pl.BlockSpec((1,) + wd.shape[1:], lambda i, be, bc, bs: (be[i], 0, 0))],
        out_specs=pl.BlockSpec((EXPERT_BLOCK, D), lambda i, be, bc, bs: (i, 0)),
    )
    return pl.pallas_call(
        _expert_kernel,
        grid_spec=grid_spec,
        out_shape=jax.ShapeDtypeStruct((n_rows, D), BF16),
        compiler_params=_cparams("arbitrary"),
        name="moe_experts",
    )(block_expert, block_count, block_src, buf, wgu, wd)


def _combine_kernel(src_tab, dst_tab, cnt_tab, tot_tab, y_hbm, x_ref, pos_ref, gate_ref, fin_ref, xo_ref,
                    srt_ref, sems, *, final, n_tiles, n_cls):
    t = pl.program_id(0)
    slot = t % 2
    tm = x_ref.shape[0]
    ls = srt_ref.shape[1]

    tabs = (src_tab, dst_tab, cnt_tab, tot_tab)

    def copier(buf_slot):
        return lambda lo, go, rows: pltpu.make_async_copy(
            y_hbm.at[pl.ds(go, rows)], srt_ref.at[buf_slot, pl.ds(lo, rows)], sems.at[buf_slot])

    @pl.when(t == 0)
    def _():
        srt_ref[...] = jnp.zeros_like(srt_ref)
        _start_pieces(tabs, t, n_cls, copier(slot))

    @pl.when(t + 1 < n_tiles)
    def _():
        _start_pieces(tabs, t + 1, n_cls, copier(1 - slot))

    _await_pieces(tabs, t, ls, copier(slot))
    j = lax.broadcasted_iota(I32, (tm, ls), 1)
    pick = jnp.where((pos_ref[:, 0:1] == j) | (pos_ref[:, 1:2] == j), 1.0, 0.0).astype(BF16)
    y = jnp.dot(pick, srt_ref[slot], preferred_element_type=F32)
    xn = x_ref[...] + gate_ref[0, 0] * y
    if final:
        xn = xn * lax.rsqrt(jnp.mean(xn * xn, axis=-1, keepdims=True) + RMS_EPS) * fin_ref[...]
    xo_ref[...] = xn


def _combine(y_buf, pos_t, tabs, n_cls, xs, mod, final_gain, n_tiles, tm, tiles_per_seq, final):
    N, D = xs.shape
    nmod = mod.shape[1]
    ls = 2 * tm + N_EXPERTS * RUN_ALIGN
    out_rows = n_tiles * tm if final else N
    kwargs = {} if final else {"input_output_aliases": {5: 0}}
    grid_spec = pltpu.PrefetchScalarGridSpec(
        num_scalar_prefetch=4,
        grid=(n_tiles,),
        in_specs=[pl.BlockSpec(memory_space=pl.ANY),
                  pl.BlockSpec((tm, D), lambda t, *_: (t, 0)),
                  pl.BlockSpec((tm, 2), lambda t, *_: (t, 0)),
                  pl.BlockSpec((1, 1, 1, D), lambda t, *_: (5, jnp.minimum(t // tiles_per_seq, nmod - 1), 0, 0)),
                  pl.BlockSpec((1, D), lambda t, *_: (0, 0))],
        out_specs=pl.BlockSpec((tm, D), lambda t, *_: (t, 0)),
        scratch_shapes=[pltpu.VMEM((2, ls, D), BF16), pltpu.SemaphoreType.DMA((2,))],
    )
    return pl.pallas_call(
        functools.partial(_combine_kernel, final=final, n_tiles=n_tiles, n_cls=n_cls),
        grid_spec=grid_spec,
        out_shape=jax.ShapeDtypeStruct((out_rows, D), F32),
        compiler_params=_cparams("arbitrary"),
        name="moe_combine_final" if final else "moe_combine",
        **kwargs,
    )(*tabs, y_buf, xs, pos_t, mod, final_gain.reshape(1, D))


def _moe(f, routing, xs, mod, wgu, wd, final_gain, n_tok, tm, tiles_per_seq, final):
    nt = n_tok // tm
    e, gates, rank, cnt = routing
    e, gates, rank, cnt = e[:, :n_tok], gates[:, :n_tok], rank[:, :n_tok], cnt[:nt]
    n = cnt[:, :, 0].astype(I32)
    rlen = (n + RUN_ALIGN - 1) // RUN_ALIGN * RUN_ALIGN
    lo = jnp.cumsum(rlen, axis=1) - rlen
    region = jnp.sum(rlen, axis=0)
    region_pad = (region + EXPERT_BLOCK - 1) // EXPERT_BLOCK * EXPERT_BLOCK
    pends = jnp.cumsum(region_pad)
    pstarts = pends - region_pad
    go = pstarts[None, :] + jnp.cumsum(rlen, axis=0) - rlen
    n_blocks = -(-(2 * n_tok + nt * N_EXPERTS * RUN_ALIGN) // EXPERT_BLOCK) + N_EXPERTS
    blk0 = jnp.arange(n_blocks, dtype=I32) * EXPERT_BLOCK
    block_expert = jnp.minimum(jnp.sum((blk0[:, None] >= pends[None, :]).astype(I32), axis=1), N_EXPERTS - 1)
    block_used = (blk0 < (pstarts + region)[block_expert]).astype(I32)
    hot = e.reshape(2, nt, tm, 1) == jnp.arange(N_EXPERTS, dtype=I32)
    pos = jnp.sum(jnp.where(hot, lo[None, :, None, :], 0), axis=-1).reshape(2, n_tok) + rank
    tabs, n_cls = _chunk_tables(lo, go, rlen, tm)
    gap_tabs, gap_cls = _chunk_tables(jnp.zeros((1, N_EXPERTS), I32), (pstarts + region)[None, :],
                                      (region_pad - region)[None, :], EXPERT_BLOCK)
    buf = _dispatch(f, pos, gates.T, tabs, n_cls, gap_tabs, gap_cls, n_blocks * EXPERT_BLOCK, nt, tm)
    last_used = jnp.max(jnp.where(block_used > 0, jnp.arange(n_blocks, dtype=I32), 0))
    block_src = jnp.where(block_used > 0, jnp.arange(n_blocks, dtype=I32), last_used)
    y_buf = _experts(buf, block_expert.astype(I32), block_used, block_src, wgu, wd)
    return _combine(y_buf, pos.T, tabs, n_cls, xs, mod, final_gain, nt, tm, tiles_per_seq, final)


def _rope_tables(S, tm):
    rows = S // GRID_W
    row = jnp.repeat(jnp.arange(rows, dtype=F32), GRID_W)
    col = jnp.tile(jnp.arange(GRID_W, dtype=F32), rows)
    half = ATTN_HEAD_DIM // 4
    inv_freq = ROPE_THETA ** (-jnp.arange(half, dtype=F32) / half)
    ang_r = row[:, None] * inv_freq[None, :]
    ang_c = col[:, None] * inv_freq[None, :]
    zeros = jnp.zeros_like(ang_r)
    cos = jnp.concatenate([jnp.cos(ang_r)] * 2 + [jnp.cos(ang_c)] * 2, axis=1)
    s1 = jnp.concatenate([-jnp.sin(ang_r), zeros, -jnp.sin(ang_c), zeros], axis=1)
    s2 = jnp.concatenate([zeros, jnp.sin(ang_r), zeros, jnp.sin(ang_c)], axis=1)
    def finish(tab, fill):
        tab = jnp.tile(tab, (1, LANES // ATTN_HEAD_DIM))
        return jnp.concatenate([tab, jnp.full((tm, LANES), fill, F32)], axis=0)
    return finish(cos, 1.0), finish(s1, 0.0), finish(s2, 0.0)


def kernel(x, c, ctx, c_ctx, ada_w, ada_b, norm_mix_g, norm_ffn_g, final_g, attn_w_qkv, attn_w_o, attn_sinks,
           gla_w_in, gla_w_a1, gla_w_a2, gla_b_a, gla_norm_g, gla_w_o, router_w, router_b,
           moe_w_gate, moe_w_up, moe_w_down):
    B, S, D = x.shape
    C = ctx.shape[1]
    depth = ada_w.shape[0]
    tm = TOKEN_TILE
    assert S % tm == 0 and (B * C) % tm == 0 and S % ATTN_BLOCK == 0 and C % ATTN_BLOCK == 0
    assert (B * S) % C == 0 and S % GLA_CHUNK == 0 and C % GLA_CHUNK == 0
    n_lat = B * S
    n_lat_tiles = n_lat // tm
    tiles_per_seq = S // tm

    rpad = -(-(B + 1) // 8) * 8
    cc = jnp.zeros((rpad, D), F32).at[:B].set(c).at[B].set(c_ctx)
    mods = _ada_table(cc, ada_w, ada_b)
    mods = mods[:, :B + 1].reshape(depth, B + 1, 6, 1, D).transpose(0, 2, 1, 3, 4)

    xs = (x.reshape(n_lat, D), ctx.reshape(B * C, D))
    rope = _rope_tables(S, tm)
    q_dim = ATTN_KV_HEADS * ATTN_GROUP * ATTN_HEAD_DIM
    kv_dim = ATTN_KV_HEADS * ATTN_HEAD_DIM
    kd = GLA_HEADS * GLA_KEY_DIM
    vd = GLA_HEADS * GLA_VAL_DIM

    def dup_heads(w):
        w = w.reshape(D, ATTN_KV_HEADS, 1, ATTN_HEAD_DIM)
        return jnp.broadcast_to(w, (D, ATTN_KV_HEADS, LANES // ATTN_HEAD_DIM, ATTN_HEAD_DIM)).reshape(D, -1)

    for i in range(depth):
        last = i == depth - 1
        mod = mods[i]
        j = i // 2
        if i % 2 == 0:
            wqkv = attn_w_qkv[j]
            w = jnp.concatenate([wqkv[:, :q_dim], dup_heads(wqkv[:, q_dim:q_dim + kv_dim])], axis=1).astype(BF16)
            wvt = wqkv[:, q_dim + kv_dim:].T.astype(BF16)
            q, k, vt = _qkv_proj(xs, norm_mix_g[i], mod, 0, w, wvt, rope, tm, n_lat_tiles, tiles_per_seq)
            sink_row = jnp.repeat(attn_sinks[j].astype(F32).reshape(ATTN_KV_HEADS, ATTN_GROUP) * LOG2_E,
                                  ATTN_BLOCK, axis=1)
            o = _attention(q, k, vt, sink_row[:, None, :], B, S, C, not last)
            xs, f, *routing = _post_mixer(o, attn_w_o[j].astype(BF16), xs, mod, norm_ffn_g[i], router_w, router_b,
                                          tm, n_lat_tiles, tiles_per_seq)
        else:
            a1 = jnp.zeros((D, LANES), F32).at[:, :2 * GLA_GATE_RANK].set(
                jnp.concatenate([gla_w_a1[j, 0], gla_w_a1[j, 1]], axis=1))
            w = jnp.concatenate([gla_w_in[j], a1], axis=1).astype(BF16)
            w2 = jnp.zeros((LANES, 2 * kd), F32)
            w2 = w2.at[:GLA_GATE_RANK, :kd].set(gla_w_a2[j, 0]).at[GLA_GATE_RANK:2 * GLA_GATE_RANK, kd:].set(gla_w_a2[j, 1])
            ba = gla_b_a[j].reshape(1, 2 * kd).astype(F32)
            qk, v, og, la, chunk_tot = _gla_in_proj(xs, norm_mix_g[i], mod, 0, w, w2.astype(BF16), ba, tm,
                                                    n_lat_tiles, tiles_per_seq)
            o_fwd, o_bwd = _gla_scan(qk, v, la, chunk_tot, B, S, C)
            xs, f, *routing = _post_mixer(o_fwd, gla_w_o[j].astype(BF16), xs, mod, norm_ffn_g[i], router_w, router_b,
                                          tm, n_lat_tiles, tiles_per_seq,
                                          gla_extra=(o_bwd, og, gla_norm_g[j].reshape(1, GLA_VAL_DIM).astype(F32)))
        wgu = jnp.concatenate([moe_w_gate[i], moe_w_up[i]], axis=2).astype(BF16)
        wd = moe_w_down[i].astype(BF16)
        n_tok = n_lat if last else n_lat + B * C
        xs = _moe(f, routing, xs, mod, wgu, wd, final_g, n_tok, tm, tiles_per_seq, last)
    return xs.reshape(B, S, D)
```

```python
import functools

import numpy as np
import jax
import jax.numpy as jnp
from jax import lax
from jax.experimental import pallas as pl
from jax.experimental.pallas import tpu as pltpu

F32 = jnp.float32
BF16 = jnp.bfloat16
I32 = jnp.int32

LANES = 128
VMEM_LIMIT_BYTES = 56 * 1024 * 1024

RMS_EPS = 1e-6
GRID_W = 64
ROPE_THETA = 10000.0
ATTN_HEAD_DIM = 64
ATTN_KV_HEADS = 4
ATTN_GROUP = 4
ATTN_BLOCK = 128
LOG2_E = 1.4426950408889634
ATTN_Q_SCALE = ATTN_HEAD_DIM ** -0.5 * LOG2_E
GLA_HEADS = 4
GLA_KEY_DIM = 128
GLA_VAL_DIM = 256
GLA_GATE_RANK = 16
GLA_GATE_NORM = 16.0
GLA_CHUNK = 128
GLA_STAT_ROWS = 64
GLA_BOUNDED_TOTAL = 40.0
N_EXPERTS = 16
N_GROUPS = 4
EXPERTS_PER_GROUP = 4
EXPERT_BLOCK = 512
RUN_ALIGN = 16
TOKEN_TILE = 512
PROJ_TILE = 1024


def _cparams(*sem):
    return pltpu.CompilerParams(dimension_semantics=sem, vmem_limit_bytes=VMEM_LIMIT_BYTES)


def _norm_mod(x, gain, shift, scale):
    h = x * lax.rsqrt(jnp.mean(x * x, axis=-1, keepdims=True) + RMS_EPS) * gain
    return h * (1.0 + scale) + shift


def _ada_kernel(c_ref, w_ref, b_ref, o_ref):
    c = c_ref[...]
    s = (c * jax.nn.sigmoid(c)).astype(BF16)
    o_ref[0] = jnp.dot(s, w_ref[0].astype(BF16), preferred_element_type=F32) + b_ref[0]


def _ada_table(cc, ada_w, ada_b):
    L, D, D6 = ada_w.shape
    R = cc.shape[0]
    tn = 1536
    return pl.pallas_call(
        _ada_kernel,
        grid=(L, D6 // tn),
        in_specs=[pl.BlockSpec((R, D), lambda l, j: (0, 0)),
                  pl.BlockSpec((1, D, tn), lambda l, j: (l, 0, j)),
                  pl.BlockSpec((1, 1, tn), lambda l, j: (l, 0, j))],
        out_specs=pl.BlockSpec((1, R, tn), lambda l, j: (l, 0, j)),
        out_shape=jax.ShapeDtypeStruct((L, R, D6), F32),
        compiler_params=_cparams("parallel", "parallel"),
        name="ada_table",
    )(cc, ada_w, ada_b.reshape(L, 1, D6))


def _stream_tile(x_refs, n_lat_tiles):
    if len(x_refs) == 1:
        return x_refs[0][...]
    return jnp.where(pl.program_id(0) < n_lat_tiles, x_refs[0][...], x_refs[1][...])


def _stream_args(xs, tm, n_lat_tiles):
    if isinstance(xs, tuple):
        D = xs[0].shape[1]
        return list(xs), [pl.BlockSpec((tm, D), lambda t: (jnp.minimum(t, n_lat_tiles - 1), 0)),
                          pl.BlockSpec((tm, D), lambda t: (jnp.maximum(t - n_lat_tiles, 0), 0))]
    return [xs], [pl.BlockSpec((tm, xs.shape[1]), lambda t: (t, 0))]


def _qkv_kernel(*refs, n_x, n_lat_tiles):
    x_refs = refs[:n_x]
    g_ref, sh_ref, sc_ref, w_ref, wvt_ref, cos_ref, s1_ref, s2_ref, q_ref, k_ref, vt_ref = refs[n_x:]
    h = _norm_mod(_stream_tile(x_refs, n_lat_tiles), g_ref[...], sh_ref[0, 0], sc_ref[0, 0]).astype(BF16)
    z = jnp.dot(h, w_ref[...], preferred_element_type=F32)
    vt_ref[...] = lax.dot_general(wvt_ref[...], h, (((1,), (1,)), ((), ())), preferred_element_type=F32).astype(BF16)
    cos, s1, s2 = cos_ref[...], s1_ref[...], s2_ref[...]
    nq = q_ref.shape[1] // LANES
    nk = k_ref.shape[1] // LANES
    for j in range(nq + nk):
        zc = z[:, j * LANES:(j + 1) * LANES]
        r = zc * cos + pltpu.roll(zc, LANES - 16, 1) * s1 + pltpu.roll(zc, 16, 1) * s2
        if j < nq:
            q_ref[:, j * LANES:(j + 1) * LANES] = (r * ATTN_Q_SCALE).astype(BF16)
        else:
            k_ref[:, (j - nq) * LANES:(j - nq + 1) * LANES] = r.astype(BF16)


def _gla_in_kernel(x_ref, g_ref, sh_ref, sc_ref, w_ref, w2_ref, ba_ref, qk_ref, v_ref, og_ref, la_ref, tot_ref):
    h = _norm_mod(x_ref[...], g_ref[...], sh_ref[0, 0], sc_ref[0, 0]).astype(BF16)
    kd = GLA_HEADS * GLA_KEY_DIM
    vd = GLA_HEADS * GLA_VAL_DIM
    a1 = jnp.dot(h, w_ref[:, 2 * kd + 2 * vd:], preferred_element_type=F32).astype(BF16)
    pre = jnp.dot(a1, w2_ref[...], preferred_element_type=F32) + ba_ref[...]
    z = jnp.dot(h, w_ref[:, :2 * kd + 2 * vd], preferred_element_type=F32)
    la = (jnp.minimum(pre, 0.0) - jnp.log1p(jnp.exp(-jnp.abs(pre)))) * (1.0 / GLA_GATE_NORM)
    la_ref[...] = la
    qk_ref[:, :kd] = (z[:, :kd] * (GLA_KEY_DIM ** -0.5)).astype(BF16)
    qk_ref[:, kd:] = z[:, kd:2 * kd].astype(BF16)
    v_ref[...] = z[:, 2 * kd:2 * kd + vd].astype(BF16)
    og_ref[...] = z[:, 2 * kd + vd:2 * kd + 2 * vd].astype(BF16)
    nc = la.shape[0] // GLA_STAT_ROWS
    tot = jnp.sum(la.reshape(nc, GLA_STAT_ROWS, la.shape[1]), axis=1)
    lane = lax.broadcasted_iota(I32, (nc, LANES), 1)
    acc = jnp.zeros((nc, LANES), F32)
    for hd in range(2 * GLA_HEADS):
        worst = jnp.min(tot[:, hd * GLA_KEY_DIM:(hd + 1) * GLA_KEY_DIM], axis=1, keepdims=True)
        acc = jnp.where(lane == hd, worst, acc)
    tot_ref[...] = acc


def _tile_specs(D, tm, n_lat_tiles, tiles_per_seq, n_mod_rows):
    def mod_idx(which):
        return lambda t: (which, jnp.minimum(t // tiles_per_seq, n_mod_rows - 1), 0, 0)
    return mod_idx, [pl.BlockSpec((tm, D), lambda t: (t, 0)),
                     pl.BlockSpec((1, D), lambda t: (0, 0))]


def _qkv_proj(xs, gain, mod, which, w, wvt, rope, tm, n_lat_tiles, tiles_per_seq):
    x_args, x_specs = _stream_args(xs, tm, n_lat_tiles)
    N = sum(a.shape[0] for a in x_args)
    D = x_args[0].shape[1]
    nmod = mod.shape[1]
    mod_idx, specs = _tile_specs(D, tm, n_lat_tiles, tiles_per_seq, nmod)
    qd = ATTN_KV_HEADS * ATTN_GROUP * ATTN_HEAD_DIM
    kd = ATTN_KV_HEADS * LANES
    vd = wvt.shape[0]
    rope_idx = lambda t: (jnp.where(t < n_lat_tiles, t % tiles_per_seq, tiles_per_seq), 0)
    in_specs = x_specs + specs[1:] + [
        pl.BlockSpec((1, 1, 1, D), mod_idx(which)),
        pl.BlockSpec((1, 1, 1, D), mod_idx(which + 1)),
        pl.BlockSpec(w.shape, lambda t: (0, 0)),
        pl.BlockSpec(wvt.shape, lambda t: (0, 0)),
        pl.BlockSpec((tm, LANES), rope_idx),
        pl.BlockSpec((tm, LANES), rope_idx),
        pl.BlockSpec((tm, LANES), rope_idx),
    ]
    return pl.pallas_call(
        functools.partial(_qkv_kernel, n_x=len(x_args), n_lat_tiles=n_lat_tiles),
        grid=(N // tm,),
        in_specs=in_specs,
        out_specs=[pl.BlockSpec((tm, qd), lambda t: (t, 0)),
                   pl.BlockSpec((tm, kd), lambda t: (t, 0)),
                   pl.BlockSpec((vd, tm), lambda t: (0, t))],
        out_shape=[jax.ShapeDtypeStruct((N, qd), BF16),
                   jax.ShapeDtypeStruct((N, kd), BF16),
                   jax.ShapeDtypeStruct((vd, N), BF16)],
        compiler_params=_cparams("parallel"),
        name="attn_qkv_proj",
    )(*x_args, gain.reshape(1, D), mod, mod, w, wvt, *rope)


def _gla_in_proj(xs, gain, mod, which, w, w2, ba, tm, n_lat_tiles, tiles_per_seq):
    N, D = xs.shape
    nmod = mod.shape[1]
    mod_idx, specs = _tile_specs(D, tm, n_lat_tiles, tiles_per_seq, nmod)
    kd = GLA_HEADS * GLA_KEY_DIM
    vd = GLA_HEADS * GLA_VAL_DIM
    in_specs = specs + [
        pl.BlockSpec((1, 1, 1, D), mod_idx(which)),
        pl.BlockSpec((1, 1, 1, D), mod_idx(which + 1)),
        pl.BlockSpec(w.shape, lambda t: (0, 0)),
        pl.BlockSpec(w2.shape, lambda t: (0, 0)),
        pl.BlockSpec((1, 2 * kd), lambda t: (0, 0)),
    ]
    row = lambda width: pl.BlockSpec((tm, width), lambda t: (t, 0))
    return pl.pallas_call(
        _gla_in_kernel,
        grid=(N // tm,),
        in_specs=in_specs,
        out_specs=[row(2 * kd), row(vd), row(vd), row(2 * kd),
                   pl.BlockSpec((tm // GLA_STAT_ROWS, LANES), lambda t: (t, 0))],
        out_shape=[jax.ShapeDtypeStruct((N, 2 * kd), BF16),
                   jax.ShapeDtypeStruct((N, vd), BF16),
                   jax.ShapeDtypeStruct((N, vd), BF16),
                   jax.ShapeDtypeStruct((N, 2 * kd), F32),
                   jax.ShapeDtypeStruct((N // GLA_STAT_ROWS, LANES), F32)],
        compiler_params=_cparams("parallel"),
        name="gla_in_proj",
    )(xs, gain.reshape(1, D), mod, mod, w, w2, ba)


def _attn_kernel(*refs, window, nq):
    if window:
        q_ref, kp, kc, kn, kx, vp, vc, vn, vx, sink_ref, tri_ref, o_ref = refs
        k_parts, v_parts = (kp, kc, kn, kx), (vp, vc, vn, vx)
    else:
        q_ref, kx, vx, sink_ref, _, o_ref = refs
        k_parts, v_parts = (kx,), (vx,)
    tq = q_ref.shape[0]
    lane = lax.broadcasted_iota(I32, (tq, LANES), 1)
    first_head = lane < ATTN_HEAD_DIM
    if window:
        j = pl.program_id(1)
        bias_prev = jnp.where(j > 0, tri_ref[0], -jnp.inf)
        bias_next = jnp.where(j < nq - 1, tri_ref[1], -jnp.inf)
        bias_prev = jnp.concatenate([bias_prev] * ATTN_GROUP, axis=1)
        bias_next = jnp.concatenate([bias_next] * ATTN_GROUP, axis=1)

    def scores(kh):
        ks = slice(kh * LANES, (kh + 1) * LANES)
        kk = jnp.concatenate([p[:, ks] for p in k_parts], axis=0) if window else kx[:, ks]
        qa = q_ref[:, (2 * kh) * LANES:(2 * kh + 1) * LANES]
        qb = q_ref[:, (2 * kh + 1) * LANES:(2 * kh + 2) * LANES]
        zero = jnp.zeros_like(qa)
        qs = jnp.concatenate([jnp.where(first_head, qa, zero), jnp.where(first_head, zero, qa),
                              jnp.where(first_head, qb, zero), jnp.where(first_head, zero, qb)], axis=0)
        return lax.dot_general(kk, qs, (((1,), (1,)), ((), ())), preferred_element_type=F32)

    def softmax(s, kh):
        if window:
            s = jnp.concatenate([s[:tq] + bias_prev, s[tq:2 * tq], s[2 * tq:3 * tq] + bias_next, s[3 * tq:]], axis=0)
        sink = sink_ref[kh]
        m = jnp.maximum(jnp.max(s, axis=0, keepdims=True), sink)
        p = jnp.exp2(s - m)
        l = jnp.sum(p, axis=0, keepdims=True) + jnp.exp2(sink - m)
        return p.astype(BF16), l

    def values(p, l, kh):
        hs = slice(kh * ATTN_HEAD_DIM, (kh + 1) * ATTN_HEAD_DIM)
        vv = jnp.concatenate([part[hs, :] for part in v_parts], axis=1) if window else vx[hs, :]
        o = jnp.dot(vv, p, preferred_element_type=F32) / l
        for g in range(ATTN_GROUP):
            h = kh * ATTN_GROUP + g
            o_ref[h * ATTN_HEAD_DIM:(h + 1) * ATTN_HEAD_DIM, :] = o[:, g * tq:(g + 1) * tq].astype(BF16)

    s_next = scores(0)
    pending = None
    for kh in range(ATTN_KV_HEADS):
        s_cur = s_next
        if kh + 1 < ATTN_KV_HEADS:
            s_next = scores(kh + 1)
        if pending is not None:
            values(*pending)
        pending = softmax(s_cur, kh) + (kh,)
    values(*pending)


def _attention(q, k, vt, sink_row, B, S, C, need_ctx):
    N, qd = q.shape
    kd = k.shape[1]
    vd = vt.shape[0]
    tq = ATTN_BLOCK
    nq = S // tq
    ctx_blk0 = (B * S) // C
    qmap = lambda b, j: (b * nq + j, 0)
    prev = lambda b, j: (b * nq + jnp.maximum(j - 1, 0), 0)
    nxt = lambda b, j: (b * nq + jnp.minimum(j + 1, nq - 1), 0)
    cmap = lambda b, j: (ctx_blk0 + b, 0)
    cols = lambda m: (lambda b, j: m(b, j)[::-1])
    kblk = lambda m: pl.BlockSpec((tq, kd), m)
    vblk = lambda m: pl.BlockSpec((vd, tq), cols(m))
    kctx = pl.BlockSpec((C, kd), cmap)
    vctx = pl.BlockSpec((vd, C), cols(cmap))
    sink_spec = pl.BlockSpec(sink_row.shape, lambda b, j: (0, 0, 0))
    key = np.arange(tq)[:, None]
    qry = np.arange(tq)[None, :]
    tri = jnp.asarray(np.stack([np.where(key >= qry, 0.0, -np.inf), np.where(key <= qry, 0.0, -np.inf)]), F32)
    o_lat = pl.pallas_call(
        functools.partial(_attn_kernel, window=True, nq=nq),
        grid=(B, nq),
        in_specs=[pl.BlockSpec((tq, qd), qmap), kblk(prev), kblk(qmap), kblk(nxt), kctx,
                  vblk(prev), vblk(qmap), vblk(nxt), vctx, sink_spec,
                  pl.BlockSpec(tri.shape, lambda b, j: (0, 0, 0))],
        out_specs=pl.BlockSpec((qd, tq), cols(qmap)),
        out_shape=jax.ShapeDtypeStruct((qd, N), BF16),
        compiler_params=_cparams("parallel", "parallel"),
        name="attn_window",
    )(q, k, k, k, k, vt, vt, vt, vt, sink_row, tri)
    if not need_ctx:
        return o_lat
    ncq = C // tq
    lat_blks = (B * S) // tq
    cq = lambda b, j: (lat_blks + b * ncq + j, 0)
    return pl.pallas_call(
        functools.partial(_attn_kernel, window=False, nq=ncq),
        grid=(B, ncq),
        in_specs=[pl.BlockSpec((tq, qd), cq), kctx, vctx, sink_spec,
                  pl.BlockSpec(memory_space=pl.ANY)],
        out_specs=pl.BlockSpec((qd, tq), cols(cq)),
        out_shape=jax.ShapeDtypeStruct((qd, N), BF16),
        input_output_aliases={4: 0},
        compiler_params=_cparams("parallel", "parallel"),
        name="attn_context",
    )(q, k, vt, sink_row, o_lat)


def _gla_constants(C):
    levels = []
    m = 1
    while m < C:
        levels.append(m)
        m *= 2
    t = np.arange(C)[:, None]
    u = np.arange(C)[None, :]
    secs = [(u <= t), (u > t)]
    masks = []
    for m in levels:
        base = (t // (2 * m)) * (2 * m)
        ref = base + m - 1
        second = t >= base + m
        secs.append(np.where(second, (u > ref) & (u <= t), (u > t) & (u <= ref)))
        masks.append((t // (2 * m) == u // (2 * m)) & second & (u < (u // (2 * m)) * (2 * m) + m))
    masks.append(t == u)
    masks.append(u <= t)
    mf = np.concatenate([s.astype(np.float32) for s in secs], axis=0)
    kf = np.stack([mk.astype(np.float32) for mk in masks], axis=0)
    mb = np.concatenate([s.astype(np.float32)[::-1, ::-1] for s in secs], axis=0)
    kb = np.stack([mk.astype(np.float32)[::-1, ::-1] for mk in masks], axis=0)
    return np.stack([mf, mb]), np.stack([kf, kb]), len(levels)


def _gla_chunk(q, k, v, g, st_ref, mm_ref, mk_ref, d, nl, bounded):
    C, dk = q.shape
    nt = (((1,), (1,)), ((), ()))
    g_hi = g.astype(BF16)
    g_lo = (g - g_hi.astype(F32)).astype(BF16)
    mmat = mm_ref[d, :2 * C] if bounded else mm_ref[d]
    e2 = jnp.dot(mmat, jnp.concatenate([g_hi, g_lo], axis=1), preferred_element_type=F32)
    ee = e2[:, :dk] + e2[:, dk:]
    ex = jnp.exp(ee)
    qf, kf = q.astype(F32), k.astype(F32)
    qe = (qf * ex[:C]).astype(BF16)
    ke = (kf * ex[C:2 * C]).astype(BF16)
    st = st_ref[...]
    o = lax.dot_general(qe, st.astype(BF16), nt, preferred_element_type=F32)
    if bounded:
        ki = (kf * jnp.exp(-ee[:C])).astype(BF16)
        a = mk_ref[d, nl + 1] * lax.dot_general(qe, ki, nt, preferred_element_type=F32)
    else:
        a = mk_ref[d, nl] * lax.dot_general(q, k, nt, preferred_element_type=F32)
        for i in range(nl):
            xl = ex[(2 + i) * C:(3 + i) * C]
            ql = (qf * xl).astype(BF16)
            kl = (kf * xl).astype(BF16)
            a = a + mk_ref[d, i] * lax.dot_general(ql, kl, nt, preferred_element_type=F32)
    o = o + jnp.dot(a.astype(BF16), v, preferred_element_type=F32)
    decay = jnp.exp(jnp.sum(g, axis=0, keepdims=True))
    st_ref[...] = st * decay + lax.dot_general(v, ke, (((0,), (0,)), ((), ())), preferred_element_type=F32)
    return o


def _gla_kernel(okf_ref, okb_ref, qf, kf, vf, lf, qb, kb, vb, lb, mm_ref, mk_ref, of_ref, ob_ref, st_ref,
                *, nl, seg_chunks, lat_segs, ctx_seg0):
    C = GLA_CHUNK
    H, DK, DV = GLA_HEADS, GLA_KEY_DIM, GLA_VAL_DIM
    b, j = pl.program_id(0), pl.program_id(1)

    @pl.when(j == 0)
    def _():
        st_ref[...] = jnp.zeros_like(st_ref)

    seg_f = jnp.where(j == 0, ctx_seg0 + b, b * lat_segs + j - 1)
    seg_b = jnp.where(j == 0, ctx_seg0 + b, b * lat_segs + lat_segs - j)

    def body(i, carry):
        cf, cb = i, seg_chunks - 1 - i
        rf = pl.ds(pl.multiple_of(cf * C, C), C)
        rb = pl.ds(pl.multiple_of(cb * C, C), C)
        bounded = (okf_ref[seg_f * seg_chunks + cf] != 0) & (okb_ref[seg_b * seg_chunks + cb] != 0)

        def advance_general():
            for h in range(H):
                ks, vs = slice(h * DK, (h + 1) * DK), slice(h * DV, (h + 1) * DV)
                o = _gla_chunk(qf[rf, ks], kf[rf, ks], vf[rf, vs], lf[rf, ks], st_ref.at[h],
                               mm_ref, mk_ref, 0, nl, False)
                of_ref[rf, vs] = o.astype(BF16)
                o = _gla_chunk(qb[rb, ks], kb[rb, ks], vb[rb, vs], lb[rb, ks], st_ref.at[H + h],
                               mm_ref, mk_ref, 1, nl, False)
                ob_ref[rb, vs] = o.astype(BF16)

        def advance_bounded():
            nt = (((1,), (1,)), ((), ()))
            kw = H * DK
            sides = ((qf, kf, vf, lf, rf, of_ref, 0), (qb, kb, vb, lb, rb, ob_ref, 1))
            pre = []
            for q_r, k_r, v_r, l_r, rows, o_r, d in sides:
                g = l_r[rows, :]
                g_hi = g.astype(BF16)
                g_lo = (g - g_hi.astype(F32)).astype(BF16)
                e2 = jnp.dot(mm_ref[d, :2 * C], jnp.concatenate([g_hi, g_lo], axis=1),
                             preferred_element_type=F32)
                pre.append((e2[:, :kw] + e2[:, kw:], g))
            units = []
            for (ee, g), (q_r, k_r, v_r, l_r, rows, o_r, d) in zip(pre, sides):
                ex = jnp.exp(ee)
                kf32 = k_r[rows, :].astype(F32)
                qe = (q_r[rows, :].astype(F32) * ex[:C]).astype(BF16)
                ke = (kf32 * ex[C:]).astype(BF16)
                ki = (kf32 * jnp.exp(-ee[:C])).astype(BF16)
                decay = jnp.exp(jnp.sum(g, axis=0, keepdims=True))
                for h in range(H):
                    ks = slice(h * DK, (h + 1) * DK)
                    units.append((d, h, qe[:, ks], ke[:, ks], ki[:, ks], decay[:, ks], v_r, rows, o_r))
            inter, score = [], []
            for d, h, qe, ke, ki, decay, v_r, rows, o_r in units:
                st = st_ref[d * H + h]
                inter.append(lax.dot_general(qe, st.astype(BF16), nt, preferred_element_type=F32))
                score.append(lax.dot_general(qe, ki, nt, preferred_element_type=F32))
            for n, (d, h, qe, ke, ki, decay, v_r, rows, o_r) in enumerate(units):
                vs = slice(h * DV, (h + 1) * DV)
                v = v_r[rows, vs]
                a = (mk_ref[d, nl + 1] * score[n]).astype(BF16)
                o = inter[n] + jnp.dot(a, v, preferred_element_type=F32)
                upd = lax.dot_general(v, ke, (((0,), (0,)), ((), ())), preferred_element_type=F32)
                o_r[rows, vs] = o.astype(BF16)
                st_ref[d * H + h] = st_ref[d * H + h] * decay + upd

        @pl.when(bounded)
        def _():
            advance_bounded()

        @pl.when(jnp.logical_not(bounded))
        def _():
            advance_general()

        return carry

    lax.fori_loop(0, seg_chunks, body, 0)


def _gla_scan(qk, v, la, chunk_tot, B, S, C):
    N = qk.shape[0]
    H, DK, DV = GLA_HEADS, GLA_KEY_DIM, GLA_VAL_DIM
    seg = C
    assert S % seg == 0 and seg % GLA_CHUNK == 0
    lat_segs = S // seg
    ctx_seg0 = (B * S) // seg
    mm, mk, nl = _gla_constants(GLA_CHUNK)
    mm = jnp.asarray(mm, BF16)
    mk = jnp.asarray(mk, F32)
    chunk_tot = jnp.sum(chunk_tot.reshape(-1, GLA_CHUNK // GLA_STAT_ROWS, LANES), axis=1)
    ok = chunk_tot[:, :2 * H] >= -GLA_BOUNDED_TOTAL
    okf = jnp.all(ok[:, :H], axis=1).astype(I32)
    okb = jnp.all(ok[:, H:], axis=1).astype(I32)
    fwd = lambda col: (lambda b, j, *_: (jnp.where(j == 0, ctx_seg0 + b, b * lat_segs + j - 1), col))
    bwd = lambda col: (lambda b, j, *_: (jnp.where(j == 0, ctx_seg0 + b, b * lat_segs + lat_segs - j), col))
    kw, vw = H * DK, H * DV
    in_specs = [
        pl.BlockSpec((seg, kw), fwd(0)), pl.BlockSpec((seg, kw), fwd(1)), pl.BlockSpec((seg, vw), fwd(0)),
        pl.BlockSpec((seg, kw), fwd(0)),
        pl.BlockSpec((seg, kw), bwd(0)), pl.BlockSpec((seg, kw), bwd(1)), pl.BlockSpec((seg, vw), bwd(0)),
        pl.BlockSpec((seg, kw), bwd(1)),
        pl.BlockSpec(mm.shape, lambda b, j, *_: (0, 0, 0)),
        pl.BlockSpec(mk.shape, lambda b, j, *_: (0, 0, 0, 0)),
    ]
    grid_spec = pltpu.PrefetchScalarGridSpec(
        num_scalar_prefetch=2,
        grid=(B, lat_segs + 1),
        in_specs=in_specs,
        out_specs=[pl.BlockSpec((seg, vw), fwd(0)), pl.BlockSpec((seg, vw), bwd(0))],
        scratch_shapes=[pltpu.VMEM((2 * H, DV, DK), F32)],
    )
    return pl.pallas_call(
        functools.partial(_gla_kernel, nl=nl, seg_chunks=seg // GLA_CHUNK, lat_segs=lat_segs, ctx_seg0=ctx_seg0),
        grid_spec=grid_spec,
        out_shape=[jax.ShapeDtypeStruct((N, vw), BF16), jax.ShapeDtypeStruct((N, vw), BF16)],
        compiler_params=_cparams("parallel", "arbitrary"),
        name="gla_scan",
    )(okf, okb, qk, qk, v, la, qk, qk, v, la, mm, mk)


def _post_kernel(*refs, gla, n_x, n_lat_tiles):
    x_refs, refs = refs[:n_x], refs[n_x:]
    route_refs = refs[-4:]
    route_in = refs[-9:-6]
    refs = refs[:-9] + refs[-6:-4]
    x_tile = _stream_tile(x_refs, n_lat_tiles)
    if gla:
        o_ref, ob_ref, og_ref, ng_ref, w_ref, gate_ref, fg_ref, fsh_ref, fsc_ref, xo_ref, f_ref = refs
        o = o_ref[...].astype(F32) + ob_ref[...].astype(F32)
        g = og_ref[...].astype(F32)
        parts = []
        for h in range(GLA_HEADS):
            oh = o[:, h * GLA_VAL_DIM:(h + 1) * GLA_VAL_DIM]
            parts.append(oh * lax.rsqrt(jnp.mean(oh * oh, axis=-1, keepdims=True) + RMS_EPS) * ng_ref[...])
        mix = (jnp.concatenate(parts, axis=1) * (g * jax.nn.sigmoid(g))).astype(BF16)
        y = jnp.dot(mix, w_ref[...], preferred_element_type=F32)
    else:
        o_ref, w_ref, gate_ref, fg_ref, fsh_ref, fsc_ref, xo_ref, f_ref = refs
        y = lax.dot_general(o_ref[...], w_ref[...], (((0,), (0,)), ((), ())), preferred_element_type=F32)
    xn = x_tile + gate_ref[0, 0] * y
    xo_ref[...] = xn
    f = _norm_mod(xn, fg_ref[...], fsh_ref[0, 0], fsc_ref[0, 0])
    f_ref[...] = f.astype(BF16)
    _route_tile(f, *route_in, *route_refs)


def _post_mixer(o, w_o, xs, mod, ffn_gain, router_w, router_b, tm, n_lat_tiles, tiles_per_seq, gla_extra=None):
    args, specs = _stream_args(xs, tm, n_lat_tiles)
    n_x = len(args)
    N = sum(a.shape[0] for a in args)
    D = args[0].shape[1]
    r_args, r_in, r_out, r_shapes = _route_io(router_w, router_b, N, D, tm)
    nmod = mod.shape[1]
    mod_idx = lambda which: (lambda t: (which, jnp.minimum(t // tiles_per_seq, nmod - 1), 0, 0))
    row = lambda width: pl.BlockSpec((tm, width), lambda t: (t, 0))
    const = lambda a: pl.BlockSpec(a.shape, lambda t: (0,) * a.ndim)
    if gla_extra is None:
        args, specs = args + [o], specs + [pl.BlockSpec((o.shape[0], tm), lambda t: (0, t))]
    else:
        o_bwd, og, ng = gla_extra
        args += [o, o_bwd, og, ng]
        specs += [row(o.shape[1]), row(o_bwd.shape[1]), row(og.shape[1]), const(ng)]
    args += [w_o, mod, ffn_gain.reshape(1, D), mod, mod]
    specs += [const(w_o), pl.BlockSpec((1, 1, 1, D), mod_idx(2)), pl.BlockSpec((1, D), lambda t: (0, 0)),
              pl.BlockSpec((1, 1, 1, D), mod_idx(3)), pl.BlockSpec((1, 1, 1, D), mod_idx(4))]
    args += r_args
    specs += r_in
    return pl.pallas_call(
        functools.partial(_post_kernel, gla=gla_extra is not None, n_x=n_x, n_lat_tiles=n_lat_tiles),
        grid=(N // tm,),
        in_specs=specs,
        out_specs=[row(D), row(D)] + r_out,
        out_shape=[jax.ShapeDtypeStruct((N, D), F32), jax.ShapeDtypeStruct((N, D), BF16)] + r_shapes,
        input_output_aliases={0: 0} if n_x == 1 else {},
        compiler_params=_cparams("parallel"),
        name="post_mixer_gla" if gla_extra is not None else "post_mixer_attn",
    )(*args)


def _route_tile(f, rw_ref, rb_ref, tri_ref, e_ref, gate_ref, rank_ref, cnt_ref):
    fh = f.astype(BF16)
    fl = (f - fh.astype(F32)).astype(BF16)
    rw = rw_ref[...]
    wh = rw.astype(BF16)
    wl = (rw - wh.astype(F32)).astype(BF16)
    logits = (jnp.dot(fh, wh, preferred_element_type=F32) + jnp.dot(fh, wl, preferred_element_type=F32)
              + jnp.dot(fl, wh, preferred_element_type=F32))
    lt = logits.T[:N_EXPERTS]
    scores = jax.nn.sigmoid(lt)
    sel = scores + rb_ref[...]
    srow = [sel[e:e + 1] for e in range(N_EXPERTS)]
    prow = [scores[e:e + 1] for e in range(N_EXPERTS)]
    gscore = []
    for g in range(N_GROUPS):
        a, b, c, d = srow[4 * g:4 * g + 4]
        hi1, lo1, hi2, lo2 = jnp.maximum(a, b), jnp.minimum(a, b), jnp.maximum(c, d), jnp.minimum(c, d)
        gscore.append(jnp.maximum(hi1, hi2) + jnp.maximum(jnp.minimum(hi1, hi2), jnp.maximum(lo1, lo2)))
    best, grp = gscore[0], jnp.zeros_like(gscore[0], dtype=I32)
    for g in range(1, N_GROUPS):
        better = gscore[g] > best
        grp = jnp.where(better, g, grp)
        best = jnp.where(better, gscore[g], best)
    s_in, p_in = [], []
    for k in range(EXPERTS_PER_GROUP):
        sv, pv = srow[k], prow[k]
        for g in range(1, N_GROUPS):
            sv = jnp.where(grp == g, srow[4 * g + k], sv)
            pv = jnp.where(grp == g, prow[4 * g + k], pv)
        s_in.append(sv)
        p_in.append(pv)
    i1, v1, g1 = jnp.zeros_like(grp), s_in[0], p_in[0]
    for k in range(1, EXPERTS_PER_GROUP):
        better = s_in[k] > v1
        i1 = jnp.where(better, k, i1)
        g1 = jnp.where(better, p_in[k], g1)
        v1 = jnp.where(better, s_in[k], v1)
    i2, v2, g2 = jnp.zeros_like(grp), jnp.full_like(v1, -jnp.inf), jnp.zeros_like(v1)
    for k in range(EXPERTS_PER_GROUP):
        better = (i1 != k) & (s_in[k] > v2)
        i2 = jnp.where(better, k, i2)
        g2 = jnp.where(better, p_in[k], g2)
        v2 = jnp.where(better, s_in[k], v2)
    e1 = grp * EXPERTS_PER_GROUP + i1
    e2 = grp * EXPERTS_PER_GROUP + i2
    tot = g1 + g2
    e_ref[0:1, :] = e1
    e_ref[1:2, :] = e2
    gate_ref[0:1, :] = g1 / tot
    gate_ref[1:2, :] = g2 / tot
    eid = lax.broadcasted_iota(I32, scores.shape, 0)
    hot1 = eid == e1
    hot2 = eid == e2
    onehot = jnp.where(hot1 | hot2, 1.0, 0.0).astype(BF16)
    cum = jnp.dot(onehot, tri_ref[...], preferred_element_type=F32)
    rank_ref[0:1, :] = jnp.sum(jnp.where(hot1, cum, 0.0), axis=0, keepdims=True).astype(I32) - 1
    rank_ref[1:2, :] = jnp.sum(jnp.where(hot2, cum, 0.0), axis=0, keepdims=True).astype(I32) - 1
    cnt_ref[0] = jnp.broadcast_to(cum[:, cum.shape[1] - 1:], cnt_ref.shape[1:])


def _route_io(router_w, router_b, N, D, tm):
    rw = jnp.zeros((D, LANES), F32).at[:, :N_EXPERTS].set(router_w.astype(F32))
    rb = router_b.astype(F32).reshape(N_EXPERTS, 1)
    tri = jnp.asarray(np.triu(np.ones((tm, tm), np.float32)), BF16)
    in_specs = [pl.BlockSpec((D, LANES), lambda t: (0, 0)), pl.BlockSpec((N_EXPERTS, 1), lambda t: (0, 0)),
                pl.BlockSpec((tm, tm), lambda t: (0, 0))]
    lane_row = lambda dt: (pl.BlockSpec((2, tm), lambda t: (0, t)), jax.ShapeDtypeStruct((2, N), dt))
    outs = [lane_row(I32), lane_row(F32), lane_row(I32),
            (pl.BlockSpec((1, N_EXPERTS, LANES), lambda t: (t, 0, 0)),
             jax.ShapeDtypeStruct((N // tm, N_EXPERTS, LANES), F32))]
    return [rw, rb, tri], in_specs, [o[0] for o in outs], [o[1] for o in outs]


def _chunk_tables(lo, go, rlen, tm):
    n_cls = (tm // RUN_ALIGN).bit_length()
    units = rlen // RUN_ALIGN
    cls = jnp.arange(n_cls, dtype=I32)
    flag = (units[:, :, None] >> cls) & 1
    rows = flag * (RUN_ALIGN << cls)
    above = jnp.cumsum(rows[..., ::-1], axis=-1)[..., ::-1] - rows
    src = lo[:, :, None] + above
    dst = go[:, :, None] + above
    slot = jnp.cumsum(flag, axis=1) - 1
    hit = (flag[:, None] == 1) & (slot[:, None] == jnp.arange(N_EXPERTS, dtype=I32)[None, :, None, None])
    compact = lambda a: jnp.sum(jnp.where(hit, a[:, None], 0), axis=2).transpose(0, 2, 1)
    flat = lambda a: a.reshape(-1).astype(I32)
    return (flat(compact(src)), flat(compact(dst)), flat(jnp.sum(flag, axis=1)), flat(jnp.sum(units, axis=1))), n_cls


def _start_pieces(tabs, tile, n_cls, make_copy, wait=False):
    src_ref, dst_ref, cnt_ref = tabs[:3]
    for b in range(n_cls):
        base = (tile * n_cls + b) * N_EXPERTS

        def body(i, carry, base=base, rows=RUN_ALIGN << b):
            copy = make_copy(pl.multiple_of(src_ref[base + i], RUN_ALIGN),
                             pl.multiple_of(dst_ref[base + i], RUN_ALIGN), rows)
            copy.wait() if wait else copy.start()
            return carry

        lax.fori_loop(0, cnt_ref[tile * n_cls + b], body, 0)


def _await_pieces(tabs, tile, max_rows, make_copy):
    units = tabs[3][tile]
    for b in range((max_rows // RUN_ALIGN).bit_length()):
        @pl.when(((units >> b) & 1) != 0)
        def _(rows=RUN_ALIGN << b):
            make_copy(0, 0, rows).wait()


def _dispatch_kernel(src_tab, dst_tab, cnt_tab, tot_tab, gap_src, gap_dst, gap_cnt, f_ref, pos_ref, gates_ref,
                     buf_hbm, srt_ref, zero_ref, sems, *, n_tiles, n_cls, gap_cls):
    t = pl.program_id(0)
    gaps = (gap_src, gap_dst, gap_cnt)
    gap_copy = lambda lo, go, rows: pltpu.make_async_copy(
        zero_ref.at[pl.ds(lo, rows)], buf_hbm.at[pl.ds(go, rows)], sems.at[2])

    @pl.when(t == 0)
    def _():
        zero_ref[...] = jnp.zeros_like(zero_ref)
        _start_pieces(gaps, 0, gap_cls, gap_copy)
    slot = t % 2
    tm, D = f_ref.shape
    ls = srt_ref.shape[1]
    j = lax.broadcasted_iota(I32, (ls, tm), 0)
    lane = lax.broadcasted_iota(I32, (tm, LANES), 1)
    perms, gcols = [], jnp.zeros((ls, LANES), F32)
    for k in range(2):
        perm = jnp.where(pos_ref[k:k + 1, :] == j, 1.0, 0.0).astype(BF16)
        g = gates_ref[:, k:k + 1]
        g_hi = g.astype(BF16).astype(F32)
        g_lo = (g - g_hi).astype(BF16).astype(F32)
        pieces = jnp.where(lane == 0, g_hi, jnp.where(lane == 1, g_lo, 0.0)).astype(BF16)
        gcols = gcols + jnp.dot(perm, pieces, preferred_element_type=F32)
        perms.append(perm)
    srt_ref[slot, :, :D] = jnp.dot(perms[0] + perms[1], f_ref[...], preferred_element_type=F32).astype(BF16)
    srt_ref[slot, :, D:] = gcols.astype(BF16)

    tabs = (src_tab, dst_tab, cnt_tab, tot_tab)

    def copier(buf_slot):
        return lambda lo, go, rows: pltpu.make_async_copy(
            srt_ref.at[buf_slot, pl.ds(lo, rows)], buf_hbm.at[pl.ds(go, rows)], sems.at[buf_slot])

    _start_pieces(tabs, t, n_cls, copier(slot))

    @pl.when(t > 0)
    def _():
        _await_pieces(tabs, t - 1, ls, copier(1 - slot))

    @pl.when(t == n_tiles - 1)
    def _():
        _await_pieces(tabs, t, ls, copier(slot))
        _start_pieces(gaps, 0, gap_cls, gap_copy, wait=True)


def _dispatch(f, pos, gates_t, tabs, n_cls, gap_tabs, gap_cls, n_rows, n_tiles, tm):
    D = f.shape[1]
    ls = 2 * tm + N_EXPERTS * RUN_ALIGN
    width = D + LANES
    grid_spec = pltpu.PrefetchScalarGridSpec(
        num_scalar_prefetch=7,
        grid=(n_tiles,),
        in_specs=[pl.BlockSpec((tm, D), lambda t, *_: (t, 0)), pl.BlockSpec((2, tm), lambda t, *_: (0, t)),
                  pl.BlockSpec((tm, 2), lambda t, *_: (t, 0))],
        out_specs=pl.BlockSpec(memory_space=pl.ANY),
        scratch_shapes=[pltpu.VMEM((2, ls, width), BF16), pltpu.VMEM((EXPERT_BLOCK, width), BF16),
                        pltpu.SemaphoreType.DMA((3,))],
    )
    return pl.pallas_call(
        functools.partial(_dispatch_kernel, n_tiles=n_tiles, n_cls=n_cls, gap_cls=gap_cls),
        grid_spec=grid_spec,
        out_shape=jax.ShapeDtypeStruct((n_rows, width), BF16),
        compiler_params=_cparams("arbitrary"),
        name="moe_dispatch",
    )(*tabs, *gap_tabs[:3], f, pos, gates_t)


def _expert_kernel(be_ref, bc_ref, bs_ref, x_ref, wgu_ref, wd_ref, y_ref):
    i = pl.program_id(0)
    D = y_ref.shape[1]

    @pl.when(bc_ref[i] > 0)
    def _():
        hu = jnp.dot(x_ref[:, :D], wgu_ref[0], preferred_element_type=F32)
        de = hu.shape[1] // 2
        gate, up = hu[:, :de], hu[:, de:]
        hid = (gate * jax.nn.sigmoid(gate) * up).astype(BF16)
        pieces = x_ref[:, D:].astype(F32)
        route_gate = pieces[:, 0:1] + pieces[:, 1:2]
        y_ref[...] = (jnp.dot(hid, wd_ref[0], preferred_element_type=F32) * route_gate).astype(BF16)

    @pl.when(bc_ref[i] == 0)
    def _():
        y_ref[...] = jnp.zeros_like(y_ref)


def _experts(buf, block_expert, block_count, block_src, wgu, wd):
    n_rows = buf.shape[0]
    D = wgu.shape[1]
    nb = n_rows // EXPERT_BLOCK
    grid_spec = pltpu.PrefetchScalarGridSpec(
        num_scalar_prefetch=3,
        grid=(nb,),
        in_specs=[pl.BlockSpec((EXPERT_BLOCK, buf.shape[1]), lambda i, be, bc, bs: (bs[i], 0)),
                  pl.BlockSpec((1,) + wgu.shape[1:], lambda i, be, bc, bs: (be[i], 0, 0)),
                  pl.BlockSpec((1,) + wd.shape[1:], lambda i, be, bc, bs: (be[i], 0, 0))],
        out_specs=pl.BlockSpec((EXPERT_BLOCK, D), lambda i, be, bc, bs: (i, 0)),
    )
    return pl.pallas_call(
        _expert_kernel,
        grid_spec=grid_spec,
        out_shape=jax.ShapeDtypeStruct((n_rows, D), BF16),
        compiler_params=_cparams("arbitrary"),
        name="moe_experts",
    )(block_expert, block_count, block_src, buf, wgu, wd)


def _combine_kernel(src_tab, dst_tab, cnt_tab, tot_tab, y_hbm, x_ref, pos_ref, gate_ref, fin_ref, xo_ref,
                    srt_ref, sems, *, final, n_tiles, n_cls):
    t = pl.program_id(0)
    slot = t % 2
    tm = x_ref.shape[0]
    ls = srt_ref.shape[1]

    tabs = (src_tab, dst_tab, cnt_tab, tot_tab)

    def copier(buf_slot):
        return lambda lo, go, rows: pltpu.make_async_copy(
            y_hbm.at[pl.ds(go, rows)], srt_ref.at[buf_slot, pl.ds(lo, rows)], sems.at[buf_slot])

    @pl.when(t == 0)
    def _():
        srt_ref[...] = jnp.zeros_like(srt_ref)
        _start_pieces(tabs, t, n_cls, copier(slot))

    @pl.when(t + 1 < n_tiles)
    def _():
        _start_pieces(tabs, t + 1, n_cls, copier(1 - slot))

    _await_pieces(tabs, t, ls, copier(slot))
    j = lax.broadcasted_iota(I32, (tm, ls), 1)
    pick = jnp.where((pos_ref[:, 0:1] == j) | (pos_ref[:, 1:2] == j), 1.0, 0.0).astype(BF16)
    y = jnp.dot(pick, srt_ref[slot], preferred_element_type=F32)
    xn = x_ref[...] + gate_ref[0, 0] * y
    if final:
        xn = xn * lax.rsqrt(jnp.mean(xn * xn, axis=-1, keepdims=True) + RMS_EPS) * fin_ref[...]
    xo_ref[...] = xn


def _combine(y_buf, pos_t, tabs, n_cls, xs, mod, final_gain, n_tiles, tm, tiles_per_seq, final):
    N, D = xs.shape
    nmod = mod.shape[1]
    ls = 2 * tm + N_EXPERTS * RUN_ALIGN
    out_rows = n_tiles * tm if final else N
    kwargs = {} if final else {"input_output_aliases": {5: 0}}
    grid_spec = pltpu.PrefetchScalarGridSpec(
        num_scalar_prefetch=4,
        grid=(n_tiles,),
        in_specs=[pl.BlockSpec(memory_space=pl.ANY),
                  pl.BlockSpec((tm, D), lambda t, *_: (t, 0)),
                  pl.BlockSpec((tm, 2), lambda t, *_: (t, 0)),
                  pl.BlockSpec((1, 1, 1, D), lambda t, *_: (5, jnp.minimum(t // tiles_per_seq, nmod - 1), 0, 0)),
                  pl.BlockSpec((1, D), lambda t, *_: (0, 0))],
        out_specs=pl.BlockSpec((tm, D), lambda t, *_: (t, 0)),
        scratch_shapes=[pltpu.VMEM((2, ls, D), BF16), pltpu.SemaphoreType.DMA((2,))],
    )
    return pl.pallas_call(
        functools.partial(_combine_kernel, final=final, n_tiles=n_tiles, n_cls=n_cls),
        grid_spec=grid_spec,
        out_shape=jax.ShapeDtypeStruct((out_rows, D), F32),
        compiler_params=_cparams("arbitrary"),
        name="moe_combine_final" if final else "moe_combine",
        **kwargs,
    )(*tabs, y_buf, xs, pos_t, mod, final_gain.reshape(1, D))


def _moe(f, routing, xs, mod, wgu, wd, final_gain, n_tok, tm, tiles_per_seq, final):
    nt = n_tok // tm
    e, gates, rank, cnt = routing
    e, gates, rank, cnt = e[:, :n_tok], gates[:, :n_tok], rank[:, :n_tok], cnt[:nt]
    n = cnt[:, :, 0].astype(I32)
    rlen = (n + RUN_ALIGN - 1) // RUN_ALIGN * RUN_ALIGN
    lo = jnp.cumsum(rlen, axis=1) - rlen
    region = jnp.sum(rlen, axis=0)
    region_pad = (region + EXPERT_BLOCK - 1) // EXPERT_BLOCK * EXPERT_BLOCK
    pends = jnp.cumsum(region_pad)
    pstarts = pends - region_pad
    go = pstarts[None, :] + jnp.cumsum(rlen, axis=0) - rlen
    n_blocks = -(-(2 * n_tok + nt * N_EXPERTS * RUN_ALIGN) // EXPERT_BLOCK) + N_EXPERTS
    blk0 = jnp.arange(n_blocks, dtype=I32) * EXPERT_BLOCK
    block_expert = jnp.minimum(jnp.sum((blk0[:, None] >= pends[None, :]).astype(I32), axis=1), N_EXPERTS - 1)
    block_used = (blk0 < (pstarts + region)[block_expert]).astype(I32)
    hot = e.reshape(2, nt, tm, 1) == jnp.arange(N_EXPERTS, dtype=I32)
    pos = jnp.sum(jnp.where(hot, lo[None, :, None, :], 0), axis=-1).reshape(2, n_tok) + rank
    tabs, n_cls = _chunk_tables(lo, go, rlen, tm)
    gap_tabs, gap_cls = _chunk_tables(jnp.zeros((1, N_EXPERTS), I32), (pstarts + region)[None, :],
                                      (region_pad - region)[None, :], EXPERT_BLOCK)
    buf = _dispatch(f, pos, gates.T, tabs, n_cls, gap_tabs, gap_cls, n_blocks * EXPERT_BLOCK, nt, tm)
    last_used = jnp.max(jnp.where(block_used > 0, jnp.arange(n_blocks, dtype=I32), 0))
    block_src = jnp.where(block_used > 0, jnp.arange(n_blocks, dtype=I32), last_used)
    y_buf = _experts(buf, block_expert.astype(I32), block_used, block_src, wgu, wd)
    return _combine(y_buf, pos.T, tabs, n_cls, xs, mod, final_gain, nt, tm, tiles_per_seq, final)


def _rope_tables(S, tm):
    rows = S // GRID_W
    row = jnp.repeat(jnp.arange(rows, dtype=F32), GRID_W)
    col = jnp.tile(jnp.arange(GRID_W, dtype=F32), rows)
    half = ATTN_HEAD_DIM // 4
    inv_freq = ROPE_THETA ** (-jnp.arange(half, dtype=F32) / half)
    ang_r = row[:, None] * inv_freq[None, :]
    ang_c = col[:, None] * inv_freq[None, :]
    zeros = jnp.zeros_like(ang_r)
    cos = jnp.concatenate([jnp.cos(ang_r)] * 2 + [jnp.cos(ang_c)] * 2, axis=1)
    s1 = jnp.concatenate([-jnp.sin(ang_r), zeros, -jnp.sin(ang_c), zeros], axis=1)
    s2 = jnp.concatenate([zeros, jnp.sin(ang_r), zeros, jnp.sin(ang_c)], axis=1)
    def finish(tab, fill):
        tab = jnp.tile(tab, (1, LANES // ATTN_HEAD_DIM))
        return jnp.concatenate([tab, jnp.full((tm, LANES), fill, F32)], axis=0)
    return finish(cos, 1.0), finish(s1, 0.0), finish(s2, 0.0)


def kernel(x, c, ctx, c_ctx, ada_w, ada_b, norm_mix_g, norm_ffn_g, final_g, attn_w_qkv, attn_w_o, attn_sinks,
           gla_w_in, gla_w_a1, gla_w_a2, gla_b_a, gla_norm_g, gla_w_o, router_w, router_b,
           moe_w_gate, moe_w_up, moe_w_down):
    B, S, D = x.shape
    C = ctx.shape[1]
    depth = ada_w.shape[0]
    tm = TOKEN_TILE
    assert S % tm == 0 and (B * C) % tm == 0 and S % ATTN_BLOCK == 0 and C % ATTN_BLOCK == 0
    assert (B * S) % C == 0 and S % GLA_CHUNK == 0 and C % GLA_CHUNK == 0
    n_lat = B * S
    n_lat_tiles = n_lat // tm
    tiles_per_seq = S // tm

    rpad = -(-(B + 1) // 8) * 8
    cc = jnp.zeros((rpad, D), F32).at[:B].set(c).at[B].set(c_ctx)
    mods = _ada_table(cc, ada_w, ada_b)
    mods = mods[:, :B + 1].reshape(depth, B + 1, 6, 1, D).transpose(0, 2, 1, 3, 4)

    xs = (x.reshape(n_lat, D), ctx.reshape(B * C, D))
    ptm = PROJ_TILE if S % PROJ_TILE == 0 and (B * C) % PROJ_TILE == 0 else tm
    rope = _rope_tables(S, ptm)
    q_dim = ATTN_KV_HEADS * ATTN_GROUP * ATTN_HEAD_DIM
    kv_dim = ATTN_KV_HEADS * ATTN_HEAD_DIM
    kd = GLA_HEADS * GLA_KEY_DIM
    vd = GLA_HEADS * GLA_VAL_DIM

    def dup_heads(w):
        w = w.reshape(D, ATTN_KV_HEADS, 1, ATTN_HEAD_DIM)
        return jnp.broadcast_to(w, (D, ATTN_KV_HEADS, LANES // ATTN_HEAD_DIM, ATTN_HEAD_DIM)).reshape(D, -1)

    for i in range(depth):
        last = i == depth - 1
        mod = mods[i]
        j = i // 2
        if i % 2 == 0:
            wqkv = attn_w_qkv[j]
            w = jnp.concatenate([wqkv[:, :q_dim], dup_heads(wqkv[:, q_dim:q_dim + kv_dim])], axis=1).astype(BF16)
            wvt = wqkv[:, q_dim + kv_dim:].T.astype(BF16)
            q, k, vt = _qkv_proj(xs, norm_mix_g[i], mod, 0, w, wvt, rope, ptm, n_lat // ptm, S // ptm)
            sink_row = jnp.repeat(attn_sinks[j].astype(F32).reshape(ATTN_KV_HEADS, ATTN_GROUP) * LOG2_E,
                                  ATTN_BLOCK, axis=1)
            o = _attention(q, k, vt, sink_row[:, None, :], B, S, C, not last)
            xs, f, *routing = _post_mixer(o, attn_w_o[j].astype(BF16), xs, mod, norm_ffn_g[i], router_w, router_b,
                                          tm, n_lat_tiles, tiles_per_seq)
        else:
            a1 = jnp.zeros((D, LANES), F32).at[:, :2 * GLA_GATE_RANK].set(
                jnp.concatenate([gla_w_a1[j, 0], gla_w_a1[j, 1]], axis=1))
            w = jnp.concatenate([gla_w_in[j], a1], axis=1).astype(BF16)
            w2 = jnp.zeros((LANES, 2 * kd), F32)
            w2 = w2.at[:GLA_GATE_RANK, :kd].set(gla_w_a2[j, 0]).at[GLA_GATE_RANK:2 * GLA_GATE_RANK, kd:].set(gla_w_a2[j, 1])
            ba = gla_b_a[j].reshape(1, 2 * kd).astype(F32)
            qk, v, og, la, chunk_tot = _gla_in_proj(xs, norm_mix_g[i], mod, 0, w, w2.astype(BF16), ba, tm,
                                                    n_lat_tiles, tiles_per_seq)
            o_fwd, o_bwd = _gla_scan(qk, v, la, chunk_tot, B, S, C)
            xs, f, *routing = _post_mixer(o_fwd, gla_w_o[j].astype(BF16), xs, mod, norm_ffn_g[i], router_w, router_b,
                                          tm, n_lat_tiles, tiles_per_seq,
                                          gla_extra=(o_bwd, og, gla_norm_g[j].reshape(1, GLA_VAL_DIM).astype(F32)))
        wgu = jnp.concatenate([moe_w_gate[i], moe_w_up[i]], axis=2).astype(BF16)
        wd = moe_w_down[i].astype(BF16)
        n_tok = n_lat if last else n_lat + B * C
        xs = _moe(f, routing, xs, mod, wgu, wd, final_g, n_tok, tm, tiles_per_seq, last)
    return xs.reshape(B, S, D)
```

```python
import functools

import numpy as np
import jax
import jax.numpy as jnp
from jax import lax
from jax.experimental import pallas as pl
from jax.experimental.pallas import tpu as pltpu

F32 = jnp.float32
BF16 = jnp.bfloat16
I32 = jnp.int32

LANES = 128
VMEM_LIMIT_BYTES = 56 * 1024 * 1024

RMS_EPS = 1e-6
GRID_W = 64
ROPE_THETA = 10000.0
ATTN_HEAD_DIM = 64
ATTN_KV_HEADS = 4
ATTN_GROUP = 4
ATTN_BLOCK = 128
LOG2_E = 1.4426950408889634
ATTN_Q_SCALE = ATTN_HEAD_DIM ** -0.5 * LOG2_E
GLA_HEADS = 4
GLA_KEY_DIM = 128
GLA_VAL_DIM = 256
GLA_GATE_RANK = 16
GLA_GATE_NORM = 16.0
GLA_CHUNK = 128
GLA_STAT_ROWS = 64
GLA_BOUNDED_TOTAL = 40.0
N_EXPERTS = 16
N_GROUPS = 4
EXPERTS_PER_GROUP = 4
EXPERT_BLOCK = 512
RUN_ALIGN = 16
TOKEN_TILE = 512
PROJ_TILE = 1024


def _cparams(*sem):
    return pltpu.CompilerParams(dimension_semantics=sem, vmem_limit_bytes=VMEM_LIMIT_BYTES)


def _norm_mod(x, gain, shift, scale):
    h = x * lax.rsqrt(jnp.mean(x * x, axis=-1, keepdims=True) + RMS_EPS) * gain
    return h * (1.0 + scale) + shift


def _ada_kernel(c_ref, w_ref, b_ref, o_ref):
    c = c_ref[...]
    s = (c * jax.nn.sigmoid(c)).astype(BF16)
    o_ref[0] = jnp.dot(s, w_ref[0].astype(BF16), preferred_element_type=F32) + b_ref[0]


def _ada_table(cc, ada_w, ada_b):
    L, D, D6 = ada_w.shape
    R = cc.shape[0]
    tn = 1536
    return pl.pallas_call(
        _ada_kernel,
        grid=(L, D6 // tn),
        in_specs=[pl.BlockSpec((R, D), lambda l, j: (0, 0)),
                  pl.BlockSpec((1, D, tn), lambda l, j: (l, 0, j)),
                  pl.BlockSpec((1, 1, tn), lambda l, j: (l, 0, j))],
        out_specs=pl.BlockSpec((1, R, tn), lambda l, j: (l, 0, j)),
        out_shape=jax.ShapeDtypeStruct((L, R, D6), F32),
        compiler_params=_cparams("parallel", "parallel"),
        name="ada_table",
    )(cc, ada_w, ada_b.reshape(L, 1, D6))


def _stream_tile(x_refs, n_lat_tiles):
    if len(x_refs) == 1:
        return x_refs[0][...]
    return jnp.where(pl.program_id(0) < n_lat_tiles, x_refs[0][...], x_refs[1][...])


def _stream_args(xs, tm, n_lat_tiles):
    if isinstance(xs, tuple):
        D = xs[0].shape[1]
        return list(xs), [pl.BlockSpec((tm, D), lambda t: (jnp.minimum(t, n_lat_tiles - 1), 0)),
                          pl.BlockSpec((tm, D), lambda t: (jnp.maximum(t - n_lat_tiles, 0), 0))]
    return [xs], [pl.BlockSpec((tm, xs.shape[1]), lambda t: (t, 0))]


def _qkv_kernel(*refs, n_x, n_lat_tiles):
    x_refs = refs[:n_x]
    g_ref, sh_ref, sc_ref, w_ref, wvt_ref, cos_ref, s1_ref, s2_ref, q_ref, k_ref, vt_ref = refs[n_x:]
    h = _norm_mod(_stream_tile(x_refs, n_lat_tiles), g_ref[...], sh_ref[0, 0], sc_ref[0, 0]).astype(BF16)
    z = jnp.dot(h, w_ref[...], preferred_element_type=F32)
    vt_ref[...] = lax.dot_general(wvt_ref[...], h, (((1,), (1,)), ((), ())), preferred_element_type=F32).astype(BF16)
    cos, s1, s2 = cos_ref[...], s1_ref[...], s2_ref[...]
    nq = q_ref.shape[1] // LANES
    nk = k_ref.shape[1] // LANES
    for j in range(nq + nk):
        zc = z[:, j * LANES:(j + 1) * LANES]
        r = zc * cos + pltpu.roll(zc, LANES - 16, 1) * s1 + pltpu.roll(zc, 16, 1) * s2
        if j < nq:
            q_ref[:, j * LANES:(j + 1) * LANES] = (r * ATTN_Q_SCALE).astype(BF16)
        else:
            k_ref[:, (j - nq) * LANES:(j - nq + 1) * LANES] = r.astype(BF16)


def _gla_in_kernel(x_ref, g_ref, sh_ref, sc_ref, w_ref, w2_ref, ba_ref, qk_ref, v_ref, og_ref, la_ref, tot_ref):
    h = _norm_mod(x_ref[...], g_ref[...], sh_ref[0, 0], sc_ref[0, 0]).astype(BF16)
    kd = GLA_HEADS * GLA_KEY_DIM
    vd = GLA_HEADS * GLA_VAL_DIM
    a1 = jnp.dot(h, w_ref[:, 2 * kd + 2 * vd:], preferred_element_type=F32).astype(BF16)
    pre = jnp.dot(a1, w2_ref[...], preferred_element_type=F32) + ba_ref[...]
    z = jnp.dot(h, w_ref[:, :2 * kd + 2 * vd], preferred_element_type=F32)
    la = (jnp.minimum(pre, 0.0) - jnp.log1p(jnp.exp(-jnp.abs(pre)))) * (1.0 / GLA_GATE_NORM)
    la_ref[...] = la
    qk_ref[:, :kd] = (z[:, :kd] * (GLA_KEY_DIM ** -0.5)).astype(BF16)
    qk_ref[:, kd:] = z[:, kd:2 * kd].astype(BF16)
    v_ref[...] = z[:, 2 * kd:2 * kd + vd].astype(BF16)
    og_ref[...] = z[:, 2 * kd + vd:2 * kd + 2 * vd].astype(BF16)
    nc = la.shape[0] // GLA_STAT_ROWS
    tot = jnp.sum(la.reshape(nc, GLA_STAT_ROWS, la.shape[1]), axis=1)
    lane = lax.broadcasted_iota(I32, (nc, LANES), 1)
    acc = jnp.zeros((nc, LANES), F32)
    for hd in range(2 * GLA_HEADS):
        worst = jnp.min(tot[:, hd * GLA_KEY_DIM:(hd + 1) * GLA_KEY_DIM], axis=1, keepdims=True)
        acc = jnp.where(lane == hd, worst, acc)
    tot_ref[...] = acc


def _tile_specs(D, tm, n_lat_tiles, tiles_per_seq, n_mod_rows):
    def mod_idx(which):
        return lambda t: (which, jnp.minimum(t // tiles_per_seq, n_mod_rows - 1), 0, 0)
    return mod_idx, [pl.BlockSpec((tm, D), lambda t: (t, 0)),
                     pl.BlockSpec((1, D), lambda t: (0, 0))]


def _qkv_proj(xs, gain, mod, which, w, wvt, rope, tm, n_lat_tiles, tiles_per_seq):
    x_args, x_specs = _stream_args(xs, tm, n_lat_tiles)
    N = sum(a.shape[0] for a in x_args)
    D = x_args[0].shape[1]
    nmod = mod.shape[1]
    mod_idx, specs = _tile_specs(D, tm, n_lat_tiles, tiles_per_seq, nmod)
    qd = ATTN_KV_HEADS * ATTN_GROUP * ATTN_HEAD_DIM
    kd = ATTN_KV_HEADS * LANES
    vd = wvt.shape[0]
    rope_idx = lambda t: (jnp.where(t < n_lat_tiles, t % tiles_per_seq, tiles_per_seq), 0)
    in_specs = x_specs + specs[1:] + [
        pl.BlockSpec((1, 1, 1, D), mod_idx(which)),
        pl.BlockSpec((1, 1, 1, D), mod_idx(which + 1)),
        pl.BlockSpec(w.shape, lambda t: (0, 0)),
        pl.BlockSpec(wvt.shape, lambda t: (0, 0)),
        pl.BlockSpec((tm, LANES), rope_idx),
        pl.BlockSpec((tm, LANES), rope_idx),
        pl.BlockSpec((tm, LANES), rope_idx),
    ]
    return pl.pallas_call(
        functools.partial(_qkv_kernel, n_x=len(x_args), n_lat_tiles=n_lat_tiles),
        grid=(N // tm,),
        in_specs=in_specs,
        out_specs=[pl.BlockSpec((tm, qd), lambda t: (t, 0)),
                   pl.BlockSpec((tm, kd), lambda t: (t, 0)),
                   pl.BlockSpec((vd, tm), lambda t: (0, t))],
        out_shape=[jax.ShapeDtypeStruct((N, qd), BF16),
                   jax.ShapeDtypeStruct((N, kd), BF16),
                   jax.ShapeDtypeStruct((vd, N), BF16)],
        compiler_params=_cparams("parallel"),
        name="attn_qkv_proj",
    )(*x_args, gain.reshape(1, D), mod, mod, w, wvt, *rope)


def _gla_in_proj(xs, gain, mod, which, w, w2, ba, tm, n_lat_tiles, tiles_per_seq):
    N, D = xs.shape
    nmod = mod.shape[1]
    mod_idx, specs = _tile_specs(D, tm, n_lat_tiles, tiles_per_seq, nmod)
    kd = GLA_HEADS * GLA_KEY_DIM
    vd = GLA_HEADS * GLA_VAL_DIM
    in_specs = specs + [
        pl.BlockSpec((1, 1, 1, D), mod_idx(which)),
        pl.BlockSpec((1, 1, 1, D), mod_idx(which + 1)),
        pl.BlockSpec(w.shape, lambda t: (0, 0)),
        pl.BlockSpec(w2.shape, lambda t: (0, 0)),
        pl.BlockSpec((1, 2 * kd), lambda t: (0, 0)),
    ]
    row = lambda width: pl.BlockSpec((tm, width), lambda t: (t, 0))
    return pl.pallas_call(
        _gla_in_kernel,
        grid=(N // tm,),
        in_specs=in_specs,
        out_specs=[row(2 * kd), row(vd), row(vd), row(2 * kd),
                   pl.BlockSpec((tm // GLA_STAT_ROWS, LANES), lambda t: (t, 0))],
        out_shape=[jax.ShapeDtypeStruct((N, 2 * kd), BF16),
                   jax.ShapeDtypeStruct((N, vd), BF16),
                   jax.ShapeDtypeStruct((N, vd), BF16),
                   jax.ShapeDtypeStruct((N, 2 * kd), F32),
                   jax.ShapeDtypeStruct((N // GLA_STAT_ROWS, LANES), F32)],
        compiler_params=_cparams("parallel"),
        name="gla_in_proj",
    )(xs, gain.reshape(1, D), mod, mod, w, w2, ba)


def _attn_kernel(*refs, window, nq):
    if window:
        q_ref, kp, kc, kn, kx, vp, vc, vn, vx, sink_ref, tri_ref, o_ref = refs
        k_parts, v_parts = (kp, kc, kn, kx), (vp, vc, vn, vx)
    else:
        q_ref, kx, vx, sink_ref, _, o_ref = refs
        k_parts, v_parts = (kx,), (vx,)
    tq = q_ref.shape[0]
    lane = lax.broadcasted_iota(I32, (tq, LANES), 1)
    first_head = lane < ATTN_HEAD_DIM
    if window:
        j = pl.program_id(1)
        bias_prev = jnp.where(j > 0, tri_ref[0], -jnp.inf)
        bias_next = jnp.where(j < nq - 1, tri_ref[1], -jnp.inf)
        bias_prev = jnp.concatenate([bias_prev] * ATTN_GROUP, axis=1)
        bias_next = jnp.concatenate([bias_next] * ATTN_GROUP, axis=1)

    def scores(kh):
        ks = slice(kh * LANES, (kh + 1) * LANES)
        kk = jnp.concatenate([p[:, ks] for p in k_parts], axis=0) if window else kx[:, ks]
        qa = q_ref[:, (2 * kh) * LANES:(2 * kh + 1) * LANES]
        qb = q_ref[:, (2 * kh + 1) * LANES:(2 * kh + 2) * LANES]
        zero = jnp.zeros_like(qa)
        qs = jnp.concatenate([jnp.where(first_head, qa, zero), jnp.where(first_head, zero, qa),
                              jnp.where(first_head, qb, zero), jnp.where(first_head, zero, qb)], axis=0)
        return lax.dot_general(kk, qs, (((1,), (1,)), ((), ())), preferred_element_type=F32)

    def softmax(s, kh):
        if window:
            s = jnp.concatenate([s[:tq] + bias_prev, s[tq:2 * tq], s[2 * tq:3 * tq] + bias_next, s[3 * tq:]], axis=0)
        sink = sink_ref[kh]
        m = jnp.maximum(jnp.max(s, axis=0, keepdims=True), sink)
        p = jnp.exp2(s - m)
        l = jnp.sum(p, axis=0, keepdims=True) + jnp.exp2(sink - m)
        return p.astype(BF16), l

    def values(p, l, kh):
        hs = slice(kh * ATTN_HEAD_DIM, (kh + 1) * ATTN_HEAD_DIM)
        vv = jnp.concatenate([part[hs, :] for part in v_parts], axis=1) if window else vx[hs, :]
        o = jnp.dot(vv, p, preferred_element_type=F32) / l
        for g in range(ATTN_GROUP):
            h = kh * ATTN_GROUP + g
            o_ref[h * ATTN_HEAD_DIM:(h + 1) * ATTN_HEAD_DIM, :] = o[:, g * tq:(g + 1) * tq].astype(BF16)

    s_next = scores(0)
    pending = None
    for kh in range(ATTN_KV_HEADS):
        s_cur = s_next
        if kh + 1 < ATTN_KV_HEADS:
            s_next = scores(kh + 1)
        if pending is not None:
            values(*pending)
        pending = softmax(s_cur, kh) + (kh,)
    values(*pending)


def _attention(q, k, vt, sink_row, B, S, C, need_ctx):
    N, qd = q.shape
    kd = k.shape[1]
    vd = vt.shape[0]
    tq = ATTN_BLOCK
    nq = S // tq
    ctx_blk0 = (B * S) // C
    qmap = lambda b, j: (b * nq + j, 0)
    prev = lambda b, j: (b * nq + jnp.maximum(j - 1, 0), 0)
    nxt = lambda b, j: (b * nq + jnp.minimum(j + 1, nq - 1), 0)
    cmap = lambda b, j: (ctx_blk0 + b, 0)
    cols = lambda m: (lambda b, j: m(b, j)[::-1])
    kblk = lambda m: pl.BlockSpec((tq, kd), m)
    vblk = lambda m: pl.BlockSpec((vd, tq), cols(m))
    kctx = pl.BlockSpec((C, kd), cmap)
    vctx = pl.BlockSpec((vd, C), cols(cmap))
    sink_spec = pl.BlockSpec(sink_row.shape, lambda b, j: (0, 0, 0))
    key = np.arange(tq)[:, None]
    qry = np.arange(tq)[None, :]
    tri = jnp.asarray(np.stack([np.where(key >= qry, 0.0, -np.inf), np.where(key <= qry, 0.0, -np.inf)]), F32)
    o_lat = pl.pallas_call(
        functools.partial(_attn_kernel, window=True, nq=nq),
        grid=(B, nq),
        in_specs=[pl.BlockSpec((tq, qd), qmap), kblk(prev), kblk(qmap), kblk(nxt), kctx,
                  vblk(prev), vblk(qmap), vblk(nxt), vctx, sink_spec,
                  pl.BlockSpec(tri.shape, lambda b, j: (0, 0, 0))],
        out_specs=pl.BlockSpec((qd, tq), cols(qmap)),
        out_shape=jax.ShapeDtypeStruct((qd, N), BF16),
        compiler_params=_cparams("parallel", "parallel"),
        name="attn_window",
    )(q, k, k, k, k, vt, vt, vt, vt, sink_row, tri)
    if not need_ctx:
        return o_lat
    ncq = C // tq
    lat_blks = (B * S) // tq
    cq = lambda b, j: (lat_blks + b * ncq + j, 0)
    return pl.pallas_call(
        functools.partial(_attn_kernel, window=False, nq=ncq),
        grid=(B, ncq),
        in_specs=[pl.BlockSpec((tq, qd), cq), kctx, vctx, sink_spec,
                  pl.BlockSpec(memory_space=pl.ANY)],
        out_specs=pl.BlockSpec((qd, tq), cols(cq)),
        out_shape=jax.ShapeDtypeStruct((qd, N), BF16),
        input_output_aliases={4: 0},
        compiler_params=_cparams("parallel", "parallel"),
        name="attn_context",
    )(q, k, vt, sink_row, o_lat)


def _gla_constants(C):
    levels = []
    m = 1
    while m < C:
        levels.append(m)
        m *= 2
    t = np.arange(C)[:, None]
    u = np.arange(C)[None, :]
    secs = [(u <= t), (u > t)]
    masks = []
    for m in levels:
        base = (t // (2 * m)) * (2 * m)
        ref = base + m - 1
        second = t >= base + m
        secs.append(np.where(second, (u > ref) & (u <= t), (u > t) & (u <= ref)))
        masks.append((t // (2 * m) == u // (2 * m)) & second & (u < (u // (2 * m)) * (2 * m) + m))
    masks.append(t == u)
    masks.append(u <= t)
    mf = np.concatenate([s.astype(np.float32) for s in secs], axis=0)
    kf = np.stack([mk.astype(np.float32) for mk in masks], axis=0)
    mb = np.concatenate([s.astype(np.float32)[::-1, ::-1] for s in secs], axis=0)
    kb = np.stack([mk.astype(np.float32)[::-1, ::-1] for mk in masks], axis=0)
    return np.stack([mf, mb]), np.stack([kf, kb]), len(levels)


def _gla_chunk(q, k, v, g, st_ref, mm_ref, mk_ref, d, nl, bounded):
    C, dk = q.shape
    nt = (((1,), (1,)), ((), ()))
    g_hi = g.astype(BF16)
    g_lo = (g - g_hi.astype(F32)).astype(BF16)
    mmat = mm_ref[d, :2 * C] if bounded else mm_ref[d]
    e2 = jnp.dot(mmat, jnp.concatenate([g_hi, g_lo], axis=1), preferred_element_type=F32)
    ee = e2[:, :dk] + e2[:, dk:]
    ex = jnp.exp(ee)
    qf, kf = q.astype(F32), k.astype(F32)
    qe = (qf * ex[:C]).astype(BF16)
    ke = (kf * ex[C:2 * C]).astype(BF16)
    st = st_ref[...]
    o = lax.dot_general(qe, st.astype(BF16), nt, preferred_element_type=F32)
    if bounded:
        ki = (kf * jnp.exp(-ee[:C])).astype(BF16)
        a = mk_ref[d, nl + 1] * lax.dot_general(qe, ki, nt, preferred_element_type=F32)
    else:
        a = mk_ref[d, nl] * lax.dot_general(q, k, nt, preferred_element_type=F32)
        for i in range(nl):
            xl = ex[(2 + i) * C:(3 + i) * C]
            ql = (qf * xl).astype(BF16)
            kl = (kf * xl).astype(BF16)
            a = a + mk_ref[d, i] * lax.dot_general(ql, kl, nt, preferred_element_type=F32)
    o = o + jnp.dot(a.astype(BF16), v, preferred_element_type=F32)
    decay = jnp.exp(jnp.sum(g, axis=0, keepdims=True))
    st_ref[...] = st * decay + lax.dot_general(v, ke, (((0,), (0,)), ((), ())), preferred_element_type=F32)
    return o


def _gla_kernel(okf_ref, okb_ref, qf, kf, vf, lf, qb, kb, vb, lb, mm_ref, mk_ref, of_ref, ob_ref, st_ref,
                *, nl, seg_chunks, lat_segs, ctx_seg0):
    C = GLA_CHUNK
    H, DK, DV = GLA_HEADS, GLA_KEY_DIM, GLA_VAL_DIM
    b, j = pl.program_id(0), pl.program_id(1)

    @pl.when(j == 0)
    def _():
        st_ref[...] = jnp.zeros_like(st_ref)

    seg_f = jnp.where(j == 0, ctx_seg0 + b, b * lat_segs + j - 1)
    seg_b = jnp.where(j == 0, ctx_seg0 + b, b * lat_segs + lat_segs - j)

    def body(i, carry):
        cf, cb = i, seg_chunks - 1 - i
        rf = pl.ds(pl.multiple_of(cf * C, C), C)
        rb = pl.ds(pl.multiple_of(cb * C, C), C)
        bounded = (okf_ref[seg_f * seg_chunks + cf] != 0) & (okb_ref[seg_b * seg_chunks + cb] != 0)

        def advance_general():
            for h in range(H):
                ks, vs = slice(h * DK, (h + 1) * DK), slice(h * DV, (h + 1) * DV)
                o = _gla_chunk(qf[rf, ks], kf[rf, ks], vf[rf, vs], lf[rf, ks], st_ref.at[h],
                               mm_ref, mk_ref, 0, nl, False)
                of_ref[rf, vs] = o.astype(BF16)
                o = _gla_chunk(qb[rb, ks], kb[rb, ks], vb[rb, vs], lb[rb, ks], st_ref.at[H + h],
                               mm_ref, mk_ref, 1, nl, False)
                ob_ref[rb, vs] = o.astype(BF16)

        def advance_bounded():
            nt = (((1,), (1,)), ((), ()))
            kw = H * DK
            sides = ((qf, kf, vf, lf, rf, of_ref, 0), (qb, kb, vb, lb, rb, ob_ref, 1))
            pre = []
            for q_r, k_r, v_r, l_r, rows, o_r, d in sides:
                g = l_r[rows, :]
                g_hi = g.astype(BF16)
                g_lo = (g - g_hi.astype(F32)).astype(BF16)
                e2 = jnp.dot(mm_ref[d, :2 * C], jnp.concatenate([g_hi, g_lo], axis=1),
                             preferred_element_type=F32)
                pre.append((e2[:, :kw] + e2[:, kw:], g))
            units = []
            for (ee, g), (q_r, k_r, v_r, l_r, rows, o_r, d) in zip(pre, sides):
                ex = jnp.exp(ee)
                kf32 = k_r[rows, :].astype(F32)
                qe = (q_r[rows, :].astype(F32) * ex[:C]).astype(BF16)
                ke = (kf32 * ex[C:]).astype(BF16)
                ki = (kf32 * jnp.exp(-ee[:C])).astype(BF16)
                decay = jnp.exp(jnp.sum(g, axis=0, keepdims=True))
                for h in range(H):
                    ks = slice(h * DK, (h + 1) * DK)
                    units.append((d, h, qe[:, ks], ke[:, ks], ki[:, ks], decay[:, ks], v_r, rows, o_r))
            inter, score = [], []
            for d, h, qe, ke, ki, decay, v_r, rows, o_r in units:
                st = st_ref[d * H + h]
                inter.append(lax.dot_general(qe, st.astype(BF16), nt, preferred_element_type=F32))
                score.append(lax.dot_general(qe, ki, nt, preferred_element_type=F32))
            for n, (d, h, qe, ke, ki, decay, v_r, rows, o_r) in enumerate(units):
                vs = slice(h * DV, (h + 1) * DV)
                v = v_r[rows, vs]
                a = (mk_ref[d, nl + 1] * score[n]).astype(BF16)
                o = inter[n] + jnp.dot(a, v, preferred_element_type=F32)
                upd = lax.dot_general(v, ke, (((0,), (0,)), ((), ())), preferred_element_type=F32)
                o_r[rows, vs] = o.astype(BF16)
                st_ref[d * H + h] = st_ref[d * H + h] * decay + upd

        @pl.when(bounded)
        def _():
            advance_bounded()

        @pl.when(jnp.logical_not(bounded))
        def _():
            advance_general()

        return carry

    lax.fori_loop(0, seg_chunks, body, 0)


def _gla_scan(qk, v, la, chunk_tot, B, S, C):
    N = qk.shape[0]
    H, DK, DV = GLA_HEADS, GLA_KEY_DIM, GLA_VAL_DIM
    seg = C
    assert S % seg == 0 and seg % GLA_CHUNK == 0
    lat_segs = S // seg
    ctx_seg0 = (B * S) // seg
    mm, mk, nl = _gla_constants(GLA_CHUNK)
    mm = jnp.asarray(mm, BF16)
    mk = jnp.asarray(mk, F32)
    chunk_tot = jnp.sum(chunk_tot.reshape(-1, GLA_CHUNK // GLA_STAT_ROWS, LANES), axis=1)
    ok = chunk_tot[:, :2 * H] >= -GLA_BOUNDED_TOTAL
    okf = jnp.all(ok[:, :H], axis=1).astype(I32)
    okb = jnp.all(ok[:, H:], axis=1).astype(I32)
    fwd = lambda col: (lambda b, j, *_: (jnp.where(j == 0, ctx_seg0 + b, b * lat_segs + j - 1), col))
    bwd = lambda col: (lambda b, j, *_: (jnp.where(j == 0, ctx_seg0 + b, b * lat_segs + lat_segs - j), col))
    kw, vw = H * DK, H * DV
    in_specs = [
        pl.BlockSpec((seg, kw), fwd(0)), pl.BlockSpec((seg, kw), fwd(1)), pl.BlockSpec((seg, vw), fwd(0)),
        pl.BlockSpec((seg, kw), fwd(0)),
        pl.BlockSpec((seg, kw), bwd(0)), pl.BlockSpec((seg, kw), bwd(1)), pl.BlockSpec((seg, vw), bwd(0)),
        pl.BlockSpec((seg, kw), bwd(1)),
        pl.BlockSpec(mm.shape, lambda b, j, *_: (0, 0, 0)),
        pl.BlockSpec(mk.shape, lambda b, j, *_: (0, 0, 0, 0)),
    ]
    grid_spec = pltpu.PrefetchScalarGridSpec(
        num_scalar_prefetch=2,
        grid=(B, lat_segs + 1),
        in_specs=in_specs,
        out_specs=[pl.BlockSpec((seg, vw), fwd(0)), pl.BlockSpec((seg, vw), bwd(0))],
        scratch_shapes=[pltpu.VMEM((2 * H, DV, DK), F32)],
    )
    return pl.pallas_call(
        functools.partial(_gla_kernel, nl=nl, seg_chunks=seg // GLA_CHUNK, lat_segs=lat_segs, ctx_seg0=ctx_seg0),
        grid_spec=grid_spec,
        out_shape=[jax.ShapeDtypeStruct((N, vw), BF16), jax.ShapeDtypeStruct((N, vw), BF16)],
        compiler_params=_cparams("parallel", "arbitrary"),
        name="gla_scan",
    )(okf, okb, qk, qk, v, la, qk, qk, v, la, mm, mk)


def _post_kernel(*refs, gla, n_x, n_lat_tiles):
    x_refs, refs = refs[:n_x], refs[n_x:]
    route_refs = refs[-4:]
    route_in = refs[-9:-6]
    refs = refs[:-9] + refs[-6:-4]
    x_tile = _stream_tile(x_refs, n_lat_tiles)
    if gla:
        o_ref, ob_ref, og_ref, ng_ref, w_ref, gate_ref, fg_ref, fsh_ref, fsc_ref, xo_ref, f_ref = refs
        o = o_ref[...].astype(F32) + ob_ref[...].astype(F32)
        g = og_ref[...].astype(F32)
        parts = []
        for h in range(GLA_HEADS):
            oh = o[:, h * GLA_VAL_DIM:(h + 1) * GLA_VAL_DIM]
            parts.append(oh * lax.rsqrt(jnp.mean(oh * oh, axis=-1, keepdims=True) + RMS_EPS) * ng_ref[...])
        mix = (jnp.concatenate(parts, axis=1) * (g * jax.nn.sigmoid(g))).astype(BF16)
        y = jnp.dot(mix, w_ref[...], preferred_element_type=F32)
    else:
        o_ref, w_ref, gate_ref, fg_ref, fsh_ref, fsc_ref, xo_ref, f_ref = refs
        y = lax.dot_general(o_ref[...], w_ref[...], (((0,), (0,)), ((), ())), preferred_element_type=F32)
    xn = x_tile + gate_ref[0, 0] * y
    xo_ref[...] = xn
    f = _norm_mod(xn, fg_ref[...], fsh_ref[0, 0], fsc_ref[0, 0])
    f_ref[...] = f.astype(BF16)
    _route_tile(f, *route_in, *route_refs)


def _post_mixer(o, w_o, xs, mod, ffn_gain, router_w, router_b, tm, n_lat_tiles, tiles_per_seq, gla_extra=None):
    args, specs = _stream_args(xs, tm, n_lat_tiles)
    n_x = len(args)
    N = sum(a.shape[0] for a in args)
    D = args[0].shape[1]
    r_args, r_in, r_out, r_shapes = _route_io(router_w, router_b, N, D, tm)
    nmod = mod.shape[1]
    mod_idx = lambda which: (lambda t: (which, jnp.minimum(t // tiles_per_seq, nmod - 1), 0, 0))
    row = lambda width: pl.BlockSpec((tm, width), lambda t: (t, 0))
    const = lambda a: pl.BlockSpec(a.shape, lambda t: (0,) * a.ndim)
    if gla_extra is None:
        args, specs = args + [o], specs + [pl.BlockSpec((o.shape[0], tm), lambda t: (0, t))]
    else:
        o_bwd, og, ng = gla_extra
        args += [o, o_bwd, og, ng]
        specs += [row(o.shape[1]), row(o_bwd.shape[1]), row(og.shape[1]), const(ng)]
    args += [w_o, mod, ffn_gain.reshape(1, D), mod, mod]
    specs += [const(w_o), pl.BlockSpec((1, 1, 1, D), mod_idx(2)), pl.BlockSpec((1, D), lambda t: (0, 0)),
              pl.BlockSpec((1, 1, 1, D), mod_idx(3)), pl.BlockSpec((1, 1, 1, D), mod_idx(4))]
    args += r_args
    specs += r_in
    return pl.pallas_call(
        functools.partial(_post_kernel, gla=gla_extra is not None, n_x=n_x, n_lat_tiles=n_lat_tiles),
        grid=(N // tm,),
        in_specs=specs,
        out_specs=[row(D), row(D)] + r_out,
        out_shape=[jax.ShapeDtypeStruct((N, D), F32), jax.ShapeDtypeStruct((N, D), BF16)] + r_shapes,
        input_output_aliases={0: 0} if n_x == 1 else {},
        compiler_params=_cparams("parallel"),
        name="post_mixer_gla" if gla_extra is not None else "post_mixer_attn",
    )(*args)


def _route_tile(f, rw_ref, rb_ref, tri_ref, e_ref, gate_ref, rank_ref, cnt_ref):
    tm = f.shape[0]
    fh = f.astype(BF16)
    fl = (f - fh.astype(F32)).astype(BF16)
    parts = jnp.dot(jnp.concatenate([fh, fl], axis=0), rw_ref[...], preferred_element_type=F32)
    logits = (parts[:tm, :LANES] + parts[:tm, LANES:]) + (parts[tm:, :LANES] + parts[tm:, LANES:])
    lt = logits.T[:N_EXPERTS]
    scores = jax.nn.sigmoid(lt)
    sel = scores + rb_ref[...]
    srow = [sel[e:e + 1] for e in range(N_EXPERTS)]
    prow = [scores[e:e + 1] for e in range(N_EXPERTS)]
    gscore = []
    for g in range(N_GROUPS):
        a, b, c, d = srow[4 * g:4 * g + 4]
        hi1, lo1, hi2, lo2 = jnp.maximum(a, b), jnp.minimum(a, b), jnp.maximum(c, d), jnp.minimum(c, d)
        gscore.append(jnp.maximum(hi1, hi2) + jnp.maximum(jnp.minimum(hi1, hi2), jnp.maximum(lo1, lo2)))
    best, grp = gscore[0], jnp.zeros_like(gscore[0], dtype=I32)
    for g in range(1, N_GROUPS):
        better = gscore[g] > best
        grp = jnp.where(better, g, grp)
        best = jnp.where(better, gscore[g], best)
    s_in, p_in = [], []
    for k in range(EXPERTS_PER_GROUP):
        sv, pv = srow[k], prow[k]
        for g in range(1, N_GROUPS):
            sv = jnp.where(grp == g, srow[4 * g + k], sv)
            pv = jnp.where(grp == g, prow[4 * g + k], pv)
        s_in.append(sv)
        p_in.append(pv)
    i1, v1, g1 = jnp.zeros_like(grp), s_in[0], p_in[0]
    for k in range(1, EXPERTS_PER_GROUP):
        better = s_in[k] > v1
        i1 = jnp.where(better, k, i1)
        g1 = jnp.where(better, p_in[k], g1)
        v1 = jnp.where(better, s_in[k], v1)
    i2, v2, g2 = jnp.zeros_like(grp), jnp.full_like(v1, -jnp.inf), jnp.zeros_like(v1)
    for k in range(EXPERTS_PER_GROUP):
        better = (i1 != k) & (s_in[k] > v2)
        i2 = jnp.where(better, k, i2)
        g2 = jnp.where(better, p_in[k], g2)
        v2 = jnp.where(better, s_in[k], v2)
    e1 = grp * EXPERTS_PER_GROUP + i1
    e2 = grp * EXPERTS_PER_GROUP + i2
    tot = g1 + g2
    e_ref[0:1, :] = e1
    e_ref[1:2, :] = e2
    gate_ref[0:1, :] = g1 / tot
    gate_ref[1:2, :] = g2 / tot
    eid = lax.broadcasted_iota(I32, scores.shape, 0)
    hot1 = eid == e1
    hot2 = eid == e2
    onehot = jnp.where(hot1 | hot2, 1.0, 0.0).astype(BF16)
    cum = jnp.dot(onehot, tri_ref[...], preferred_element_type=F32)
    rank_ref[0:1, :] = jnp.sum(jnp.where(hot1, cum, 0.0), axis=0, keepdims=True).astype(I32) - 1
    rank_ref[1:2, :] = jnp.sum(jnp.where(hot2, cum, 0.0), axis=0, keepdims=True).astype(I32) - 1
    cnt_ref[0] = jnp.broadcast_to(cum[:, cum.shape[1] - 1:], cnt_ref.shape[1:])


def _route_io(router_w, router_b, N, D, tm):
    rw = jnp.zeros((D, LANES), F32).at[:, :N_EXPERTS].set(router_w.astype(F32))
    rw_hi = rw.astype(BF16)
    rw = jnp.concatenate([rw_hi, (rw - rw_hi.astype(F32)).astype(BF16)], axis=1)
    rb = router_b.astype(F32).reshape(N_EXPERTS, 1)
    tri = jnp.asarray(np.triu(np.ones((tm, tm), np.float32)), BF16)
    in_specs = [pl.BlockSpec((D, 2 * LANES), lambda t: (0, 0)), pl.BlockSpec((N_EXPERTS, 1), lambda t: (0, 0)),
                pl.BlockSpec((tm, tm), lambda t: (0, 0))]
    lane_row = lambda dt: (pl.BlockSpec((2, tm), lambda t: (0, t)), jax.ShapeDtypeStruct((2, N), dt))
    outs = [lane_row(I32), lane_row(F32), lane_row(I32),
            (pl.BlockSpec((1, N_EXPERTS, LANES), lambda t: (t, 0, 0)),
             jax.ShapeDtypeStruct((N // tm, N_EXPERTS, LANES), F32))]
    return [rw, rb, tri], in_specs, [o[0] for o in outs], [o[1] for o in outs]


def _chunk_tables(lo, go, rlen, tm):
    n_cls = (tm // RUN_ALIGN).bit_length()
    units = rlen // RUN_ALIGN
    cls = jnp.arange(n_cls, dtype=I32)
    flag = (units[:, :, None] >> cls) & 1
    rows = flag * (RUN_ALIGN << cls)
    above = jnp.cumsum(rows[..., ::-1], axis=-1)[..., ::-1] - rows
    src = lo[:, :, None] + above
    dst = go[:, :, None] + above
    slot = jnp.cumsum(flag, axis=1) - 1
    hit = (flag[:, None] == 1) & (slot[:, None] == jnp.arange(N_EXPERTS, dtype=I32)[None, :, None, None])
    compact = lambda a: jnp.sum(jnp.where(hit, a[:, None], 0), axis=2).transpose(0, 2, 1)
    flat = lambda a: a.reshape(-1).astype(I32)
    return (flat(compact(src)), flat(compact(dst)), flat(jnp.sum(flag, axis=1)), flat(jnp.sum(units, axis=1))), n_cls


def _start_pieces(tabs, tile, n_cls, make_copy, wait=False):
    src_ref, dst_ref, cnt_ref = tabs[:3]
    for b in range(n_cls):
        base = (tile * n_cls + b) * N_EXPERTS

        def body(i, carry, base=base, rows=RUN_ALIGN << b):
            copy = make_copy(pl.multiple_of(src_ref[base + i], RUN_ALIGN),
                             pl.multiple_of(dst_ref[base + i], RUN_ALIGN), rows)
            copy.wait() if wait else copy.start()
            return carry

        lax.fori_loop(0, cnt_ref[tile * n_cls + b], body, 0)


def _await_pieces(tabs, tile, max_rows, make_copy):
    units = tabs[3][tile]
    for b in range((max_rows // RUN_ALIGN).bit_length()):
        @pl.when(((units >> b) & 1) != 0)
        def _(rows=RUN_ALIGN << b):
            make_copy(0, 0, rows).wait()


def _dispatch_kernel(src_tab, dst_tab, cnt_tab, tot_tab, gap_src, gap_dst, gap_cnt, f_ref, pos_ref, gates_ref,
                     buf_hbm, srt_ref, zero_ref, sems, *, n_tiles, n_cls, gap_cls):
    t = pl.program_id(0)
    gaps = (gap_src, gap_dst, gap_cnt)
    gap_copy = lambda lo, go, rows: pltpu.make_async_copy(
        zero_ref.at[pl.ds(lo, rows)], buf_hbm.at[pl.ds(go, rows)], sems.at[2])

    @pl.when(t == 0)
    def _():
        zero_ref[...] = jnp.zeros_like(zero_ref)
        _start_pieces(gaps, 0, gap_cls, gap_copy)
    slot = t % 2
    tm, D = f_ref.shape
    ls = srt_ref.shape[1]
    j = lax.broadcasted_iota(I32, (ls, tm), 0)
    lane = lax.broadcasted_iota(I32, (tm, LANES), 1)
    perms, gcols = [], jnp.zeros((ls, LANES), F32)
    for k in range(2):
        perm = jnp.where(pos_ref[k:k + 1, :] == j, 1.0, 0.0).astype(BF16)
        g = gates_ref[:, k:k + 1]
        g_hi = g.astype(BF16).astype(F32)
        g_lo = (g - g_hi).astype(BF16).astype(F32)
        pieces = jnp.where(lane == 0, g_hi, jnp.where(lane == 1, g_lo, 0.0)).astype(BF16)
        gcols = gcols + jnp.dot(perm, pieces, preferred_element_type=F32)
        perms.append(perm)
    srt_ref[slot, :, :D] = jnp.dot(perms[0] + perms[1], f_ref[...], preferred_element_type=F32).astype(BF16)
    srt_ref[slot, :, D:] = gcols.astype(BF16)

    tabs = (src_tab, dst_tab, cnt_tab, tot_tab)

    def copier(buf_slot):
        return lambda lo, go, rows: pltpu.make_async_copy(
            srt_ref.at[buf_slot, pl.ds(lo, rows)], buf_hbm.at[pl.ds(go, rows)], sems.at[buf_slot])

    _start_pieces(tabs, t, n_cls, copier(slot))

    @pl.when(t > 0)
    def _():
        _await_pieces(tabs, t - 1, ls, copier(1 - slot))

    @pl.when(t == n_tiles - 1)
    def _():
        _await_pieces(tabs, t, ls, copier(slot))
        _start_pieces(gaps, 0, gap_cls, gap_copy, wait=True)


def _dispatch(f, pos, gates_t, tabs, n_cls, gap_tabs, gap_cls, n_rows, n_tiles, tm):
    D = f.shape[1]
    ls = 2 * tm + N_EXPERTS * RUN_ALIGN
    width = D + LANES
    grid_spec = pltpu.PrefetchScalarGridSpec(
        num_scalar_prefetch=7,
        grid=(n_tiles,),
        in_specs=[pl.BlockSpec((tm, D), lambda t, *_: (t, 0)), pl.BlockSpec((2, tm), lambda t, *_: (0, t)),
                  pl.BlockSpec((tm, 2), lambda t, *_: (t, 0))],
        out_specs=pl.BlockSpec(memory_space=pl.ANY),
        scratch_shapes=[pltpu.VMEM((2, ls, width), BF16), pltpu.VMEM((EXPERT_BLOCK, width), BF16),
                        pltpu.SemaphoreType.DMA((3,))],
    )
    return pl.pallas_call(
        functools.partial(_dispatch_kernel, n_tiles=n_tiles, n_cls=n_cls, gap_cls=gap_cls),
        grid_spec=grid_spec,
        out_shape=jax.ShapeDtypeStruct((n_rows, width), BF16),
        compiler_params=_cparams("arbitrary"),
        name="moe_dispatch",
    )(*tabs, *gap_tabs[:3], f, pos, gates_t)


def _expert_kernel(be_ref, bc_ref, bs_ref, x_ref, wg_ref, wu_ref, wd_ref, y_ref):
    i = pl.program_id(0)
    D = y_ref.shape[1]

    @pl.when(bc_ref[i] > 0)
    def _():
        x = x_ref[:, :D]
        gate = jnp.dot(x, wg_ref[0], preferred_element_type=F32)
        up = jnp.dot(x, wu_ref[0], preferred_element_type=F32)
        hid = (gate * jax.nn.sigmoid(gate) * up).astype(BF16)
        pieces = x_ref[:, D:].astype(F32)
        route_gate = pieces[:, 0:1] + pieces[:, 1:2]
        y_ref[...] = (jnp.dot(hid, wd_ref[0], preferred_element_type=F32) * route_gate).astype(BF16)

    @pl.when(bc_ref[i] == 0)
    def _():
        y_ref[...] = jnp.zeros_like(y_ref)


def _experts(buf, block_expert, block_count, block_src, wg, wu, wd):
    n_rows = buf.shape[0]
    D = wg.shape[1]
    nb = n_rows // EXPERT_BLOCK
    grid_spec = pltpu.PrefetchScalarGridSpec(
        num_scalar_prefetch=3,
        grid=(nb,),
        in_specs=[pl.BlockSpec((EXPERT_BLOCK, buf.shape[1]), lambda i, be, bc, bs: (bs[i], 0)),
                  pl.BlockSpec((1,) + wg.shape[1:], lambda i, be, bc, bs: (be[i], 0, 0)),
                  pl.BlockSpec((1,) + wu.shape[1:], lambda i, be, bc, bs: (be[i], 0, 0)),
                  pl.BlockSpec((1,) + wd.shape[1:], lambda i, be, bc, bs: (be[i], 0, 0))],
        out_specs=pl.BlockSpec((EXPERT_BLOCK, D), lambda i, be, bc, bs: (i, 0)),
    )
    return pl.pallas_call(
        _expert_kernel,
        grid_spec=grid_spec,
        out_shape=jax.ShapeDtypeStruct((n_rows, D), BF16),
        compiler_params=_cparams("arbitrary"),
        name="moe_experts",
    )(block_expert, block_count, block_src, buf, wg, wu, wd)


def _combine_kernel(src_tab, dst_tab, cnt_tab, tot_tab, y_hbm, x_ref, pos_ref, gate_ref, fin_ref, xo_ref,
                    srt_ref, sems, *, final, n_tiles, n_cls):
    t = pl.program_id(0)
    slot = t % 2
    tm = x_ref.shape[0]
    ls = srt_ref.shape[1]

    tabs = (src_tab, dst_tab, cnt_tab, tot_tab)

    def copier(buf_slot):
        return lambda lo, go, rows: pltpu.make_async_copy(
            y_hbm.at[pl.ds(go, rows)], srt_ref.at[buf_slot, pl.ds(lo, rows)], sems.at[buf_slot])

    @pl.when(t == 0)
    def _():
        srt_ref[...] = jnp.zeros_like(srt_ref)
        _start_pieces(tabs, t, n_cls, copier(slot))

    @pl.when(t + 1 < n_tiles)
    def _():
        _start_pieces(tabs, t + 1, n_cls, copier(1 - slot))

    _await_pieces(tabs, t, ls, copier(slot))
    j = lax.broadcasted_iota(I32, (tm, ls), 1)
    pick = jnp.where((pos_ref[:, 0:1] == j) | (pos_ref[:, 1:2] == j), 1.0, 0.0).astype(BF16)
    y = jnp.dot(pick, srt_ref[slot], preferred_element_type=F32)
    xn = x_ref[...] + gate_ref[0, 0] * y
    if final:
        xn = xn * lax.rsqrt(jnp.mean(xn * xn, axis=-1, keepdims=True) + RMS_EPS) * fin_ref[...]
    xo_ref[...] = xn


def _combine(y_buf, pos_t, tabs, n_cls, xs, mod, final_gain, n_tiles, tm, tiles_per_seq, final):
    N, D = xs.shape
    nmod = mod.shape[1]
    ls = 2 * tm + N_EXPERTS * RUN_ALIGN
    out_rows = n_tiles * tm if final else N
    kwargs = {} if final else {"input_output_aliases": {5: 0}}
    grid_spec = pltpu.PrefetchScalarGridSpec(
        num_scalar_prefetch=4,
        grid=(n_tiles,),
        in_specs=[pl.BlockSpec(memory_space=pl.ANY),
                  pl.BlockSpec((tm, D), lambda t, *_: (t, 0)),
                  pl.BlockSpec((tm, 2), lambda t, *_: (t, 0)),
                  pl.BlockSpec((1, 1, 1, D), lambda t, *_: (5, jnp.minimum(t // tiles_per_seq, nmod - 1), 0, 0)),
                  pl.BlockSpec((1, D), lambda t, *_: (0, 0))],
        out_specs=pl.BlockSpec((tm, D), lambda t, *_: (t, 0)),
        scratch_shapes=[pltpu.VMEM((2, ls, D), BF16), pltpu.SemaphoreType.DMA((2,))],
    )
    return pl.pallas_call(
        functools.partial(_combine_kernel, final=final, n_tiles=n_tiles, n_cls=n_cls),
        grid_spec=grid_spec,
        out_shape=jax.ShapeDtypeStruct((out_rows, D), F32),
        compiler_params=_cparams("arbitrary"),
        name="moe_combine_final" if final else "moe_combine",
        **kwargs,
    )(*tabs, y_buf, xs, pos_t, mod, final_gain.reshape(1, D))


def _moe(f, routing, xs, mod, wg, wu, wd, final_gain, n_tok, tm, tiles_per_seq, final):
    nt = n_tok // tm
    e, gates, rank, cnt = routing
    e, gates, rank, cnt = e[:, :n_tok], gates[:, :n_tok], rank[:, :n_tok], cnt[:nt]
    n = cnt[:, :, 0].astype(I32)
    rlen = (n + RUN_ALIGN - 1) // RUN_ALIGN * RUN_ALIGN
    lo = jnp.cumsum(rlen, axis=1) - rlen
    region = jnp.sum(rlen, axis=0)
    region_pad = (region + EXPERT_BLOCK - 1) // EXPERT_BLOCK * EXPERT_BLOCK
    pends = jnp.cumsum(region_pad)
    pstarts = pends - region_pad
    go = pstarts[None, :] + jnp.cumsum(rlen, axis=0) - rlen
    n_blocks = -(-(2 * n_tok + nt * N_EXPERTS * RUN_ALIGN) // EXPERT_BLOCK) + N_EXPERTS
    blk0 = jnp.arange(n_blocks, dtype=I32) * EXPERT_BLOCK
    block_expert = jnp.minimum(jnp.sum((blk0[:, None] >= pends[None, :]).astype(I32), axis=1), N_EXPERTS - 1)
    block_used = (blk0 < (pstarts + region)[block_expert]).astype(I32)
    hot = e.reshape(2, nt, tm, 1) == jnp.arange(N_EXPERTS, dtype=I32)
    pos = jnp.sum(jnp.where(hot, lo[None, :, None, :], 0), axis=-1).reshape(2, n_tok) + rank
    tabs, n_cls = _chunk_tables(lo, go, rlen, tm)
    gap_tabs, gap_cls = _chunk_tables(jnp.zeros((1, N_EXPERTS), I32), (pstarts + region)[None, :],
                                      (region_pad - region)[None, :], EXPERT_BLOCK)
    buf = _dispatch(f, pos, gates.T, tabs, n_cls, gap_tabs, gap_cls, n_blocks * EXPERT_BLOCK, nt, tm)
    last_used = jnp.max(jnp.where(block_used > 0, jnp.arange(n_blocks, dtype=I32), 0))
    block_src = jnp.where(block_used > 0, jnp.arange(n_blocks, dtype=I32), last_used)
    y_buf = _experts(buf, block_expert.astype(I32), block_used, block_src, wg, wu, wd)
    return _combine(y_buf, pos.T, tabs, n_cls, xs, mod, final_gain, nt, tm, tiles_per_seq, final)


def _rope_tables(S, tm):
    rows = S // GRID_W
    row = jnp.repeat(jnp.arange(rows, dtype=F32), GRID_W)
    col = jnp.tile(jnp.arange(GRID_W, dtype=F32), rows)
    half = ATTN_HEAD_DIM // 4
    inv_freq = ROPE_THETA ** (-jnp.arange(half, dtype=F32) / half)
    ang_r = row[:, None] * inv_freq[None, :]
    ang_c = col[:, None] * inv_freq[None, :]
    zeros = jnp.zeros_like(ang_r)
    cos = jnp.concatenate([jnp.cos(ang_r)] * 2 + [jnp.cos(ang_c)] * 2, axis=1)
    s1 = jnp.concatenate([-jnp.sin(ang_r), zeros, -jnp.sin(ang_c), zeros], axis=1)
    s2 = jnp.concatenate([zeros, jnp.sin(ang_r), zeros, jnp.sin(ang_c)], axis=1)
    def finish(tab, fill):
        tab = jnp.tile(tab, (1, LANES // ATTN_HEAD_DIM))
        return jnp.concatenate([tab, jnp.full((tm, LANES), fill, F32)], axis=0)
    return finish(cos, 1.0), finish(s1, 0.0), finish(s2, 0.0)


def kernel(x, c, ctx, c_ctx, ada_w, ada_b, norm_mix_g, norm_ffn_g, final_g, attn_w_qkv, attn_w_o, attn_sinks,
           gla_w_in, gla_w_a1, gla_w_a2, gla_b_a, gla_norm_g, gla_w_o, router_w, router_b,
           moe_w_gate, moe_w_up, moe_w_down):
    B, S, D = x.shape
    C = ctx.shape[1]
    depth = ada_w.shape[0]
    tm = TOKEN_TILE
    assert S % tm == 0 and (B * C) % tm == 0 and S % ATTN_BLOCK == 0 and C % ATTN_BLOCK == 0
    assert (B * S) % C == 0 and S % GLA_CHUNK == 0 and C % GLA_CHUNK == 0
    n_lat = B * S
    n_lat_tiles = n_lat // tm
    tiles_per_seq = S // tm

    rpad = -(-(B + 1) // 8) * 8
    cc = jnp.zeros((rpad, D), F32).at[:B].set(c).at[B].set(c_ctx)
    mods = _ada_table(cc, ada_w, ada_b)
    mods = mods[:, :B + 1].reshape(depth, B + 1, 6, 1, D).transpose(0, 2, 1, 3, 4)

    xs = (x.reshape(n_lat, D), ctx.reshape(B * C, D))
    ptm = PROJ_TILE if S % PROJ_TILE == 0 and (B * C) % PROJ_TILE == 0 else tm
    rope = _rope_tables(S, ptm)
    q_dim = ATTN_KV_HEADS * ATTN_GROUP * ATTN_HEAD_DIM
    kv_dim = ATTN_KV_HEADS * ATTN_HEAD_DIM
    kd = GLA_HEADS * GLA_KEY_DIM
    vd = GLA_HEADS * GLA_VAL_DIM

    def dup_heads(w):
        w = w.reshape(D, ATTN_KV_HEADS, 1, ATTN_HEAD_DIM)
        return jnp.broadcast_to(w, (D, ATTN_KV_HEADS, LANES // ATTN_HEAD_DIM, ATTN_HEAD_DIM)).reshape(D, -1)

    for i in range(depth):
        last = i == depth - 1
        mod = mods[i]
        j = i // 2
        if i % 2 == 0:
            wqkv = attn_w_qkv[j]
            w = jnp.concatenate([wqkv[:, :q_dim], dup_heads(wqkv[:, q_dim:q_dim + kv_dim])], axis=1).astype(BF16)
            wvt = wqkv[:, q_dim + kv_dim:].T.astype(BF16)
            q, k, vt = _qkv_proj(xs, norm_mix_g[i], mod, 0, w, wvt, rope, ptm, n_lat // ptm, S // ptm)
            sink_row = jnp.repeat(attn_sinks[j].astype(F32).reshape(ATTN_KV_HEADS, ATTN_GROUP) * LOG2_E,
                                  ATTN_BLOCK, axis=1)
            o = _attention(q, k, vt, sink_row[:, None, :], B, S, C, not last)
            xs, f, *routing = _post_mixer(o, attn_w_o[j].astype(BF16), xs, mod, norm_ffn_g[i], router_w, router_b,
                                          tm, n_lat_tiles, tiles_per_seq)
        else:
            a1 = jnp.zeros((D, LANES), F32).at[:, :2 * GLA_GATE_RANK].set(
                jnp.concatenate([gla_w_a1[j, 0], gla_w_a1[j, 1]], axis=1))
            w = jnp.concatenate([gla_w_in[j], a1], axis=1).astype(BF16)
            w2 = jnp.zeros((LANES, 2 * kd), F32)
            w2 = w2.at[:GLA_GATE_RANK, :kd].set(gla_w_a2[j, 0]).at[GLA_GATE_RANK:2 * GLA_GATE_RANK, kd:].set(gla_w_a2[j, 1])
            ba = gla_b_a[j].reshape(1, 2 * kd).astype(F32)
            qk, v, og, la, chunk_tot = _gla_in_proj(xs, norm_mix_g[i], mod, 0, w, w2.astype(BF16), ba, tm,
                                                    n_lat_tiles, tiles_per_seq)
            o_fwd, o_bwd = _gla_scan(qk, v, la, chunk_tot, B, S, C)
            xs, f, *routing = _post_mixer(o_fwd, gla_w_o[j].astype(BF16), xs, mod, norm_ffn_g[i], router_w, router_b,
                                          tm, n_lat_tiles, tiles_per_seq,
                                          gla_extra=(o_bwd, og, gla_norm_g[j].reshape(1, GLA_VAL_DIM).astype(F32)))
        wg, wu = moe_w_gate[i].astype(BF16), moe_w_up[i].astype(BF16)
        wd = moe_w_down[i].astype(BF16)
        n_tok = n_lat if last else n_lat + B * C
        xs = _moe(f, routing, xs, mod, wg, wu, wd, final_g, n_tok, tm, tiles_per_seq, last)
    return xs.reshape(B, S, D)
```

```python
import functools

import numpy as np
import jax
import jax.numpy as jnp
from jax import lax
from jax.experimental import pallas as pl
from jax.experimental.pallas import tpu as pltpu

F32 = jnp.float32
BF16 = jnp.bfloat16
I32 = jnp.int32

LANES = 128
VMEM_LIMIT_BYTES = 56 * 1024 * 1024

RMS_EPS = 1e-6
GRID_W = 64
ROPE_THETA = 10000.0
ATTN_HEAD_DIM = 64
ATTN_KV_HEADS = 4
ATTN_GROUP = 4
ATTN_BLOCK = 128
ATTN_STEP_BLOCKS = 2
LOG2_E = 1.4426950408889634
ATTN_Q_SCALE = ATTN_HEAD_DIM ** -0.5 * LOG2_E
GLA_HEADS = 4
GLA_KEY_DIM = 128
GLA_VAL_DIM = 256
GLA_GATE_RANK = 16
GLA_GATE_NORM = 16.0
GLA_CHUNK = 128
GLA_STAT_ROWS = 64
GLA_BOUNDED_TOTAL = 40.0
N_EXPERTS = 16
N_GROUPS = 4
EXPERTS_PER_GROUP = 4
EXPERT_BLOCK = 512
RUN_ALIGN = 16
TOKEN_TILE = 512
PROJ_TILE = 1024


def _cparams(*sem):
    return pltpu.CompilerParams(dimension_semantics=sem, vmem_limit_bytes=VMEM_LIMIT_BYTES)


def _norm_mod(x, gain, shift, scale):
    h = x * lax.rsqrt(jnp.mean(x * x, axis=-1, keepdims=True) + RMS_EPS) * gain
    return h * (1.0 + scale) + shift


def _ada_kernel(c_ref, w_ref, b_ref, o_ref):
    c = c_ref[...]
    s = (c * jax.nn.sigmoid(c)).astype(BF16)
    o_ref[0] = jnp.dot(s, w_ref[0].astype(BF16), preferred_element_type=F32) + b_ref[0]


def _ada_table(cc, ada_w, ada_b):
    L, D, D6 = ada_w.shape
    R = cc.shape[0]
    tn = 1536
    return pl.pallas_call(
        _ada_kernel,
        grid=(L, D6 // tn),
        in_specs=[pl.BlockSpec((R, D), lambda l, j: (0, 0)),
                  pl.BlockSpec((1, D, tn), lambda l, j: (l, 0, j)),
                  pl.BlockSpec((1, 1, tn), lambda l, j: (l, 0, j))],
        out_specs=pl.BlockSpec((1, R, tn), lambda l, j: (l, 0, j)),
        out_shape=jax.ShapeDtypeStruct((L, R, D6), F32),
        compiler_params=_cparams("parallel", "parallel"),
        name="ada_table",
    )(cc, ada_w, ada_b.reshape(L, 1, D6))


def _stream_tile(x_refs, n_lat_tiles):
    if len(x_refs) == 1:
        return x_refs[0][...]
    return jnp.where(pl.program_id(0) < n_lat_tiles, x_refs[0][...], x_refs[1][...])


def _stream_args(xs, tm, n_lat_tiles):
    if isinstance(xs, tuple):
        D = xs[0].shape[1]
        return list(xs), [pl.BlockSpec((tm, D), lambda t: (jnp.minimum(t, n_lat_tiles - 1), 0)),
                          pl.BlockSpec((tm, D), lambda t: (jnp.maximum(t - n_lat_tiles, 0), 0))]
    return [xs], [pl.BlockSpec((tm, xs.shape[1]), lambda t: (t, 0))]


def _qkv_kernel(*refs, n_x, n_lat_tiles):
    x_refs = refs[:n_x]
    g_ref, sh_ref, sc_ref, w_ref, wvt_ref, cos_ref, s1_ref, s2_ref, q_ref, k_ref, vt_ref = refs[n_x:]
    h = _norm_mod(_stream_tile(x_refs, n_lat_tiles), g_ref[...], sh_ref[0, 0], sc_ref[0, 0]).astype(BF16)
    z = jnp.dot(h, w_ref[...], preferred_element_type=F32)
    vt_ref[...] = lax.dot_general(wvt_ref[...], h, (((1,), (1,)), ((), ())), preferred_element_type=F32).astype(BF16)
    cos, s1, s2 = cos_ref[...], s1_ref[...], s2_ref[...]
    nq = q_ref.shape[1] // LANES
    nk = k_ref.shape[1] // LANES
    for j in range(nq + nk):
        zc = z[:, j * LANES:(j + 1) * LANES]
        r = zc * cos + pltpu.roll(zc, LANES - 16, 1) * s1 + pltpu.roll(zc, 16, 1) * s2
        if j < nq:
            q_ref[:, j * LANES:(j + 1) * LANES] = (r * ATTN_Q_SCALE).astype(BF16)
        else:
            k_ref[:, (j - nq) * LANES:(j - nq + 1) * LANES] = r.astype(BF16)


def _gla_in_kernel(x_ref, g_ref, sh_ref, sc_ref, w_ref, w2_ref, ba_ref, qk_ref, v_ref, og_ref, la_ref, tot_ref):
    h = _norm_mod(x_ref[...], g_ref[...], sh_ref[0, 0], sc_ref[0, 0]).astype(BF16)
    kd = GLA_HEADS * GLA_KEY_DIM
    vd = GLA_HEADS * GLA_VAL_DIM
    a1 = jnp.dot(h, w_ref[:, 2 * kd + 2 * vd:], preferred_element_type=F32).astype(BF16)
    pre = jnp.dot(a1, w2_ref[...], preferred_element_type=F32) + ba_ref[...]
    z = jnp.dot(h, w_ref[:, :2 * kd + 2 * vd], preferred_element_type=F32)
    la = (jnp.minimum(pre, 0.0) - jnp.log1p(jnp.exp(-jnp.abs(pre)))) * (1.0 / GLA_GATE_NORM)
    la_ref[...] = la
    qk_ref[:, :kd] = (z[:, :kd] * (GLA_KEY_DIM ** -0.5)).astype(BF16)
    qk_ref[:, kd:] = z[:, kd:2 * kd].astype(BF16)
    v_ref[...] = z[:, 2 * kd:2 * kd + vd].astype(BF16)
    og_ref[...] = z[:, 2 * kd + vd:2 * kd + 2 * vd].astype(BF16)
    nc = la.shape[0] // GLA_STAT_ROWS
    tot = jnp.sum(la.reshape(nc, GLA_STAT_ROWS, la.shape[1]), axis=1)
    lane = lax.broadcasted_iota(I32, (nc, LANES), 1)
    acc = jnp.zeros((nc, LANES), F32)
    for hd in range(2 * GLA_HEADS):
        worst = jnp.min(tot[:, hd * GLA_KEY_DIM:(hd + 1) * GLA_KEY_DIM], axis=1, keepdims=True)
        acc = jnp.where(lane == hd, worst, acc)
    tot_ref[...] = acc


def _tile_specs(D, tm, n_lat_tiles, tiles_per_seq, n_mod_rows):
    def mod_idx(which):
        return lambda t: (which, jnp.minimum(t // tiles_per_seq, n_mod_rows - 1), 0, 0)
    return mod_idx, [pl.BlockSpec((tm, D), lambda t: (t, 0)),
                     pl.BlockSpec((1, D), lambda t: (0, 0))]


def _qkv_proj(xs, gain, mod, which, w, wvt, rope, tm, n_lat_tiles, tiles_per_seq):
    x_args, x_specs = _stream_args(xs, tm, n_lat_tiles)
    N = sum(a.shape[0] for a in x_args)
    D = x_args[0].shape[1]
    nmod = mod.shape[1]
    mod_idx, specs = _tile_specs(D, tm, n_lat_tiles, tiles_per_seq, nmod)
    qd = ATTN_KV_HEADS * ATTN_GROUP * ATTN_HEAD_DIM
    kd = ATTN_KV_HEADS * LANES
    vd = wvt.shape[0]
    rope_idx = lambda t: (jnp.where(t < n_lat_tiles, t % tiles_per_seq, tiles_per_seq), 0)
    in_specs = x_specs + specs[1:] + [
        pl.BlockSpec((1, 1, 1, D), mod_idx(which)),
        pl.BlockSpec((1, 1, 1, D), mod_idx(which + 1)),
        pl.BlockSpec(w.shape, lambda t: (0, 0)),
        pl.BlockSpec(wvt.shape, lambda t: (0, 0)),
        pl.BlockSpec((tm, LANES), rope_idx),
        pl.BlockSpec((tm, LANES), rope_idx),
        pl.BlockSpec((tm, LANES), rope_idx),
    ]
    return pl.pallas_call(
        functools.partial(_qkv_kernel, n_x=len(x_args), n_lat_tiles=n_lat_tiles),
        grid=(N // tm,),
        in_specs=in_specs,
        out_specs=[pl.BlockSpec((tm, qd), lambda t: (t, 0)),
                   pl.BlockSpec((tm, kd), lambda t: (t, 0)),
                   pl.BlockSpec((vd, tm), lambda t: (0, t))],
        out_shape=[jax.ShapeDtypeStruct((N, qd), BF16),
                   jax.ShapeDtypeStruct((N, kd), BF16),
                   jax.ShapeDtypeStruct((vd, N), BF16)],
        compiler_params=_cparams("parallel"),
        name="attn_qkv_proj",
    )(*x_args, gain.reshape(1, D), mod, mod, w, wvt, *rope)


def _gla_in_proj(xs, gain, mod, which, w, w2, ba, tm, n_lat_tiles, tiles_per_seq):
    N, D = xs.shape
    nmod = mod.shape[1]
    mod_idx, specs = _tile_specs(D, tm, n_lat_tiles, tiles_per_seq, nmod)
    kd = GLA_HEADS * GLA_KEY_DIM
    vd = GLA_HEADS * GLA_VAL_DIM
    in_specs = specs + [
        pl.BlockSpec((1, 1, 1, D), mod_idx(which)),
        pl.BlockSpec((1, 1, 1, D), mod_idx(which + 1)),
        pl.BlockSpec(w.shape, lambda t: (0, 0)),
        pl.BlockSpec(w2.shape, lambda t: (0, 0)),
        pl.BlockSpec((1, 2 * kd), lambda t: (0, 0)),
    ]
    row = lambda width: pl.BlockSpec((tm, width), lambda t: (t, 0))
    return pl.pallas_call(
        _gla_in_kernel,
        grid=(N // tm,),
        in_specs=in_specs,
        out_specs=[row(2 * kd), row(vd), row(vd), row(2 * kd),
                   pl.BlockSpec((tm // GLA_STAT_ROWS, LANES), lambda t: (t, 0))],
        out_shape=[jax.ShapeDtypeStruct((N, 2 * kd), BF16),
                   jax.ShapeDtypeStruct((N, vd), BF16),
                   jax.ShapeDtypeStruct((N, vd), BF16),
                   jax.ShapeDtypeStruct((N, 2 * kd), F32),
                   jax.ShapeDtypeStruct((N // GLA_STAT_ROWS, LANES), F32)],
        compiler_params=_cparams("parallel"),
        name="gla_in_proj",
    )(xs, gain.reshape(1, D), mod, mod, w, w2, ba)


def _attn_kernel(*refs, window, nq, nb):
    tq = ATTN_BLOCK
    if window:
        q_ref, kp, km, kn, kx, vp, vm, vn, vx, sink_ref, tri_ref, o_ref = refs
        k_blocks = ([lambda ks: kp[:, ks]] + [lambda ks, i=i: km[i * tq:(i + 1) * tq, ks] for i in range(nb)]
                    + [lambda ks: kn[:, ks]])
        v_blocks = ([lambda hs: vp[hs, :]] + [lambda hs, i=i: vm[hs, i * tq:(i + 1) * tq] for i in range(nb)]
                    + [lambda hs: vn[hs, :]])
    else:
        q_ref, kx, vx, sink_ref, _, o_ref = refs
    lane = lax.broadcasted_iota(I32, (tq, LANES), 1)
    first_head = lane < ATTN_HEAD_DIM
    if window:
        bias = []
        for i in range(nb):
            j = pl.program_id(1) * nb + i
            bias_prev = jnp.where(j > 0, tri_ref[0], -jnp.inf)
            bias_next = jnp.where(j < nq - 1, tri_ref[1], -jnp.inf)
            bias.append((jnp.concatenate([bias_prev] * ATTN_GROUP, axis=1),
                         jnp.concatenate([bias_next] * ATTN_GROUP, axis=1)))

    def scores(i, kh):
        ks = slice(kh * LANES, (kh + 1) * LANES)
        rows = slice(i * tq, (i + 1) * tq)
        kk = jnp.concatenate([blk(ks) for blk in k_blocks[i:i + 3]] + [kx[:, ks]], axis=0) if window else kx[:, ks]
        qa = q_ref[rows, (2 * kh) * LANES:(2 * kh + 1) * LANES]
        qb = q_ref[rows, (2 * kh + 1) * LANES:(2 * kh + 2) * LANES]
        zero = jnp.zeros_like(qa)
        qs = jnp.concatenate([jnp.where(first_head, qa, zero), jnp.where(first_head, zero, qa),
                              jnp.where(first_head, qb, zero), jnp.where(first_head, zero, qb)], axis=0)
        return lax.dot_general(kk, qs, (((1,), (1,)), ((), ())), preferred_element_type=F32)

    def softmax(s, i, kh):
        if window:
            s = jnp.concatenate([s[:tq] + bias[i][0], s[tq:2 * tq], s[2 * tq:3 * tq] + bias[i][1], s[3 * tq:]], axis=0)
        sink = sink_ref[kh]
        m = jnp.maximum(jnp.max(s, axis=0, keepdims=True), sink)
        p = jnp.exp2(s - m)
        l = jnp.sum(p, axis=0, keepdims=True) + jnp.exp2(sink - m)
        return p.astype(BF16), l

    def values(p, l, i, kh):
        hs = slice(kh * ATTN_HEAD_DIM, (kh + 1) * ATTN_HEAD_DIM)
        vv = jnp.concatenate([blk(hs) for blk in v_blocks[i:i + 3]] + [vx[hs, :]], axis=1) if window else vx[hs, :]
        o = jnp.dot(vv, p, preferred_element_type=F32) / l
        for g in range(ATTN_GROUP):
            h = kh * ATTN_GROUP + g
            o_ref[h * ATTN_HEAD_DIM:(h + 1) * ATTN_HEAD_DIM, i * tq:(i + 1) * tq] = o[:, g * tq:(g + 1) * tq].astype(BF16)

    units = [(i, kh) for i in range(nb) for kh in range(ATTN_KV_HEADS)]
    s_next = scores(*units[0])
    pending = None
    for n, unit in enumerate(units):
        s_cur = s_next
        if n + 1 < len(units):
            s_next = scores(*units[n + 1])
        if pending is not None:
            values(*pending)
        pending = softmax(s_cur, *unit) + unit
    values(*pending)


def _attention(q, k, vt, sink_row, B, S, C, need_ctx):
    N, qd = q.shape
    kd = k.shape[1]
    vd = vt.shape[0]
    tq = ATTN_BLOCK
    nq = S // tq
    nb = ATTN_STEP_BLOCKS
    assert nq % nb == 0 and C == nb * tq
    steps = nq // nb
    ctx_blk0 = (B * S) // C
    mid = lambda b, j: (b * steps + j, 0)
    prev = lambda b, j: (b * nq + jnp.maximum(j * nb - 1, 0), 0)
    nxt = lambda b, j: (b * nq + jnp.minimum(j * nb + nb, nq - 1), 0)
    cmap = lambda b, j: (ctx_blk0 + b, 0)
    cols = lambda m: (lambda b, j: m(b, j)[::-1])
    kctx = pl.BlockSpec((C, kd), cmap)
    vctx = pl.BlockSpec((vd, C), cols(cmap))
    sink_spec = pl.BlockSpec(sink_row.shape, lambda b, j: (0, 0, 0))
    key = np.arange(tq)[:, None]
    qry = np.arange(tq)[None, :]
    tri = jnp.asarray(np.stack([np.where(key >= qry, 0.0, -np.inf), np.where(key <= qry, 0.0, -np.inf)]), F32)
    o_lat = pl.pallas_call(
        functools.partial(_attn_kernel, window=True, nq=nq, nb=nb),
        grid=(B, steps),
        in_specs=[pl.BlockSpec((nb * tq, qd), mid),
                  pl.BlockSpec((tq, kd), prev), pl.BlockSpec((nb * tq, kd), mid), pl.BlockSpec((tq, kd), nxt), kctx,
                  pl.BlockSpec((vd, tq), cols(prev)), pl.BlockSpec((vd, nb * tq), cols(mid)),
                  pl.BlockSpec((vd, tq), cols(nxt)), vctx, sink_spec,
                  pl.BlockSpec(tri.shape, lambda b, j: (0, 0, 0))],
        out_specs=pl.BlockSpec((qd, nb * tq), cols(mid)),
        out_shape=jax.ShapeDtypeStruct((qd, N), BF16),
        compiler_params=_cparams("parallel", "parallel"),
        name="attn_window",
    )(q, k, k, k, k, vt, vt, vt, vt, sink_row, tri)
    if not need_ctx:
        return o_lat
    return pl.pallas_call(
        functools.partial(_attn_kernel, window=False, nq=nb, nb=nb),
        grid=(B, 1),
        in_specs=[pl.BlockSpec((C, qd), cmap), kctx, vctx, sink_spec,
                  pl.BlockSpec(memory_space=pl.ANY)],
        out_specs=pl.BlockSpec((qd, C), cols(cmap)),
        out_shape=jax.ShapeDtypeStruct((qd, N), BF16),
        input_output_aliases={4: 0},
        compiler_params=_cparams("parallel", "parallel"),
        name="attn_context",
    )(q, k, vt, sink_row, o_lat)


def _gla_constants(C):
    levels = []
    m = 1
    while m < C:
        levels.append(m)
        m *= 2
    t = np.arange(C)[:, None]
    u = np.arange(C)[None, :]
    secs = [(u <= t), (u > t)]
    masks = []
    for m in levels:
        base = (t // (2 * m)) * (2 * m)
        ref = base + m - 1
        second = t >= base + m
        secs.append(np.where(second, (u > ref) & (u <= t), (u > t) & (u <= ref)))
        masks.append((t // (2 * m) == u // (2 * m)) & second & (u < (u // (2 * m)) * (2 * m) + m))
    masks.append(t == u)
    masks.append(u <= t)
    mf = np.concatenate([s.astype(np.float32) for s in secs], axis=0)
    kf = np.stack([mk.astype(np.float32) for mk in masks], axis=0)
    mb = np.concatenate([s.astype(np.float32)[::-1, ::-1] for s in secs], axis=0)
    kb = np.stack([mk.astype(np.float32)[::-1, ::-1] for mk in masks], axis=0)
    return np.stack([mf, mb]), np.stack([kf, kb]), len(levels)


def _gla_chunk(q, k, v, g, st_ref, mm_ref, mk_ref, d, nl, bounded):
    C, dk = q.shape
    nt = (((1,), (1,)), ((), ()))
    g_hi = g.astype(BF16)
    g_lo = (g - g_hi.astype(F32)).astype(BF16)
    mmat = mm_ref[d, :2 * C] if bounded else mm_ref[d]
    e2 = jnp.dot(mmat, jnp.concatenate([g_hi, g_lo], axis=1), preferred_element_type=F32)
    ee = e2[:, :dk] + e2[:, dk:]
    ex = jnp.exp(ee)
    qf, kf = q.astype(F32), k.astype(F32)
    qe = (qf * ex[:C]).astype(BF16)
    ke = (kf * ex[C:2 * C]).astype(BF16)
    st = st_ref[...]
    o = lax.dot_general(qe, st.astype(BF16), nt, preferred_element_type=F32)
    if bounded:
        ki = (kf * jnp.exp(-ee[:C])).astype(BF16)
        a = mk_ref[d, nl + 1] * lax.dot_general(qe, ki, nt, preferred_element_type=F32)
    else:
        a = mk_ref[d, nl] * lax.dot_general(q, k, nt, preferred_element_type=F32)
        for i in range(nl):
            xl = ex[(2 + i) * C:(3 + i) * C]
            ql = (qf * xl).astype(BF16)
            kl = (kf * xl).astype(BF16)
            a = a + mk_ref[d, i] * lax.dot_general(ql, kl, nt, preferred_element_type=F32)
    o = o + jnp.dot(a.astype(BF16), v, preferred_element_type=F32)
    decay = jnp.exp(jnp.sum(g, axis=0, keepdims=True))
    st_ref[...] = st * decay + lax.dot_general(v, ke, (((0,), (0,)), ((), ())), preferred_element_type=F32)
    return o


def _gla_kernel(okf_ref, okb_ref, qf, kf, vf, lf, qb, kb, vb, lb, mm_ref, mk_ref, of_ref, ob_ref, st_ref,
                *, nl, seg_chunks, lat_segs, ctx_seg0):
    C = GLA_CHUNK
    H, DK, DV = GLA_HEADS, GLA_KEY_DIM, GLA_VAL_DIM
    b, j = pl.program_id(0), pl.program_id(1)

    @pl.when(j == 0)
    def _():
        st_ref[...] = jnp.zeros_like(st_ref)

    seg_f = jnp.where(j == 0, ctx_seg0 + b, b * lat_segs + j - 1)
    seg_b = jnp.where(j == 0, ctx_seg0 + b, b * lat_segs + lat_segs - j)

    def body(i, carry):
        cf, cb = i, seg_chunks - 1 - i
        rf = pl.ds(pl.multiple_of(cf * C, C), C)
        rb = pl.ds(pl.multiple_of(cb * C, C), C)
        bounded = (okf_ref[seg_f * seg_chunks + cf] != 0) & (okb_ref[seg_b * seg_chunks + cb] != 0)

        def advance_general():
            for h in range(H):
                ks, vs = slice(h * DK, (h + 1) * DK), slice(h * DV, (h + 1) * DV)
                o = _gla_chunk(qf[rf, ks], kf[rf, ks], vf[rf, vs], lf[rf, ks], st_ref.at[h],
                               mm_ref, mk_ref, 0, nl, False)
                of_ref[rf, vs] = o.astype(BF16)
                o = _gla_chunk(qb[rb, ks], kb[rb, ks], vb[rb, vs], lb[rb, ks], st_ref.at[H + h],
                               mm_ref, mk_ref, 1, nl, False)
                ob_ref[rb, vs] = o.astype(BF16)

        def advance_bounded():
            nt = (((1,), (1,)), ((), ()))
            kw = H * DK
            sides = ((qf, kf, vf, lf, rf, of_ref, 0), (qb, kb, vb, lb, rb, ob_ref, 1))
            pre = []
            for q_r, k_r, v_r, l_r, rows, o_r, d in sides:
                g = l_r[rows, :]
                g_hi = g.astype(BF16)
                g_lo = (g - g_hi.astype(F32)).astype(BF16)
                e2 = jnp.dot(mm_ref[d, :2 * C], jnp.concatenate([g_hi, g_lo], axis=1),
                             preferred_element_type=F32)
                pre.append((e2[:, :kw] + e2[:, kw:], g))
            units = []
            for (ee, g), (q_r, k_r, v_r, l_r, rows, o_r, d) in zip(pre, sides):
                ex = jnp.exp(ee)
                kf32 = k_r[rows, :].astype(F32)
                qe = (q_r[rows, :].astype(F32) * ex[:C]).astype(BF16)
                ke = (kf32 * ex[C:]).astype(BF16)
                ki = (kf32 * jnp.exp(-ee[:C])).astype(BF16)
                decay = jnp.exp(jnp.sum(g, axis=0, keepdims=True))
                for h in range(H):
                    ks = slice(h * DK, (h + 1) * DK)
                    units.append((d, h, qe[:, ks], ke[:, ks], ki[:, ks], decay[:, ks], v_r, rows, o_r))
            inter, score = [], []
            for d, h, qe, ke, ki, decay, v_r, rows, o_r in units:
                st = st_ref[d * H + h]
                inter.append(lax.dot_general(qe, st.astype(BF16), nt, preferred_element_type=F32))
                score.append(lax.dot_general(qe, ki, nt, preferred_element_type=F32))
            for n, (d, h, qe, ke, ki, decay, v_r, rows, o_r) in enumerate(units):
                vs = slice(h * DV, (h + 1) * DV)
                v = v_r[rows, vs]
                a = (mk_ref[d, nl + 1] * score[n]).astype(BF16)
                o = inter[n] + jnp.dot(a, v, preferred_element_type=F32)
                upd = lax.dot_general(v, ke, (((0,), (0,)), ((), ())), preferred_element_type=F32)
                o_r[rows, vs] = o.astype(BF16)
                st_ref[d * H + h] = st_ref[d * H + h] * decay + upd

        @pl.when(bounded)
        def _():
            advance_bounded()

        @pl.when(jnp.logical_not(bounded))
        def _():
            advance_general()

        return carry

    lax.fori_loop(0, seg_chunks, body, 0)


def _gla_scan(qk, v, la, chunk_tot, B, S, C):
    N = qk.shape[0]
    H, DK, DV = GLA_HEADS, GLA_KEY_DIM, GLA_VAL_DIM
    seg = C
    assert S % seg == 0 and seg % GLA_CHUNK == 0
    lat_segs = S // seg
    ctx_seg0 = (B * S) // seg
    mm, mk, nl = _gla_constants(GLA_CHUNK)
    mm = jnp.asarray(mm, BF16)
    mk = jnp.asarray(mk, F32)
    chunk_tot = jnp.sum(chunk_tot.reshape(-1, GLA_CHUNK // GLA_STAT_ROWS, LANES), axis=1)
    ok = chunk_tot[:, :2 * H] >= -GLA_BOUNDED_TOTAL
    okf = jnp.all(ok[:, :H], axis=1).astype(I32)
    okb = jnp.all(ok[:, H:], axis=1).astype(I32)
    fwd = lambda col: (lambda b, j, *_: (jnp.where(j == 0, ctx_seg0 + b, b * lat_segs + j - 1), col))
    bwd = lambda col: (lambda b, j, *_: (jnp.where(j == 0, ctx_seg0 + b, b * lat_segs + lat_segs - j), col))
    kw, vw = H * DK, H * DV
    in_specs = [
        pl.BlockSpec((seg, kw), fwd(0)), pl.BlockSpec((seg, kw), fwd(1)), pl.BlockSpec((seg, vw), fwd(0)),
        pl.BlockSpec((seg, kw), fwd(0)),
        pl.BlockSpec((seg, kw), bwd(0)), pl.BlockSpec((seg, kw), bwd(1)), pl.BlockSpec((seg, vw), bwd(0)),
        pl.BlockSpec((seg, kw), bwd(1)),
        pl.BlockSpec(mm.shape, lambda b, j, *_: (0, 0, 0)),
        pl.BlockSpec(mk.shape, lambda b, j, *_: (0, 0, 0, 0)),
    ]
    grid_spec = pltpu.PrefetchScalarGridSpec(
        num_scalar_prefetch=2,
        grid=(B, lat_segs + 1),
        in_specs=in_specs,
        out_specs=[pl.BlockSpec((seg, vw), fwd(0)), pl.BlockSpec((seg, vw), bwd(0))],
        scratch_shapes=[pltpu.VMEM((2 * H, DV, DK), F32)],
    )
    return pl.pallas_call(
        functools.partial(_gla_kernel, nl=nl, seg_chunks=seg // GLA_CHUNK, lat_segs=lat_segs, ctx_seg0=ctx_seg0),
        grid_spec=grid_spec,
        out_shape=[jax.ShapeDtypeStruct((N, vw), BF16), jax.ShapeDtypeStruct((N, vw), BF16)],
        compiler_params=_cparams("parallel", "arbitrary"),
        name="gla_scan",
    )(okf, okb, qk, qk, v, la, qk, qk, v, la, mm, mk)


def _post_kernel(*refs, gla, n_x, n_lat_tiles):
    x_refs, refs = refs[:n_x], refs[n_x:]
    route_refs = refs[-4:]
    route_in = refs[-9:-6]
    refs = refs[:-9] + refs[-6:-4]
    x_tile = _stream_tile(x_refs, n_lat_tiles)
    if gla:
        o_ref, ob_ref, og_ref, ng_ref, w_ref, gate_ref, fg_ref, fsh_ref, fsc_ref, xo_ref, f_ref = refs
        o = o_ref[...].astype(F32) + ob_ref[...].astype(F32)
        g = og_ref[...].astype(F32)
        parts = []
        for h in range(GLA_HEADS):
            oh = o[:, h * GLA_VAL_DIM:(h + 1) * GLA_VAL_DIM]
            parts.append(oh * lax.rsqrt(jnp.mean(oh * oh, axis=-1, keepdims=True) + RMS_EPS) * ng_ref[...])
        mix = (jnp.concatenate(parts, axis=1) * (g * jax.nn.sigmoid(g))).astype(BF16)
        y = jnp.dot(mix, w_ref[...], preferred_element_type=F32)
    else:
        o_ref, w_ref, gate_ref, fg_ref, fsh_ref, fsc_ref, xo_ref, f_ref = refs
        y = lax.dot_general(o_ref[...], w_ref[...], (((0,), (0,)), ((), ())), preferred_element_type=F32)
    xn = x_tile + gate_ref[0, 0] * y
    xo_ref[...] = xn
    f = _norm_mod(xn, fg_ref[...], fsh_ref[0, 0], fsc_ref[0, 0])
    f_ref[...] = f.astype(BF16)
    _route_tile(f, *route_in, *route_refs)


def _post_mixer(o, w_o, xs, mod, ffn_gain, router_w, router_b, tm, n_lat_tiles, tiles_per_seq, gla_extra=None):
    args, specs = _stream_args(xs, tm, n_lat_tiles)
    n_x = len(args)
    N = sum(a.shape[0] for a in args)
    D = args[0].shape[1]
    r_args, r_in, r_out, r_shapes = _route_io(router_w, router_b, N, D, tm)
    nmod = mod.shape[1]
    mod_idx = lambda which: (lambda t: (which, jnp.minimum(t // tiles_per_seq, nmod - 1), 0, 0))
    row = lambda width: pl.BlockSpec((tm, width), lambda t: (t, 0))
    const = lambda a: pl.BlockSpec(a.shape, lambda t: (0,) * a.ndim)
    if gla_extra is None:
        args, specs = args + [o], specs + [pl.BlockSpec((o.shape[0], tm), lambda t: (0, t))]
    else:
        o_bwd, og, ng = gla_extra
        args += [o, o_bwd, og, ng]
        specs += [row(o.shape[1]), row(o_bwd.shape[1]), row(og.shape[1]), const(ng)]
    args += [w_o, mod, ffn_gain.reshape(1, D), mod, mod]
    specs += [const(w_o), pl.BlockSpec((1, 1, 1, D), mod_idx(2)), pl.BlockSpec((1, D), lambda t: (0, 0)),
              pl.BlockSpec((1, 1, 1, D), mod_idx(3)), pl.BlockSpec((1, 1, 1, D), mod_idx(4))]
    args += r_args
    specs += r_in
    return pl.pallas_call(
        functools.partial(_post_kernel, gla=gla_extra is not None, n_x=n_x, n_lat_tiles=n_lat_tiles),
        grid=(N // tm,),
        in_specs=specs,
        out_specs=[row(D), row(D)] + r_out,
        out_shape=[jax.ShapeDtypeStruct((N, D), F32), jax.ShapeDtypeStruct((N, D), BF16)] + r_shapes,
        input_output_aliases={0: 0} if n_x == 1 else {},
        compiler_params=_cparams("parallel"),
        name="post_mixer_gla" if gla_extra is not None else "post_mixer_attn",
    )(*args)


def _route_tile(f, rw_ref, rb_ref, tri_ref, e_ref, gate_ref, rank_ref, cnt_ref):
    tm = f.shape[0]
    fh = f.astype(BF16)
    fl = (f - fh.astype(F32)).astype(BF16)
    parts = jnp.dot(jnp.concatenate([fh, fl], axis=0), rw_ref[...], preferred_element_type=F32)
    logits = (parts[:tm, :LANES] + parts[:tm, LANES:]) + (parts[tm:, :LANES] + parts[tm:, LANES:])
    lt = logits.T[:N_EXPERTS]
    scores = jax.nn.sigmoid(lt)
    sel = scores + rb_ref[...]
    srow = [sel[e:e + 1] for e in range(N_EXPERTS)]
    prow = [scores[e:e + 1] for e in range(N_EXPERTS)]
    gscore = []
    for g in range(N_GROUPS):
        a, b, c, d = srow[4 * g:4 * g + 4]
        hi1, lo1, hi2, lo2 = jnp.maximum(a, b), jnp.minimum(a, b), jnp.maximum(c, d), jnp.minimum(c, d)
        gscore.append(jnp.maximum(hi1, hi2) + jnp.maximum(jnp.minimum(hi1, hi2), jnp.maximum(lo1, lo2)))
    best, grp = gscore[0], jnp.zeros_like(gscore[0], dtype=I32)
    for g in range(1, N_GROUPS):
        better = gscore[g] > best
        grp = jnp.where(better, g, grp)
        best = jnp.where(better, gscore[g], best)
    s_in, p_in = [], []
    for k in range(EXPERTS_PER_GROUP):
        sv, pv = srow[k], prow[k]
        for g in range(1, N_GROUPS):
            sv = jnp.where(grp == g, srow[4 * g + k], sv)
            pv = jnp.where(grp == g, prow[4 * g + k], pv)
        s_in.append(sv)
        p_in.append(pv)
    i1, v1, g1 = jnp.zeros_like(grp), s_in[0], p_in[0]
    for k in range(1, EXPERTS_PER_GROUP):
        better = s_in[k] > v1
        i1 = jnp.where(better, k, i1)
        g1 = jnp.where(better, p_in[k], g1)
        v1 = jnp.where(better, s_in[k], v1)
    i2, v2, g2 = jnp.zeros_like(grp), jnp.full_like(v1, -jnp.inf), jnp.zeros_like(v1)
    for k in range(EXPERTS_PER_GROUP):
        better = (i1 != k) & (s_in[k] > v2)
        i2 = jnp.where(better, k, i2)
        g2 = jnp.where(better, p_in[k], g2)
        v2 = jnp.where(better, s_in[k], v2)
    e1 = grp * EXPERTS_PER_GROUP + i1
    e2 = grp * EXPERTS_PER_GROUP + i2
    tot = g1 + g2
    e_ref[0:1, :] = e1
    e_ref[1:2, :] = e2
    gate_ref[0:1, :] = g1 / tot
    gate_ref[1:2, :] = g2 / tot
    eid = lax.broadcasted_iota(I32, scores.shape, 0)
    hot1 = eid == e1
    hot2 = eid == e2
    onehot = jnp.where(hot1 | hot2, 1.0, 0.0).astype(BF16)
    cum = jnp.dot(onehot, tri_ref[...], preferred_element_type=F32)
    rank_ref[0:1, :] = jnp.sum(jnp.where(hot1, cum, 0.0), axis=0, keepdims=True).astype(I32) - 1
    rank_ref[1:2, :] = jnp.sum(jnp.where(hot2, cum, 0.0), axis=0, keepdims=True).astype(I32) - 1
    cnt_ref[0] = jnp.broadcast_to(cum[:, cum.shape[1] - 1:], cnt_ref.shape[1:])


def _route_io(router_w, router_b, N, D, tm):
    rw = jnp.zeros((D, LANES), F32).at[:, :N_EXPERTS].set(router_w.astype(F32))
    rw_hi = rw.astype(BF16)
    rw = jnp.concatenate([rw_hi, (rw - rw_hi.astype(F32)).astype(BF16)], axis=1)
    rb = router_b.astype(F32).reshape(N_EXPERTS, 1)
    tri = jnp.asarray(np.triu(np.ones((tm, tm), np.float32)), BF16)
    in_specs = [pl.BlockSpec((D, 2 * LANES), lambda t: (0, 0)), pl.BlockSpec((N_EXPERTS, 1), lambda t: (0, 0)),
                pl.BlockSpec((tm, tm), lambda t: (0, 0))]
    lane_row = lambda dt: (pl.BlockSpec((2, tm), lambda t: (0, t)), jax.ShapeDtypeStruct((2, N), dt))
    outs = [lane_row(I32), lane_row(F32), lane_row(I32),
            (pl.BlockSpec((1, N_EXPERTS, LANES), lambda t: (t, 0, 0)),
             jax.ShapeDtypeStruct((N // tm, N_EXPERTS, LANES), F32))]
    return [rw, rb, tri], in_specs, [o[0] for o in outs], [o[1] for o in outs]


def _chunk_tables(lo, go, rlen, tm):
    n_cls = (tm // RUN_ALIGN).bit_length()
    units = rlen // RUN_ALIGN
    cls = jnp.arange(n_cls, dtype=I32)
    flag = (units[:, :, None] >> cls) & 1
    rows = flag * (RUN_ALIGN << cls)
    above = jnp.cumsum(rows[..., ::-1], axis=-1)[..., ::-1] - rows
    src = lo[:, :, None] + above
    dst = go[:, :, None] + above
    slot = jnp.cumsum(flag, axis=1) - 1
    hit = (flag[:, None] == 1) & (slot[:, None] == jnp.arange(N_EXPERTS, dtype=I32)[None, :, None, None])
    compact = lambda a: jnp.sum(jnp.where(hit, a[:, None], 0), axis=2).transpose(0, 2, 1)
    flat = lambda a: a.reshape(-1).astype(I32)
    return (flat(compact(src)), flat(compact(dst)), flat(jnp.sum(flag, axis=1)), flat(jnp.sum(units, axis=1))), n_cls


def _start_pieces(tabs, tile, n_cls, make_copy, wait=False):
    src_ref, dst_ref, cnt_ref = tabs[:3]
    for b in range(n_cls):
        base = (tile * n_cls + b) * N_EXPERTS

        def body(i, carry, base=base, rows=RUN_ALIGN << b):
            copy = make_copy(pl.multiple_of(src_ref[base + i], RUN_ALIGN),
                             pl.multiple_of(dst_ref[base + i], RUN_ALIGN), rows)
            copy.wait() if wait else copy.start()
            return carry

        lax.fori_loop(0, cnt_ref[tile * n_cls + b], body, 0)


def _await_pieces(tabs, tile, max_rows, make_copy):
    units = tabs[3][tile]
    for b in range((max_rows // RUN_ALIGN).bit_length()):
        @pl.when(((units >> b) & 1) != 0)
        def _(rows=RUN_ALIGN << b):
            make_copy(0, 0, rows).wait()


def _dispatch_kernel(src_tab, dst_tab, cnt_tab, tot_tab, gap_src, gap_dst, gap_cnt, f_ref, pos_ref, gates_ref,
                     buf_hbm, srt_ref, zero_ref, sems, *, n_tiles, n_cls, gap_cls):
    t = pl.program_id(0)
    gaps = (gap_src, gap_dst, gap_cnt)
    gap_copy = lambda lo, go, rows: pltpu.make_async_copy(
        zero_ref.at[pl.ds(lo, rows)], buf_hbm.at[pl.ds(go, rows)], sems.at[2])

    @pl.when(t == 0)
    def _():
        zero_ref[...] = jnp.zeros_like(zero_ref)
        _start_pieces(gaps, 0, gap_cls, gap_copy)
    slot = t % 2
    tm, D = f_ref.shape
    ls = srt_ref.shape[1]
    j = lax.broadcasted_iota(I32, (ls, tm), 0)
    hit0 = pos_ref[0:1, :] == j
    hit1 = pos_ref[1:2, :] == j
    perm = jnp.where(hit0 | hit1, 1.0, 0.0).astype(BF16)
    srt_ref[slot, :, :D] = jnp.dot(perm, f_ref[...], preferred_element_type=F32).astype(BF16)
    picked = jnp.where(hit0, gates_ref[0:1, :], jnp.where(hit1, gates_ref[1:2, :], 0.0))
    g = jnp.sum(picked, axis=1, keepdims=True)
    g_hi = g.astype(BF16).astype(F32)
    g_lo = (g - g_hi).astype(BF16).astype(F32)
    lane = lax.broadcasted_iota(I32, (ls, LANES), 1)
    srt_ref[slot, :, D:] = jnp.where(lane == 0, g_hi, jnp.where(lane == 1, g_lo, 0.0)).astype(BF16)

    tabs = (src_tab, dst_tab, cnt_tab, tot_tab)

    def copier(buf_slot):
        return lambda lo, go, rows: pltpu.make_async_copy(
            srt_ref.at[buf_slot, pl.ds(lo, rows)], buf_hbm.at[pl.ds(go, rows)], sems.at[buf_slot])

    _start_pieces(tabs, t, n_cls, copier(slot))

    @pl.when(t > 0)
    def _():
        _await_pieces(tabs, t - 1, ls, copier(1 - slot))

    @pl.when(t == n_tiles - 1)
    def _():
        _await_pieces(tabs, t, ls, copier(slot))
        _start_pieces(gaps, 0, gap_cls, gap_copy, wait=True)


def _dispatch(f, pos, gates, tabs, n_cls, gap_tabs, gap_cls, n_rows, n_tiles, tm):
    D = f.shape[1]
    ls = 2 * tm + N_EXPERTS * RUN_ALIGN
    width = D + LANES
    grid_spec = pltpu.PrefetchScalarGridSpec(
        num_scalar_prefetch=7,
        grid=(n_tiles,),
        in_specs=[pl.BlockSpec((tm, D), lambda t, *_: (t, 0)), pl.BlockSpec((2, tm), lambda t, *_: (0, t)),
                  pl.BlockSpec((2, tm), lambda t, *_: (0, t))],
        out_specs=pl.BlockSpec(memory_space=pl.ANY),
        scratch_shapes=[pltpu.VMEM((2, ls, width), BF16), pltpu.VMEM((EXPERT_BLOCK, width), BF16),
                        pltpu.SemaphoreType.DMA((3,))],
    )
    return pl.pallas_call(
        functools.partial(_dispatch_kernel, n_tiles=n_tiles, n_cls=n_cls, gap_cls=gap_cls),
        grid_spec=grid_spec,
        out_shape=jax.ShapeDtypeStruct((n_rows, width), BF16),
        compiler_params=_cparams("arbitrary"),
        name="moe_dispatch",
    )(*tabs, *gap_tabs[:3], f, pos, gates)


def _expert_kernel(be_ref, bc_ref, bs_ref, x_ref, wg_ref, wu_ref, wd_ref, y_ref):
    i = pl.program_id(0)
    D = y_ref.shape[1]

    @pl.when(bc_ref[i] > 0)
    def _():
        x = x_ref[:, :D]
        gate = jnp.dot(x, wg_ref[0], preferred_element_type=F32)
        up = jnp.dot(x, wu_ref[0], preferred_element_type=F32)
        hid = (gate * jax.nn.sigmoid(gate) * up).astype(BF16)
        pieces = x_ref[:, D:].astype(F32)
        route_gate = pieces[:, 0:1] + pieces[:, 1:2]
        y_ref[...] = (jnp.dot(hid, wd_ref[0], preferred_element_type=F32) * route_gate).astype(BF16)

    @pl.when(bc_ref[i] == 0)
    def _():
        y_ref[...] = jnp.zeros_like(y_ref)


def _experts(buf, block_expert, block_count, block_src, wg, wu, wd):
    n_rows = buf.shape[0]
    D = wg.shape[1]
    nb = n_rows // EXPERT_BLOCK
    grid_spec = pltpu.PrefetchScalarGridSpec(
        num_scalar_prefetch=3,
        grid=(nb,),
        in_specs=[pl.BlockSpec((EXPERT_BLOCK, buf.shape[1]), lambda i, be, bc, bs: (bs[i], 0)),
                  pl.BlockSpec((1,) + wg.shape[1:], lambda i, be, bc, bs: (be[i], 0, 0)),
                  pl.BlockSpec((1,) + wu.shape[1:], lambda i, be, bc, bs: (be[i], 0, 0)),
                  pl.BlockSpec((1,) + wd.shape[1:], lambda i, be, bc, bs: (be[i], 0, 0))],
        out_specs=pl.BlockSpec((EXPERT_BLOCK, D), lambda i, be, bc, bs: (i, 0)),
    )
    return pl.pallas_call(
        _expert_kernel,
        grid_spec=grid_spec,
        out_shape=jax.ShapeDtypeStruct((n_rows, D), BF16),
        compiler_params=_cparams("arbitrary"),
        name="moe_experts",
    )(block_expert, block_count, block_src, buf, wg, wu, wd)


def _combine_kernel(src_tab, dst_tab, cnt_tab, tot_tab, y_hbm, x_ref, pos_ref, gate_ref, fin_ref, xo_ref,
                    srt_ref, sems, *, final, n_tiles, n_cls):
    t = pl.program_id(0)
    slot = t % 2
    tm = x_ref.shape[0]
    ls = srt_ref.shape[1]

    tabs = (src_tab, dst_tab, cnt_tab, tot_tab)

    def copier(buf_slot):
        return lambda lo, go, rows: pltpu.make_async_copy(
            y_hbm.at[pl.ds(go, rows)], srt_ref.at[buf_slot, pl.ds(lo, rows)], sems.at[buf_slot])

    @pl.when(t == 0)
    def _():
        srt_ref[...] = jnp.zeros_like(srt_ref)
        _start_pieces(tabs, t, n_cls, copier(slot))

    @pl.when(t + 1 < n_tiles)
    def _():
        _start_pieces(tabs, t + 1, n_cls, copier(1 - slot))

    _await_pieces(tabs, t, ls, copier(slot))
    j = lax.broadcasted_iota(I32, (tm, ls), 1)
    pick = jnp.where((pos_ref[:, 0:1] == j) | (pos_ref[:, 1:2] == j), 1.0, 0.0).astype(BF16)
    y = jnp.dot(pick, srt_ref[slot], preferred_element_type=F32)
    xn = x_ref[...] + gate_ref[0, 0] * y
    if final:
        xn = xn * lax.rsqrt(jnp.mean(xn * xn, axis=-1, keepdims=True) + RMS_EPS) * fin_ref[...]
    xo_ref[...] = xn


def _combine(y_buf, pos_t, tabs, n_cls, xs, mod, final_gain, n_tiles, tm, tiles_per_seq, final):
    N, D = xs.shape
    nmod = mod.shape[1]
    ls = 2 * tm + N_EXPERTS * RUN_ALIGN
    out_rows = n_tiles * tm if final else N
    kwargs = {} if final else {"input_output_aliases": {5: 0}}
    grid_spec = pltpu.PrefetchScalarGridSpec(
        num_scalar_prefetch=4,
        grid=(n_tiles,),
        in_specs=[pl.BlockSpec(memory_space=pl.ANY),
                  pl.BlockSpec((tm, D), lambda t, *_: (t, 0)),
                  pl.BlockSpec((tm, 2), lambda t, *_: (t, 0)),
                  pl.BlockSpec((1, 1, 1, D), lambda t, *_: (5, jnp.minimum(t // tiles_per_seq, nmod - 1), 0, 0)),
                  pl.BlockSpec((1, D), lambda t, *_: (0, 0))],
        out_specs=pl.BlockSpec((tm, D), lambda t, *_: (t, 0)),
        scratch_shapes=[pltpu.VMEM((2, ls, D), BF16), pltpu.SemaphoreType.DMA((2,))],
    )
    return pl.pallas_call(
        functools.partial(_combine_kernel, final=final, n_tiles=n_tiles, n_cls=n_cls),
        grid_spec=grid_spec,
        out_shape=jax.ShapeDtypeStruct((out_rows, D), F32),
        compiler_params=_cparams("arbitrary"),
        name="moe_combine_final" if final else "moe_combine",
        **kwargs,
    )(*tabs, y_buf, xs, pos_t, mod, final_gain.reshape(1, D))


def _moe(f, routing, xs, mod, wg, wu, wd, final_gain, n_tok, tm, tiles_per_seq, final):
    nt = n_tok // tm
    e, gates, rank, cnt = routing
    e, gates, rank, cnt = e[:, :n_tok], gates[:, :n_tok], rank[:, :n_tok], cnt[:nt]
    n = cnt[:, :, 0].astype(I32)
    rlen = (n + RUN_ALIGN - 1) // RUN_ALIGN * RUN_ALIGN
    lo = jnp.cumsum(rlen, axis=1) - rlen
    region = jnp.sum(rlen, axis=0)
    region_pad = (region + EXPERT_BLOCK - 1) // EXPERT_BLOCK * EXPERT_BLOCK
    pends = jnp.cumsum(region_pad)
    pstarts = pends - region_pad
    go = pstarts[None, :] + jnp.cumsum(rlen, axis=0) - rlen
    n_blocks = -(-(2 * n_tok + nt * N_EXPERTS * RUN_ALIGN) // EXPERT_BLOCK) + N_EXPERTS
    blk0 = jnp.arange(n_blocks, dtype=I32) * EXPERT_BLOCK
    block_expert = jnp.minimum(jnp.sum((blk0[:, None] >= pends[None, :]).astype(I32), axis=1), N_EXPERTS - 1)
    block_used = (blk0 < (pstarts + region)[block_expert]).astype(I32)
    hot = e.reshape(2, nt, tm, 1) == jnp.arange(N_EXPERTS, dtype=I32)
    pos = jnp.sum(jnp.where(hot, lo[None, :, None, :], 0), axis=-1).reshape(2, n_tok) + rank
    tabs, n_cls = _chunk_tables(lo, go, rlen, tm)
    gap_tabs, gap_cls = _chunk_tables(jnp.zeros((1, N_EXPERTS), I32), (pstarts + region)[None, :],
                                      (region_pad - region)[None, :], EXPERT_BLOCK)
    buf = _dispatch(f, pos, gates, tabs, n_cls, gap_tabs, gap_cls, n_blocks * EXPERT_BLOCK, nt, tm)
    last_used = jnp.max(jnp.where(block_used > 0, jnp.arange(n_blocks, dtype=I32), 0))
    block_src = jnp.where(block_used > 0, jnp.arange(n_blocks, dtype=I32), last_used)
    y_buf = _experts(buf, block_expert.astype(I32), block_used, block_src, wg, wu, wd)
    return _combine(y_buf, pos.T, tabs, n_cls, xs, mod, final_gain, nt, tm, tiles_per_seq, final)


def _rope_tables(S, tm):
    rows = S // GRID_W
    row = jnp.repeat(jnp.arange(rows, dtype=F32), GRID_W)
    col = jnp.tile(jnp.arange(GRID_W, dtype=F32), rows)
    half = ATTN_HEAD_DIM // 4
    inv_freq = ROPE_THETA ** (-jnp.arange(half, dtype=F32) / half)
    ang_r = row[:, None] * inv_freq[None, :]
    ang_c = col[:, None] * inv_freq[None, :]
    zeros = jnp.zeros_like(ang_r)
    cos = jnp.concatenate([jnp.cos(ang_r)] * 2 + [jnp.cos(ang_c)] * 2, axis=1)
    s1 = jnp.concatenate([-jnp.sin(ang_r), zeros, -jnp.sin(ang_c), zeros], axis=1)
    s2 = jnp.concatenate([zeros, jnp.sin(ang_r), zeros, jnp.sin(ang_c)], axis=1)
    def finish(tab, fill):
        tab = jnp.tile(tab, (1, LANES // ATTN_HEAD_DIM))
        return jnp.concatenate([tab, jnp.full((tm, LANES), fill, F32)], axis=0)
    return finish(cos, 1.0), finish(s1, 0.0), finish(s2, 0.0)


def kernel(x, c, ctx, c_ctx, ada_w, ada_b, norm_mix_g, norm_ffn_g, final_g, attn_w_qkv, attn_w_o, attn_sinks,
           gla_w_in, gla_w_a1, gla_w_a2, gla_b_a, gla_norm_g, gla_w_o, router_w, router_b,
           moe_w_gate, moe_w_up, moe_w_down):
    B, S, D = x.shape
    C = ctx.shape[1]
    depth = ada_w.shape[0]
    tm = TOKEN_TILE
    assert S % tm == 0 and (B * C) % tm == 0 and S % ATTN_BLOCK == 0 and C % ATTN_BLOCK == 0
    assert (B * S) % C == 0 and S % GLA_CHUNK == 0 and C % GLA_CHUNK == 0
    n_lat = B * S
    n_lat_tiles = n_lat // tm
    tiles_per_seq = S // tm

    rpad = -(-(B + 1) // 8) * 8
    cc = jnp.zeros((rpad, D), F32).at[:B].set(c).at[B].set(c_ctx)
    mods = _ada_table(cc, ada_w, ada_b)
    mods = mods[:, :B + 1].reshape(depth, B + 1, 6, 1, D).transpose(0, 2, 1, 3, 4)

    xs = (x.reshape(n_lat, D), ctx.reshape(B * C, D))
    ptm = PROJ_TILE if S % PROJ_TILE == 0 and (B * C) % PROJ_TILE == 0 else tm
    rope = _rope_tables(S, ptm)
    q_dim = ATTN_KV_HEADS * ATTN_GROUP * ATTN_HEAD_DIM
    kv_dim = ATTN_KV_HEADS * ATTN_HEAD_DIM
    kd = GLA_HEADS * GLA_KEY_DIM
    vd = GLA_HEADS * GLA_VAL_DIM

    def dup_heads(w):
        w = w.reshape(D, ATTN_KV_HEADS, 1, ATTN_HEAD_DIM)
        return jnp.broadcast_to(w, (D, ATTN_KV_HEADS, LANES // ATTN_HEAD_DIM, ATTN_HEAD_DIM)).reshape(D, -1)

    for i in range(depth):
        last = i == depth - 1
        mod = mods[i]
        j = i // 2
        if i % 2 == 0:
            wqkv = attn_w_qkv[j]
            w = jnp.concatenate([wqkv[:, :q_dim], dup_heads(wqkv[:, q_dim:q_dim + kv_dim])], axis=1).astype(BF16)
            wvt = wqkv[:, q_dim + kv_dim:].T.astype(BF16)
            q, k, vt = _qkv_proj(xs, norm_mix_g[i], mod, 0, w, wvt, rope, ptm, n_lat // ptm, S // ptm)
            sink_row = jnp.repeat(attn_sinks[j].astype(F32).reshape(ATTN_KV_HEADS, ATTN_GROUP) * LOG2_E,
                                  ATTN_BLOCK, axis=1)
            o = _attention(q, k, vt, sink_row[:, None, :], B, S, C, not last)
            xs, f, *routing = _post_mixer(o, attn_w_o[j].astype(BF16), xs, mod, norm_ffn_g[i], router_w, router_b,
                                          tm, n_lat_tiles, tiles_per_seq)
        else:
            a1 = jnp.zeros((D, LANES), F32).at[:, :2 * GLA_GATE_RANK].set(
                jnp.concatenate([gla_w_a1[j, 0], gla_w_a1[j, 1]], axis=1))
            w = jnp.concatenate([gla_w_in[j], a1], axis=1).astype(BF16)
            w2 = jnp.zeros((LANES, 2 * kd), F32)
            w2 = w2.at[:GLA_GATE_RANK, :kd].set(gla_w_a2[j, 0]).at[GLA_GATE_RANK:2 * GLA_GATE_RANK, kd:].set(gla_w_a2[j, 1])
            ba = gla_b_a[j].reshape(1, 2 * kd).astype(F32)
            qk, v, og, la, chunk_tot = _gla_in_proj(xs, norm_mix_g[i], mod, 0, w, w2.astype(BF16), ba, tm,
                                                    n_lat_tiles, tiles_per_seq)
            o_fwd, o_bwd = _gla_scan(qk, v, la, chunk_tot, B, S, C)
            xs, f, *routing = _post_mixer(o_fwd, gla_w_o[j].astype(BF16), xs, mod, norm_ffn_g[i], router_w, router_b,
                                          tm, n_lat_tiles, tiles_per_seq,
                                          gla_extra=(o_bwd, og, gla_norm_g[j].reshape(1, GLA_VAL_DIM).astype(F32)))
        wg, wu = moe_w_gate[i].astype(BF16), moe_w_up[i].astype(BF16)
        wd = moe_w_down[i].astype(BF16)
        n_tok = n_lat if last else n_lat + B * C
        xs = _moe(f, routing, xs, mod, wg, wu, wd, final_g, n_tok, tm, tiles_per_seq, last)
    return xs.reshape(B, S, D)
```

```python
import functools

import numpy as np
import jax
import jax.numpy as jnp
from jax import lax
from jax.experimental import pallas as pl
from jax.experimental.pallas import tpu as pltpu

F32 = jnp.float32
BF16 = jnp.bfloat16
I32 = jnp.int32

LANES = 128
VMEM_LIMIT_BYTES = 56 * 1024 * 1024

RMS_EPS = 1e-6
GRID_W = 64
ROPE_THETA = 10000.0
ATTN_HEAD_DIM = 64
ATTN_KV_HEADS = 4
ATTN_GROUP = 4
ATTN_BLOCK = 128
ATTN_STEP_BLOCKS = 2
LOG2_E = 1.4426950408889634
ATTN_Q_SCALE = ATTN_HEAD_DIM ** -0.5 * LOG2_E
GLA_HEADS = 4
GLA_KEY_DIM = 128
GLA_VAL_DIM = 256
GLA_GATE_RANK = 16
GLA_GATE_NORM = 16.0
GLA_CHUNK = 128
GLA_STAT_ROWS = 64
GLA_BOUNDED_TOTAL = 40.0
N_EXPERTS = 16
N_GROUPS = 4
EXPERTS_PER_GROUP = 4
EXPERT_BLOCK = 512
RUN_ALIGN = 16
TOKEN_TILE = 512
PROJ_TILE = 1024


def _cparams(*sem):
    return pltpu.CompilerParams(dimension_semantics=sem, vmem_limit_bytes=VMEM_LIMIT_BYTES)


def _norm_mod(x, gain, shift, scale):
    h = x * lax.rsqrt(jnp.mean(x * x, axis=-1, keepdims=True) + RMS_EPS) * gain
    return h * (1.0 + scale) + shift


def _ada_kernel(c_ref, w_ref, b_ref, o_ref):
    c = c_ref[...]
    s = (c * jax.nn.sigmoid(c)).astype(BF16)
    o_ref[0] = jnp.dot(s, w_ref[0].astype(BF16), preferred_element_type=F32) + b_ref[0]


def _ada_table(cc, ada_w, ada_b):
    L, D, D6 = ada_w.shape
    R = cc.shape[0]
    tn = 1536
    return pl.pallas_call(
        _ada_kernel,
        grid=(L, D6 // tn),
        in_specs=[pl.BlockSpec((R, D), lambda l, j: (0, 0)),
                  pl.BlockSpec((1, D, tn), lambda l, j: (l, 0, j)),
                  pl.BlockSpec((1, 1, tn), lambda l, j: (l, 0, j))],
        out_specs=pl.BlockSpec((1, R, tn), lambda l, j: (l, 0, j)),
        out_shape=jax.ShapeDtypeStruct((L, R, D6), F32),
        compiler_params=_cparams("parallel", "parallel"),
        name="ada_table",
    )(cc, ada_w, ada_b.reshape(L, 1, D6))


def _stream_tile(x_refs, n_lat_tiles):
    if len(x_refs) == 1:
        return x_refs[0][...]
    return jnp.where(pl.program_id(0) < n_lat_tiles, x_refs[0][...], x_refs[1][...])


def _stream_args(xs, tm, n_lat_tiles):
    if isinstance(xs, tuple):
        D = xs[0].shape[1]
        return list(xs), [pl.BlockSpec((tm, D), lambda t: (jnp.minimum(t, n_lat_tiles - 1), 0)),
                          pl.BlockSpec((tm, D), lambda t: (jnp.maximum(t - n_lat_tiles, 0), 0))]
    return [xs], [pl.BlockSpec((tm, xs.shape[1]), lambda t: (t, 0))]


def _qkv_kernel(*refs, n_x, n_lat_tiles):
    x_refs = refs[:n_x]
    g_ref, sh_ref, sc_ref, w_ref, wvt_ref, cos_ref, s1_ref, s2_ref, q_ref, k_ref, vt_ref = refs[n_x:]
    h = _norm_mod(_stream_tile(x_refs, n_lat_tiles), g_ref[...], sh_ref[0, 0], sc_ref[0, 0]).astype(BF16)
    z = jnp.dot(h, w_ref[...], preferred_element_type=F32)
    vt = lax.dot_general(wvt_ref[...], h, (((1,), (1,)), ((), ())), preferred_element_type=F32).astype(BF16)
    for c in range(vt_ref.shape[0]):
        vt_ref[c] = vt[:, c * ATTN_BLOCK:(c + 1) * ATTN_BLOCK]
    cos, s1, s2 = cos_ref[...], s1_ref[...], s2_ref[...]
    nq = q_ref.shape[1] // LANES
    nk = k_ref.shape[1] // LANES
    for j in range(nq + nk):
        zc = z[:, j * LANES:(j + 1) * LANES]
        r = zc * cos + pltpu.roll(zc, LANES - 16, 1) * s1 + pltpu.roll(zc, 16, 1) * s2
        if j < nq:
            q_ref[:, j * LANES:(j + 1) * LANES] = (r * ATTN_Q_SCALE).astype(BF16)
        else:
            k_ref[:, (j - nq) * LANES:(j - nq + 1) * LANES] = r.astype(BF16)


def _gla_in_kernel(x_ref, g_ref, sh_ref, sc_ref, w_ref, w2_ref, ba_ref, qk_ref, v_ref, og_ref, la_ref, tot_ref):
    h = _norm_mod(x_ref[...], g_ref[...], sh_ref[0, 0], sc_ref[0, 0]).astype(BF16)
    kd = GLA_HEADS * GLA_KEY_DIM
    vd = GLA_HEADS * GLA_VAL_DIM
    a1 = jnp.dot(h, w_ref[:, 2 * kd + 2 * vd:], preferred_element_type=F32).astype(BF16)
    pre = jnp.dot(a1, w2_ref[...], preferred_element_type=F32) + ba_ref[...]
    z = jnp.dot(h, w_ref[:, :2 * kd + 2 * vd], preferred_element_type=F32)
    la = (jnp.minimum(pre, 0.0) - jnp.log1p(jnp.exp(-jnp.abs(pre)))) * (1.0 / GLA_GATE_NORM)
    la_ref[...] = la
    qk_ref[:, :kd] = (z[:, :kd] * (GLA_KEY_DIM ** -0.5)).astype(BF16)
    qk_ref[:, kd:] = z[:, kd:2 * kd].astype(BF16)
    v_ref[...] = z[:, 2 * kd:2 * kd + vd].astype(BF16)
    og_ref[...] = z[:, 2 * kd + vd:2 * kd + 2 * vd].astype(BF16)
    nc = la.shape[0] // GLA_STAT_ROWS
    tot = jnp.sum(la.reshape(nc, GLA_STAT_ROWS, la.shape[1]), axis=1)
    lane = lax.broadcasted_iota(I32, (nc, LANES), 1)
    acc = jnp.zeros((nc, LANES), F32)
    for hd in range(2 * GLA_HEADS):
        worst = jnp.min(tot[:, hd * GLA_KEY_DIM:(hd + 1) * GLA_KEY_DIM], axis=1, keepdims=True)
        acc = jnp.where(lane == hd, worst, acc)
    tot_ref[...] = acc


def _tile_specs(D, tm, n_lat_tiles, tiles_per_seq, n_mod_rows):
    def mod_idx(which):
        return lambda t: (which, jnp.minimum(t // tiles_per_seq, n_mod_rows - 1), 0, 0)
    return mod_idx, [pl.BlockSpec((tm, D), lambda t: (t, 0)),
                     pl.BlockSpec((1, D), lambda t: (0, 0))]


def _qkv_proj(xs, gain, mod, which, w, wvt, rope, tm, n_lat_tiles, tiles_per_seq):
    x_args, x_specs = _stream_args(xs, tm, n_lat_tiles)
    N = sum(a.shape[0] for a in x_args)
    D = x_args[0].shape[1]
    nmod = mod.shape[1]
    mod_idx, specs = _tile_specs(D, tm, n_lat_tiles, tiles_per_seq, nmod)
    qd = ATTN_KV_HEADS * ATTN_GROUP * ATTN_HEAD_DIM
    kd = ATTN_KV_HEADS * LANES
    vd = wvt.shape[0]
    rope_idx = lambda t: (jnp.where(t < n_lat_tiles, t % tiles_per_seq, tiles_per_seq), 0)
    in_specs = x_specs + specs[1:] + [
        pl.BlockSpec((1, 1, 1, D), mod_idx(which)),
        pl.BlockSpec((1, 1, 1, D), mod_idx(which + 1)),
        pl.BlockSpec(w.shape, lambda t: (0, 0)),
        pl.BlockSpec(wvt.shape, lambda t: (0, 0)),
        pl.BlockSpec((tm, LANES), rope_idx),
        pl.BlockSpec((tm, LANES), rope_idx),
        pl.BlockSpec((tm, LANES), rope_idx),
    ]
    return pl.pallas_call(
        functools.partial(_qkv_kernel, n_x=len(x_args), n_lat_tiles=n_lat_tiles),
        grid=(N // tm,),
        in_specs=in_specs,
        out_specs=[pl.BlockSpec((tm, qd), lambda t: (t, 0)),
                   pl.BlockSpec((tm, kd), lambda t: (t, 0)),
                   pl.BlockSpec((tm // ATTN_BLOCK, vd, ATTN_BLOCK), lambda t: (t, 0, 0))],
        out_shape=[jax.ShapeDtypeStruct((N, qd), BF16),
                   jax.ShapeDtypeStruct((N, kd), BF16),
                   jax.ShapeDtypeStruct((N // ATTN_BLOCK, vd, ATTN_BLOCK), BF16)],
        compiler_params=_cparams("parallel"),
        name="attn_qkv_proj",
    )(*x_args, gain.reshape(1, D), mod, mod, w, wvt, *rope)


def _gla_in_proj(xs, gain, mod, which, w, w2, ba, tm, n_lat_tiles, tiles_per_seq):
    N, D = xs.shape
    nmod = mod.shape[1]
    mod_idx, specs = _tile_specs(D, tm, n_lat_tiles, tiles_per_seq, nmod)
    kd = GLA_HEADS * GLA_KEY_DIM
    vd = GLA_HEADS * GLA_VAL_DIM
    in_specs = specs + [
        pl.BlockSpec((1, 1, 1, D), mod_idx(which)),
        pl.BlockSpec((1, 1, 1, D), mod_idx(which + 1)),
        pl.BlockSpec(w.shape, lambda t: (0, 0)),
        pl.BlockSpec(w2.shape, lambda t: (0, 0)),
        pl.BlockSpec((1, 2 * kd), lambda t: (0, 0)),
    ]
    row = lambda width: pl.BlockSpec((tm, width), lambda t: (t, 0))
    return pl.pallas_call(
        _gla_in_kernel,
        grid=(N // tm,),
        in_specs=in_specs,
        out_specs=[row(2 * kd), row(vd), row(vd), row(2 * kd),
                   pl.BlockSpec((tm // GLA_STAT_ROWS, LANES), lambda t: (t, 0))],
        out_shape=[jax.ShapeDtypeStruct((N, 2 * kd), BF16),
                   jax.ShapeDtypeStruct((N, vd), BF16),
                   jax.ShapeDtypeStruct((N, vd), BF16),
                   jax.ShapeDtypeStruct((N, 2 * kd), F32),
                   jax.ShapeDtypeStruct((N // GLA_STAT_ROWS, LANES), F32)],
        compiler_params=_cparams("parallel"),
        name="gla_in_proj",
    )(xs, gain.reshape(1, D), mod, mod, w, w2, ba)


def _attn_kernel(*refs, window, nq, nb):
    tq = ATTN_BLOCK
    if window:
        q_ref, kp, km, kn, kx, vp, vm, vn, vx, sink_ref, tri_ref, o_ref = refs
        k_blocks = ([lambda ks: kp[:, ks]] + [lambda ks, i=i: km[i * tq:(i + 1) * tq, ks] for i in range(nb)]
                    + [lambda ks: kn[:, ks]])
        v_blocks = ([lambda hs: vp[0, hs, :]] + [lambda hs, i=i: vm[i, hs, :] for i in range(nb)]
                    + [lambda hs: vn[0, hs, :]])
    else:
        q_ref, kx, vx, sink_ref, _, o_ref = refs
    lane = lax.broadcasted_iota(I32, (tq, LANES), 1)
    first_head = lane < ATTN_HEAD_DIM
    if window:
        bias = []
        for i in range(nb):
            j = pl.program_id(1) * nb + i
            bias_prev = jnp.where(j > 0, tri_ref[0], -jnp.inf)
            bias_next = jnp.where(j < nq - 1, tri_ref[1], -jnp.inf)
            bias.append((jnp.concatenate([bias_prev] * ATTN_GROUP, axis=1),
                         jnp.concatenate([bias_next] * ATTN_GROUP, axis=1)))

    def scores(i, kh):
        ks = slice(kh * LANES, (kh + 1) * LANES)
        rows = slice(i * tq, (i + 1) * tq)
        kk = jnp.concatenate([blk(ks) for blk in k_blocks[i:i + 3]] + [kx[:, ks]], axis=0) if window else kx[:, ks]
        qa = q_ref[rows, (2 * kh) * LANES:(2 * kh + 1) * LANES]
        qb = q_ref[rows, (2 * kh + 1) * LANES:(2 * kh + 2) * LANES]
        zero = jnp.zeros_like(qa)
        qs = jnp.concatenate([jnp.where(first_head, qa, zero), jnp.where(first_head, zero, qa),
                              jnp.where(first_head, qb, zero), jnp.where(first_head, zero, qb)], axis=0)
        return lax.dot_general(kk, qs, (((1,), (1,)), ((), ())), preferred_element_type=F32)

    def softmax(s, i, kh):
        if window:
            s = jnp.concatenate([s[:tq] + bias[i][0], s[tq:2 * tq], s[2 * tq:3 * tq] + bias[i][1], s[3 * tq:]], axis=0)
        sink = sink_ref[kh]
        m = jnp.maximum(jnp.max(s, axis=0, keepdims=True), sink)
        p = jnp.exp2(s - m)
        l = jnp.sum(p, axis=0, keepdims=True) + jnp.exp2(sink - m)
        return p.astype(BF16), l

    def values(p, l, i, kh):
        hs = slice(kh * ATTN_HEAD_DIM, (kh + 1) * ATTN_HEAD_DIM)
        v_ctx = [vx[c, hs, :] for c in range(vx.shape[0])]
        vv = jnp.concatenate(([blk(hs) for blk in v_blocks[i:i + 3]] if window else []) + v_ctx, axis=1)
        o = jnp.dot(vv, p, preferred_element_type=F32) / l
        for g in range(ATTN_GROUP):
            h = kh * ATTN_GROUP + g
            o_ref[i, h * ATTN_HEAD_DIM:(h + 1) * ATTN_HEAD_DIM, :] = o[:, g * tq:(g + 1) * tq].astype(BF16)

    units = [(i, kh) for i in range(nb) for kh in range(ATTN_KV_HEADS)]
    s_next = scores(*units[0])
    pending = None
    for n, unit in enumerate(units):
        s_cur = s_next
        if n + 1 < len(units):
            s_next = scores(*units[n + 1])
        if pending is not None:
            values(*pending)
        pending = softmax(s_cur, *unit) + unit
    values(*pending)


def _attention(q, k, vt, sink_row, B, S, C, need_ctx):
    N, qd = q.shape
    kd = k.shape[1]
    vd = vt.shape[1]
    tq = ATTN_BLOCK
    nq = S // tq
    nb = ATTN_STEP_BLOCKS
    assert nq % nb == 0 and C == nb * tq
    steps = nq // nb
    ctx_blk0 = (B * S) // C
    mid = lambda b, j: (b * steps + j, 0)
    prev = lambda b, j: (b * nq + jnp.maximum(j * nb - 1, 0), 0)
    nxt = lambda b, j: (b * nq + jnp.minimum(j * nb + nb, nq - 1), 0)
    cmap = lambda b, j: (ctx_blk0 + b, 0)
    cols = lambda m: (lambda b, j: m(b, j)[::-1])
    kctx = pl.BlockSpec((C, kd), cmap)
    slab = lambda n, m: pl.BlockSpec((n, vd, tq), lambda b, j: (m(b, j)[0], 0, 0))
    vctx = slab(C // tq, cmap)
    sink_spec = pl.BlockSpec(sink_row.shape, lambda b, j: (0, 0, 0))
    key = np.arange(tq)[:, None]
    qry = np.arange(tq)[None, :]
    tri = jnp.asarray(np.stack([np.where(key >= qry, 0.0, -np.inf), np.where(key <= qry, 0.0, -np.inf)]), F32)
    o_lat = pl.pallas_call(
        functools.partial(_attn_kernel, window=True, nq=nq, nb=nb),
        grid=(B, steps),
        in_specs=[pl.BlockSpec((nb * tq, qd), mid),
                  pl.BlockSpec((tq, kd), prev), pl.BlockSpec((nb * tq, kd), mid), pl.BlockSpec((tq, kd), nxt), kctx,
                  slab(1, prev), slab(nb, mid), slab(1, nxt), vctx, sink_spec,
                  pl.BlockSpec(tri.shape, lambda b, j: (0, 0, 0))],
        out_specs=pl.BlockSpec((nb, qd, tq), lambda b, j: (mid(b, j)[0], 0, 0)),
        out_shape=jax.ShapeDtypeStruct((N // tq, qd, tq), BF16),
        compiler_params=_cparams("parallel", "parallel"),
        name="attn_window",
    )(q, k, k, k, k, vt, vt, vt, vt, sink_row, tri)
    if not need_ctx:
        return o_lat
    return pl.pallas_call(
        functools.partial(_attn_kernel, window=False, nq=nb, nb=nb),
        grid=(B, 1),
        in_specs=[pl.BlockSpec((C, qd), cmap), kctx, vctx, sink_spec,
                  pl.BlockSpec(memory_space=pl.ANY)],
        out_specs=pl.BlockSpec((nb, qd, tq), lambda b, j: (cmap(b, j)[0], 0, 0)),
        out_shape=jax.ShapeDtypeStruct((N // tq, qd, tq), BF16),
        input_output_aliases={4: 0},
        compiler_params=_cparams("parallel", "parallel"),
        name="attn_context",
    )(q, k, vt, sink_row, o_lat)


def _gla_constants(C):
    levels = []
    m = 1
    while m < C:
        levels.append(m)
        m *= 2
    t = np.arange(C)[:, None]
    u = np.arange(C)[None, :]
    secs = [(u <= t), (u > t)]
    masks = []
    for m in levels:
        base = (t // (2 * m)) * (2 * m)
        ref = base + m - 1
        second = t >= base + m
        secs.append(np.where(second, (u > ref) & (u <= t), (u > t) & (u <= ref)))
        masks.append((t // (2 * m) == u // (2 * m)) & second & (u < (u // (2 * m)) * (2 * m) + m))
    masks.append(t == u)
    masks.append(u <= t)
    mf = np.concatenate([s.astype(np.float32) for s in secs], axis=0)
    kf = np.stack([mk.astype(np.float32) for mk in masks], axis=0)
    mb = np.concatenate([s.astype(np.float32)[::-1, ::-1] for s in secs], axis=0)
    kb = np.stack([mk.astype(np.float32)[::-1, ::-1] for mk in masks], axis=0)
    return np.stack([mf, mb]), np.stack([kf, kb]), len(levels)


def _gla_chunk(q, k, v, g, st_ref, mm_ref, mk_ref, d, nl, bounded):
    C, dk = q.shape
    nt = (((1,), (1,)), ((), ()))
    g_hi = g.astype(BF16)
    g_lo = (g - g_hi.astype(F32)).astype(BF16)
    mmat = mm_ref[d, :2 * C] if bounded else mm_ref[d]
    e2 = jnp.dot(mmat, jnp.concatenate([g_hi, g_lo], axis=1), preferred_element_type=F32)
    ee = e2[:, :dk] + e2[:, dk:]
    ex = jnp.exp(ee)
    qf, kf = q.astype(F32), k.astype(F32)
    qe = (qf * ex[:C]).astype(BF16)
    ke = (kf * ex[C:2 * C]).astype(BF16)
    st = st_ref[...]
    o = lax.dot_general(qe, st.astype(BF16), nt, preferred_element_type=F32)
    if bounded:
        ki = (kf * jnp.exp(-ee[:C])).astype(BF16)
        a = mk_ref[d, nl + 1] * lax.dot_general(qe, ki, nt, preferred_element_type=F32)
    else:
        a = mk_ref[d, nl] * lax.dot_general(q, k, nt, preferred_element_type=F32)
        for i in range(nl):
            xl = ex[(2 + i) * C:(3 + i) * C]
            ql = (qf * xl).astype(BF16)
            kl = (kf * xl).astype(BF16)
            a = a + mk_ref[d, i] * lax.dot_general(ql, kl, nt, preferred_element_type=F32)
    o = o + jnp.dot(a.astype(BF16), v, preferred_element_type=F32)
    decay = jnp.exp(jnp.sum(g, axis=0, keepdims=True))
    st_ref[...] = st * decay + lax.dot_general(v, ke, (((0,), (0,)), ((), ())), preferred_element_type=F32)
    return o


def _gla_kernel(okf_ref, okb_ref, qf, kf, vf, lf, qb, kb, vb, lb, mm_ref, mk_ref, of_ref, ob_ref, st_ref,
                *, nl, seg_chunks, lat_segs, ctx_seg0):
    C = GLA_CHUNK
    H, DK, DV = GLA_HEADS, GLA_KEY_DIM, GLA_VAL_DIM
    b, j = pl.program_id(0), pl.program_id(1)

    @pl.when(j == 0)
    def _():
        st_ref[...] = jnp.zeros_like(st_ref)

    seg_f = jnp.where(j == 0, ctx_seg0 + b, b * lat_segs + j - 1)
    seg_b = jnp.where(j == 0, ctx_seg0 + b, b * lat_segs + lat_segs - j)

    nt = (((1,), (1,)), ((), ()))
    kw = H * DK

    def rows_of(i):
        cf, cb = i, seg_chunks - 1 - i
        return slice(cf * C, (cf + 1) * C), slice(cb * C, (cb + 1) * C)

    def advance_general(i):
        rf, rb = rows_of(i)
        for h in range(H):
            ks, vs = slice(h * DK, (h + 1) * DK), slice(h * DV, (h + 1) * DV)
            o = _gla_chunk(qf[rf, ks], kf[rf, ks], vf[rf, vs], lf[rf, ks], st_ref.at[h],
                           mm_ref, mk_ref, 0, nl, False)
            of_ref[rf, vs] = o.astype(BF16)
            o = _gla_chunk(qb[rb, ks], kb[rb, ks], vb[rb, vs], lb[rb, ks], st_ref.at[H + h],
                           mm_ref, mk_ref, 1, nl, False)
            ob_ref[rb, vs] = o.astype(BF16)

    def bounded_front(i):
        rf, rb = rows_of(i)
        sides = ((qf, kf, vf, lf, rf, of_ref, 0), (qb, kb, vb, lb, rb, ob_ref, 1))
        pre = []
        for q_r, k_r, v_r, l_r, rows, o_r, d in sides:
            g = l_r[rows, :]
            g_hi = g.astype(BF16)
            g_lo = (g - g_hi.astype(F32)).astype(BF16)
            e2 = jnp.dot(mm_ref[d, :2 * C], jnp.concatenate([g_hi, g_lo], axis=1),
                         preferred_element_type=F32)
            pre.append((e2[:, :kw] + e2[:, kw:], g))
        units = []
        for (ee, g), (q_r, k_r, v_r, l_r, rows, o_r, d) in zip(pre, sides):
            ex = jnp.exp(ee)
            kf32 = k_r[rows, :].astype(F32)
            qe = (q_r[rows, :].astype(F32) * ex[:C]).astype(BF16)
            ke = (kf32 * ex[C:]).astype(BF16)
            ki = (kf32 * jnp.exp(-ee[:C])).astype(BF16)
            decay = jnp.exp(jnp.sum(g, axis=0, keepdims=True))
            for h in range(H):
                ks = slice(h * DK, (h + 1) * DK)
                units.append([d, h, qe[:, ks], ke[:, ks], ki[:, ks], decay[:, ks], v_r, rows, o_r])
        for unit in units:
            d, h, qe, ke, ki = unit[:5]
            score = lax.dot_general(qe, ki, nt, preferred_element_type=F32)
            unit.append((mk_ref[d, nl + 1] * score).astype(BF16))
        return units

    def bounded_back(units):
        inter = [lax.dot_general(u[2], st_ref[u[0] * H + u[1]].astype(BF16), nt, preferred_element_type=F32)
                 for u in units]
        for n, (d, h, qe, ke, ki, decay, v_r, rows, o_r, a) in enumerate(units):
            vs = slice(h * DV, (h + 1) * DV)
            v = v_r[rows, vs]
            o = inter[n] + jnp.dot(a, v, preferred_element_type=F32)
            upd = lax.dot_general(v, ke, (((0,), (0,)), ((), ())), preferred_element_type=F32)
            o_r[rows, vs] = o.astype(BF16)
            st_ref[d * H + h] = st_ref[d * H + h] * decay + upd

    bounded = None
    for i in range(seg_chunks):
        ok = (okf_ref[seg_f * seg_chunks + i] != 0) & (okb_ref[seg_b * seg_chunks + seg_chunks - 1 - i] != 0)
        bounded = ok if bounded is None else bounded & ok

    @pl.when(bounded)
    def _():
        fronts = [bounded_front(i) for i in range(seg_chunks)]
        for units in fronts:
            bounded_back(units)

    @pl.when(jnp.logical_not(bounded))
    def _():
        for i in range(seg_chunks):
            advance_general(i)


def _gla_scan(qk, v, la, chunk_tot, B, S, C):
    N = qk.shape[0]
    H, DK, DV = GLA_HEADS, GLA_KEY_DIM, GLA_VAL_DIM
    seg = C
    assert S % seg == 0 and seg % GLA_CHUNK == 0
    lat_segs = S // seg
    ctx_seg0 = (B * S) // seg
    mm, mk, nl = _gla_constants(GLA_CHUNK)
    mm = jnp.asarray(mm, BF16)
    mk = jnp.asarray(mk, F32)
    chunk_tot = jnp.sum(chunk_tot.reshape(-1, GLA_CHUNK // GLA_STAT_ROWS, LANES), axis=1)
    ok = chunk_tot[:, :2 * H] >= -GLA_BOUNDED_TOTAL
    okf = jnp.all(ok[:, :H], axis=1).astype(I32)
    okb = jnp.all(ok[:, H:], axis=1).astype(I32)
    fwd = lambda col: (lambda b, j, *_: (jnp.where(j == 0, ctx_seg0 + b, b * lat_segs + j - 1), col))
    bwd = lambda col: (lambda b, j, *_: (jnp.where(j == 0, ctx_seg0 + b, b * lat_segs + lat_segs - j), col))
    kw, vw = H * DK, H * DV
    in_specs = [
        pl.BlockSpec((seg, kw), fwd(0)), pl.BlockSpec((seg, kw), fwd(1)), pl.BlockSpec((seg, vw), fwd(0)),
        pl.BlockSpec((seg, kw), fwd(0)),
        pl.BlockSpec((seg, kw), bwd(0)), pl.BlockSpec((seg, kw), bwd(1)), pl.BlockSpec((seg, vw), bwd(0)),
        pl.BlockSpec((seg, kw), bwd(1)),
        pl.BlockSpec(mm.shape, lambda b, j, *_: (0, 0, 0)),
        pl.BlockSpec(mk.shape, lambda b, j, *_: (0, 0, 0, 0)),
    ]
    grid_spec = pltpu.PrefetchScalarGridSpec(
        num_scalar_prefetch=2,
        grid=(B, lat_segs + 1),
        in_specs=in_specs,
        out_specs=[pl.BlockSpec((seg, vw), fwd(0)), pl.BlockSpec((seg, vw), bwd(0))],
        scratch_shapes=[pltpu.VMEM((2 * H, DV, DK), F32)],
    )
    return pl.pallas_call(
        functools.partial(_gla_kernel, nl=nl, seg_chunks=seg // GLA_CHUNK, lat_segs=lat_segs, ctx_seg0=ctx_seg0),
        grid_spec=grid_spec,
        out_shape=[jax.ShapeDtypeStruct((N, vw), BF16), jax.ShapeDtypeStruct((N, vw), BF16)],
        compiler_params=_cparams("parallel", "arbitrary"),
        name="gla_scan",
    )(okf, okb, qk, qk, v, la, qk, qk, v, la, mm, mk)


def _post_kernel(*refs, gla, n_x, n_lat_tiles):
    x_refs, refs = refs[:n_x], refs[n_x:]
    route_refs = refs[-4:]
    route_in = refs[-9:-6]
    refs = refs[:-9] + refs[-6:-4]
    x_tile = _stream_tile(x_refs, n_lat_tiles)
    if gla:
        o_ref, ob_ref, og_ref, ng_ref, w_ref, gate_ref, fg_ref, fsh_ref, fsc_ref, xo_ref, f_ref = refs
        o = o_ref[...].astype(F32) + ob_ref[...].astype(F32)
        g = og_ref[...].astype(F32)
        parts = []
        for h in range(GLA_HEADS):
            oh = o[:, h * GLA_VAL_DIM:(h + 1) * GLA_VAL_DIM]
            parts.append(oh * lax.rsqrt(jnp.mean(oh * oh, axis=-1, keepdims=True) + RMS_EPS) * ng_ref[...])
        mix = (jnp.concatenate(parts, axis=1) * (g * jax.nn.sigmoid(g))).astype(BF16)
        y = jnp.dot(mix, w_ref[...], preferred_element_type=F32)
    else:
        o_ref, w_ref, gate_ref, fg_ref, fsh_ref, fsc_ref, xo_ref, f_ref = refs
        o_t = jnp.concatenate([o_ref[c] for c in range(o_ref.shape[0])], axis=1)
        y = lax.dot_general(o_t, w_ref[...], (((0,), (0,)), ((), ())), preferred_element_type=F32)
    xn = x_tile + gate_ref[0, 0] * y
    xo_ref[...] = xn
    f = _norm_mod(xn, fg_ref[...], fsh_ref[0, 0], fsc_ref[0, 0])
    f_ref[...] = f.astype(BF16)
    _route_tile(f, *route_in, *route_refs)


def _post_mixer(o, w_o, xs, mod, ffn_gain, router_w, router_b, tm, n_lat_tiles, tiles_per_seq, gla_extra=None):
    args, specs = _stream_args(xs, tm, n_lat_tiles)
    n_x = len(args)
    N = sum(a.shape[0] for a in args)
    D = args[0].shape[1]
    r_args, r_in, r_out, r_shapes = _route_io(router_w, router_b, N, D, tm)
    nmod = mod.shape[1]
    mod_idx = lambda which: (lambda t: (which, jnp.minimum(t // tiles_per_seq, nmod - 1), 0, 0))
    row = lambda width: pl.BlockSpec((tm, width), lambda t: (t, 0))
    const = lambda a: pl.BlockSpec(a.shape, lambda t: (0,) * a.ndim)
    if gla_extra is None:
        args, specs = args + [o], specs + [pl.BlockSpec((tm // o.shape[2],) + o.shape[1:], lambda t: (t, 0, 0))]
    else:
        o_bwd, og, ng = gla_extra
        args += [o, o_bwd, og, ng]
        specs += [row(o.shape[1]), row(o_bwd.shape[1]), row(og.shape[1]), const(ng)]
    args += [w_o, mod, ffn_gain.reshape(1, D), mod, mod]
    specs += [const(w_o), pl.BlockSpec((1, 1, 1, D), mod_idx(2)), pl.BlockSpec((1, D), lambda t: (0, 0)),
              pl.BlockSpec((1, 1, 1, D), mod_idx(3)), pl.BlockSpec((1, 1, 1, D), mod_idx(4))]
    args += r_args
    specs += r_in
    return pl.pallas_call(
        functools.partial(_post_kernel, gla=gla_extra is not None, n_x=n_x, n_lat_tiles=n_lat_tiles),
        grid=(N // tm,),
        in_specs=specs,
        out_specs=[row(D), row(D)] + r_out,
        out_shape=[jax.ShapeDtypeStruct((N, D), F32), jax.ShapeDtypeStruct((N, D), BF16)] + r_shapes,
        input_output_aliases={0: 0} if n_x == 1 else {},
        compiler_params=_cparams("parallel"),
        name="post_mixer_gla" if gla_extra is not None else "post_mixer_attn",
    )(*args)


def _route_tile(f, rw_ref, rb_ref, tri_ref, e_ref, gate_ref, rank_ref, cnt_ref):
    tm = f.shape[0]
    fh = f.astype(BF16)
    fl = (f - fh.astype(F32)).astype(BF16)
    parts = jnp.dot(jnp.concatenate([fh, fl], axis=0), rw_ref[...], preferred_element_type=F32)
    logits = (parts[:tm, :LANES] + parts[:tm, LANES:]) + (parts[tm:, :LANES] + parts[tm:, LANES:])
    lt = logits.T[:N_EXPERTS]
    scores = jax.nn.sigmoid(lt)
    sel = scores + rb_ref[...]
    srow = [sel[e:e + 1] for e in range(N_EXPERTS)]
    prow = [scores[e:e + 1] for e in range(N_EXPERTS)]
    gscore = []
    for g in range(N_GROUPS):
        a, b, c, d = srow[4 * g:4 * g + 4]
        hi1, lo1, hi2, lo2 = jnp.maximum(a, b), jnp.minimum(a, b), jnp.maximum(c, d), jnp.minimum(c, d)
        gscore.append(jnp.maximum(hi1, hi2) + jnp.maximum(jnp.minimum(hi1, hi2), jnp.maximum(lo1, lo2)))
    best, grp = gscore[0], jnp.zeros_like(gscore[0], dtype=I32)
    for g in range(1, N_GROUPS):
        better = gscore[g] > best
        grp = jnp.where(better, g, grp)
        best = jnp.where(better, gscore[g], best)
    s_in, p_in = [], []
    for k in range(EXPERTS_PER_GROUP):
        sv, pv = srow[k], prow[k]
        for g in range(1, N_GROUPS):
            sv = jnp.where(grp == g, srow[4 * g + k], sv)
            pv = jnp.where(grp == g, prow[4 * g + k], pv)
        s_in.append(sv)
        p_in.append(pv)
    i1, v1, g1 = jnp.zeros_like(grp), s_in[0], p_in[0]
    for k in range(1, EXPERTS_PER_GROUP):
        better = s_in[k] > v1
        i1 = jnp.where(better, k, i1)
        g1 = jnp.where(better, p_in[k], g1)
        v1 = jnp.where(better, s_in[k], v1)
    i2, v2, g2 = jnp.zeros_like(grp), jnp.full_like(v1, -jnp.inf), jnp.zeros_like(v1)
    for k in range(EXPERTS_PER_GROUP):
        better = (i1 != k) & (s_in[k] > v2)
        i2 = jnp.where(better, k, i2)
        g2 = jnp.where(better, p_in[k], g2)
        v2 = jnp.where(better, s_in[k], v2)
    e1 = grp * EXPERTS_PER_GROUP + i1
    e2 = grp * EXPERTS_PER_GROUP + i2
    tot = g1 + g2
    e_ref[0:1, :] = e1
    e_ref[1:2, :] = e2
    gate_ref[0:1, :] = g1 / tot
    gate_ref[1:2, :] = g2 / tot
    eid = lax.broadcasted_iota(I32, scores.shape, 0)
    hot1 = eid == e1
    hot2 = eid == e2
    onehot = jnp.where(hot1 | hot2, 1.0, 0.0).astype(BF16)
    cum = jnp.dot(onehot, tri_ref[...], preferred_element_type=F32)
    rank_ref[0:1, :] = jnp.sum(jnp.where(hot1, cum, 0.0), axis=0, keepdims=True).astype(I32) - 1
    rank_ref[1:2, :] = jnp.sum(jnp.where(hot2, cum, 0.0), axis=0, keepdims=True).astype(I32) - 1
    cnt_ref[0] = jnp.broadcast_to(cum[:, cum.shape[1] - 1:], cnt_ref.shape[1:])


def _route_io(router_w, router_b, N, D, tm):
    rw = jnp.zeros((D, LANES), F32).at[:, :N_EXPERTS].set(router_w.astype(F32))
    rw_hi = rw.astype(BF16)
    rw = jnp.concatenate([rw_hi, (rw - rw_hi.astype(F32)).astype(BF16)], axis=1)
    rb = router_b.astype(F32).reshape(N_EXPERTS, 1)
    tri = jnp.asarray(np.triu(np.ones((tm, tm), np.float32)), BF16)
    in_specs = [pl.BlockSpec((D, 2 * LANES), lambda t: (0, 0)), pl.BlockSpec((N_EXPERTS, 1), lambda t: (0, 0)),
                pl.BlockSpec((tm, tm), lambda t: (0, 0))]
    lane_row = lambda dt: (pl.BlockSpec((2, tm), lambda t: (0, t)), jax.ShapeDtypeStruct((2, N), dt))
    outs = [lane_row(I32), lane_row(F32), lane_row(I32),
            (pl.BlockSpec((1, N_EXPERTS, LANES), lambda t: (t, 0, 0)),
             jax.ShapeDtypeStruct((N // tm, N_EXPERTS, LANES), F32))]
    return [rw, rb, tri], in_specs, [o[0] for o in outs], [o[1] for o in outs]


def _chunk_tables(lo, go, rlen, tm):
    n_cls = (tm // RUN_ALIGN).bit_length()
    units = rlen // RUN_ALIGN
    cls = jnp.arange(n_cls, dtype=I32)
    flag = (units[:, :, None] >> cls) & 1
    rows = flag * (RUN_ALIGN << cls)
    above = jnp.cumsum(rows[..., ::-1], axis=-1)[..., ::-1] - rows
    src = lo[:, :, None] + above
    dst = go[:, :, None] + above
    slot = jnp.cumsum(flag, axis=1) - 1
    hit = (flag[:, None] == 1) & (slot[:, None] == jnp.arange(N_EXPERTS, dtype=I32)[None, :, None, None])
    compact = lambda a: jnp.sum(jnp.where(hit, a[:, None], 0), axis=2).transpose(0, 2, 1)
    flat = lambda a: a.reshape(-1).astype(I32)
    return (flat(compact(src)), flat(compact(dst)), flat(jnp.sum(flag, axis=1)), flat(jnp.sum(units, axis=1))), n_cls


def _start_pieces(tabs, tile, n_cls, make_copy, wait=False):
    src_ref, dst_ref, cnt_ref = tabs[:3]
    for b in range(n_cls):
        base = (tile * n_cls + b) * N_EXPERTS

        def body(i, carry, base=base, rows=RUN_ALIGN << b):
            copy = make_copy(pl.multiple_of(src_ref[base + i], RUN_ALIGN),
                             pl.multiple_of(dst_ref[base + i], RUN_ALIGN), rows)
            copy.wait() if wait else copy.start()
            return carry

        lax.fori_loop(0, cnt_ref[tile * n_cls + b], body, 0)


def _await_pieces(tabs, tile, max_rows, make_copy):
    units = tabs[3][tile]
    for b in range((max_rows // RUN_ALIGN).bit_length()):
        @pl.when(((units >> b) & 1) != 0)
        def _(rows=RUN_ALIGN << b):
            make_copy(0, 0, rows).wait()


def _dispatch_kernel(src_tab, dst_tab, cnt_tab, tot_tab, gap_src, gap_dst, gap_cnt, f_ref, pos_ref, gates_ref,
                     buf_hbm, srt_ref, zero_ref, sems, *, n_tiles, n_cls, gap_cls):
    t = pl.program_id(0)
    gaps = (gap_src, gap_dst, gap_cnt)
    gap_copy = lambda lo, go, rows: pltpu.make_async_copy(
        zero_ref.at[pl.ds(lo, rows)], buf_hbm.at[pl.ds(go, rows)], sems.at[2])

    @pl.when(t == 0)
    def _():
        zero_ref[...] = jnp.zeros_like(zero_ref)
        _start_pieces(gaps, 0, gap_cls, gap_copy)
    slot = t % 2
    tm, D = f_ref.shape
    ls = srt_ref.shape[1]
    j = lax.broadcasted_iota(I32, (ls, tm), 0)
    hit0 = pos_ref[0:1, :] == j
    hit1 = pos_ref[1:2, :] == j
    perm = jnp.where(hit0 | hit1, 1.0, 0.0).astype(BF16)
    srt_ref[slot, :, :D] = jnp.dot(perm, f_ref[...], preferred_element_type=F32).astype(BF16)
    picked = jnp.where(hit0, gates_ref[0:1, :], jnp.where(hit1, gates_ref[1:2, :], 0.0))
    g = jnp.sum(picked, axis=1, keepdims=True)
    g_hi = g.astype(BF16).astype(F32)
    g_lo = (g - g_hi).astype(BF16).astype(F32)
    lane = lax.broadcasted_iota(I32, (ls, LANES), 1)
    srt_ref[slot, :, D:] = jnp.where(lane == 0, g_hi, jnp.where(lane == 1, g_lo, 0.0)).astype(BF16)

    tabs = (src_tab, dst_tab, cnt_tab, tot_tab)

    def copier(buf_slot):
        return lambda lo, go, rows: pltpu.make_async_copy(
            srt_ref.at[buf_slot, pl.ds(lo, rows)], buf_hbm.at[pl.ds(go, rows)], sems.at[buf_slot])

    _start_pieces(tabs, t, n_cls, copier(slot))

    @pl.when(t > 0)
    def _():
        _await_pieces(tabs, t - 1, ls, copier(1 - slot))

    @pl.when(t == n_tiles - 1)
    def _():
        _await_pieces(tabs, t, ls, copier(slot))
        _start_pieces(gaps, 0, gap_cls, gap_copy, wait=True)


def _dispatch(f, pos, gates, tabs, n_cls, gap_tabs, gap_cls, n_rows, n_tiles, tm):
    D = f.shape[1]
    ls = 2 * tm + N_EXPERTS * RUN_ALIGN
    width = D + LANES
    grid_spec = pltpu.PrefetchScalarGridSpec(
        num_scalar_prefetch=7,
        grid=(n_tiles,),
        in_specs=[pl.BlockSpec((tm, D), lambda t, *_: (t, 0)), pl.BlockSpec((2, tm), lambda t, *_: (0, t)),
                  pl.BlockSpec((2, tm), lambda t, *_: (0, t))],
        out_specs=pl.BlockSpec(memory_space=pl.ANY),
        scratch_shapes=[pltpu.VMEM((2, ls, width), BF16), pltpu.VMEM((EXPERT_BLOCK, width), BF16),
                        pltpu.SemaphoreType.DMA((3,))],
    )
    return pl.pallas_call(
        functools.partial(_dispatch_kernel, n_tiles=n_tiles, n_cls=n_cls, gap_cls=gap_cls),
        grid_spec=grid_spec,
        out_shape=jax.ShapeDtypeStruct((n_rows, width), BF16),
        compiler_params=_cparams("arbitrary"),
        name="moe_dispatch",
    )(*tabs, *gap_tabs[:3], f, pos, gates)


def _expert_kernel(be_ref, bc_ref, bs_ref, x_ref, wg_ref, wu_ref, wd_ref, y_ref):
    i = pl.program_id(0)
    D = y_ref.shape[1]

    @pl.when(bc_ref[i] > 0)
    def _():
        x = x_ref[:, :D]
        gate = jnp.dot(x, wg_ref[0], preferred_element_type=F32)
        up = jnp.dot(x, wu_ref[0], preferred_element_type=F32)
        hid = (gate * jax.nn.sigmoid(gate) * up).astype(BF16)
        pieces = x_ref[:, D:].astype(F32)
        route_gate = pieces[:, 0:1] + pieces[:, 1:2]
        y_ref[...] = (jnp.dot(hid, wd_ref[0], preferred_element_type=F32) * route_gate).astype(BF16)

    @pl.when(bc_ref[i] == 0)
    def _():
        y_ref[...] = jnp.zeros_like(y_ref)


def _experts(buf, block_expert, block_count, block_src, wg, wu, wd):
    n_rows = buf.shape[0]
    D = wg.shape[1]
    nb = n_rows // EXPERT_BLOCK
    grid_spec = pltpu.PrefetchScalarGridSpec(
        num_scalar_prefetch=3,
        grid=(nb,),
        in_specs=[pl.BlockSpec((EXPERT_BLOCK, buf.shape[1]), lambda i, be, bc, bs: (bs[i], 0)),
                  pl.BlockSpec((1,) + wg.shape[1:], lambda i, be, bc, bs: (be[i], 0, 0)),
                  pl.BlockSpec((1,) + wu.shape[1:], lambda i, be, bc, bs: (be[i], 0, 0)),
                  pl.BlockSpec((1,) + wd.shape[1:], lambda i, be, bc, bs: (be[i], 0, 0))],
        out_specs=pl.BlockSpec((EXPERT_BLOCK, D), lambda i, be, bc, bs: (i, 0)),
    )
    return pl.pallas_call(
        _expert_kernel,
        grid_spec=grid_spec,
        out_shape=jax.ShapeDtypeStruct((n_rows, D), BF16),
        compiler_params=_cparams("arbitrary"),
        name="moe_experts",
    )(block_expert, block_count, block_src, buf, wg, wu, wd)


def _combine_kernel(src_tab, dst_tab, cnt_tab, tot_tab, y_hbm, x_ref, pos_ref, gate_ref, fin_ref, xo_ref,
                    srt_ref, sems, *, final, n_tiles, n_cls):
    t = pl.program_id(0)
    slot = t % 2
    tm = x_ref.shape[0]
    ls = srt_ref.shape[1]

    tabs = (src_tab, dst_tab, cnt_tab, tot_tab)

    def copier(buf_slot):
        return lambda lo, go, rows: pltpu.make_async_copy(
            y_hbm.at[pl.ds(go, rows)], srt_ref.at[buf_slot, pl.ds(lo, rows)], sems.at[buf_slot])

    @pl.when(t == 0)
    def _():
        srt_ref[...] = jnp.zeros_like(srt_ref)
        _start_pieces(tabs, t, n_cls, copier(slot))

    @pl.when(t + 1 < n_tiles)
    def _():
        _start_pieces(tabs, t + 1, n_cls, copier(1 - slot))

    _await_pieces(tabs, t, ls, copier(slot))
    j = lax.broadcasted_iota(I32, (tm, ls), 1)
    pick = jnp.where((pos_ref[:, 0:1] == j) | (pos_ref[:, 1:2] == j), 1.0, 0.0).astype(BF16)
    y = jnp.dot(pick, srt_ref[slot], preferred_element_type=F32)
    xn = x_ref[...] + gate_ref[0, 0] * y
    if final:
        xn = xn * lax.rsqrt(jnp.mean(xn * xn, axis=-1, keepdims=True) + RMS_EPS) * fin_ref[...]
    xo_ref[...] = xn


def _combine(y_buf, pos_t, tabs, n_cls, xs, mod, final_gain, n_tiles, tm, tiles_per_seq, final):
    N, D = xs.shape
    nmod = mod.shape[1]
    ls = 2 * tm + N_EXPERTS * RUN_ALIGN
    out_rows = n_tiles * tm if final else N
    kwargs = {} if final else {"input_output_aliases": {5: 0}}
    grid_spec = pltpu.PrefetchScalarGridSpec(
        num_scalar_prefetch=4,
        grid=(n_tiles,),
        in_specs=[pl.BlockSpec(memory_space=pl.ANY),
                  pl.BlockSpec((tm, D), lambda t, *_: (t, 0)),
                  pl.BlockSpec((tm, 2), lambda t, *_: (t, 0)),
                  pl.BlockSpec((1, 1, 1, D), lambda t, *_: (5, jnp.minimum(t // tiles_per_seq, nmod - 1), 0, 0)),
                  pl.BlockSpec((1, D), lambda t, *_: (0, 0))],
        out_specs=pl.BlockSpec((tm, D), lambda t, *_: (t, 0)),
        scratch_shapes=[pltpu.VMEM((2, ls, D), BF16), pltpu.SemaphoreType.DMA((2,))],
    )
    return pl.pallas_call(
        functools.partial(_combine_kernel, final=final, n_tiles=n_tiles, n_cls=n_cls),
        grid_spec=grid_spec,
        out_shape=jax.ShapeDtypeStruct((out_rows, D), F32),
        compiler_params=_cparams("arbitrary"),
        name="moe_combine_final" if final else "moe_combine",
        **kwargs,
    )(*tabs, y_buf, xs, pos_t, mod, final_gain.reshape(1, D))


def _moe(f, routing, xs, mod, wg, wu, wd, final_gain, n_tok, tm, tiles_per_seq, final):
    nt = n_tok // tm
    e, gates, rank, cnt = routing
    e, gates, rank, cnt = e[:, :n_tok], gates[:, :n_tok], rank[:, :n_tok], cnt[:nt]
    n = cnt[:, :, 0].astype(I32)
    rlen = (n + RUN_ALIGN - 1) // RUN_ALIGN * RUN_ALIGN
    lo = jnp.cumsum(rlen, axis=1) - rlen
    region = jnp.sum(rlen, axis=0)
    region_pad = (region + EXPERT_BLOCK - 1) // EXPERT_BLOCK * EXPERT_BLOCK
    pends = jnp.cumsum(region_pad)
    pstarts = pends - region_pad
    go = pstarts[None, :] + jnp.cumsum(rlen, axis=0) - rlen
    n_blocks = -(-(2 * n_tok + nt * N_EXPERTS * RUN_ALIGN) // EXPERT_BLOCK) + N_EXPERTS
    blk0 = jnp.arange(n_blocks, dtype=I32) * EXPERT_BLOCK
    block_expert = jnp.minimum(jnp.sum((blk0[:, None] >= pends[None, :]).astype(I32), axis=1), N_EXPERTS - 1)
    block_used = (blk0 < (pstarts + region)[block_expert]).astype(I32)
    hot = e.reshape(2, nt, tm, 1) == jnp.arange(N_EXPERTS, dtype=I32)
    pos = jnp.sum(jnp.where(hot, lo[None, :, None, :], 0), axis=-1).reshape(2, n_tok) + rank
    tabs, n_cls = _chunk_tables(lo, go, rlen, tm)
    gap_tabs, gap_cls = _chunk_tables(jnp.zeros((1, N_EXPERTS), I32), (pstarts + region)[None, :],
                                      (region_pad - region)[None, :], EXPERT_BLOCK)
    buf = _dispatch(f, pos, gates, tabs, n_cls, gap_tabs, gap_cls, n_blocks * EXPERT_BLOCK, nt, tm)
    last_used = jnp.max(jnp.where(block_used > 0, jnp.arange(n_blocks, dtype=I32), 0))
    block_src = jnp.where(block_used > 0, jnp.arange(n_blocks, dtype=I32), last_used)
    y_buf = _experts(buf, block_expert.astype(I32), block_used, block_src, wg, wu, wd)
    return _combine(y_buf, pos.T, tabs, n_cls, xs, mod, final_gain, nt, tm, tiles_per_seq, final)


def _rope_tables(S, tm):
    rows = S // GRID_W
    row = jnp.repeat(jnp.arange(rows, dtype=F32), GRID_W)
    col = jnp.tile(jnp.arange(GRID_W, dtype=F32), rows)
    half = ATTN_HEAD_DIM // 4
    inv_freq = ROPE_THETA ** (-jnp.arange(half, dtype=F32) / half)
    ang_r = row[:, None] * inv_freq[None, :]
    ang_c = col[:, None] * inv_freq[None, :]
    zeros = jnp.zeros_like(ang_r)
    cos = jnp.concatenate([jnp.cos(ang_r)] * 2 + [jnp.cos(ang_c)] * 2, axis=1)
    s1 = jnp.concatenate([-jnp.sin(ang_r), zeros, -jnp.sin(ang_c), zeros], axis=1)
    s2 = jnp.concatenate([zeros, jnp.sin(ang_r), zeros, jnp.sin(ang_c)], axis=1)
    def finish(tab, fill):
        tab = jnp.tile(tab, (1, LANES // ATTN_HEAD_DIM))
        return jnp.concatenate([tab, jnp.full((tm, LANES), fill, F32)], axis=0)
    return finish(cos, 1.0), finish(s1, 0.0), finish(s2, 0.0)


def kernel(x, c, ctx, c_ctx, ada_w, ada_b, norm_mix_g, norm_ffn_g, final_g, attn_w_qkv, attn_w_o, attn_sinks,
           gla_w_in, gla_w_a1, gla_w_a2, gla_b_a, gla_norm_g, gla_w_o, router_w, router_b,
           moe_w_gate, moe_w_up, moe_w_down):
    B, S, D = x.shape
    C = ctx.shape[1]
    depth = ada_w.shape[0]
    tm = TOKEN_TILE
    assert S % tm == 0 and (B * C) % tm == 0 and S % ATTN_BLOCK == 0 and C % ATTN_BLOCK == 0
    assert (B * S) % C == 0 and S % GLA_CHUNK == 0 and C % GLA_CHUNK == 0
    n_lat = B * S
    n_lat_tiles = n_lat // tm
    tiles_per_seq = S // tm

    rpad = -(-(B + 1) // 8) * 8
    cc = jnp.zeros((rpad, D), F32).at[:B].set(c).at[B].set(c_ctx)
    mods = _ada_table(cc, ada_w, ada_b)
    mods = mods[:, :B + 1].reshape(depth, B + 1, 6, 1, D).transpose(0, 2, 1, 3, 4)

    xs = (x.reshape(n_lat, D), ctx.reshape(B * C, D))
    ptm = PROJ_TILE if S % PROJ_TILE == 0 and (B * C) % PROJ_TILE == 0 else tm
    rope = _rope_tables(S, ptm)
    q_dim = ATTN_KV_HEADS * ATTN_GROUP * ATTN_HEAD_DIM
    kv_dim = ATTN_KV_HEADS * ATTN_HEAD_DIM
    kd = GLA_HEADS * GLA_KEY_DIM
    vd = GLA_HEADS * GLA_VAL_DIM

    def dup_heads(w):
        w = w.reshape(D, ATTN_KV_HEADS, 1, ATTN_HEAD_DIM)
        return jnp.broadcast_to(w, (D, ATTN_KV_HEADS, LANES // ATTN_HEAD_DIM, ATTN_HEAD_DIM)).reshape(D, -1)

    for i in range(depth):
        last = i == depth - 1
        mod = mods[i]
        j = i // 2
        if i % 2 == 0:
            wqkv = attn_w_qkv[j]
            w = jnp.concatenate([wqkv[:, :q_dim], dup_heads(wqkv[:, q_dim:q_dim + kv_dim])], axis=1).astype(BF16)
            wvt = wqkv[:, q_dim + kv_dim:].T.astype(BF16)
            q, k, vt = _qkv_proj(xs, norm_mix_g[i], mod, 0, w, wvt, rope, ptm, n_lat // ptm, S // ptm)
            sink_row = jnp.repeat(attn_sinks[j].astype(F32).reshape(ATTN_KV_HEADS, ATTN_GROUP) * LOG2_E,
                                  ATTN_BLOCK, axis=1)
            o = _attention(q, k, vt, sink_row[:, None, :], B, S, C, not last)
            xs, f, *routing = _post_mixer(o, attn_w_o[j].astype(BF16), xs, mod, norm_ffn_g[i], router_w, router_b,
                                          tm, n_lat_tiles, tiles_per_seq)
        else:
            a1 = jnp.zeros((D, LANES), F32).at[:, :2 * GLA_GATE_RANK].set(
                jnp.concatenate([gla_w_a1[j, 0], gla_w_a1[j, 1]], axis=1))
            w = jnp.concatenate([gla_w_in[j], a1], axis=1).astype(BF16)
            w2 = jnp.zeros((LANES, 2 * kd), F32)
            w2 = w2.at[:GLA_GATE_RANK, :kd].set(gla_w_a2[j, 0]).at[GLA_GATE_RANK:2 * GLA_GATE_RANK, kd:].set(gla_w_a2[j, 1])
            ba = gla_b_a[j].reshape(1, 2 * kd).astype(F32)
            qk, v, og, la, chunk_tot = _gla_in_proj(xs, norm_mix_g[i], mod, 0, w, w2.astype(BF16), ba, tm,
                                                    n_lat_tiles, tiles_per_seq)
            o_fwd, o_bwd = _gla_scan(qk, v, la, chunk_tot, B, S, C)
            xs, f, *routing = _post_mixer(o_fwd, gla_w_o[j].astype(BF16), xs, mod, norm_ffn_g[i], router_w, router_b,
                                          tm, n_lat_tiles, tiles_per_seq,
                                          gla_extra=(o_bwd, og, gla_norm_g[j].reshape(1, GLA_VAL_DIM).astype(F32)))
        wg, wu = moe_w_gate[i].astype(BF16), moe_w_up[i].astype(BF16)
        wd = moe_w_down[i].astype(BF16)
        n_tok = n_lat if last else n_lat + B * C
        xs = _moe(f, routing, xs, mod, wg, wu, wd, final_g, n_tok, tm, tiles_per_seq, last)
    return xs.reshape(B, S, D)
```

```python
import functools

import numpy as np
import jax
import jax.numpy as jnp
from jax import lax
from jax.experimental import pallas as pl
from jax.experimental.pallas import tpu as pltpu

F32 = jnp.float32
BF16 = jnp.bfloat16
I32 = jnp.int32

LANES = 128
VMEM_LIMIT_BYTES = 56 * 1024 * 1024

RMS_EPS = 1e-6
GRID_W = 64
ROPE_THETA = 10000.0
ATTN_HEAD_DIM = 64
ATTN_KV_HEADS = 4
ATTN_GROUP = 4
ATTN_BLOCK = 128
ATTN_STEP_BLOCKS = 2
LOG2_E = 1.4426950408889634
ATTN_Q_SCALE = ATTN_HEAD_DIM ** -0.5 * LOG2_E
GLA_HEADS = 4
GLA_KEY_DIM = 128
GLA_VAL_DIM = 256
GLA_GATE_RANK = 16
GLA_GATE_NORM = 16.0
GLA_CHUNK = 128
GLA_STAT_ROWS = 64
GLA_BOUNDED_TOTAL = 40.0
N_EXPERTS = 16
N_GROUPS = 4
EXPERTS_PER_GROUP = 4
EXPERT_BLOCK = 1024
RUN_ALIGN = 16
TOKEN_TILE = 512
PROJ_TILE = 1024


def _cparams(*sem):
    return pltpu.CompilerParams(dimension_semantics=sem, vmem_limit_bytes=VMEM_LIMIT_BYTES)


def _norm_mod(x, gain, shift, scale):
    h = x * lax.rsqrt(jnp.mean(x * x, axis=-1, keepdims=True) + RMS_EPS) * gain
    return h * (1.0 + scale) + shift


def _ada_kernel(c_ref, w_ref, b_ref, o_ref):
    c = c_ref[...]
    s = (c * jax.nn.sigmoid(c)).astype(BF16)
    o_ref[0] = jnp.dot(s, w_ref[0].astype(BF16), preferred_element_type=F32) + b_ref[0]


def _ada_table(cc, ada_w, ada_b):
    L, D, D6 = ada_w.shape
    R = cc.shape[0]
    tn = 1536
    return pl.pallas_call(
        _ada_kernel,
        grid=(L, D6 // tn),
        in_specs=[pl.BlockSpec((R, D), lambda l, j: (0, 0)),
                  pl.BlockSpec((1, D, tn), lambda l, j: (l, 0, j)),
                  pl.BlockSpec((1, 1, tn), lambda l, j: (l, 0, j))],
        out_specs=pl.BlockSpec((1, R, tn), lambda l, j: (l, 0, j)),
        out_shape=jax.ShapeDtypeStruct((L, R, D6), F32),
        compiler_params=_cparams("parallel", "parallel"),
        name="ada_table",
    )(cc, ada_w, ada_b.reshape(L, 1, D6))


def _stream_tile(x_refs, n_lat_tiles):
    if len(x_refs) == 1:
        return x_refs[0][...]
    return jnp.where(pl.program_id(0) < n_lat_tiles, x_refs[0][...], x_refs[1][...])


def _stream_args(xs, tm, n_lat_tiles):
    if isinstance(xs, tuple):
        D = xs[0].shape[1]
        return list(xs), [pl.BlockSpec((tm, D), lambda t: (jnp.minimum(t, n_lat_tiles - 1), 0)),
                          pl.BlockSpec((tm, D), lambda t: (jnp.maximum(t - n_lat_tiles, 0), 0))]
    return [xs], [pl.BlockSpec((tm, xs.shape[1]), lambda t: (t, 0))]


def _qkv_kernel(*refs, n_x, n_lat_tiles):
    x_refs = refs[:n_x]
    g_ref, sh_ref, sc_ref, w_ref, wvt_ref, cos_ref, s1_ref, s2_ref, q_ref, k_ref, vt_ref = refs[n_x:]
    h = _norm_mod(_stream_tile(x_refs, n_lat_tiles), g_ref[...], sh_ref[0, 0], sc_ref[0, 0]).astype(BF16)
    z = jnp.dot(h, w_ref[...], preferred_element_type=F32)
    vt = lax.dot_general(wvt_ref[...], h, (((1,), (1,)), ((), ())), preferred_element_type=F32).astype(BF16)
    for c in range(vt_ref.shape[0]):
        vt_ref[c] = vt[:, c * ATTN_BLOCK:(c + 1) * ATTN_BLOCK]
    cos, s1, s2 = cos_ref[...], s1_ref[...], s2_ref[...]
    nq = q_ref.shape[1] // LANES
    nk = k_ref.shape[1] // LANES
    for j in range(nq + nk):
        zc = z[:, j * LANES:(j + 1) * LANES]
        r = zc * cos + pltpu.roll(zc, LANES - 16, 1) * s1 + pltpu.roll(zc, 16, 1) * s2
        if j < nq:
            q_ref[:, j * LANES:(j + 1) * LANES] = (r * ATTN_Q_SCALE).astype(BF16)
        else:
            k_ref[:, (j - nq) * LANES:(j - nq + 1) * LANES] = r.astype(BF16)


def _gla_in_kernel(x_ref, g_ref, sh_ref, sc_ref, w_ref, w2_ref, ba_ref, qk_ref, v_ref, og_ref, la_ref, tot_ref):
    h = _norm_mod(x_ref[...], g_ref[...], sh_ref[0, 0], sc_ref[0, 0]).astype(BF16)
    kd = GLA_HEADS * GLA_KEY_DIM
    vd = GLA_HEADS * GLA_VAL_DIM
    a1 = jnp.dot(h, w_ref[:, 2 * kd + 2 * vd:], preferred_element_type=F32).astype(BF16)
    pre = jnp.dot(a1, w2_ref[...], preferred_element_type=F32) + ba_ref[...]
    z = jnp.dot(h, w_ref[:, :2 * kd + 2 * vd], preferred_element_type=F32)
    la = (jnp.minimum(pre, 0.0) - jnp.log1p(jnp.exp(-jnp.abs(pre)))) * (1.0 / GLA_GATE_NORM)
    la_ref[...] = la
    qk_ref[:, :kd] = (z[:, :kd] * (GLA_KEY_DIM ** -0.5)).astype(BF16)
    qk_ref[:, kd:] = z[:, kd:2 * kd].astype(BF16)
    v_ref[...] = z[:, 2 * kd:2 * kd + vd].astype(BF16)
    og_ref[...] = z[:, 2 * kd + vd:2 * kd + 2 * vd].astype(BF16)
    nc = la.shape[0] // GLA_STAT_ROWS
    tot = jnp.sum(la.reshape(nc, GLA_STAT_ROWS, la.shape[1]), axis=1)
    lane = lax.broadcasted_iota(I32, (nc, LANES), 1)
    acc = jnp.zeros((nc, LANES), F32)
    for hd in range(2 * GLA_HEADS):
        worst = jnp.min(tot[:, hd * GLA_KEY_DIM:(hd + 1) * GLA_KEY_DIM], axis=1, keepdims=True)
        acc = jnp.where(lane == hd, worst, acc)
    tot_ref[...] = acc


def _tile_specs(D, tm, n_lat_tiles, tiles_per_seq, n_mod_rows):
    def mod_idx(which):
        return lambda t: (which, jnp.minimum(t // tiles_per_seq, n_mod_rows - 1), 0, 0)
    return mod_idx, [pl.BlockSpec((tm, D), lambda t: (t, 0)),
                     pl.BlockSpec((1, D), lambda t: (0, 0))]


def _qkv_proj(xs, gain, mod, which, w, wvt, rope, tm, n_lat_tiles, tiles_per_seq):
    x_args, x_specs = _stream_args(xs, tm, n_lat_tiles)
    N = sum(a.shape[0] for a in x_args)
    D = x_args[0].shape[1]
    nmod = mod.shape[1]
    mod_idx, specs = _tile_specs(D, tm, n_lat_tiles, tiles_per_seq, nmod)
    qd = ATTN_KV_HEADS * ATTN_GROUP * ATTN_HEAD_DIM
    kd = ATTN_KV_HEADS * LANES
    vd = wvt.shape[0]
    rope_idx = lambda t: (jnp.where(t < n_lat_tiles, t % tiles_per_seq, tiles_per_seq), 0)
    in_specs = x_specs + specs[1:] + [
        pl.BlockSpec((1, 1, 1, D), mod_idx(which)),
        pl.BlockSpec((1, 1, 1, D), mod_idx(which + 1)),
        pl.BlockSpec(w.shape, lambda t: (0, 0)),
        pl.BlockSpec(wvt.shape, lambda t: (0, 0)),
        pl.BlockSpec((tm, LANES), rope_idx),
        pl.BlockSpec((tm, LANES), rope_idx),
        pl.BlockSpec((tm, LANES), rope_idx),
    ]
    return pl.pallas_call(
        functools.partial(_qkv_kernel, n_x=len(x_args), n_lat_tiles=n_lat_tiles),
        grid=(N // tm,),
        in_specs=in_specs,
        out_specs=[pl.BlockSpec((tm, qd), lambda t: (t, 0)),
                   pl.BlockSpec((tm, kd), lambda t: (t, 0)),
                   pl.BlockSpec((tm // ATTN_BLOCK, vd, ATTN_BLOCK), lambda t: (t, 0, 0))],
        out_shape=[jax.ShapeDtypeStruct((N, qd), BF16),
                   jax.ShapeDtypeStruct((N, kd), BF16),
                   jax.ShapeDtypeStruct((N // ATTN_BLOCK, vd, ATTN_BLOCK), BF16)],
        compiler_params=_cparams("parallel"),
        name="attn_qkv_proj",
    )(*x_args, gain.reshape(1, D), mod, mod, w, wvt, *rope)


def _gla_in_proj(xs, gain, mod, which, w, w2, ba, tm, n_lat_tiles, tiles_per_seq):
    N, D = xs.shape
    nmod = mod.shape[1]
    mod_idx, specs = _tile_specs(D, tm, n_lat_tiles, tiles_per_seq, nmod)
    kd = GLA_HEADS * GLA_KEY_DIM
    vd = GLA_HEADS * GLA_VAL_DIM
    in_specs = specs + [
        pl.BlockSpec((1, 1, 1, D), mod_idx(which)),
        pl.BlockSpec((1, 1, 1, D), mod_idx(which + 1)),
        pl.BlockSpec(w.shape, lambda t: (0, 0)),
        pl.BlockSpec(w2.shape, lambda t: (0, 0)),
        pl.BlockSpec((1, 2 * kd), lambda t: (0, 0)),
    ]
    row = lambda width: pl.BlockSpec((tm, width), lambda t: (t, 0))
    return pl.pallas_call(
        _gla_in_kernel,
        grid=(N // tm,),
        in_specs=in_specs,
        out_specs=[row(2 * kd), row(vd), row(vd), row(2 * kd),
                   pl.BlockSpec((tm // GLA_STAT_ROWS, LANES), lambda t: (t, 0))],
        out_shape=[jax.ShapeDtypeStruct((N, 2 * kd), BF16),
                   jax.ShapeDtypeStruct((N, vd), BF16),
                   jax.ShapeDtypeStruct((N, vd), BF16),
                   jax.ShapeDtypeStruct((N, 2 * kd), F32),
                   jax.ShapeDtypeStruct((N // GLA_STAT_ROWS, LANES), F32)],
        compiler_params=_cparams("parallel"),
        name="gla_in_proj",
    )(xs, gain.reshape(1, D), mod, mod, w, w2, ba)


def _attn_kernel(*refs, window, nq, nb):
    tq = ATTN_BLOCK
    if window:
        q_ref, kp, km, kn, kx, vp, vm, vn, vx, sink_ref, tri_ref, o_ref = refs
        k_blocks = ([lambda ks: kp[:, ks]] + [lambda ks, i=i: km[i * tq:(i + 1) * tq, ks] for i in range(nb)]
                    + [lambda ks: kn[:, ks]])
        v_blocks = ([lambda hs: vp[0, hs, :]] + [lambda hs, i=i: vm[i, hs, :] for i in range(nb)]
                    + [lambda hs: vn[0, hs, :]])
    else:
        q_ref, kx, vx, sink_ref, _, o_ref = refs
    lane = lax.broadcasted_iota(I32, (tq, LANES), 1)
    first_head = lane < ATTN_HEAD_DIM
    if window:
        bias = []
        for i in range(nb):
            j = pl.program_id(1) * nb + i
            bias_prev = jnp.where(j > 0, tri_ref[0], -jnp.inf)
            bias_next = jnp.where(j < nq - 1, tri_ref[1], -jnp.inf)
            bias.append((jnp.concatenate([bias_prev] * ATTN_GROUP, axis=1),
                         jnp.concatenate([bias_next] * ATTN_GROUP, axis=1)))

    def scores(i, kh):
        ks = slice(kh * LANES, (kh + 1) * LANES)
        rows = slice(i * tq, (i + 1) * tq)
        kk = jnp.concatenate([blk(ks) for blk in k_blocks[i:i + 3]] + [kx[:, ks]], axis=0) if window else kx[:, ks]
        qa = q_ref[rows, (2 * kh) * LANES:(2 * kh + 1) * LANES]
        qb = q_ref[rows, (2 * kh + 1) * LANES:(2 * kh + 2) * LANES]
        zero = jnp.zeros_like(qa)
        qs = jnp.concatenate([jnp.where(first_head, qa, zero), jnp.where(first_head, zero, qa),
                              jnp.where(first_head, qb, zero), jnp.where(first_head, zero, qb)], axis=0)
        return lax.dot_general(kk, qs, (((1,), (1,)), ((), ())), preferred_element_type=F32)

    def softmax(s, i, kh):
        if window:
            s = jnp.concatenate([s[:tq] + bias[i][0], s[tq:2 * tq], s[2 * tq:3 * tq] + bias[i][1], s[3 * tq:]], axis=0)
        sink = sink_ref[kh]
        m = jnp.maximum(jnp.max(s, axis=0, keepdims=True), sink)
        p = jnp.exp2(s - m)
        l = jnp.sum(p, axis=0, keepdims=True) + jnp.exp2(sink - m)
        return p.astype(BF16), l

    def values(p, l, i, kh):
        hs = slice(kh * ATTN_HEAD_DIM, (kh + 1) * ATTN_HEAD_DIM)
        v_ctx = [vx[c, hs, :] for c in range(vx.shape[0])]
        vv = jnp.concatenate(([blk(hs) for blk in v_blocks[i:i + 3]] if window else []) + v_ctx, axis=1)
        o = jnp.dot(vv, p, preferred_element_type=F32) / l
        for g in range(ATTN_GROUP):
            h = kh * ATTN_GROUP + g
            o_ref[i, h * ATTN_HEAD_DIM:(h + 1) * ATTN_HEAD_DIM, :] = o[:, g * tq:(g + 1) * tq].astype(BF16)

    units = [(i, kh) for i in range(nb) for kh in range(ATTN_KV_HEADS)]
    s_next = scores(*units[0])
    pending = None
    for n, unit in enumerate(units):
        s_cur = s_next
        if n + 1 < len(units):
            s_next = scores(*units[n + 1])
        if pending is not None:
            values(*pending)
        pending = softmax(s_cur, *unit) + unit
    values(*pending)


def _attention(q, k, vt, sink_row, B, S, C, need_ctx):
    N, qd = q.shape
    kd = k.shape[1]
    vd = vt.shape[1]
    tq = ATTN_BLOCK
    nq = S // tq
    nb = ATTN_STEP_BLOCKS
    assert nq % nb == 0 and C == nb * tq
    steps = nq // nb
    ctx_blk0 = (B * S) // C
    mid = lambda b, j: (b * steps + j, 0)
    prev = lambda b, j: (b * nq + jnp.maximum(j * nb - 1, 0), 0)
    nxt = lambda b, j: (b * nq + jnp.minimum(j * nb + nb, nq - 1), 0)
    cmap = lambda b, j: (ctx_blk0 + b, 0)
    kctx = pl.BlockSpec((C, kd), cmap)
    slab = lambda n, m: pl.BlockSpec((n, vd, tq), lambda b, j: (m(b, j)[0], 0, 0))
    vctx = slab(C // tq, cmap)
    sink_spec = pl.BlockSpec(sink_row.shape, lambda b, j: (0, 0, 0))
    key = np.arange(tq)[:, None]
    qry = np.arange(tq)[None, :]
    tri = jnp.asarray(np.stack([np.where(key >= qry, 0.0, -np.inf), np.where(key <= qry, 0.0, -np.inf)]), F32)
    o_lat = pl.pallas_call(
        functools.partial(_attn_kernel, window=True, nq=nq, nb=nb),
        grid=(B, steps),
        in_specs=[pl.BlockSpec((nb * tq, qd), mid),
                  pl.BlockSpec((tq, kd), prev), pl.BlockSpec((nb * tq, kd), mid), pl.BlockSpec((tq, kd), nxt), kctx,
                  slab(1, prev), slab(nb, mid), slab(1, nxt), vctx, sink_spec,
                  pl.BlockSpec(tri.shape, lambda b, j: (0, 0, 0))],
        out_specs=pl.BlockSpec((nb, qd, tq), lambda b, j: (mid(b, j)[0], 0, 0)),
        out_shape=jax.ShapeDtypeStruct((N // tq, qd, tq), BF16),
        compiler_params=_cparams("parallel", "parallel"),
        name="attn_window",
    )(q, k, k, k, k, vt, vt, vt, vt, sink_row, tri)
    if not need_ctx:
        return o_lat
    return pl.pallas_call(
        functools.partial(_attn_kernel, window=False, nq=nb, nb=nb),
        grid=(B, 1),
        in_specs=[pl.BlockSpec((C, qd), cmap), kctx, vctx, sink_spec,
                  pl.BlockSpec(memory_space=pl.ANY)],
        out_specs=pl.BlockSpec((nb, qd, tq), lambda b, j: (cmap(b, j)[0], 0, 0)),
        out_shape=jax.ShapeDtypeStruct((N // tq, qd, tq), BF16),
        input_output_aliases={4: 0},
        compiler_params=_cparams("parallel", "parallel"),
        name="attn_context",
    )(q, k, vt, sink_row, o_lat)


def _gla_constants(C):
    levels = []
    m = 1
    while m < C:
        levels.append(m)
        m *= 2
    t = np.arange(C)[:, None]
    u = np.arange(C)[None, :]
    secs = [(u <= t), (u > t)]
    masks = []
    for m in levels:
        base = (t // (2 * m)) * (2 * m)
        ref = base + m - 1
        second = t >= base + m
        secs.append(np.where(second, (u > ref) & (u <= t), (u > t) & (u <= ref)))
        masks.append((t // (2 * m) == u // (2 * m)) & second & (u < (u // (2 * m)) * (2 * m) + m))
    masks.append(t == u)
    masks.append(u <= t)
    mf = np.concatenate([s.astype(np.float32) for s in secs], axis=0)
    kf = np.stack([mk.astype(np.float32) for mk in masks], axis=0)
    mb = np.concatenate([s.astype(np.float32)[::-1, ::-1] for s in secs], axis=0)
    kb = np.stack([mk.astype(np.float32)[::-1, ::-1] for mk in masks], axis=0)
    return np.stack([mf, mb]), np.stack([kf, kb]), len(levels)


def _gla_chunk(q, k, v, g, st_ref, mm_ref, mk_ref, d, nl, bounded):
    C, dk = q.shape
    nt = (((1,), (1,)), ((), ()))
    g_hi = g.astype(BF16)
    g_lo = (g - g_hi.astype(F32)).astype(BF16)
    mmat = mm_ref[d, :2 * C] if bounded else mm_ref[d]
    e2 = jnp.dot(mmat, jnp.concatenate([g_hi, g_lo], axis=1), preferred_element_type=F32)
    ee = e2[:, :dk] + e2[:, dk:]
    ex = jnp.exp(ee)
    qf, kf = q.astype(F32), k.astype(F32)
    qe = (qf * ex[:C]).astype(BF16)
    ke = (kf * ex[C:2 * C]).astype(BF16)
    st = st_ref[...]
    o = lax.dot_general(qe, st.astype(BF16), nt, preferred_element_type=F32)
    if bounded:
        ki = (kf * jnp.exp(-ee[:C])).astype(BF16)
        a = mk_ref[d, nl + 1] * lax.dot_general(qe, ki, nt, preferred_element_type=F32)
    else:
        a = mk_ref[d, nl] * lax.dot_general(q, k, nt, preferred_element_type=F32)
        for i in range(nl):
            xl = ex[(2 + i) * C:(3 + i) * C]
            ql = (qf * xl).astype(BF16)
            kl = (kf * xl).astype(BF16)
            a = a + mk_ref[d, i] * lax.dot_general(ql, kl, nt, preferred_element_type=F32)
    o = o + jnp.dot(a.astype(BF16), v, preferred_element_type=F32)
    decay = jnp.exp(jnp.sum(g, axis=0, keepdims=True))
    st_ref[...] = st * decay + lax.dot_general(v, ke, (((0,), (0,)), ((), ())), preferred_element_type=F32)
    return o


def _gla_kernel(okf_ref, okb_ref, qf, kf, vf, lf, qb, kb, vb, lb, mm_ref, mk_ref, of_ref, ob_ref, st_ref,
                *, nl, seg_chunks, lat_segs, ctx_seg0):
    C = GLA_CHUNK
    H, DK, DV = GLA_HEADS, GLA_KEY_DIM, GLA_VAL_DIM
    b, j = pl.program_id(0), pl.program_id(1)

    @pl.when(j == 0)
    def _():
        st_ref[...] = jnp.zeros_like(st_ref)

    seg_f = jnp.where(j == 0, ctx_seg0 + b, b * lat_segs + j - 1)
    seg_b = jnp.where(j == 0, ctx_seg0 + b, b * lat_segs + lat_segs - j)

    nt = (((1,), (1,)), ((), ()))
    kw = H * DK

    def rows_of(i):
        cf, cb = i, seg_chunks - 1 - i
        return slice(cf * C, (cf + 1) * C), slice(cb * C, (cb + 1) * C)

    def advance_general(i):
        rf, rb = rows_of(i)
        for h in range(H):
            ks, vs = slice(h * DK, (h + 1) * DK), slice(h * DV, (h + 1) * DV)
            o = _gla_chunk(qf[rf, ks], kf[rf, ks], vf[rf, vs], lf[rf, ks], st_ref.at[h],
                           mm_ref, mk_ref, 0, nl, False)
            of_ref[rf, vs] = o.astype(BF16)
            o = _gla_chunk(qb[rb, ks], kb[rb, ks], vb[rb, vs], lb[rb, ks], st_ref.at[H + h],
                           mm_ref, mk_ref, 1, nl, False)
            ob_ref[rb, vs] = o.astype(BF16)

    def bounded_front(i):
        rf, rb = rows_of(i)
        sides = ((qf, kf, vf, lf, rf, of_ref, 0), (qb, kb, vb, lb, rb, ob_ref, 1))
        pre = []
        for q_r, k_r, v_r, l_r, rows, o_r, d in sides:
            g = l_r[rows, :]
            g_hi = g.astype(BF16)
            g_lo = (g - g_hi.astype(F32)).astype(BF16)
            e2 = jnp.dot(mm_ref[d, :2 * C], jnp.concatenate([g_hi, g_lo], axis=1),
                         preferred_element_type=F32)
            pre.append((e2[:, :kw] + e2[:, kw:], g))
        units = []
        for (ee, g), (q_r, k_r, v_r, l_r, rows, o_r, d) in zip(pre, sides):
            ex = jnp.exp(ee)
            kf32 = k_r[rows, :].astype(F32)
            qe = (q_r[rows, :].astype(F32) * ex[:C]).astype(BF16)
            ke = (kf32 * ex[C:]).astype(BF16)
            ki = (kf32 * jnp.exp(-ee[:C])).astype(BF16)
            decay = jnp.exp(jnp.sum(g, axis=0, keepdims=True))
            for h in range(H):
                ks = slice(h * DK, (h + 1) * DK)
                units.append([d, h, qe[:, ks], ke[:, ks], ki[:, ks], decay[:, ks], v_r, rows, o_r])
        for unit in units:
            d, h, qe, ke, ki = unit[:5]
            score = lax.dot_general(qe, ki, nt, preferred_element_type=F32)
            unit.append((mk_ref[d, nl + 1] * score).astype(BF16))
        return units

    def bounded_back(units):
        inter = [lax.dot_general(u[2], st_ref[u[0] * H + u[1]].astype(BF16), nt, preferred_element_type=F32)
                 for u in units]
        for n, (d, h, qe, ke, ki, decay, v_r, rows, o_r, a) in enumerate(units):
            vs = slice(h * DV, (h + 1) * DV)
            v = v_r[rows, vs]
            o = inter[n] + jnp.dot(a, v, preferred_element_type=F32)
            upd = lax.dot_general(v, ke, (((0,), (0,)), ((), ())), preferred_element_type=F32)
            o_r[rows, vs] = o.astype(BF16)
            st_ref[d * H + h] = st_ref[d * H + h] * decay + upd

    bounded = None
    for i in range(seg_chunks):
        ok = (okf_ref[seg_f * seg_chunks + i] != 0) & (okb_ref[seg_b * seg_chunks + seg_chunks - 1 - i] != 0)
        bounded = ok if bounded is None else bounded & ok

    @pl.when(bounded)
    def _():
        fronts = [bounded_front(i) for i in range(seg_chunks)]
        for units in fronts:
            bounded_back(units)

    @pl.when(jnp.logical_not(bounded))
    def _():
        for i in range(seg_chunks):
            advance_general(i)


def _gla_scan(qk, v, la, chunk_tot, B, S, C):
    N = qk.shape[0]
    H, DK, DV = GLA_HEADS, GLA_KEY_DIM, GLA_VAL_DIM
    seg = C
    assert S % seg == 0 and seg % GLA_CHUNK == 0
    lat_segs = S // seg
    ctx_seg0 = (B * S) // seg
    mm, mk, nl = _gla_constants(GLA_CHUNK)
    mm = jnp.asarray(mm, BF16)
    mk = jnp.asarray(mk, F32)
    chunk_tot = jnp.sum(chunk_tot.reshape(-1, GLA_CHUNK // GLA_STAT_ROWS, LANES), axis=1)
    ok = chunk_tot[:, :2 * H] >= -GLA_BOUNDED_TOTAL
    okf = jnp.all(ok[:, :H], axis=1).astype(I32)
    okb = jnp.all(ok[:, H:], axis=1).astype(I32)
    fwd = lambda col: (lambda b, j, *_: (jnp.where(j == 0, ctx_seg0 + b, b * lat_segs + j - 1), col))
    bwd = lambda col: (lambda b, j, *_: (jnp.where(j == 0, ctx_seg0 + b, b * lat_segs + lat_segs - j), col))
    kw, vw = H * DK, H * DV
    in_specs = [
        pl.BlockSpec((seg, kw), fwd(0)), pl.BlockSpec((seg, kw), fwd(1)), pl.BlockSpec((seg, vw), fwd(0)),
        pl.BlockSpec((seg, kw), fwd(0)),
        pl.BlockSpec((seg, kw), bwd(0)), pl.BlockSpec((seg, kw), bwd(1)), pl.BlockSpec((seg, vw), bwd(0)),
        pl.BlockSpec((seg, kw), bwd(1)),
        pl.BlockSpec(mm.shape, lambda b, j, *_: (0, 0, 0)),
        pl.BlockSpec(mk.shape, lambda b, j, *_: (0, 0, 0, 0)),
    ]
    grid_spec = pltpu.PrefetchScalarGridSpec(
        num_scalar_prefetch=2,
        grid=(B, lat_segs + 1),
        in_specs=in_specs,
        out_specs=[pl.BlockSpec((seg, vw), fwd(0)), pl.BlockSpec((seg, vw), bwd(0))],
        scratch_shapes=[pltpu.VMEM((2 * H, DV, DK), F32)],
    )
    return pl.pallas_call(
        functools.partial(_gla_kernel, nl=nl, seg_chunks=seg // GLA_CHUNK, lat_segs=lat_segs, ctx_seg0=ctx_seg0),
        grid_spec=grid_spec,
        out_shape=[jax.ShapeDtypeStruct((N, vw), BF16), jax.ShapeDtypeStruct((N, vw), BF16)],
        compiler_params=_cparams("parallel", "arbitrary"),
        name="gla_scan",
    )(okf, okb, qk, qk, v, la, qk, qk, v, la, mm, mk)


def _post_kernel(*refs, gla, n_x, n_lat_tiles):
    x_refs, refs = refs[:n_x], refs[n_x:]
    route_refs = refs[-3:]
    route_in = refs[-8:-5]
    refs = refs[:-8] + refs[-5:-3]
    x_tile = _stream_tile(x_refs, n_lat_tiles)
    if gla:
        o_ref, ob_ref, og_ref, ng_ref, w_ref, gate_ref, fg_ref, fsh_ref, fsc_ref, xo_ref, f_ref = refs
        o = o_ref[...].astype(F32) + ob_ref[...].astype(F32)
        g = og_ref[...].astype(F32)
        parts = []
        for h in range(GLA_HEADS):
            oh = o[:, h * GLA_VAL_DIM:(h + 1) * GLA_VAL_DIM]
            parts.append(oh * lax.rsqrt(jnp.mean(oh * oh, axis=-1, keepdims=True) + RMS_EPS) * ng_ref[...])
        mix = (jnp.concatenate(parts, axis=1) * (g * jax.nn.sigmoid(g))).astype(BF16)
        y = jnp.dot(mix, w_ref[...], preferred_element_type=F32)
    else:
        o_ref, w_ref, gate_ref, fg_ref, fsh_ref, fsc_ref, xo_ref, f_ref = refs
        o_t = jnp.concatenate([o_ref[c] for c in range(o_ref.shape[0])], axis=1)
        y = lax.dot_general(o_t, w_ref[...], (((0,), (0,)), ((), ())), preferred_element_type=F32)
    xn = x_tile + gate_ref[0, 0] * y
    xo_ref[...] = xn
    f = _norm_mod(xn, fg_ref[...], fsh_ref[0, 0], fsc_ref[0, 0])
    f_ref[...] = f.astype(BF16)
    _route_tile(f, *route_in, *route_refs)


def _post_mixer(o, w_o, xs, mod, ffn_gain, router_w, router_b, tm, n_lat_tiles, tiles_per_seq, gla_extra=None):
    args, specs = _stream_args(xs, tm, n_lat_tiles)
    n_x = len(args)
    N = sum(a.shape[0] for a in args)
    D = args[0].shape[1]
    r_args, r_in, r_out, r_shapes = _route_io(router_w, router_b, N, D, tm)
    nmod = mod.shape[1]
    mod_idx = lambda which: (lambda t: (which, jnp.minimum(t // tiles_per_seq, nmod - 1), 0, 0))
    row = lambda width: pl.BlockSpec((tm, width), lambda t: (t, 0))
    const = lambda a: pl.BlockSpec(a.shape, lambda t: (0,) * a.ndim)
    if gla_extra is None:
        args, specs = args + [o], specs + [pl.BlockSpec((tm // o.shape[2],) + o.shape[1:], lambda t: (t, 0, 0))]
    else:
        o_bwd, og, ng = gla_extra
        args += [o, o_bwd, og, ng]
        specs += [row(o.shape[1]), row(o_bwd.shape[1]), row(og.shape[1]), const(ng)]
    args += [w_o, mod, ffn_gain.reshape(1, D), mod, mod]
    specs += [const(w_o), pl.BlockSpec((1, 1, 1, D), mod_idx(2)), pl.BlockSpec((1, D), lambda t: (0, 0)),
              pl.BlockSpec((1, 1, 1, D), mod_idx(3)), pl.BlockSpec((1, 1, 1, D), mod_idx(4))]
    args += r_args
    specs += r_in
    return pl.pallas_call(
        functools.partial(_post_kernel, gla=gla_extra is not None, n_x=n_x, n_lat_tiles=n_lat_tiles),
        grid=(N // tm,),
        in_specs=specs,
        out_specs=[row(D), row(D)] + r_out,
        out_shape=[jax.ShapeDtypeStruct((N, D), F32), jax.ShapeDtypeStruct((N, D), BF16)] + r_shapes,
        input_output_aliases={0: 0} if n_x == 1 else {},
        compiler_params=_cparams("parallel"),
        name="post_mixer_gla" if gla_extra is not None else "post_mixer_attn",
    )(*args)


def _route_tile(f, rw_ref, rb_ref, tri_ref, gate_ref, pos_ref, cnt_ref):
    tm = f.shape[0]
    fh = f.astype(BF16)
    fl = (f - fh.astype(F32)).astype(BF16)
    parts = jnp.dot(jnp.concatenate([fh, fl], axis=0), rw_ref[...], preferred_element_type=F32)
    logits = (parts[:tm, :LANES] + parts[:tm, LANES:]) + (parts[tm:, :LANES] + parts[tm:, LANES:])
    lt = logits.T[:N_EXPERTS]
    scores = jax.nn.sigmoid(lt)
    sel = scores + rb_ref[...]
    srow = [sel[e:e + 1] for e in range(N_EXPERTS)]
    prow = [scores[e:e + 1] for e in range(N_EXPERTS)]
    gscore = []
    for g in range(N_GROUPS):
        a, b, c, d = srow[4 * g:4 * g + 4]
        hi1, lo1, hi2, lo2 = jnp.maximum(a, b), jnp.minimum(a, b), jnp.maximum(c, d), jnp.minimum(c, d)
        gscore.append(jnp.maximum(hi1, hi2) + jnp.maximum(jnp.minimum(hi1, hi2), jnp.maximum(lo1, lo2)))
    best, grp = gscore[0], jnp.zeros_like(gscore[0], dtype=I32)
    for g in range(1, N_GROUPS):
        better = gscore[g] > best
        grp = jnp.where(better, g, grp)
        best = jnp.where(better, gscore[g], best)
    s_in, p_in = [], []
    for k in range(EXPERTS_PER_GROUP):
        sv, pv = srow[k], prow[k]
        for g in range(1, N_GROUPS):
            sv = jnp.where(grp == g, srow[4 * g + k], sv)
            pv = jnp.where(grp == g, prow[4 * g + k], pv)
        s_in.append(sv)
        p_in.append(pv)
    i1, v1, g1 = jnp.zeros_like(grp), s_in[0], p_in[0]
    for k in range(1, EXPERTS_PER_GROUP):
        better = s_in[k] > v1
        i1 = jnp.where(better, k, i1)
        g1 = jnp.where(better, p_in[k], g1)
        v1 = jnp.where(better, s_in[k], v1)
    i2, v2, g2 = jnp.zeros_like(grp), jnp.full_like(v1, -jnp.inf), jnp.zeros_like(v1)
    for k in range(EXPERTS_PER_GROUP):
        better = (i1 != k) & (s_in[k] > v2)
        i2 = jnp.where(better, k, i2)
        g2 = jnp.where(better, p_in[k], g2)
        v2 = jnp.where(better, s_in[k], v2)
    e1 = grp * EXPERTS_PER_GROUP + i1
    e2 = grp * EXPERTS_PER_GROUP + i2
    tot = g1 + g2
    gate_ref[0:1, :] = g1 / tot
    gate_ref[1:2, :] = g2 / tot
    eid = lax.broadcasted_iota(I32, scores.shape, 0)
    hot1 = eid == e1
    hot2 = eid == e2
    onehot = jnp.where(hot1 | hot2, 1.0, 0.0).astype(BF16)
    cum = jnp.dot(onehot, tri_ref[...], preferred_element_type=F32)
    count = cum[:, cum.shape[1] - 1:]
    cnt_ref[0] = jnp.broadcast_to(count, cnt_ref.shape[1:])
    run_len = jnp.floor((count + (RUN_ALIGN - 1)) * (1.0 / RUN_ALIGN)) * RUN_ALIGN
    ends = jnp.broadcast_to(run_len, (N_EXPERTS, LANES))
    row = lax.broadcasted_iota(I32, (N_EXPERTS, LANES), 0)
    step = 1
    while step < N_EXPERTS:
        ends = ends + jnp.where(row >= step, pltpu.roll(ends, step, 0), 0.0)
        step *= 2
    slot = cum + (ends[:, 0:1] - run_len - 1.0)
    pos_ref[0:1, :] = jnp.sum(jnp.where(hot1, slot, 0.0), axis=0, keepdims=True).astype(I32)
    pos_ref[1:2, :] = jnp.sum(jnp.where(hot2, slot, 0.0), axis=0, keepdims=True).astype(I32)


def _route_io(router_w, router_b, N, D, tm):
    rw = jnp.zeros((D, LANES), F32).at[:, :N_EXPERTS].set(router_w.astype(F32))
    rw_hi = rw.astype(BF16)
    rw = jnp.concatenate([rw_hi, (rw - rw_hi.astype(F32)).astype(BF16)], axis=1)
    rb = router_b.astype(F32).reshape(N_EXPERTS, 1)
    tri = jnp.asarray(np.triu(np.ones((tm, tm), np.float32)), BF16)
    in_specs = [pl.BlockSpec((D, 2 * LANES), lambda t: (0, 0)), pl.BlockSpec((N_EXPERTS, 1), lambda t: (0, 0)),
                pl.BlockSpec((tm, tm), lambda t: (0, 0))]
    lane_row = lambda dt: (pl.BlockSpec((2, tm), lambda t: (0, t)), jax.ShapeDtypeStruct((2, N), dt))
    outs = [lane_row(F32), lane_row(I32),
            (pl.BlockSpec((1, N_EXPERTS, LANES), lambda t: (t, 0, 0)),
             jax.ShapeDtypeStruct((N // tm, N_EXPERTS, LANES), F32))]
    return [rw, rb, tri], in_specs, [o[0] for o in outs], [o[1] for o in outs]


def _chunk_tables(lo, go, rlen, tm):
    n_cls = (tm // RUN_ALIGN).bit_length()
    units = rlen // RUN_ALIGN
    cls = jnp.arange(n_cls, dtype=I32)
    flag = (units[:, :, None] >> cls) & 1
    rows = flag * (RUN_ALIGN << cls)
    above = jnp.cumsum(rows[..., ::-1], axis=-1)[..., ::-1] - rows
    src = lo[:, :, None] + above
    dst = go[:, :, None] + above
    slot = jnp.cumsum(flag, axis=1) - 1
    hit = (flag[:, None] == 1) & (slot[:, None] == jnp.arange(N_EXPERTS, dtype=I32)[None, :, None, None])
    compact = lambda a: jnp.sum(jnp.where(hit, a[:, None], 0), axis=2).transpose(0, 2, 1)
    flat = lambda a: a.reshape(-1).astype(I32)
    return (flat(compact(src)), flat(compact(dst)), flat(jnp.sum(flag, axis=1)), flat(jnp.sum(units, axis=1))), n_cls


def _start_pieces(tabs, tile, n_cls, make_copy, wait=False):
    src_ref, dst_ref, cnt_ref = tabs[:3]
    for b in range(n_cls):
        base = (tile * n_cls + b) * N_EXPERTS

        def body(i, carry, base=base, rows=RUN_ALIGN << b):
            copy = make_copy(pl.multiple_of(src_ref[base + i], RUN_ALIGN),
                             pl.multiple_of(dst_ref[base + i], RUN_ALIGN), rows)
            copy.wait() if wait else copy.start()
            return carry

        lax.fori_loop(0, cnt_ref[tile * n_cls + b], body, 0)


def _await_pieces(tabs, tile, max_rows, make_copy):
    units = tabs[3][tile]
    for b in range((max_rows // RUN_ALIGN).bit_length()):
        @pl.when(((units >> b) & 1) != 0)
        def _(rows=RUN_ALIGN << b):
            make_copy(0, 0, rows).wait()


def _dispatch_kernel(src_tab, dst_tab, cnt_tab, tot_tab, gap_src, gap_dst, gap_cnt, f_ref, pos_ref, gates_ref,
                     buf_hbm, srt_ref, zero_ref, sems, *, n_tiles, n_cls, gap_cls):
    t = pl.program_id(0)
    gaps = (gap_src, gap_dst, gap_cnt)
    gap_copy = lambda lo, go, rows: pltpu.make_async_copy(
        zero_ref.at[pl.ds(lo, rows)], buf_hbm.at[pl.ds(go, rows)], sems.at[2])

    @pl.when(t == 0)
    def _():
        zero_ref[...] = jnp.zeros_like(zero_ref)
        _start_pieces(gaps, 0, gap_cls, gap_copy)
    slot = t % 2
    tm, D = f_ref.shape
    ls = srt_ref.shape[1]
    j = lax.broadcasted_iota(I32, (ls, tm), 0)
    hit0 = pos_ref[0:1, :] == j
    hit1 = pos_ref[1:2, :] == j
    perm = jnp.where(hit0 | hit1, 1.0, 0.0).astype(BF16)
    srt_ref[slot, :, :D] = jnp.dot(perm, f_ref[...], preferred_element_type=F32).astype(BF16)
    picked = jnp.where(hit0, gates_ref[0:1, :], jnp.where(hit1, gates_ref[1:2, :], 0.0))
    g = jnp.sum(picked, axis=1, keepdims=True)
    g_hi = g.astype(BF16).astype(F32)
    g_lo = (g - g_hi).astype(BF16).astype(F32)
    lane = lax.broadcasted_iota(I32, (ls, LANES), 1)
    srt_ref[slot, :, D:] = jnp.where(lane == 0, g_hi, jnp.where(lane == 1, g_lo, 0.0)).astype(BF16)

    tabs = (src_tab, dst_tab, cnt_tab, tot_tab)

    def copier(buf_slot):
        return lambda lo, go, rows: pltpu.make_async_copy(
            srt_ref.at[buf_slot, pl.ds(lo, rows)], buf_hbm.at[pl.ds(go, rows)], sems.at[buf_slot])

    _start_pieces(tabs, t, n_cls, copier(slot))

    @pl.when(t > 0)
    def _():
        _await_pieces(tabs, t - 1, ls, copier(1 - slot))

    @pl.when(t == n_tiles - 1)
    def _():
        _await_pieces(tabs, t, ls, copier(slot))
        _start_pieces(gaps, 0, gap_cls, gap_copy, wait=True)


def _dispatch(f, pos, gates, tabs, n_cls, gap_tabs, gap_cls, n_rows, n_tiles, tm):
    D = f.shape[1]
    ls = 2 * tm + N_EXPERTS * RUN_ALIGN
    width = D + LANES
    grid_spec = pltpu.PrefetchScalarGridSpec(
        num_scalar_prefetch=7,
        grid=(n_tiles,),
        in_specs=[pl.BlockSpec((tm, D), lambda t, *_: (t, 0)), pl.BlockSpec((2, tm), lambda t, *_: (0, t)),
                  pl.BlockSpec((2, tm), lambda t, *_: (0, t))],
        out_specs=pl.BlockSpec(memory_space=pl.ANY),
        scratch_shapes=[pltpu.VMEM((2, ls, width), BF16), pltpu.VMEM((EXPERT_BLOCK, width), BF16),
                        pltpu.SemaphoreType.DMA((3,))],
    )
    return pl.pallas_call(
        functools.partial(_dispatch_kernel, n_tiles=n_tiles, n_cls=n_cls, gap_cls=gap_cls),
        grid_spec=grid_spec,
        out_shape=jax.ShapeDtypeStruct((n_rows, width), BF16),
        compiler_params=_cparams("arbitrary"),
        name="moe_dispatch",
    )(*tabs, *gap_tabs[:3], f, pos, gates)


def _expert_kernel(be_ref, bc_ref, bs_ref, x_ref, wg_ref, wu_ref, wd_ref, y_ref):
    i = pl.program_id(0)
    D = y_ref.shape[1]

    @pl.when(bc_ref[i] > 0)
    def _():
        x = x_ref[:, :D]
        gate = jnp.dot(x, wg_ref[0], preferred_element_type=F32)
        up = jnp.dot(x, wu_ref[0], preferred_element_type=F32)
        hid = (gate * jax.nn.sigmoid(gate) * up).astype(BF16)
        pieces = x_ref[:, D:].astype(F32)
        route_gate = pieces[:, 0:1] + pieces[:, 1:2]
        y_ref[...] = (jnp.dot(hid, wd_ref[0], preferred_element_type=F32) * route_gate).astype(BF16)

    @pl.when(bc_ref[i] == 0)
    def _():
        y_ref[...] = jnp.zeros_like(y_ref)


def _experts(buf, block_expert, block_count, block_src, wg, wu, wd):
    n_rows = buf.shape[0]
    D = wg.shape[1]
    nb = n_rows // EXPERT_BLOCK
    grid_spec = pltpu.PrefetchScalarGridSpec(
        num_scalar_prefetch=3,
        grid=(nb,),
        in_specs=[pl.BlockSpec((EXPERT_BLOCK, buf.shape[1]), lambda i, be, bc, bs: (bs[i], 0)),
                  pl.BlockSpec((1,) + wg.shape[1:], lambda i, be, bc, bs: (be[i], 0, 0)),
                  pl.BlockSpec((1,) + wu.shape[1:], lambda i, be, bc, bs: (be[i], 0, 0)),
                  pl.BlockSpec((1,) + wd.shape[1:], lambda i, be, bc, bs: (be[i], 0, 0))],
        out_specs=pl.BlockSpec((EXPERT_BLOCK, D), lambda i, be, bc, bs: (i, 0)),
    )
    return pl.pallas_call(
        _expert_kernel,
        grid_spec=grid_spec,
        out_shape=jax.ShapeDtypeStruct((n_rows, D), BF16),
        compiler_params=_cparams("arbitrary"),
        name="moe_experts",
    )(block_expert, block_count, block_src, buf, wg, wu, wd)


def _combine_kernel(src_tab, dst_tab, cnt_tab, tot_tab, y_hbm, x_ref, pos_ref, gate_ref, fin_ref, xo_ref,
                    srt_ref, sems, *, final, n_tiles, n_cls):
    t = pl.program_id(0)
    slot = t % 2
    tm = x_ref.shape[0]
    ls = srt_ref.shape[1]

    tabs = (src_tab, dst_tab, cnt_tab, tot_tab)

    def copier(buf_slot):
        return lambda lo, go, rows: pltpu.make_async_copy(
            y_hbm.at[pl.ds(go, rows)], srt_ref.at[buf_slot, pl.ds(lo, rows)], sems.at[buf_slot])

    @pl.when(t == 0)
    def _():
        srt_ref[...] = jnp.zeros_like(srt_ref)
        _start_pieces(tabs, t, n_cls, copier(slot))

    @pl.when(t + 1 < n_tiles)
    def _():
        _start_pieces(tabs, t + 1, n_cls, copier(1 - slot))

    _await_pieces(tabs, t, ls, copier(slot))
    j = lax.broadcasted_iota(I32, (tm, ls), 1)
    pick = jnp.where((pos_ref[:, 0:1] == j) | (pos_ref[:, 1:2] == j), 1.0, 0.0).astype(BF16)
    y = jnp.dot(pick, srt_ref[slot], preferred_element_type=F32)
    xn = x_ref[...] + gate_ref[0, 0] * y
    if final:
        xn = xn * lax.rsqrt(jnp.mean(xn * xn, axis=-1, keepdims=True) + RMS_EPS) * fin_ref[...]
    xo_ref[...] = xn


def _combine(y_buf, pos_t, tabs, n_cls, xs, mod, final_gain, n_tiles, tm, tiles_per_seq, final):
    N, D = xs.shape
    nmod = mod.shape[1]
    ls = 2 * tm + N_EXPERTS * RUN_ALIGN
    out_rows = n_tiles * tm if final else N
    kwargs = {} if final else {"input_output_aliases": {5: 0}}
    grid_spec = pltpu.PrefetchScalarGridSpec(
        num_scalar_prefetch=4,
        grid=(n_tiles,),
        in_specs=[pl.BlockSpec(memory_space=pl.ANY),
                  pl.BlockSpec((tm, D), lambda t, *_: (t, 0)),
                  pl.BlockSpec((tm, 2), lambda t, *_: (t, 0)),
                  pl.BlockSpec((1, 1, 1, D), lambda t, *_: (5, jnp.minimum(t // tiles_per_seq, nmod - 1), 0, 0)),
                  pl.BlockSpec((1, D), lambda t, *_: (0, 0))],
        out_specs=pl.BlockSpec((tm, D), lambda t, *_: (t, 0)),
        scratch_shapes=[pltpu.VMEM((2, ls, D), BF16), pltpu.SemaphoreType.DMA((2,))],
    )
    return pl.pallas_call(
        functools.partial(_combine_kernel, final=final, n_tiles=n_tiles, n_cls=n_cls),
        grid_spec=grid_spec,
        out_shape=jax.ShapeDtypeStruct((out_rows, D), F32),
        compiler_params=_cparams("arbitrary"),
        name="moe_combine_final" if final else "moe_combine",
        **kwargs,
    )(*tabs, y_buf, xs, pos_t, mod, final_gain.reshape(1, D))


def _moe(f, routing, xs, mod, wg, wu, wd, final_gain, n_tok, tm, tiles_per_seq, final):
    nt = n_tok // tm
    gates, pos, cnt = routing
    gates, pos, cnt = gates[:, :n_tok], pos[:, :n_tok], cnt[:nt]
    n = cnt[:, :, 0].astype(I32)
    rlen = (n + RUN_ALIGN - 1) // RUN_ALIGN * RUN_ALIGN
    lo = jnp.cumsum(rlen, axis=1) - rlen
    region = jnp.sum(rlen, axis=0)
    region_pad = (region + EXPERT_BLOCK - 1) // EXPERT_BLOCK * EXPERT_BLOCK
    pends = jnp.cumsum(region_pad)
    pstarts = pends - region_pad
    go = pstarts[None, :] + jnp.cumsum(rlen, axis=0) - rlen
    n_blocks = -(-(2 * n_tok + nt * N_EXPERTS * RUN_ALIGN) // EXPERT_BLOCK) + N_EXPERTS
    blk0 = jnp.arange(n_blocks, dtype=I32) * EXPERT_BLOCK
    block_expert = jnp.minimum(jnp.sum((blk0[:, None] >= pends[None, :]).astype(I32), axis=1), N_EXPERTS - 1)
    block_used = (blk0 < (pstarts + region)[block_expert]).astype(I32)
    tabs, n_cls = _chunk_tables(lo, go, rlen, tm)
    gap_tabs, gap_cls = _chunk_tables(jnp.zeros((1, N_EXPERTS), I32), (pstarts + region)[None, :],
                                      (region_pad - region)[None, :], EXPERT_BLOCK)
    buf = _dispatch(f, pos, gates, tabs, n_cls, gap_tabs, gap_cls, n_blocks * EXPERT_BLOCK, nt, tm)
    last_used = jnp.max(jnp.where(block_used > 0, jnp.arange(n_blocks, dtype=I32), 0))
    block_src = jnp.where(block_used > 0, jnp.arange(n_blocks, dtype=I32), last_used)
    y_buf = _experts(buf, block_expert.astype(I32), block_used, block_src, wg, wu, wd)
    return _combine(y_buf, pos.T, tabs, n_cls, xs, mod, final_gain, nt, tm, tiles_per_seq, final)


def _rope_tables(S, tm):
    rows = S // GRID_W
    row = jnp.repeat(jnp.arange(rows, dtype=F32), GRID_W)
    col = jnp.tile(jnp.arange(GRID_W, dtype=F32), rows)
    half = ATTN_HEAD_DIM // 4
    inv_freq = ROPE_THETA ** (-jnp.arange(half, dtype=F32) / half)
    ang_r = row[:, None] * inv_freq[None, :]
    ang_c = col[:, None] * inv_freq[None, :]
    zeros = jnp.zeros_like(ang_r)
    cos = jnp.concatenate([jnp.cos(ang_r)] * 2 + [jnp.cos(ang_c)] * 2, axis=1)
    s1 = jnp.concatenate([-jnp.sin(ang_r), zeros, -jnp.sin(ang_c), zeros], axis=1)
    s2 = jnp.concatenate([zeros, jnp.sin(ang_r), zeros, jnp.sin(ang_c)], axis=1)
    def finish(tab, fill):
        tab = jnp.tile(tab, (1, LANES // ATTN_HEAD_DIM))
        return jnp.concatenate([tab, jnp.full((tm, LANES), fill, F32)], axis=0)
    return finish(cos, 1.0), finish(s1, 0.0), finish(s2, 0.0)


def kernel(x, c, ctx, c_ctx, ada_w, ada_b, norm_mix_g, norm_ffn_g, final_g, attn_w_qkv, attn_w_o, attn_sinks,
           gla_w_in, gla_w_a1, gla_w_a2, gla_b_a, gla_norm_g, gla_w_o, router_w, router_b,
           moe_w_gate, moe_w_up, moe_w_down):
    B, S, D = x.shape
    C = ctx.shape[1]
    depth = ada_w.shape[0]
    tm = TOKEN_TILE
    assert S % tm == 0 and (B * C) % tm == 0 and S % ATTN_BLOCK == 0 and C % ATTN_BLOCK == 0
    assert (B * S) % C == 0 and S % GLA_CHUNK == 0 and C % GLA_CHUNK == 0
    n_lat = B * S
    n_lat_tiles = n_lat // tm
    tiles_per_seq = S // tm

    rpad = -(-(B + 1) // 8) * 8
    cc = jnp.zeros((rpad, D), F32).at[:B].set(c).at[B].set(c_ctx)
    mods = _ada_table(cc, ada_w, ada_b)
    mods = mods[:, :B + 1].reshape(depth, B + 1, 6, 1, D).transpose(0, 2, 1, 3, 4)

    xs = (x.reshape(n_lat, D), ctx.reshape(B * C, D))
    ptm = PROJ_TILE if S % PROJ_TILE == 0 and (B * C) % PROJ_TILE == 0 else tm
    rope = _rope_tables(S, ptm)
    q_dim = ATTN_KV_HEADS * ATTN_GROUP * ATTN_HEAD_DIM
    kv_dim = ATTN_KV_HEADS * ATTN_HEAD_DIM
    kd = GLA_HEADS * GLA_KEY_DIM
    vd = GLA_HEADS * GLA_VAL_DIM

    def dup_heads(w):
        w = w.reshape(D, ATTN_KV_HEADS, 1, ATTN_HEAD_DIM)
        return jnp.broadcast_to(w, (D, ATTN_KV_HEADS, LANES // ATTN_HEAD_DIM, ATTN_HEAD_DIM)).reshape(D, -1)

    for i in range(depth):
        last = i == depth - 1
        mod = mods[i]
        j = i // 2
        if i % 2 == 0:
            wqkv = attn_w_qkv[j]
            w = jnp.concatenate([wqkv[:, :q_dim], dup_heads(wqkv[:, q_dim:q_dim + kv_dim])], axis=1).astype(BF16)
            wvt = wqkv[:, q_dim + kv_dim:].T.astype(BF16)
            q, k, vt = _qkv_proj(xs, norm_mix_g[i], mod, 0, w, wvt, rope, ptm, n_lat // ptm, S // ptm)
            sink_row = jnp.repeat(attn_sinks[j].astype(F32).reshape(ATTN_KV_HEADS, ATTN_GROUP) * LOG2_E,
                                  ATTN_BLOCK, axis=1)
            o = _attention(q, k, vt, sink_row[:, None, :], B, S, C, not last)
            xs, f, *routing = _post_mixer(o, attn_w_o[j].astype(BF16), xs, mod, norm_ffn_g[i], router_w, router_b,
                                          tm, n_lat_tiles, tiles_per_seq)
        else:
            a1 = jnp.zeros((D, LANES), F32).at[:, :2 * GLA_GATE_RANK].set(
                jnp.concatenate([gla_w_a1[j, 0], gla_w_a1[j, 1]], axis=1))
            w = jnp.concatenate([gla_w_in[j], a1], axis=1).astype(BF16)
            w2 = jnp.zeros((LANES, 2 * kd), F32)
            w2 = w2.at[:GLA_GATE_RANK, :kd].set(gla_w_a2[j, 0]).at[GLA_GATE_RANK:2 * GLA_GATE_RANK, kd:].set(gla_w_a2[j, 1])
            ba = gla_b_a[j].reshape(1, 2 * kd).astype(F32)
            qk, v, og, la, chunk_tot = _gla_in_proj(xs, norm_mix_g[i], mod, 0, w, w2.astype(BF16), ba, tm,
                                                    n_lat_tiles, tiles_per_seq)
            o_fwd, o_bwd = _gla_scan(qk, v, la, chunk_tot, B, S, C)
            xs, f, *routing = _post_mixer(o_fwd, gla_w_o[j].astype(BF16), xs, mod, norm_ffn_g[i], router_w, router_b,
                                          tm, n_lat_tiles, tiles_per_seq,
                                          gla_extra=(o_bwd, og, gla_norm_g[j].reshape(1, GLA_VAL_DIM).astype(F32)))
        wg, wu = moe_w_gate[i].astype(BF16), moe_w_up[i].astype(BF16)
        wd = moe_w_down[i].astype(BF16)
        n_tok = n_lat if last else n_lat + B * C
        xs = _moe(f, routing, xs, mod, wg, wu, wd, final_g, n_tok, tm, tiles_per_seq, last)
    return xs.reshape(B, S, D)
```

```python
import functools

import numpy as np
import jax
import jax.numpy as jnp
from jax import lax
from jax.experimental import pallas as pl
from jax.experimental.pallas import tpu as pltpu

F32 = jnp.float32
BF16 = jnp.bfloat16
I32 = jnp.int32

LANES = 128
VMEM_LIMIT_BYTES = 56 * 1024 * 1024

RMS_EPS = 1e-6
GRID_W = 64
ROPE_THETA = 10000.0
ATTN_HEAD_DIM = 64
ROPE_PAIR = ATTN_HEAD_DIM // 4
ATTN_KV_HEADS = 4
ATTN_GROUP = 4
ATTN_BLOCK = 128
ATTN_STEP_BLOCKS = 4
LOG2_E = 1.4426950408889634
ATTN_Q_SCALE = ATTN_HEAD_DIM ** -0.5 * LOG2_E
GLA_HEADS = 4
GLA_KEY_DIM = 128
GLA_VAL_DIM = 256
GLA_GATE_RANK = 16
GLA_GATE_NORM = 16.0
GLA_CHUNK = 128
GLA_STAT_ROWS = 64
GLA_BOUNDED_TOTAL = 40.0
N_EXPERTS = 16
N_GROUPS = 4
EXPERTS_PER_GROUP = 4
EXPERT_BLOCK = 1024
RUN_ALIGN = 16
TOKEN_TILE = 512
PROJ_TILE = 1024


def _cparams(*sem):
    return pltpu.CompilerParams(dimension_semantics=sem, vmem_limit_bytes=VMEM_LIMIT_BYTES)


def _norm_mod(x, gain, shift, scale):
    h = x * lax.rsqrt(jnp.mean(x * x, axis=-1, keepdims=True) + RMS_EPS) * gain
    return h * (1.0 + scale) + shift


def _ada_kernel(c_ref, w_ref, b_ref, o_ref):
    c = c_ref[...]
    s = (c * jax.nn.sigmoid(c)).astype(BF16)
    o_ref[0] = jnp.dot(s, w_ref[0].astype(BF16), preferred_element_type=F32) + b_ref[0]


def _ada_table(cc, ada_w, ada_b):
    L, D, D6 = ada_w.shape
    R = cc.shape[0]
    tn = 1536
    return pl.pallas_call(
        _ada_kernel,
        grid=(L, D6 // tn),
        in_specs=[pl.BlockSpec((R, D), lambda l, j: (0, 0)),
                  pl.BlockSpec((1, D, tn), lambda l, j: (l, 0, j)),
                  pl.BlockSpec((1, 1, tn), lambda l, j: (l, 0, j))],
        out_specs=pl.BlockSpec((1, R, tn), lambda l, j: (l, 0, j)),
        out_shape=jax.ShapeDtypeStruct((L, R, D6), F32),
        compiler_params=_cparams("parallel", "parallel"),
        name="ada_table",
    )(cc, ada_w, ada_b.reshape(L, 1, D6))


def _stream_tile(x_refs, n_lat_tiles):
    if len(x_refs) == 1:
        return x_refs[0][...]
    return jnp.where(pl.program_id(0) < n_lat_tiles, x_refs[0][...], x_refs[1][...])


def _stream_args(xs, tm, n_lat_tiles):
    if isinstance(xs, tuple):
        D = xs[0].shape[1]
        return list(xs), [pl.BlockSpec((tm, D), lambda t: (jnp.minimum(t, n_lat_tiles - 1), 0)),
                          pl.BlockSpec((tm, D), lambda t: (jnp.maximum(t - n_lat_tiles, 0), 0))]
    return [xs], [pl.BlockSpec((tm, xs.shape[1]), lambda t: (t, 0))]


def _qkv_kernel(*refs, n_x, n_lat_tiles):
    x_refs = refs[:n_x]
    g_ref, sh_ref, sc_ref, w_ref, wvt_ref, cos_ref, s1_ref, s2_ref, q_ref, k_ref, vt_ref = refs[n_x:]
    h = _norm_mod(_stream_tile(x_refs, n_lat_tiles), g_ref[...], sh_ref[0, 0], sc_ref[0, 0]).astype(BF16)
    z = jnp.dot(h, w_ref[...], preferred_element_type=F32)
    vt = lax.dot_general(wvt_ref[...], h, (((1,), (1,)), ((), ())), preferred_element_type=F32).astype(BF16)
    for c in range(vt_ref.shape[0]):
        vt_ref[c] = vt[:, c * ATTN_BLOCK:(c + 1) * ATTN_BLOCK]
    cos, s1, s2 = cos_ref[...], s1_ref[...], s2_ref[...]
    nq = q_ref.shape[1] // LANES
    nk = k_ref.shape[1] // LANES
    for j in range(nq + nk):
        zc = z[:, j * LANES:(j + 1) * LANES]
        r = zc * cos + pltpu.roll(zc, LANES - ROPE_PAIR, 1) * s1 + pltpu.roll(zc, ROPE_PAIR, 1) * s2
        if j < nq:
            q_ref[:, j * LANES:(j + 1) * LANES] = (r * ATTN_Q_SCALE).astype(BF16)
        else:
            k_ref[:, (j - nq) * LANES:(j - nq + 1) * LANES] = r.astype(BF16)


def _gla_in_kernel(x_ref, g_ref, sh_ref, sc_ref, w_ref, w2_ref, ba_ref, qk_ref, v_ref, og_ref, la_ref, tot_ref):
    h = _norm_mod(x_ref[...], g_ref[...], sh_ref[0, 0], sc_ref[0, 0]).astype(BF16)
    kd = GLA_HEADS * GLA_KEY_DIM
    vd = GLA_HEADS * GLA_VAL_DIM
    a1 = jnp.dot(h, w_ref[:, 2 * kd + 2 * vd:], preferred_element_type=F32).astype(BF16)
    pre = jnp.dot(a1, w2_ref[...], preferred_element_type=F32) + ba_ref[...]
    z = jnp.dot(h, w_ref[:, :2 * kd + 2 * vd], preferred_element_type=F32)
    la = (jnp.minimum(pre, 0.0) - jnp.log1p(jnp.exp(-jnp.abs(pre)))) * (1.0 / GLA_GATE_NORM)
    la_ref[...] = la
    qk_ref[:, :kd] = (z[:, :kd] * (GLA_KEY_DIM ** -0.5)).astype(BF16)
    qk_ref[:, kd:] = z[:, kd:2 * kd].astype(BF16)
    v_ref[...] = z[:, 2 * kd:2 * kd + vd].astype(BF16)
    og_ref[...] = z[:, 2 * kd + vd:2 * kd + 2 * vd].astype(BF16)
    nc = la.shape[0] // GLA_STAT_ROWS
    tot = jnp.sum(la.reshape(nc, GLA_STAT_ROWS, la.shape[1]), axis=1)
    lane = lax.broadcasted_iota(I32, (nc, LANES), 1)
    acc = jnp.zeros((nc, LANES), F32)
    for hd in range(2 * GLA_HEADS):
        worst = jnp.min(tot[:, hd * GLA_KEY_DIM:(hd + 1) * GLA_KEY_DIM], axis=1, keepdims=True)
        acc = jnp.where(lane == hd, worst, acc)
    tot_ref[...] = acc


def _tile_specs(D, tm, n_lat_tiles, tiles_per_seq, n_mod_rows):
    def mod_idx(which):
        return lambda t: (which, jnp.minimum(t // tiles_per_seq, n_mod_rows - 1), 0, 0)
    return mod_idx, [pl.BlockSpec((tm, D), lambda t: (t, 0)),
                     pl.BlockSpec((1, D), lambda t: (0, 0))]


def _qkv_proj(xs, gain, mod, which, w, wvt, rope, tm, n_lat_tiles, tiles_per_seq):
    x_args, x_specs = _stream_args(xs, tm, n_lat_tiles)
    N = sum(a.shape[0] for a in x_args)
    D = x_args[0].shape[1]
    nmod = mod.shape[1]
    mod_idx, specs = _tile_specs(D, tm, n_lat_tiles, tiles_per_seq, nmod)
    qd = ATTN_KV_HEADS * ATTN_GROUP * ATTN_HEAD_DIM
    kd = ATTN_KV_HEADS * LANES
    vd = wvt.shape[0]
    rope_idx = lambda t: (jnp.where(t < n_lat_tiles, t % tiles_per_seq, tiles_per_seq), 0)
    in_specs = x_specs + specs[1:] + [
        pl.BlockSpec((1, 1, 1, D), mod_idx(which)),
        pl.BlockSpec((1, 1, 1, D), mod_idx(which + 1)),
        pl.BlockSpec(w.shape, lambda t: (0, 0)),
        pl.BlockSpec(wvt.shape, lambda t: (0, 0)),
        pl.BlockSpec((tm, LANES), rope_idx),
        pl.BlockSpec((tm, LANES), rope_idx),
        pl.BlockSpec((tm, LANES), rope_idx),
    ]
    return pl.pallas_call(
        functools.partial(_qkv_kernel, n_x=len(x_args), n_lat_tiles=n_lat_tiles),
        grid=(N // tm,),
        in_specs=in_specs,
        out_specs=[pl.BlockSpec((tm, qd), lambda t: (t, 0)),
                   pl.BlockSpec((tm, kd), lambda t: (t, 0)),
                   pl.BlockSpec((tm // ATTN_BLOCK, vd, ATTN_BLOCK), lambda t: (t, 0, 0))],
        out_shape=[jax.ShapeDtypeStruct((N, qd), BF16),
                   jax.ShapeDtypeStruct((N, kd), BF16),
                   jax.ShapeDtypeStruct((N // ATTN_BLOCK, vd, ATTN_BLOCK), BF16)],
        compiler_params=_cparams("parallel"),
        name="attn_qkv_proj",
    )(*x_args, gain.reshape(1, D), mod, mod, w, wvt, *rope)


def _gla_in_proj(xs, gain, mod, which, w, w2, ba, tm, n_lat_tiles, tiles_per_seq):
    N, D = xs.shape
    nmod = mod.shape[1]
    mod_idx, specs = _tile_specs(D, tm, n_lat_tiles, tiles_per_seq, nmod)
    kd = GLA_HEADS * GLA_KEY_DIM
    vd = GLA_HEADS * GLA_VAL_DIM
    in_specs = specs + [
        pl.BlockSpec((1, 1, 1, D), mod_idx(which)),
        pl.BlockSpec((1, 1, 1, D), mod_idx(which + 1)),
        pl.BlockSpec(w.shape, lambda t: (0, 0)),
        pl.BlockSpec(w2.shape, lambda t: (0, 0)),
        pl.BlockSpec((1, 2 * kd), lambda t: (0, 0)),
    ]
    row = lambda width: pl.BlockSpec((tm, width), lambda t: (t, 0))
    return pl.pallas_call(
        _gla_in_kernel,
        grid=(N // tm,),
        in_specs=in_specs,
        out_specs=[row(2 * kd), row(vd), row(vd), row(2 * kd),
                   pl.BlockSpec((tm // GLA_STAT_ROWS, LANES), lambda t: (t, 0))],
        out_shape=[jax.ShapeDtypeStruct((N, 2 * kd), BF16),
                   jax.ShapeDtypeStruct((N, vd), BF16),
                   jax.ShapeDtypeStruct((N, vd), BF16),
                   jax.ShapeDtypeStruct((N, 2 * kd), F32),
                   jax.ShapeDtypeStruct((N // GLA_STAT_ROWS, LANES), F32)],
        compiler_params=_cparams("parallel"),
        name="gla_in_proj",
    )(xs, gain.reshape(1, D), mod, mod, w, w2, ba)


def _attn_kernel(*refs, window, nq, nb):
    tq = ATTN_BLOCK
    if window:
        q_ref, kp, km, kn, kx, vp, vm, vn, vx, sink_ref, tri_ref, o_ref = refs
        k_blocks = ([lambda ks: kp[:, ks]] + [lambda ks, i=i: km[i * tq:(i + 1) * tq, ks] for i in range(nb)]
                    + [lambda ks: kn[:, ks]])
        v_blocks = ([lambda hs: vp[0, hs, :]] + [lambda hs, i=i: vm[i, hs, :] for i in range(nb)]
                    + [lambda hs: vn[0, hs, :]])
    else:
        q_ref, kx, vx, sink_ref, _, o_ref = refs
    lane = lax.broadcasted_iota(I32, (tq, LANES), 1)
    first_head = lane < ATTN_HEAD_DIM
    if window:
        bias = []
        for i in range(nb):
            j = pl.program_id(1) * nb + i
            bias_prev = jnp.where(j > 0, tri_ref[0], -jnp.inf)
            bias_next = jnp.where(j < nq - 1, tri_ref[1], -jnp.inf)
            bias.append((jnp.concatenate([bias_prev] * ATTN_GROUP, axis=1),
                         jnp.concatenate([bias_next] * ATTN_GROUP, axis=1)))

    def scores(i, kh):
        ks = slice(kh * LANES, (kh + 1) * LANES)
        rows = slice(i * tq, (i + 1) * tq)
        kk = jnp.concatenate([blk(ks) for blk in k_blocks[i:i + 3]] + [kx[:, ks]], axis=0) if window else kx[:, ks]
        qa = q_ref[rows, (2 * kh) * LANES:(2 * kh + 1) * LANES]
        qb = q_ref[rows, (2 * kh + 1) * LANES:(2 * kh + 2) * LANES]
        zero = jnp.zeros_like(qa)
        qs = jnp.concatenate([jnp.where(first_head, qa, zero), jnp.where(first_head, zero, qa),
                              jnp.where(first_head, qb, zero), jnp.where(first_head, zero, qb)], axis=0)
        return lax.dot_general(kk, qs, (((1,), (1,)), ((), ())), preferred_element_type=F32)

    def softmax(s, i, kh):
        if window:
            s = jnp.concatenate([s[:tq] + bias[i][0], s[tq:2 * tq], s[2 * tq:3 * tq] + bias[i][1], s[3 * tq:]], axis=0)
        sink = sink_ref[kh]
        m = jnp.maximum(jnp.max(s, axis=0, keepdims=True), sink)
        p = jnp.exp2(s - m)
        l = jnp.sum(p, axis=0, keepdims=True) + jnp.exp2(sink - m)
        return p.astype(BF16), l

    def values(p, l, i, kh):
        hs = slice(kh * ATTN_HEAD_DIM, (kh + 1) * ATTN_HEAD_DIM)
        v_ctx = [vx[c, hs, :] for c in range(vx.shape[0])]
        vv = jnp.concatenate(([blk(hs) for blk in v_blocks[i:i + 3]] if window else []) + v_ctx, axis=1)
        o = jnp.dot(vv, p, preferred_element_type=F32) / l
        for g in range(ATTN_GROUP):
            h = kh * ATTN_GROUP + g
            o_ref[i, h * ATTN_HEAD_DIM:(h + 1) * ATTN_HEAD_DIM, :] = o[:, g * tq:(g + 1) * tq].astype(BF16)

    units = [(i, kh) for i in range(nb) for kh in range(ATTN_KV_HEADS)]
    s_next = scores(*units[0])
    pending = None
    for n, unit in enumerate(units):
        s_cur = s_next
        if n + 1 < len(units):
            s_next = scores(*units[n + 1])
        if pending is not None:
            values(*pending)
        pending = softmax(s_cur, *unit) + unit
    values(*pending)


def _attention(q, k, vt, sink_row, B, S, C, need_ctx):
    N, qd = q.shape
    kd = k.shape[1]
    vd = vt.shape[1]
    tq = ATTN_BLOCK
    nq = S // tq
    nb = ATTN_STEP_BLOCKS
    nbc = C // tq
    assert nq % nb == 0 and C == nbc * tq
    steps = nq // nb
    ctx_blk0 = (B * S) // C
    mid = lambda b, j: (b * steps + j, 0)
    prev = lambda b, j: (b * nq + jnp.maximum(j * nb - 1, 0), 0)
    nxt = lambda b, j: (b * nq + jnp.minimum(j * nb + nb, nq - 1), 0)
    cmap = lambda b, j: (ctx_blk0 + b, 0)
    kctx = pl.BlockSpec((C, kd), cmap)
    slab = lambda n, m: pl.BlockSpec((n, vd, tq), lambda b, j: (m(b, j)[0], 0, 0))
    vctx = slab(C // tq, cmap)
    sink_spec = pl.BlockSpec(sink_row.shape, lambda b, j: (0, 0, 0))
    key = np.arange(tq)[:, None]
    qry = np.arange(tq)[None, :]
    tri = jnp.asarray(np.stack([np.where(key >= qry, 0.0, -np.inf), np.where(key <= qry, 0.0, -np.inf)]), F32)
    o_lat = pl.pallas_call(
        functools.partial(_attn_kernel, window=True, nq=nq, nb=nb),
        grid=(B, steps),
        in_specs=[pl.BlockSpec((nb * tq, qd), mid),
                  pl.BlockSpec((tq, kd), prev), pl.BlockSpec((nb * tq, kd), mid), pl.BlockSpec((tq, kd), nxt), kctx,
                  slab(1, prev), slab(nb, mid), slab(1, nxt), vctx, sink_spec,
                  pl.BlockSpec(tri.shape, lambda b, j: (0, 0, 0))],
        out_specs=pl.BlockSpec((nb, qd, tq), lambda b, j: (mid(b, j)[0], 0, 0)),
        out_shape=jax.ShapeDtypeStruct((N // tq, qd, tq), BF16),
        compiler_params=_cparams("parallel", "parallel"),
        name="attn_window",
    )(q, k, k, k, k, vt, vt, vt, vt, sink_row, tri)
    if not need_ctx:
        return o_lat
    return pl.pallas_call(
        functools.partial(_attn_kernel, window=False, nq=nbc, nb=nbc),
        grid=(B, 1),
        in_specs=[pl.BlockSpec((C, qd), cmap), kctx, vctx, sink_spec,
                  pl.BlockSpec(memory_space=pl.ANY)],
        out_specs=pl.BlockSpec((nbc, qd, tq), lambda b, j: (cmap(b, j)[0], 0, 0)),
        out_shape=jax.ShapeDtypeStruct((N // tq, qd, tq), BF16),
        input_output_aliases={4: 0},
        compiler_params=_cparams("parallel", "parallel"),
        name="attn_context",
    )(q, k, vt, sink_row, o_lat)


def _gla_constants(C):
    levels = []
    m = 1
    while m < C:
        levels.append(m)
        m *= 2
    t = np.arange(C)[:, None]
    u = np.arange(C)[None, :]
    secs = [(u <= t), (u > t)]
    masks = []
    for m in levels:
        base = (t // (2 * m)) * (2 * m)
        ref = base + m - 1
        second = t >= base + m
        secs.append(np.where(second, (u > ref) & (u <= t), (u > t) & (u <= ref)))
        masks.append((t // (2 * m) == u // (2 * m)) & second & (u < (u // (2 * m)) * (2 * m) + m))
    masks.append(t == u)
    masks.append(u <= t)
    mf = np.concatenate([s.astype(np.float32) for s in secs], axis=0)
    kf = np.stack([mk.astype(np.float32) for mk in masks], axis=0)
    mb = np.concatenate([s.astype(np.float32)[::-1, ::-1] for s in secs], axis=0)
    kb = np.stack([mk.astype(np.float32)[::-1, ::-1] for mk in masks], axis=0)
    return np.stack([mf, mb]), np.stack([kf, kb]), len(levels)


def _gla_chunk(q, k, v, g, st_ref, mm_ref, mk_ref, d, nl, bounded):
    C, dk = q.shape
    nt = (((1,), (1,)), ((), ()))
    g_hi = g.astype(BF16)
    g_lo = (g - g_hi.astype(F32)).astype(BF16)
    mmat = mm_ref[d, :2 * C] if bounded else mm_ref[d]
    e2 = jnp.dot(mmat, jnp.concatenate([g_hi, g_lo], axis=1), preferred_element_type=F32)
    ee = e2[:, :dk] + e2[:, dk:]
    ex = jnp.exp(ee)
    qf, kf = q.astype(F32), k.astype(F32)
    qe = (qf * ex[:C]).astype(BF16)
    ke = (kf * ex[C:2 * C]).astype(BF16)
    st = st_ref[...]
    o = lax.dot_general(qe, st.astype(BF16), nt, preferred_element_type=F32)
    if bounded:
        ki = (kf * jnp.exp(-ee[:C])).astype(BF16)
        a = mk_ref[d, nl + 1] * lax.dot_general(qe, ki, nt, preferred_element_type=F32)
    else:
        a = mk_ref[d, nl] * lax.dot_general(q, k, nt, preferred_element_type=F32)
        for i in range(nl):
            xl = ex[(2 + i) * C:(3 + i) * C]
            ql = (qf * xl).astype(BF16)
            kl = (kf * xl).astype(BF16)
            a = a + mk_ref[d, i] * lax.dot_general(ql, kl, nt, preferred_element_type=F32)
    o = o + jnp.dot(a.astype(BF16), v, preferred_element_type=F32)
    decay = jnp.exp(jnp.sum(g, axis=0, keepdims=True))
    st_ref[...] = st * decay + lax.dot_general(v, ke, (((0,), (0,)), ((), ())), preferred_element_type=F32)
    return o


def _gla_kernel(okf_ref, okb_ref, qf, kf, vf, lf, qb, kb, vb, lb, mm_ref, mk_ref, of_ref, ob_ref, st_ref,
                *, nl, seg_chunks, lat_segs, ctx_seg0):
    C = GLA_CHUNK
    H, DK, DV = GLA_HEADS, GLA_KEY_DIM, GLA_VAL_DIM
    b, j = pl.program_id(0), pl.program_id(1)

    @pl.when(j == 0)
    def _():
        st_ref[...] = jnp.zeros_like(st_ref)

    seg_f = jnp.where(j == 0, ctx_seg0 + b, b * lat_segs + j - 1)
    seg_b = jnp.where(j == 0, ctx_seg0 + b, b * lat_segs + lat_segs - j)

    nt = (((1,), (1,)), ((), ()))
    kw = H * DK

    def rows_of(i):
        cf, cb = i, seg_chunks - 1 - i
        return slice(cf * C, (cf + 1) * C), slice(cb * C, (cb + 1) * C)

    def advance_general(i):
        rf, rb = rows_of(i)
        for h in range(H):
            ks, vs = slice(h * DK, (h + 1) * DK), slice(h * DV, (h + 1) * DV)
            o = _gla_chunk(qf[rf, ks], kf[rf, ks], vf[rf, vs], lf[rf, ks], st_ref.at[h],
                           mm_ref, mk_ref, 0, nl, False)
            of_ref[rf, vs] = o.astype(BF16)
            o = _gla_chunk(qb[rb, ks], kb[rb, ks], vb[rb, vs], lb[rb, ks], st_ref.at[H + h],
                           mm_ref, mk_ref, 1, nl, False)
            ob_ref[rb, vs] = o.astype(BF16)

    def bounded_front(i):
        rf, rb = rows_of(i)
        sides = ((qf, kf, vf, lf, rf, of_ref, 0), (qb, kb, vb, lb, rb, ob_ref, 1))
        pre = []
        for q_r, k_r, v_r, l_r, rows, o_r, d in sides:
            g = l_r[rows, :]
            g_hi = g.astype(BF16)
            g_lo = (g - g_hi.astype(F32)).astype(BF16)
            e2 = jnp.dot(mm_ref[d, :2 * C], jnp.concatenate([g_hi, g_lo], axis=1),
                         preferred_element_type=F32)
            pre.append((e2[:, :kw] + e2[:, kw:], g))
        units = []
        for (ee, g), (q_r, k_r, v_r, l_r, rows, o_r, d) in zip(pre, sides):
            ex = jnp.exp(ee)
            kf32 = k_r[rows, :].astype(F32)
            qe = (q_r[rows, :].astype(F32) * ex[:C]).astype(BF16)
            ke = (kf32 * ex[C:]).astype(BF16)
            ki = (kf32 * jnp.exp(-ee[:C])).astype(BF16)
            decay = jnp.exp(jnp.sum(g, axis=0, keepdims=True))
            for h in range(H):
                ks = slice(h * DK, (h + 1) * DK)
                units.append([d, h, qe[:, ks], ke[:, ks], ki[:, ks], decay[:, ks], v_r, rows, o_r])
        for unit in units:
            d, h, qe, ke, ki = unit[:5]
            score = lax.dot_general(qe, ki, nt, preferred_element_type=F32)
            unit.append((mk_ref[d, nl + 1] * score).astype(BF16))
        return units

    def bounded_back(units):
        inter = [lax.dot_general(u[2], st_ref[u[0] * H + u[1]].astype(BF16), nt, preferred_element_type=F32)
                 for u in units]
        for n, (d, h, qe, ke, ki, decay, v_r, rows, o_r, a) in enumerate(units):
            vs = slice(h * DV, (h + 1) * DV)
            v = v_r[rows, vs]
            o = inter[n] + jnp.dot(a, v, preferred_element_type=F32)
            upd = lax.dot_general(v, ke, (((0,), (0,)), ((), ())), preferred_element_type=F32)
            o_r[rows, vs] = o.astype(BF16)
            st_ref[d * H + h] = st_ref[d * H + h] * decay + upd

    bounded = None
    for i in range(seg_chunks):
        ok = (okf_ref[seg_f * seg_chunks + i] != 0) & (okb_ref[seg_b * seg_chunks + seg_chunks - 1 - i] != 0)
        bounded = ok if bounded is None else bounded & ok

    @pl.when(bounded)
    def _():
        fronts = [bounded_front(i) for i in range(seg_chunks)]
        for units in fronts:
            bounded_back(units)

    @pl.when(jnp.logical_not(bounded))
    def _():
        for i in range(seg_chunks):
            advance_general(i)


def _gla_scan(qk, v, la, chunk_tot, B, S, C):
    N = qk.shape[0]
    H, DK, DV = GLA_HEADS, GLA_KEY_DIM, GLA_VAL_DIM
    seg = C
    assert S % seg == 0 and seg % GLA_CHUNK == 0
    lat_segs = S // seg
    ctx_seg0 = (B * S) // seg
    mm, mk, nl = _gla_constants(GLA_CHUNK)
    mm = jnp.asarray(mm, BF16)
    mk = jnp.asarray(mk, F32)
    chunk_tot = jnp.sum(chunk_tot.reshape(-1, GLA_CHUNK // GLA_STAT_ROWS, LANES), axis=1)
    ok = chunk_tot[:, :2 * H] >= -GLA_BOUNDED_TOTAL
    okf = jnp.all(ok[:, :H], axis=1).astype(I32)
    okb = jnp.all(ok[:, H:], axis=1).astype(I32)
    fwd = lambda col: (lambda b, j, *_: (jnp.where(j == 0, ctx_seg0 + b, b * lat_segs + j - 1), col))
    bwd = lambda col: (lambda b, j, *_: (jnp.where(j == 0, ctx_seg0 + b, b * lat_segs + lat_segs - j), col))
    kw, vw = H * DK, H * DV
    in_specs = [
        pl.BlockSpec((seg, kw), fwd(0)), pl.BlockSpec((seg, kw), fwd(1)), pl.BlockSpec((seg, vw), fwd(0)),
        pl.BlockSpec((seg, kw), fwd(0)),
        pl.BlockSpec((seg, kw), bwd(0)), pl.BlockSpec((seg, kw), bwd(1)), pl.BlockSpec((seg, vw), bwd(0)),
        pl.BlockSpec((seg, kw), bwd(1)),
        pl.BlockSpec(mm.shape, lambda b, j, *_: (0, 0, 0)),
        pl.BlockSpec(mk.shape, lambda b, j, *_: (0, 0, 0, 0)),
    ]
    grid_spec = pltpu.PrefetchScalarGridSpec(
        num_scalar_prefetch=2,
        grid=(B, lat_segs + 1),
        in_specs=in_specs,
        out_specs=[pl.BlockSpec((seg, vw), fwd(0)), pl.BlockSpec((seg, vw), bwd(0))],
        scratch_shapes=[pltpu.VMEM((2 * H, DV, DK), F32)],
    )
    return pl.pallas_call(
        functools.partial(_gla_kernel, nl=nl, seg_chunks=seg // GLA_CHUNK, lat_segs=lat_segs, ctx_seg0=ctx_seg0),
        grid_spec=grid_spec,
        out_shape=[jax.ShapeDtypeStruct((N, vw), BF16), jax.ShapeDtypeStruct((N, vw), BF16)],
        compiler_params=_cparams("parallel", "arbitrary"),
        name="gla_scan",
    )(okf, okb, qk, qk, v, la, qk, qk, v, la, mm, mk)


def _post_kernel(*refs, gla, n_x, n_lat_tiles):
    x_refs, refs = refs[:n_x], refs[n_x:]
    route_refs = refs[-3:]
    route_in = refs[-8:-5]
    refs = refs[:-8] + refs[-5:-3]
    x_tile = _stream_tile(x_refs, n_lat_tiles)
    if gla:
        o_ref, ob_ref, og_ref, ng_ref, w_ref, gate_ref, fg_ref, fsh_ref, fsc_ref, xo_ref, f_ref = refs
        o = o_ref[...].astype(F32) + ob_ref[...].astype(F32)
        g = og_ref[...].astype(F32)
        parts = []
        for h in range(GLA_HEADS):
            oh = o[:, h * GLA_VAL_DIM:(h + 1) * GLA_VAL_DIM]
            parts.append(oh * lax.rsqrt(jnp.mean(oh * oh, axis=-1, keepdims=True) + RMS_EPS) * ng_ref[...])
        mix = (jnp.concatenate(parts, axis=1) * (g * jax.nn.sigmoid(g))).astype(BF16)
        y = jnp.dot(mix, w_ref[...], preferred_element_type=F32)
    else:
        o_ref, w_ref, gate_ref, fg_ref, fsh_ref, fsc_ref, xo_ref, f_ref = refs
        o_t = jnp.concatenate([o_ref[c] for c in range(o_ref.shape[0])], axis=1)
        y = lax.dot_general(o_t, w_ref[...], (((0,), (0,)), ((), ())), preferred_element_type=F32)
    xn = x_tile + gate_ref[0, 0] * y
    xo_ref[...] = xn
    f = _norm_mod(xn, fg_ref[...], fsh_ref[0, 0], fsc_ref[0, 0])
    f_ref[...] = f.astype(BF16)
    _route_tile(f, *route_in, *route_refs)


def _post_mixer(o, w_o, xs, mod, ffn_gain, router_w, router_b, tm, n_lat_tiles, tiles_per_seq, gla_extra=None):
    args, specs = _stream_args(xs, tm, n_lat_tiles)
    n_x = len(args)
    N = sum(a.shape[0] for a in args)
    D = args[0].shape[1]
    r_args, r_in, r_out, r_shapes = _route_io(router_w, router_b, N, D, tm)
    nmod = mod.shape[1]
    mod_idx = lambda which: (lambda t: (which, jnp.minimum(t // tiles_per_seq, nmod - 1), 0, 0))
    row = lambda width: pl.BlockSpec((tm, width), lambda t: (t, 0))
    const = lambda a: pl.BlockSpec(a.shape, lambda t: (0,) * a.ndim)
    if gla_extra is None:
        args, specs = args + [o], specs + [pl.BlockSpec((tm // o.shape[2],) + o.shape[1:], lambda t: (t, 0, 0))]
    else:
        o_bwd, og, ng = gla_extra
        args += [o, o_bwd, og, ng]
        specs += [row(o.shape[1]), row(o_bwd.shape[1]), row(og.shape[1]), const(ng)]
    args += [w_o, mod, ffn_gain.reshape(1, D), mod, mod]
    specs += [const(w_o), pl.BlockSpec((1, 1, 1, D), mod_idx(2)), pl.BlockSpec((1, D), lambda t: (0, 0)),
              pl.BlockSpec((1, 1, 1, D), mod_idx(3)), pl.BlockSpec((1, 1, 1, D), mod_idx(4))]
    args += r_args
    specs += r_in
    return pl.pallas_call(
        functools.partial(_post_kernel, gla=gla_extra is not None, n_x=n_x, n_lat_tiles=n_lat_tiles),
        grid=(N // tm,),
        in_specs=specs,
        out_specs=[row(D), row(D)] + r_out,
        out_shape=[jax.ShapeDtypeStruct((N, D), F32), jax.ShapeDtypeStruct((N, D), BF16)] + r_shapes,
        input_output_aliases={0: 0} if n_x == 1 else {},
        compiler_params=_cparams("parallel"),
        name="post_mixer_gla" if gla_extra is not None else "post_mixer_attn",
    )(*args)


def _route_tile(f, rw_ref, rb_ref, tri_ref, gate_ref, pos_ref, cnt_ref):
    tm = f.shape[0]
    fh = f.astype(BF16)
    fl = (f - fh.astype(F32)).astype(BF16)
    parts = jnp.dot(jnp.concatenate([fh, fl], axis=0), rw_ref[...], preferred_element_type=F32)
    logits = (parts[:tm, :LANES] + parts[:tm, LANES:]) + (parts[tm:, :LANES] + parts[tm:, LANES:])
    lt = logits.T[:N_EXPERTS]
    scores = jax.nn.sigmoid(lt)
    sel = scores + rb_ref[...]
    srow = [sel[e:e + 1] for e in range(N_EXPERTS)]
    prow = [scores[e:e + 1] for e in range(N_EXPERTS)]
    gscore = []
    for g in range(N_GROUPS):
        a, b, c, d = srow[4 * g:4 * g + 4]
        hi1, lo1, hi2, lo2 = jnp.maximum(a, b), jnp.minimum(a, b), jnp.maximum(c, d), jnp.minimum(c, d)
        gscore.append(jnp.maximum(hi1, hi2) + jnp.maximum(jnp.minimum(hi1, hi2), jnp.maximum(lo1, lo2)))
    best, grp = gscore[0], jnp.zeros_like(gscore[0], dtype=I32)
    for g in range(1, N_GROUPS):
        better = gscore[g] > best
        grp = jnp.where(better, g, grp)
        best = jnp.where(better, gscore[g], best)
    s_in, p_in = [], []
    for k in range(EXPERTS_PER_GROUP):
        sv, pv = srow[k], prow[k]
        for g in range(1, N_GROUPS):
            sv = jnp.where(grp == g, srow[4 * g + k], sv)
            pv = jnp.where(grp == g, prow[4 * g + k], pv)
        s_in.append(sv)
        p_in.append(pv)
    i1, v1, g1 = jnp.zeros_like(grp), s_in[0], p_in[0]
    for k in range(1, EXPERTS_PER_GROUP):
        better = s_in[k] > v1
        i1 = jnp.where(better, k, i1)
        g1 = jnp.where(better, p_in[k], g1)
        v1 = jnp.where(better, s_in[k], v1)
    i2, v2, g2 = jnp.zeros_like(grp), jnp.full_like(v1, -jnp.inf), jnp.zeros_like(v1)
    for k in range(EXPERTS_PER_GROUP):
        better = (i1 != k) & (s_in[k] > v2)
        i2 = jnp.where(better, k, i2)
        g2 = jnp.where(better, p_in[k], g2)
        v2 = jnp.where(better, s_in[k], v2)
    e1 = grp * EXPERTS_PER_GROUP + i1
    e2 = grp * EXPERTS_PER_GROUP + i2
    tot = g1 + g2
    gate_ref[0:1, :] = g1 / tot
    gate_ref[1:2, :] = g2 / tot
    eid = lax.broadcasted_iota(I32, scores.shape, 0)
    hot1 = eid == e1
    hot2 = eid == e2
    onehot = jnp.where(hot1 | hot2, 1.0, 0.0).astype(BF16)
    cum = jnp.dot(onehot, tri_ref[...], preferred_element_type=F32)
    count = cum[:, cum.shape[1] - 1:]
    cnt_ref[0] = jnp.broadcast_to(count, cnt_ref.shape[1:])
    run_len = jnp.floor((count + (RUN_ALIGN - 1)) * (1.0 / RUN_ALIGN)) * RUN_ALIGN
    ends = jnp.broadcast_to(run_len, (N_EXPERTS, LANES))
    row = lax.broadcasted_iota(I32, (N_EXPERTS, LANES), 0)
    step = 1
    while step < N_EXPERTS:
        ends = ends + jnp.where(row >= step, pltpu.roll(ends, step, 0), 0.0)
        step *= 2
    slot = cum + (ends[:, 0:1] - run_len - 1.0)
    pos_ref[0:1, :] = jnp.sum(jnp.where(hot1, slot, 0.0), axis=0, keepdims=True).astype(I32)
    pos_ref[1:2, :] = jnp.sum(jnp.where(hot2, slot, 0.0), axis=0, keepdims=True).astype(I32)


def _route_io(router_w, router_b, N, D, tm):
    rw = jnp.zeros((D, LANES), F32).at[:, :N_EXPERTS].set(router_w.astype(F32))
    rw_hi = rw.astype(BF16)
    rw = jnp.concatenate([rw_hi, (rw - rw_hi.astype(F32)).astype(BF16)], axis=1)
    rb = router_b.astype(F32).reshape(N_EXPERTS, 1)
    tri = jnp.asarray(np.triu(np.ones((tm, tm), np.float32)), BF16)
    in_specs = [pl.BlockSpec((D, 2 * LANES), lambda t: (0, 0)), pl.BlockSpec((N_EXPERTS, 1), lambda t: (0, 0)),
                pl.BlockSpec((tm, tm), lambda t: (0, 0))]
    lane_row = lambda dt: (pl.BlockSpec((2, tm), lambda t: (0, t)), jax.ShapeDtypeStruct((2, N), dt))
    outs = [lane_row(F32), lane_row(I32),
            (pl.BlockSpec((1, N_EXPERTS, LANES), lambda t: (t, 0, 0)),
             jax.ShapeDtypeStruct((N // tm, N_EXPERTS, LANES), F32))]
    return [rw, rb, tri], in_specs, [o[0] for o in outs], [o[1] for o in outs]


def _chunk_tables(lo, go, rlen, tm):
    n_cls = (tm // RUN_ALIGN).bit_length()
    units = rlen // RUN_ALIGN
    cls = jnp.arange(n_cls, dtype=I32)
    flag = (units[:, :, None] >> cls) & 1
    rows = flag * (RUN_ALIGN << cls)
    above = jnp.cumsum(rows[..., ::-1], axis=-1)[..., ::-1] - rows
    src = lo[:, :, None] + above
    dst = go[:, :, None] + above
    slot = jnp.cumsum(flag, axis=1) - 1
    hit = (flag[:, None] == 1) & (slot[:, None] == jnp.arange(N_EXPERTS, dtype=I32)[None, :, None, None])
    compact = lambda a: jnp.sum(jnp.where(hit, a[:, None], 0), axis=2).transpose(0, 2, 1)
    flat = lambda a: a.reshape(-1).astype(I32)
    return (flat(compact(src)), flat(compact(dst)), flat(jnp.sum(flag, axis=1)), flat(jnp.sum(units, axis=1))), n_cls


def _start_pieces(tabs, tile, n_cls, make_copy, wait=False):
    src_ref, dst_ref, cnt_ref = tabs[:3]
    for b in range(n_cls):
        base = (tile * n_cls + b) * N_EXPERTS

        def body(i, carry, base=base, rows=RUN_ALIGN << b):
            copy = make_copy(pl.multiple_of(src_ref[base + i], RUN_ALIGN),
                             pl.multiple_of(dst_ref[base + i], RUN_ALIGN), rows)
            copy.wait() if wait else copy.start()
            return carry

        lax.fori_loop(0, cnt_ref[tile * n_cls + b], body, 0)


def _await_pieces(tabs, tile, max_rows, make_copy):
    units = tabs[3][tile]
    for b in range((max_rows // RUN_ALIGN).bit_length()):
        @pl.when(((units >> b) & 1) != 0)
        def _(rows=RUN_ALIGN << b):
            make_copy(0, 0, rows).wait()


def _dispatch_kernel(src_tab, dst_tab, cnt_tab, tot_tab, gap_src, gap_dst, gap_cnt, f_ref, pos_ref, gates_ref,
                     buf_hbm, srt_ref, zero_ref, sems, *, n_tiles, n_cls, gap_cls):
    t = pl.program_id(0)
    gaps = (gap_src, gap_dst, gap_cnt)
    gap_copy = lambda lo, go, rows: pltpu.make_async_copy(
        zero_ref.at[pl.ds(lo, rows)], buf_hbm.at[pl.ds(go, rows)], sems.at[2])

    @pl.when(t == 0)
    def _():
        zero_ref[...] = jnp.zeros_like(zero_ref)
        _start_pieces(gaps, 0, gap_cls, gap_copy)
    slot = t % 2
    tm, D = f_ref.shape
    ls = srt_ref.shape[1]
    j = lax.broadcasted_iota(I32, (ls, tm), 0)
    hit0 = pos_ref[0:1, :] == j
    hit1 = pos_ref[1:2, :] == j
    perm = jnp.where(hit0 | hit1, 1.0, 0.0).astype(BF16)
    srt_ref[slot, :, :D] = jnp.dot(perm, f_ref[...], preferred_element_type=F32).astype(BF16)
    picked = jnp.where(hit0, gates_ref[0:1, :], jnp.where(hit1, gates_ref[1:2, :], 0.0))
    g = jnp.sum(picked, axis=1, keepdims=True)
    g_hi = g.astype(BF16).astype(F32)
    g_lo = (g - g_hi).astype(BF16).astype(F32)
    lane = lax.broadcasted_iota(I32, (ls, LANES), 1)
    srt_ref[slot, :, D:] = jnp.where(lane == 0, g_hi, jnp.where(lane == 1, g_lo, 0.0)).astype(BF16)

    tabs = (src_tab, dst_tab, cnt_tab, tot_tab)

    def copier(buf_slot):
        return lambda lo, go, rows: pltpu.make_async_copy(
            srt_ref.at[buf_slot, pl.ds(lo, rows)], buf_hbm.at[pl.ds(go, rows)], sems.at[buf_slot])

    _start_pieces(tabs, t, n_cls, copier(slot))

    @pl.when(t > 0)
    def _():
        _await_pieces(tabs, t - 1, ls, copier(1 - slot))

    @pl.when(t == n_tiles - 1)
    def _():
        _await_pieces(tabs, t, ls, copier(slot))
        _start_pieces(gaps, 0, gap_cls, gap_copy, wait=True)


def _dispatch(f, pos, gates, tabs, n_cls, gap_tabs, gap_cls, n_rows, n_tiles, tm):
    D = f.shape[1]
    ls = 2 * tm + N_EXPERTS * RUN_ALIGN
    width = D + LANES
    grid_spec = pltpu.PrefetchScalarGridSpec(
        num_scalar_prefetch=7,
        grid=(n_tiles,),
        in_specs=[pl.BlockSpec((tm, D), lambda t, *_: (t, 0)), pl.BlockSpec((2, tm), lambda t, *_: (0, t)),
                  pl.BlockSpec((2, tm), lambda t, *_: (0, t))],
        out_specs=pl.BlockSpec(memory_space=pl.ANY),
        scratch_shapes=[pltpu.VMEM((2, ls, width), BF16), pltpu.VMEM((EXPERT_BLOCK, width), BF16),
                        pltpu.SemaphoreType.DMA((3,))],
    )
    return pl.pallas_call(
        functools.partial(_dispatch_kernel, n_tiles=n_tiles, n_cls=n_cls, gap_cls=gap_cls),
        grid_spec=grid_spec,
        out_shape=jax.ShapeDtypeStruct((n_rows, width), BF16),
        compiler_params=_cparams("arbitrary"),
        name="moe_dispatch",
    )(*tabs, *gap_tabs[:3], f, pos, gates)


def _expert_kernel(be_ref, bc_ref, bs_ref, x_ref, wg_ref, wu_ref, wd_ref, y_ref):
    i = pl.program_id(0)
    D = y_ref.shape[1]

    @pl.when(bc_ref[i] > 0)
    def _():
        x = x_ref[:, :D]
        gate = jnp.dot(x, wg_ref[0], preferred_element_type=F32)
        up = jnp.dot(x, wu_ref[0], preferred_element_type=F32)
        hid = (gate * jax.nn.sigmoid(gate) * up).astype(BF16)
        pieces = x_ref[:, D:].astype(F32)
        route_gate = pieces[:, 0:1] + pieces[:, 1:2]
        y_ref[...] = (jnp.dot(hid, wd_ref[0], preferred_element_type=F32) * route_gate).astype(BF16)

    @pl.when(bc_ref[i] == 0)
    def _():
        y_ref[...] = jnp.zeros_like(y_ref)


def _experts(buf, block_expert, block_count, block_src, wg, wu, wd):
    n_rows = buf.shape[0]
    D = wg.shape[1]
    nb = n_rows // EXPERT_BLOCK
    grid_spec = pltpu.PrefetchScalarGridSpec(
        num_scalar_prefetch=3,
        grid=(nb,),
        in_specs=[pl.BlockSpec((EXPERT_BLOCK, buf.shape[1]), lambda i, be, bc, bs: (bs[i], 0)),
                  pl.BlockSpec((1,) + wg.shape[1:], lambda i, be, bc, bs: (be[i], 0, 0)),
                  pl.BlockSpec((1,) + wu.shape[1:], lambda i, be, bc, bs: (be[i], 0, 0)),
                  pl.BlockSpec((1,) + wd.shape[1:], lambda i, be, bc, bs: (be[i], 0, 0))],
        out_specs=pl.BlockSpec((EXPERT_BLOCK, D), lambda i, be, bc, bs: (i, 0)),
    )
    return pl.pallas_call(
        _expert_kernel,
        grid_spec=grid_spec,
        out_shape=jax.ShapeDtypeStruct((n_rows, D), BF16),
        compiler_params=_cparams("arbitrary"),
        name="moe_experts",
    )(block_expert, block_count, block_src, buf, wg, wu, wd)


def _combine_kernel(src_tab, dst_tab, cnt_tab, tot_tab, y_hbm, x_ref, pos_ref, gate_ref, fin_ref, xo_ref,
                    srt_ref, sems, *, final, n_tiles, n_cls):
    t = pl.program_id(0)
    slot = t % 2
    tm = x_ref.shape[0]
    ls = srt_ref.shape[1]

    tabs = (src_tab, dst_tab, cnt_tab, tot_tab)

    def copier(buf_slot):
        return lambda lo, go, rows: pltpu.make_async_copy(
            y_hbm.at[pl.ds(go, rows)], srt_ref.at[buf_slot, pl.ds(lo, rows)], sems.at[buf_slot])

    @pl.when(t == 0)
    def _():
        srt_ref[...] = jnp.zeros_like(srt_ref)
        _start_pieces(tabs, t, n_cls, copier(slot))

    @pl.when(t + 1 < n_tiles)
    def _():
        _start_pieces(tabs, t + 1, n_cls, copier(1 - slot))

    _await_pieces(tabs, t, ls, copier(slot))
    j = lax.broadcasted_iota(I32, (tm, ls), 1)
    pick = jnp.where((pos_ref[:, 0:1] == j) | (pos_ref[:, 1:2] == j), 1.0, 0.0).astype(BF16)
    y = jnp.dot(pick, srt_ref[slot], preferred_element_type=F32)
    xn = x_ref[...] + gate_ref[0, 0] * y
    if final:
        xn = xn * lax.rsqrt(jnp.mean(xn * xn, axis=-1, keepdims=True) + RMS_EPS) * fin_ref[...]
    xo_ref[...] = xn


def _combine(y_buf, pos_t, tabs, n_cls, xs, mod, final_gain, n_tiles, tm, tiles_per_seq, final):
    N, D = xs.shape
    nmod = mod.shape[1]
    ls = 2 * tm + N_EXPERTS * RUN_ALIGN
    out_rows = n_tiles * tm if final else N
    kwargs = {} if final else {"input_output_aliases": {5: 0}}
    grid_spec = pltpu.PrefetchScalarGridSpec(
        num_scalar_prefetch=4,
        grid=(n_tiles,),
        in_specs=[pl.BlockSpec(memory_space=pl.ANY),
                  pl.BlockSpec((tm, D), lambda t, *_: (t, 0)),
                  pl.BlockSpec((tm, 2), lambda t, *_: (t, 0)),
                  pl.BlockSpec((1, 1, 1, D), lambda t, *_: (5, jnp.minimum(t // tiles_per_seq, nmod - 1), 0, 0)),
                  pl.BlockSpec((1, D), lambda t, *_: (0, 0))],
        out_specs=pl.BlockSpec((tm, D), lambda t, *_: (t, 0)),
        scratch_shapes=[pltpu.VMEM((2, ls, D), BF16), pltpu.SemaphoreType.DMA((2,))],
    )
    return pl.pallas_call(
        functools.partial(_combine_kernel, final=final, n_tiles=n_tiles, n_cls=n_cls),
        grid_spec=grid_spec,
        out_shape=jax.ShapeDtypeStruct((out_rows, D), F32),
        compiler_params=_cparams("arbitrary"),
        name="moe_combine_final" if final else "moe_combine",
        **kwargs,
    )(*tabs, y_buf, xs, pos_t, mod, final_gain.reshape(1, D))


def _moe(f, routing, xs, mod, wg, wu, wd, final_gain, n_tok, tm, tiles_per_seq, final):
    nt = n_tok // tm
    gates, pos, cnt = routing
    gates, pos, cnt = gates[:, :n_tok], pos[:, :n_tok], cnt[:nt]
    n = cnt[:, :, 0].astype(I32)
    rlen = (n + RUN_ALIGN - 1) // RUN_ALIGN * RUN_ALIGN
    lo = jnp.cumsum(rlen, axis=1) - rlen
    region = jnp.sum(rlen, axis=0)
    region_pad = (region + EXPERT_BLOCK - 1) // EXPERT_BLOCK * EXPERT_BLOCK
    pends = jnp.cumsum(region_pad)
    pstarts = pends - region_pad
    go = pstarts[None, :] + jnp.cumsum(rlen, axis=0) - rlen
    n_blocks = -(-(2 * n_tok + nt * N_EXPERTS * RUN_ALIGN) // EXPERT_BLOCK) + N_EXPERTS
    blk0 = jnp.arange(n_blocks, dtype=I32) * EXPERT_BLOCK
    block_expert = jnp.minimum(jnp.sum((blk0[:, None] >= pends[None, :]).astype(I32), axis=1), N_EXPERTS - 1)
    block_used = (blk0 < (pstarts + region)[block_expert]).astype(I32)
    tabs, n_cls = _chunk_tables(lo, go, rlen, tm)
    gap_tabs, gap_cls = _chunk_tables(jnp.zeros((1, N_EXPERTS), I32), (pstarts + region)[None, :],
                                      (region_pad - region)[None, :], EXPERT_BLOCK)
    buf = _dispatch(f, pos, gates, tabs, n_cls, gap_tabs, gap_cls, n_blocks * EXPERT_BLOCK, nt, tm)
    last_used = jnp.max(jnp.where(block_used > 0, jnp.arange(n_blocks, dtype=I32), 0))
    block_src = jnp.where(block_used > 0, jnp.arange(n_blocks, dtype=I32), last_used)
    y_buf = _experts(buf, block_expert.astype(I32), block_used, block_src, wg, wu, wd)
    return _combine(y_buf, pos.T, tabs, n_cls, xs, mod, final_gain, nt, tm, tiles_per_seq, final)


def _rope_tables(S, tm):
    rows = S // GRID_W
    row = jnp.repeat(jnp.arange(rows, dtype=F32), GRID_W)
    col = jnp.tile(jnp.arange(GRID_W, dtype=F32), rows)
    half = ROPE_PAIR
    inv_freq = ROPE_THETA ** (-jnp.arange(half, dtype=F32) / half)
    ang_r = row[:, None] * inv_freq[None, :]
    ang_c = col[:, None] * inv_freq[None, :]
    zeros = jnp.zeros_like(ang_r)
    cos = jnp.concatenate([jnp.cos(ang_r)] * 2 + [jnp.cos(ang_c)] * 2, axis=1)
    s1 = jnp.concatenate([-jnp.sin(ang_r), zeros, -jnp.sin(ang_c), zeros], axis=1)
    s2 = jnp.concatenate([zeros, jnp.sin(ang_r), zeros, jnp.sin(ang_c)], axis=1)
    def finish(tab, fill):
        tab = jnp.tile(tab, (1, LANES // ATTN_HEAD_DIM))
        return jnp.concatenate([tab, jnp.full((tm, LANES), fill, F32)], axis=0)
    return finish(cos, 1.0), finish(s1, 0.0), finish(s2, 0.0)


def kernel(x, c, ctx, c_ctx, ada_w, ada_b, norm_mix_g, norm_ffn_g, final_g, attn_w_qkv, attn_w_o, attn_sinks,
           gla_w_in, gla_w_a1, gla_w_a2, gla_b_a, gla_norm_g, gla_w_o, router_w, router_b,
           moe_w_gate, moe_w_up, moe_w_down):
    B, S, D = x.shape
    C = ctx.shape[1]
    depth = ada_w.shape[0]
    tm = TOKEN_TILE
    assert S % tm == 0 and (B * C) % tm == 0 and S % ATTN_BLOCK == 0 and C % ATTN_BLOCK == 0
    assert (B * S) % C == 0 and S % GLA_CHUNK == 0 and C % GLA_CHUNK == 0
    n_lat = B * S
    n_lat_tiles = n_lat // tm
    tiles_per_seq = S // tm

    rpad = -(-(B + 1) // 8) * 8
    cc = jnp.zeros((rpad, D), F32).at[:B].set(c).at[B].set(c_ctx)
    mods = _ada_table(cc, ada_w, ada_b)
    mods = mods[:, :B + 1].reshape(depth, B + 1, 6, 1, D).transpose(0, 2, 1, 3, 4)

    xs = (x.reshape(n_lat, D), ctx.reshape(B * C, D))
    ptm = PROJ_TILE if S % PROJ_TILE == 0 and (B * C) % PROJ_TILE == 0 else tm
    rope = _rope_tables(S, ptm)
    q_dim = ATTN_KV_HEADS * ATTN_GROUP * ATTN_HEAD_DIM
    kv_dim = ATTN_KV_HEADS * ATTN_HEAD_DIM
    kd = GLA_HEADS * GLA_KEY_DIM
    vd = GLA_HEADS * GLA_VAL_DIM

    def dup_heads(w):
        w = w.reshape(D, ATTN_KV_HEADS, 1, ATTN_HEAD_DIM)
        return jnp.broadcast_to(w, (D, ATTN_KV_HEADS, LANES // ATTN_HEAD_DIM, ATTN_HEAD_DIM)).reshape(D, -1)

    for i in range(depth):
        last = i == depth - 1
        mod = mods[i]
        j = i // 2
        if i % 2 == 0:
            wqkv = attn_w_qkv[j]
            w = jnp.concatenate([wqkv[:, :q_dim], dup_heads(wqkv[:, q_dim:q_dim + kv_dim])], axis=1).astype(BF16)
            wvt = wqkv[:, q_dim + kv_dim:].T.astype(BF16)
            q, k, vt = _qkv_proj(xs, norm_mix_g[i], mod, 0, w, wvt, rope, ptm, n_lat // ptm, S // ptm)
            sink_row = jnp.repeat(attn_sinks[j].astype(F32).reshape(ATTN_KV_HEADS, ATTN_GROUP) * LOG2_E,
                                  ATTN_BLOCK, axis=1)
            o = _attention(q, k, vt, sink_row[:, None, :], B, S, C, not last)
            xs, f, *routing = _post_mixer(o, attn_w_o[j].astype(BF16), xs, mod, norm_ffn_g[i], router_w, router_b,
                                          tm, n_lat_tiles, tiles_per_seq)
        else:
            a1 = jnp.zeros((D, LANES), F32).at[:, :2 * GLA_GATE_RANK].set(
                jnp.concatenate([gla_w_a1[j, 0], gla_w_a1[j, 1]], axis=1))
            w = jnp.concatenate([gla_w_in[j], a1], axis=1).astype(BF16)
            w2 = jnp.zeros((LANES, 2 * kd), F32)
            w2 = w2.at[:GLA_GATE_RANK, :kd].set(gla_w_a2[j, 0]).at[GLA_GATE_RANK:2 * GLA_GATE_RANK, kd:].set(gla_w_a2[j, 1])
            ba = gla_b_a[j].reshape(1, 2 * kd).astype(F32)
            qk, v, og, la, chunk_tot = _gla_in_proj(xs, norm_mix_g[i], mod, 0, w, w2.astype(BF16), ba, ptm,
                                                    n_lat // ptm, S // ptm)
            o_fwd, o_bwd = _gla_scan(qk, v, la, chunk_tot, B, S, C)
            xs, f, *routing = _post_mixer(o_fwd, gla_w_o[j].astype(BF16), xs, mod, norm_ffn_g[i], router_w, router_b,
                                          tm, n_lat_tiles, tiles_per_seq,
                                          gla_extra=(o_bwd, og, gla_norm_g[j].reshape(1, GLA_VAL_DIM).astype(F32)))
        wg, wu = moe_w_gate[i].astype(BF16), moe_w_up[i].astype(BF16)
        wd = moe_w_down[i].astype(BF16)
        n_tok = n_lat if last else n_lat + B * C
        xs = _moe(f, routing, xs, mod, wg, wu, wd, final_g, n_tok, tm, tiles_per_seq, last)
    return xs.reshape(B, S, D)
```

```python
import functools

import numpy as np
import jax
import jax.numpy as jnp
from jax import lax
from jax.experimental import pallas as pl
from jax.experimental.pallas import tpu as pltpu

F32 = jnp.float32
BF16 = jnp.bfloat16
I32 = jnp.int32

LANES = 128
VMEM_LIMIT_BYTES = 56 * 1024 * 1024

RMS_EPS = 1e-6
GRID_W = 64
ROPE_THETA = 10000.0
ATTN_HEAD_DIM = 64
ROPE_PAIR = ATTN_HEAD_DIM // 4
ATTN_KV_HEADS = 4
ATTN_GROUP = 4
ATTN_BLOCK = 128
ATTN_STEP_BLOCKS = 4
LOG2_E = 1.4426950408889634
ATTN_Q_SCALE = ATTN_HEAD_DIM ** -0.5 * LOG2_E
GLA_HEADS = 4
GLA_KEY_DIM = 128
GLA_VAL_DIM = 256
GLA_GATE_RANK = 16
GLA_GATE_NORM = 16.0
GLA_CHUNK = 128
GLA_STAT_ROWS = 64
GLA_BOUNDED_TOTAL = 40.0
N_EXPERTS = 16
N_GROUPS = 4
EXPERTS_PER_GROUP = 4
EXPERT_BLOCK = 1024
RUN_ALIGN = 16
TOKEN_TILE = 512
PROJ_TILE = 1024


def _cparams(*sem):
    return pltpu.CompilerParams(dimension_semantics=sem, vmem_limit_bytes=VMEM_LIMIT_BYTES)


def _norm_mod(x, gain, shift, scale):
    h = x * lax.rsqrt(jnp.mean(x * x, axis=-1, keepdims=True) + RMS_EPS) * gain
    return h * (1.0 + scale) + shift


def _ada_kernel(c_ref, w_ref, b_ref, o_ref):
    c = c_ref[...]
    s = (c * jax.nn.sigmoid(c)).astype(BF16)
    o_ref[0] = jnp.dot(s, w_ref[0].astype(BF16), preferred_element_type=F32) + b_ref[0]


def _ada_table(cc, ada_w, ada_b):
    L, D, D6 = ada_w.shape
    R = cc.shape[0]
    tn = 1536
    return pl.pallas_call(
        _ada_kernel,
        grid=(L, D6 // tn),
        in_specs=[pl.BlockSpec((R, D), lambda l, j: (0, 0)),
                  pl.BlockSpec((1, D, tn), lambda l, j: (l, 0, j)),
                  pl.BlockSpec((1, 1, tn), lambda l, j: (l, 0, j))],
        out_specs=pl.BlockSpec((1, R, tn), lambda l, j: (l, 0, j)),
        out_shape=jax.ShapeDtypeStruct((L, R, D6), F32),
        compiler_params=_cparams("parallel", "parallel"),
        name="ada_table",
    )(cc, ada_w, ada_b.reshape(L, 1, D6))


def _stream_tile(x_refs, n_lat_tiles):
    if len(x_refs) == 1:
        return x_refs[0][...]
    return jnp.where(pl.program_id(0) < n_lat_tiles, x_refs[0][...], x_refs[1][...])


def _stream_args(xs, tm, n_lat_tiles):
    if isinstance(xs, tuple):
        D = xs[0].shape[1]
        return list(xs), [pl.BlockSpec((tm, D), lambda t: (jnp.minimum(t, n_lat_tiles - 1), 0)),
                          pl.BlockSpec((tm, D), lambda t: (jnp.maximum(t - n_lat_tiles, 0), 0))]
    return [xs], [pl.BlockSpec((tm, xs.shape[1]), lambda t: (t, 0))]


def _qkv_kernel(*refs, n_x, n_lat_tiles):
    x_refs = refs[:n_x]
    (g_ref, sh_ref, sc_ref, w_ref, wqvt_ref, cos_ref, s1_ref, s2_ref, cost_ref, s1t_ref, s2t_ref,
     qt_ref, k_ref, vt_ref) = refs[n_x:]
    h = _norm_mod(_stream_tile(x_refs, n_lat_tiles), g_ref[...], sh_ref[0, 0], sc_ref[0, 0]).astype(BF16)
    z = jnp.dot(h, w_ref[...], preferred_element_type=F32)
    cos, s1, s2 = cos_ref[...], s1_ref[...], s2_ref[...]
    for j in range(k_ref.shape[1] // LANES):
        zc = z[:, j * LANES:(j + 1) * LANES]
        r = zc * cos + pltpu.roll(zc, LANES - ROPE_PAIR, 1) * s1 + pltpu.roll(zc, ROPE_PAIR, 1) * s2
        k_ref[:, j * LANES:(j + 1) * LANES] = r.astype(BF16)
    zt = lax.dot_general(wqvt_ref[...], h, (((1,), (1,)), ((), ())), preferred_element_type=F32)
    qd = qt_ref.shape[1]
    cost, s1t, s2t = cost_ref[...], s1t_ref[...], s2t_ref[...]
    n_slabs = qt_ref.shape[0]
    for j in range(qd // LANES):
        zc = zt[j * LANES:(j + 1) * LANES]
        below = jnp.concatenate([zc[ROPE_PAIR:], zc[:ROPE_PAIR]], axis=0)
        above = jnp.concatenate([zc[LANES - ROPE_PAIR:], zc[:LANES - ROPE_PAIR]], axis=0)
        r = ((zc * cost + below * s1t + above * s2t) * ATTN_Q_SCALE).astype(BF16)
        for c in range(n_slabs):
            qt_ref[c, j * LANES:(j + 1) * LANES, :] = r[:, c * ATTN_BLOCK:(c + 1) * ATTN_BLOCK]
    vt = zt[qd:].astype(BF16)
    for c in range(n_slabs):
        vt_ref[c] = vt[:, c * ATTN_BLOCK:(c + 1) * ATTN_BLOCK]


def _gla_in_kernel(x_ref, g_ref, sh_ref, sc_ref, w_ref, w2_ref, ba_ref, qk_ref, v_ref, og_ref, la_ref, tot_ref):
    h = _norm_mod(x_ref[...], g_ref[...], sh_ref[0, 0], sc_ref[0, 0]).astype(BF16)
    kd = GLA_HEADS * GLA_KEY_DIM
    vd = GLA_HEADS * GLA_VAL_DIM
    a1 = jnp.dot(h, w_ref[:, 2 * kd + 2 * vd:], preferred_element_type=F32).astype(BF16)
    pre = jnp.dot(a1, w2_ref[...], preferred_element_type=F32) + ba_ref[...]
    z = jnp.dot(h, w_ref[:, :2 * kd + 2 * vd], preferred_element_type=F32)
    la = (jnp.minimum(pre, 0.0) - jnp.log1p(jnp.exp(-jnp.abs(pre)))) * (1.0 / GLA_GATE_NORM)
    la_ref[...] = la
    qk_ref[:, :kd] = (z[:, :kd] * (GLA_KEY_DIM ** -0.5)).astype(BF16)
    qk_ref[:, kd:] = z[:, kd:2 * kd].astype(BF16)
    v_ref[...] = z[:, 2 * kd:2 * kd + vd].astype(BF16)
    og_ref[...] = z[:, 2 * kd + vd:2 * kd + 2 * vd].astype(BF16)
    nc = la.shape[0] // GLA_STAT_ROWS
    tot = jnp.sum(la.reshape(nc, GLA_STAT_ROWS, la.shape[1]), axis=1)
    lane = lax.broadcasted_iota(I32, (nc, LANES), 1)
    acc = jnp.zeros((nc, LANES), F32)
    for hd in range(2 * GLA_HEADS):
        worst = jnp.min(tot[:, hd * GLA_KEY_DIM:(hd + 1) * GLA_KEY_DIM], axis=1, keepdims=True)
        acc = jnp.where(lane == hd, worst, acc)
    tot_ref[...] = acc


def _tile_specs(D, tm, n_lat_tiles, tiles_per_seq, n_mod_rows):
    def mod_idx(which):
        return lambda t: (which, jnp.minimum(t // tiles_per_seq, n_mod_rows - 1), 0, 0)
    return mod_idx, [pl.BlockSpec((tm, D), lambda t: (t, 0)),
                     pl.BlockSpec((1, D), lambda t: (0, 0))]


def _qkv_proj(xs, gain, mod, which, w, wqvt, rope, tm, n_lat_tiles, tiles_per_seq):
    x_args, x_specs = _stream_args(xs, tm, n_lat_tiles)
    N = sum(a.shape[0] for a in x_args)
    D = x_args[0].shape[1]
    nmod = mod.shape[1]
    mod_idx, specs = _tile_specs(D, tm, n_lat_tiles, tiles_per_seq, nmod)
    qd = ATTN_KV_HEADS * ATTN_GROUP * ATTN_HEAD_DIM
    kd = ATTN_KV_HEADS * LANES
    vd = wqvt.shape[0] - qd
    rope_idx = lambda t: jnp.where(t < n_lat_tiles, t % tiles_per_seq, tiles_per_seq)
    rows = pl.BlockSpec((tm, LANES), lambda t: (rope_idx(t), 0))
    cols = pl.BlockSpec((LANES, tm), lambda t: (0, rope_idx(t)))
    in_specs = x_specs + specs[1:] + [
        pl.BlockSpec((1, 1, 1, D), mod_idx(which)),
        pl.BlockSpec((1, 1, 1, D), mod_idx(which + 1)),
        pl.BlockSpec(w.shape, lambda t: (0, 0)),
        pl.BlockSpec(wqvt.shape, lambda t: (0, 0)),
        rows, rows, rows, cols, cols, cols,
    ]
    slabs = lambda width: (pl.BlockSpec((tm // ATTN_BLOCK, width, ATTN_BLOCK), lambda t: (t, 0, 0)),
                           jax.ShapeDtypeStruct((N // ATTN_BLOCK, width, ATTN_BLOCK), BF16))
    outs = [slabs(qd), (pl.BlockSpec((tm, kd), lambda t: (t, 0)), jax.ShapeDtypeStruct((N, kd), BF16)), slabs(vd)]
    return pl.pallas_call(
        functools.partial(_qkv_kernel, n_x=len(x_args), n_lat_tiles=n_lat_tiles),
        grid=(N // tm,),
        in_specs=in_specs,
        out_specs=[o[0] for o in outs],
        out_shape=[o[1] for o in outs],
        compiler_params=_cparams("parallel"),
        name="attn_qkv_proj",
    )(*x_args, gain.reshape(1, D), mod, mod, w, wqvt, *rope, *[tab.T for tab in rope])


def _gla_in_proj(xs, gain, mod, which, w, w2, ba, tm, n_lat_tiles, tiles_per_seq):
    N, D = xs.shape
    nmod = mod.shape[1]
    mod_idx, specs = _tile_specs(D, tm, n_lat_tiles, tiles_per_seq, nmod)
    kd = GLA_HEADS * GLA_KEY_DIM
    vd = GLA_HEADS * GLA_VAL_DIM
    in_specs = specs + [
        pl.BlockSpec((1, 1, 1, D), mod_idx(which)),
        pl.BlockSpec((1, 1, 1, D), mod_idx(which + 1)),
        pl.BlockSpec(w.shape, lambda t: (0, 0)),
        pl.BlockSpec(w2.shape, lambda t: (0, 0)),
        pl.BlockSpec((1, 2 * kd), lambda t: (0, 0)),
    ]
    row = lambda width: pl.BlockSpec((tm, width), lambda t: (t, 0))
    return pl.pallas_call(
        _gla_in_kernel,
        grid=(N // tm,),
        in_specs=in_specs,
        out_specs=[row(2 * kd), row(vd), row(vd), row(2 * kd),
                   pl.BlockSpec((tm // GLA_STAT_ROWS, LANES), lambda t: (t, 0))],
        out_shape=[jax.ShapeDtypeStruct((N, 2 * kd), BF16),
                   jax.ShapeDtypeStruct((N, vd), BF16),
                   jax.ShapeDtypeStruct((N, vd), BF16),
                   jax.ShapeDtypeStruct((N, 2 * kd), F32),
                   jax.ShapeDtypeStruct((N // GLA_STAT_ROWS, LANES), F32)],
        compiler_params=_cparams("parallel"),
        name="gla_in_proj",
    )(xs, gain.reshape(1, D), mod, mod, w, w2, ba)


def _attn_kernel(*refs, window, nq, nb):
    tq = ATTN_BLOCK
    if window:
        q_ref, kp, km, kn, kx, vp, vm, vn, vx, sink_ref, tri_ref, o_ref = refs
        k_blocks = ([lambda ks: kp[:, ks]] + [lambda ks, i=i: km[i * tq:(i + 1) * tq, ks] for i in range(nb)]
                    + [lambda ks: kn[:, ks]])
        v_blocks = ([lambda hs: vp[0, hs, :]] + [lambda hs, i=i: vm[i, hs, :] for i in range(nb)]
                    + [lambda hs: vn[0, hs, :]])
    else:
        q_ref, kx, vx, sink_ref, _, o_ref = refs
    if window:
        bias = []
        for i in range(nb):
            j = pl.program_id(1) * nb + i
            bias_prev = jnp.where(j > 0, tri_ref[0], -jnp.inf)
            bias_next = jnp.where(j < nq - 1, tri_ref[1], -jnp.inf)
            bias.append((jnp.concatenate([bias_prev] * ATTN_GROUP, axis=1),
                         jnp.concatenate([bias_next] * ATTN_GROUP, axis=1)))

    def scores(i, kh):
        ks = slice(kh * LANES, (kh + 1) * LANES)
        kk = jnp.concatenate([blk(ks) for blk in k_blocks[i:i + 3]] + [kx[:, ks]], axis=0) if window else kx[:, ks]
        zero = jnp.zeros((LANES - ATTN_HEAD_DIM, tq), BF16)
        heads = [kh * ATTN_GROUP + g for g in range(ATTN_GROUP)]
        qs = jnp.concatenate(
            [jnp.concatenate([q_ref[i, h * ATTN_HEAD_DIM:(h + 1) * ATTN_HEAD_DIM, :], zero], axis=0) for h in heads],
            axis=1)
        return jnp.dot(kk, qs, preferred_element_type=F32)

    def softmax(s, i, kh):
        if window:
            s = jnp.concatenate([s[:tq] + bias[i][0], s[tq:2 * tq], s[2 * tq:3 * tq] + bias[i][1], s[3 * tq:]], axis=0)
        sink = sink_ref[kh]
        m = jnp.maximum(jnp.max(s, axis=0, keepdims=True), sink)
        p = jnp.exp2(s - m)
        l = jnp.sum(p, axis=0, keepdims=True) + jnp.exp2(sink - m)
        return p.astype(BF16), l

    def values(p, l, i, kh):
        hs = slice(kh * ATTN_HEAD_DIM, (kh + 1) * ATTN_HEAD_DIM)
        v_ctx = [vx[c, hs, :] for c in range(vx.shape[0])]
        vv = jnp.concatenate(([blk(hs) for blk in v_blocks[i:i + 3]] if window else []) + v_ctx, axis=1)
        o = jnp.dot(vv, p, preferred_element_type=F32) / l
        for g in range(ATTN_GROUP):
            h = kh * ATTN_GROUP + g
            o_ref[i, h * ATTN_HEAD_DIM:(h + 1) * ATTN_HEAD_DIM, :] = o[:, g * tq:(g + 1) * tq].astype(BF16)

    units = [(i, kh) for i in range(nb) for kh in range(ATTN_KV_HEADS)]
    s_next = scores(*units[0])
    pending = None
    for n, unit in enumerate(units):
        s_cur = s_next
        if n + 1 < len(units):
            s_next = scores(*units[n + 1])
        if pending is not None:
            values(*pending)
        pending = softmax(s_cur, *unit) + unit
    values(*pending)


def _attention(qt, k, vt, sink_row, B, S, C, need_ctx):
    N = k.shape[0]
    qd = qt.shape[1]
    kd = k.shape[1]
    vd = vt.shape[1]
    tq = ATTN_BLOCK
    nq = S // tq
    nb = ATTN_STEP_BLOCKS
    nbc = C // tq
    assert nq % nb == 0 and C == nbc * tq
    steps = nq // nb
    ctx_blk0 = (B * S) // C
    mid = lambda b, j: (b * steps + j, 0)
    prev = lambda b, j: (b * nq + jnp.maximum(j * nb - 1, 0), 0)
    nxt = lambda b, j: (b * nq + jnp.minimum(j * nb + nb, nq - 1), 0)
    cmap = lambda b, j: (ctx_blk0 + b, 0)
    kctx = pl.BlockSpec((C, kd), cmap)
    slab = lambda n, m, width=vd: pl.BlockSpec((n, width, tq), lambda b, j: (m(b, j)[0], 0, 0))
    vctx = slab(nbc, cmap)
    sink_spec = pl.BlockSpec(sink_row.shape, lambda b, j: (0, 0, 0))
    key = np.arange(tq)[:, None]
    qry = np.arange(tq)[None, :]
    tri = jnp.asarray(np.stack([np.where(key >= qry, 0.0, -np.inf), np.where(key <= qry, 0.0, -np.inf)]), F32)
    o_lat = pl.pallas_call(
        functools.partial(_attn_kernel, window=True, nq=nq, nb=nb),
        grid=(B, steps),
        in_specs=[slab(nb, mid, qd),
                  pl.BlockSpec((tq, kd), prev), pl.BlockSpec((nb * tq, kd), mid), pl.BlockSpec((tq, kd), nxt), kctx,
                  slab(1, prev), slab(nb, mid), slab(1, nxt), vctx, sink_spec,
                  pl.BlockSpec(tri.shape, lambda b, j: (0, 0, 0))],
        out_specs=pl.BlockSpec((nb, qd, tq), lambda b, j: (mid(b, j)[0], 0, 0)),
        out_shape=jax.ShapeDtypeStruct((N // tq, qd, tq), BF16),
        compiler_params=_cparams("parallel", "parallel"),
        name="attn_window",
    )(qt, k, k, k, k, vt, vt, vt, vt, sink_row, tri)
    if not need_ctx:
        return o_lat
    return pl.pallas_call(
        functools.partial(_attn_kernel, window=False, nq=nbc, nb=nbc),
        grid=(B, 1),
        in_specs=[slab(nbc, cmap, qd), kctx, vctx, sink_spec,
                  pl.BlockSpec(memory_space=pl.ANY)],
        out_specs=pl.BlockSpec((nbc, qd, tq), lambda b, j: (cmap(b, j)[0], 0, 0)),
        out_shape=jax.ShapeDtypeStruct((N // tq, qd, tq), BF16),
        input_output_aliases={4: 0},
        compiler_params=_cparams("parallel", "parallel"),
        name="attn_context",
    )(qt, k, vt, sink_row, o_lat)


def _gla_constants(C):
    levels = []
    m = 1
    while m < C:
        levels.append(m)
        m *= 2
    t = np.arange(C)[:, None]
    u = np.arange(C)[None, :]
    secs = [(u <= t), (u > t)]
    masks = []
    for m in levels:
        base = (t // (2 * m)) * (2 * m)
        ref = base + m - 1
        second = t >= base + m
        secs.append(np.where(second, (u > ref) & (u <= t), (u > t) & (u <= ref)))
        masks.append((t // (2 * m) == u // (2 * m)) & second & (u < (u // (2 * m)) * (2 * m) + m))
    masks.append(t == u)
    masks.append(u <= t)
    mf = np.concatenate([s.astype(np.float32) for s in secs], axis=0)
    kf = np.stack([mk.astype(np.float32) for mk in masks], axis=0)
    mb = np.concatenate([s.astype(np.float32)[::-1, ::-1] for s in secs], axis=0)
    kb = np.stack([mk.astype(np.float32)[::-1, ::-1] for mk in masks], axis=0)
    return np.stack([mf, mb]), np.stack([kf, kb]), len(levels)


def _gla_chunk(q, k, v, g, st_ref, mm_ref, mk_ref, d, nl, bounded):
    C, dk = q.shape
    nt = (((1,), (1,)), ((), ()))
    g_hi = g.astype(BF16)
    g_lo = (g - g_hi.astype(F32)).astype(BF16)
    mmat = mm_ref[d, :2 * C] if bounded else mm_ref[d]
    e2 = jnp.dot(mmat, jnp.concatenate([g_hi, g_lo], axis=1), preferred_element_type=F32)
    ee = e2[:, :dk] + e2[:, dk:]
    ex = jnp.exp(ee)
    qf, kf = q.astype(F32), k.astype(F32)
    qe = (qf * ex[:C]).astype(BF16)
    ke = (kf * ex[C:2 * C]).astype(BF16)
    st = st_ref[...]
    o = lax.dot_general(qe, st.astype(BF16), nt, preferred_element_type=F32)
    if bounded:
        ki = (kf * jnp.exp(-ee[:C])).astype(BF16)
        a = mk_ref[d, nl + 1] * lax.dot_general(qe, ki, nt, preferred_element_type=F32)
    else:
        a = mk_ref[d, nl] * lax.dot_general(q, k, nt, preferred_element_type=F32)
        for i in range(nl):
            xl = ex[(2 + i) * C:(3 + i) * C]
            ql = (qf * xl).astype(BF16)
            kl = (kf * xl).astype(BF16)
            a = a + mk_ref[d, i] * lax.dot_general(ql, kl, nt, preferred_element_type=F32)
    o = o + jnp.dot(a.astype(BF16), v, preferred_element_type=F32)
    decay = jnp.exp(jnp.sum(g, axis=0, keepdims=True))
    st_ref[...] = st * decay + lax.dot_general(v, ke, (((0,), (0,)), ((), ())), preferred_element_type=F32)
    return o


def _gla_kernel(okf_ref, okb_ref, qf, kf, vf, lf, qb, kb, vb, lb, mm_ref, mk_ref, of_ref, ob_ref, st_ref,
                *, nl, seg_chunks, lat_segs, ctx_seg0):
    C = GLA_CHUNK
    H, DK, DV = GLA_HEADS, GLA_KEY_DIM, GLA_VAL_DIM
    b, j = pl.program_id(0), pl.program_id(1)

    @pl.when(j == 0)
    def _():
        st_ref[...] = jnp.zeros_like(st_ref)

    seg_f = jnp.where(j == 0, ctx_seg0 + b, b * lat_segs + j - 1)
    seg_b = jnp.where(j == 0, ctx_seg0 + b, b * lat_segs + lat_segs - j)

    nt = (((1,), (1,)), ((), ()))
    kw = H * DK

    def rows_of(i):
        cf, cb = i, seg_chunks - 1 - i
        return slice(cf * C, (cf + 1) * C), slice(cb * C, (cb + 1) * C)

    def advance_general(i):
        rf, rb = rows_of(i)
        for h in range(H):
            ks, vs = slice(h * DK, (h + 1) * DK), slice(h * DV, (h + 1) * DV)
            o = _gla_chunk(qf[rf, ks], kf[rf, ks], vf[rf, vs], lf[rf, ks], st_ref.at[h],
                           mm_ref, mk_ref, 0, nl, False)
            of_ref[rf, vs] = o.astype(BF16)
            o = _gla_chunk(qb[rb, ks], kb[rb, ks], vb[rb, vs], lb[rb, ks], st_ref.at[H + h],
                           mm_ref, mk_ref, 1, nl, False)
            ob_ref[rb, vs] = o.astype(BF16)

    def bounded_front(i):
        rf, rb = rows_of(i)
        sides = ((qf, kf, vf, lf, rf, of_ref, 0), (qb, kb, vb, lb, rb, ob_ref, 1))
        pre = []
        for q_r, k_r, v_r, l_r, rows, o_r, d in sides:
            g = l_r[rows, :]
            g_hi = g.astype(BF16)
            g_lo = (g - g_hi.astype(F32)).astype(BF16)
            e2 = jnp.dot(mm_ref[d, :2 * C], jnp.concatenate([g_hi, g_lo], axis=1),
                         preferred_element_type=F32)
            pre.append((e2[:, :kw] + e2[:, kw:], g))
        units = []
        for (ee, g), (q_r, k_r, v_r, l_r, rows, o_r, d) in zip(pre, sides):
            ex = jnp.exp(ee)
            kf32 = k_r[rows, :].astype(F32)
            qe = (q_r[rows, :].astype(F32) * ex[:C]).astype(BF16)
            ke = (kf32 * ex[C:]).astype(BF16)
            ki = (kf32 * jnp.exp(-ee[:C])).astype(BF16)
            decay = jnp.exp(jnp.sum(g, axis=0, keepdims=True))
            for h in range(H):
                ks = slice(h * DK, (h + 1) * DK)
                units.append([d, h, qe[:, ks], ke[:, ks], ki[:, ks], decay[:, ks], v_r, rows, o_r])
        for unit in units:
            d, h, qe, ke, ki = unit[:5]
            score = lax.dot_general(qe, ki, nt, preferred_element_type=F32)
            unit.append((mk_ref[d, nl + 1] * score).astype(BF16))
        return units

    def bounded_back(units):
        inter = [lax.dot_general(u[2], st_ref[u[0] * H + u[1]].astype(BF16), nt, preferred_element_type=F32)
                 for u in units]
        for n, (d, h, qe, ke, ki, decay, v_r, rows, o_r, a) in enumerate(units):
            vs = slice(h * DV, (h + 1) * DV)
            v = v_r[rows, vs]
            o = inter[n] + jnp.dot(a, v, preferred_element_type=F32)
            upd = lax.dot_general(v, ke, (((0,), (0,)), ((), ())), preferred_element_type=F32)
            o_r[rows, vs] = o.astype(BF16)
            st_ref[d * H + h] = st_ref[d * H + h] * decay + upd

    bounded = None
    for i in range(seg_chunks):
        ok = (okf_ref[seg_f * seg_chunks + i] != 0) & (okb_ref[seg_b * seg_chunks + seg_chunks - 1 - i] != 0)
        bounded = ok if bounded is None else bounded & ok

    @pl.when(bounded)
    def _():
        fronts = [bounded_front(i) for i in range(seg_chunks)]
        for units in fronts:
            bounded_back(units)

    @pl.when(jnp.logical_not(bounded))
    def _():
        for i in range(seg_chunks):
            advance_general(i)


def _gla_scan(qk, v, la, chunk_tot, B, S, C):
    N = qk.shape[0]
    H, DK, DV = GLA_HEADS, GLA_KEY_DIM, GLA_VAL_DIM
    seg = C
    assert S % seg == 0 and seg % GLA_CHUNK == 0
    lat_segs = S // seg
    ctx_seg0 = (B * S) // seg
    mm, mk, nl = _gla_constants(GLA_CHUNK)
    mm = jnp.asarray(mm, BF16)
    mk = jnp.asarray(mk, F32)
    chunk_tot = jnp.sum(chunk_tot.reshape(-1, GLA_CHUNK // GLA_STAT_ROWS, LANES), axis=1)
    ok = chunk_tot[:, :2 * H] >= -GLA_BOUNDED_TOTAL
    okf = jnp.all(ok[:, :H], axis=1).astype(I32)
    okb = jnp.all(ok[:, H:], axis=1).astype(I32)
    fwd = lambda col: (lambda b, j, *_: (jnp.where(j == 0, ctx_seg0 + b, b * lat_segs + j - 1), col))
    bwd = lambda col: (lambda b, j, *_: (jnp.where(j == 0, ctx_seg0 + b, b * lat_segs + lat_segs - j), col))
    kw, vw = H * DK, H * DV
    in_specs = [
        pl.BlockSpec((seg, kw), fwd(0)), pl.BlockSpec((seg, kw), fwd(1)), pl.BlockSpec((seg, vw), fwd(0)),
        pl.BlockSpec((seg, kw), fwd(0)),
        pl.BlockSpec((seg, kw), bwd(0)), pl.BlockSpec((seg, kw), bwd(1)), pl.BlockSpec((seg, vw), bwd(0)),
        pl.BlockSpec((seg, kw), bwd(1)),
        pl.BlockSpec(mm.shape, lambda b, j, *_: (0, 0, 0)),
        pl.BlockSpec(mk.shape, lambda b, j, *_: (0, 0, 0, 0)),
    ]
    grid_spec = pltpu.PrefetchScalarGridSpec(
        num_scalar_prefetch=2,
        grid=(B, lat_segs + 1),
        in_specs=in_specs,
        out_specs=[pl.BlockSpec((seg, vw), fwd(0)), pl.BlockSpec((seg, vw), bwd(0))],
        scratch_shapes=[pltpu.VMEM((2 * H, DV, DK), F32)],
    )
    return pl.pallas_call(
        functools.partial(_gla_kernel, nl=nl, seg_chunks=seg // GLA_CHUNK, lat_segs=lat_segs, ctx_seg0=ctx_seg0),
        grid_spec=grid_spec,
        out_shape=[jax.ShapeDtypeStruct((N, vw), BF16), jax.ShapeDtypeStruct((N, vw), BF16)],
        compiler_params=_cparams("parallel", "arbitrary"),
        name="gla_scan",
    )(okf, okb, qk, qk, v, la, qk, qk, v, la, mm, mk)


def _post_kernel(*refs, gla, n_x, n_lat_tiles):
    x_refs, refs = refs[:n_x], refs[n_x:]
    route_refs = refs[-3:]
    route_in = refs[-8:-5]
    refs = refs[:-8] + refs[-5:-3]
    x_tile = _stream_tile(x_refs, n_lat_tiles)
    if gla:
        o_ref, ob_ref, og_ref, ng_ref, w_ref, gate_ref, fg_ref, fsh_ref, fsc_ref, xo_ref, f_ref = refs
        o = o_ref[...].astype(F32) + ob_ref[...].astype(F32)
        g = og_ref[...].astype(F32)
        parts = []
        for h in range(GLA_HEADS):
            oh = o[:, h * GLA_VAL_DIM:(h + 1) * GLA_VAL_DIM]
            parts.append(oh * lax.rsqrt(jnp.mean(oh * oh, axis=-1, keepdims=True) + RMS_EPS) * ng_ref[...])
        mix = (jnp.concatenate(parts, axis=1) * (g * jax.nn.sigmoid(g))).astype(BF16)
        y = jnp.dot(mix, w_ref[...], preferred_element_type=F32)
    else:
        o_ref, w_ref, gate_ref, fg_ref, fsh_ref, fsc_ref, xo_ref, f_ref = refs
        o_t = jnp.concatenate([o_ref[c] for c in range(o_ref.shape[0])], axis=1)
        y = lax.dot_general(o_t, w_ref[...], (((0,), (0,)), ((), ())), preferred_element_type=F32)
    xn = x_tile + gate_ref[0, 0] * y
    xo_ref[...] = xn
    f = _norm_mod(xn, fg_ref[...], fsh_ref[0, 0], fsc_ref[0, 0])
    f_ref[...] = f.astype(BF16)
    _route_tile(f, *route_in, *route_refs)


def _post_mixer(o, w_o, xs, mod, ffn_gain, router_w, router_b, tm, n_lat_tiles, tiles_per_seq, gla_extra=None):
    args, specs = _stream_args(xs, tm, n_lat_tiles)
    n_x = len(args)
    N = sum(a.shape[0] for a in args)
    D = args[0].shape[1]
    r_args, r_in, r_out, r_shapes = _route_io(router_w, router_b, N, D, tm)
    nmod = mod.shape[1]
    mod_idx = lambda which: (lambda t: (which, jnp.minimum(t // tiles_per_seq, nmod - 1), 0, 0))
    row = lambda width: pl.BlockSpec((tm, width), lambda t: (t, 0))
    const = lambda a: pl.BlockSpec(a.shape, lambda t: (0,) * a.ndim)
    if gla_extra is None:
        args, specs = args + [o], specs + [pl.BlockSpec((tm // o.shape[2],) + o.shape[1:], lambda t: (t, 0, 0))]
    else:
        o_bwd, og, ng = gla_extra
        args += [o, o_bwd, og, ng]
        specs += [row(o.shape[1]), row(o_bwd.shape[1]), row(og.shape[1]), const(ng)]
    args += [w_o, mod, ffn_gain.reshape(1, D), mod, mod]
    specs += [const(w_o), pl.BlockSpec((1, 1, 1, D), mod_idx(2)), pl.BlockSpec((1, D), lambda t: (0, 0)),
              pl.BlockSpec((1, 1, 1, D), mod_idx(3)), pl.BlockSpec((1, 1, 1, D), mod_idx(4))]
    args += r_args
    specs += r_in
    return pl.pallas_call(
        functools.partial(_post_kernel, gla=gla_extra is not None, n_x=n_x, n_lat_tiles=n_lat_tiles),
        grid=(N // tm,),
        in_specs=specs,
        out_specs=[row(D), row(D)] + r_out,
        out_shape=[jax.ShapeDtypeStruct((N, D), F32), jax.ShapeDtypeStruct((N, D), BF16)] + r_shapes,
        input_output_aliases={0: 0} if n_x == 1 else {},
        compiler_params=_cparams("parallel"),
        name="post_mixer_gla" if gla_extra is not None else "post_mixer_attn",
    )(*args)


def _route_tile(f, rw_ref, rb_ref, tri_ref, gate_ref, pos_ref, cnt_ref):
    tm = f.shape[0]
    fh = f.astype(BF16)
    fl = (f - fh.astype(F32)).astype(BF16)
    parts = jnp.dot(jnp.concatenate([fh, fl], axis=0), rw_ref[...], preferred_element_type=F32)
    logits = (parts[:tm, :LANES] + parts[:tm, LANES:]) + (parts[tm:, :LANES] + parts[tm:, LANES:])
    lt = logits.T[:N_EXPERTS]
    scores = jax.nn.sigmoid(lt)
    sel = scores + rb_ref[...]
    srow = [sel[e:e + 1] for e in range(N_EXPERTS)]
    prow = [scores[e:e + 1] for e in range(N_EXPERTS)]
    gscore = []
    for g in range(N_GROUPS):
        a, b, c, d = srow[4 * g:4 * g + 4]
        hi1, lo1, hi2, lo2 = jnp.maximum(a, b), jnp.minimum(a, b), jnp.maximum(c, d), jnp.minimum(c, d)
        gscore.append(jnp.maximum(hi1, hi2) + jnp.maximum(jnp.minimum(hi1, hi2), jnp.maximum(lo1, lo2)))
    best, grp = gscore[0], jnp.zeros_like(gscore[0], dtype=I32)
    for g in range(1, N_GROUPS):
        better = gscore[g] > best
        grp = jnp.where(better, g, grp)
        best = jnp.where(better, gscore[g], best)
    s_in, p_in = [], []
    for k in range(EXPERTS_PER_GROUP):
        sv, pv = srow[k], prow[k]
        for g in range(1, N_GROUPS):
            sv = jnp.where(grp == g, srow[4 * g + k], sv)
            pv = jnp.where(grp == g, prow[4 * g + k], pv)
        s_in.append(sv)
        p_in.append(pv)
    i1, v1, g1 = jnp.zeros_like(grp), s_in[0], p_in[0]
    for k in range(1, EXPERTS_PER_GROUP):
        better = s_in[k] > v1
        i1 = jnp.where(better, k, i1)
        g1 = jnp.where(better, p_in[k], g1)
        v1 = jnp.where(better, s_in[k], v1)
    i2, v2, g2 = jnp.zeros_like(grp), jnp.full_like(v1, -jnp.inf), jnp.zeros_like(v1)
    for k in range(EXPERTS_PER_GROUP):
        better = (i1 != k) & (s_in[k] > v2)
        i2 = jnp.where(better, k, i2)
        g2 = jnp.where(better, p_in[k], g2)
        v2 = jnp.where(better, s_in[k], v2)
    e1 = grp * EXPERTS_PER_GROUP + i1
    e2 = grp * EXPERTS_PER_GROUP + i2
    tot = g1 + g2
    gate_ref[0:1, :] = g1 / tot
    gate_ref[1:2, :] = g2 / tot
    eid = lax.broadcasted_iota(I32, scores.shape, 0)
    hot1 = eid == e1
    hot2 = eid == e2
    onehot = jnp.where(hot1 | hot2, 1.0, 0.0).astype(BF16)
    cum = jnp.dot(onehot, tri_ref[...], preferred_element_type=F32)
    count = cum[:, cum.shape[1] - 1:]
    cnt_ref[0] = jnp.broadcast_to(count, cnt_ref.shape[1:])
    run_len = jnp.floor((count + (RUN_ALIGN - 1)) * (1.0 / RUN_ALIGN)) * RUN_ALIGN
    ends = jnp.broadcast_to(run_len, (N_EXPERTS, LANES))
    row = lax.broadcasted_iota(I32, (N_EXPERTS, LANES), 0)
    step = 1
    while step < N_EXPERTS:
        ends = ends + jnp.where(row >= step, pltpu.roll(ends, step, 0), 0.0)
        step *= 2
    slot = cum + (ends[:, 0:1] - run_len - 1.0)
    pos_ref[0:1, :] = jnp.sum(jnp.where(hot1, slot, 0.0), axis=0, keepdims=True).astype(I32)
    pos_ref[1:2, :] = jnp.sum(jnp.where(hot2, slot, 0.0), axis=0, keepdims=True).astype(I32)


def _route_io(router_w, router_b, N, D, tm):
    rw = jnp.zeros((D, LANES), F32).at[:, :N_EXPERTS].set(router_w.astype(F32))
    rw_hi = rw.astype(BF16)
    rw = jnp.concatenate([rw_hi, (rw - rw_hi.astype(F32)).astype(BF16)], axis=1)
    rb = router_b.astype(F32).reshape(N_EXPERTS, 1)
    tri = jnp.asarray(np.triu(np.ones((tm, tm), np.float32)), BF16)
    in_specs = [pl.BlockSpec((D, 2 * LANES), lambda t: (0, 0)), pl.BlockSpec((N_EXPERTS, 1), lambda t: (0, 0)),
                pl.BlockSpec((tm, tm), lambda t: (0, 0))]
    lane_row = lambda dt: (pl.BlockSpec((2, tm), lambda t: (0, t)), jax.ShapeDtypeStruct((2, N), dt))
    outs = [lane_row(F32), lane_row(I32),
            (pl.BlockSpec((1, N_EXPERTS, LANES), lambda t: (t, 0, 0)),
             jax.ShapeDtypeStruct((N // tm, N_EXPERTS, LANES), F32))]
    return [rw, rb, tri], in_specs, [o[0] for o in outs], [o[1] for o in outs]


def _chunk_tables(lo, go, rlen, tm):
    n_cls = (tm // RUN_ALIGN).bit_length()
    units = rlen // RUN_ALIGN
    cls = jnp.arange(n_cls, dtype=I32)
    flag = (units[:, :, None] >> cls) & 1
    rows = flag * (RUN_ALIGN << cls)
    above = jnp.cumsum(rows[..., ::-1], axis=-1)[..., ::-1] - rows
    src = lo[:, :, None] + above
    dst = go[:, :, None] + above
    slot = jnp.cumsum(flag, axis=1) - 1
    hit = (flag[:, None] == 1) & (slot[:, None] == jnp.arange(N_EXPERTS, dtype=I32)[None, :, None, None])
    compact = lambda a: jnp.sum(jnp.where(hit, a[:, None], 0), axis=2).transpose(0, 2, 1)
    flat = lambda a: a.reshape(-1).astype(I32)
    return (flat(compact(src)), flat(compact(dst)), flat(jnp.sum(flag, axis=1)), flat(jnp.sum(units, axis=1))), n_cls


def _start_pieces(tabs, tile, n_cls, make_copy, wait=False):
    src_ref, dst_ref, cnt_ref = tabs[:3]
    for b in range(n_cls):
        base = (tile * n_cls + b) * N_EXPERTS

        def body(i, carry, base=base, rows=RUN_ALIGN << b):
            copy = make_copy(pl.multiple_of(src_ref[base + i], RUN_ALIGN),
                             pl.multiple_of(dst_ref[base + i], RUN_ALIGN), rows)
            copy.wait() if wait else copy.start()
            return carry

        lax.fori_loop(0, cnt_ref[tile * n_cls + b], body, 0)


def _await_pieces(tabs, tile, max_rows, make_copy):
    units = tabs[3][tile]
    for b in range((max_rows // RUN_ALIGN).bit_length()):
        @pl.when(((units >> b) & 1) != 0)
        def _(rows=RUN_ALIGN << b):
            make_copy(0, 0, rows).wait()


def _dispatch_kernel(src_tab, dst_tab, cnt_tab, tot_tab, gap_src, gap_dst, gap_cnt, f_ref, pos_ref, gates_ref,
                     buf_hbm, srt_ref, zero_ref, sems, *, n_tiles, n_cls, gap_cls):
    t = pl.program_id(0)
    gaps = (gap_src, gap_dst, gap_cnt)
    gap_copy = lambda lo, go, rows: pltpu.make_async_copy(
        zero_ref.at[pl.ds(lo, rows)], buf_hbm.at[pl.ds(go, rows)], sems.at[2])

    @pl.when(t == 0)
    def _():
        zero_ref[...] = jnp.zeros_like(zero_ref)
        _start_pieces(gaps, 0, gap_cls, gap_copy)
    slot = t % 2
    tm, D = f_ref.shape
    ls = srt_ref.shape[1]
    j = lax.broadcasted_iota(I32, (ls, tm), 0)
    hit0 = pos_ref[0:1, :] == j
    hit1 = pos_ref[1:2, :] == j
    perm = jnp.where(hit0 | hit1, 1.0, 0.0).astype(BF16)
    srt_ref[slot, :, :D] = jnp.dot(perm, f_ref[...], preferred_element_type=F32).astype(BF16)
    picked = jnp.where(hit0, gates_ref[0:1, :], jnp.where(hit1, gates_ref[1:2, :], 0.0))
    g = jnp.sum(picked, axis=1, keepdims=True)
    g_hi = g.astype(BF16).astype(F32)
    g_lo = (g - g_hi).astype(BF16).astype(F32)
    lane = lax.broadcasted_iota(I32, (ls, LANES), 1)
    srt_ref[slot, :, D:] = jnp.where(lane == 0, g_hi, jnp.where(lane == 1, g_lo, 0.0)).astype(BF16)

    tabs = (src_tab, dst_tab, cnt_tab, tot_tab)

    def copier(buf_slot):
        return lambda lo, go, rows: pltpu.make_async_copy(
            srt_ref.at[buf_slot, pl.ds(lo, rows)], buf_hbm.at[pl.ds(go, rows)], sems.at[buf_slot])

    _start_pieces(tabs, t, n_cls, copier(slot))

    @pl.when(t > 0)
    def _():
        _await_pieces(tabs, t - 1, ls, copier(1 - slot))

    @pl.when(t == n_tiles - 1)
    def _():
        _await_pieces(tabs, t, ls, copier(slot))
        _start_pieces(gaps, 0, gap_cls, gap_copy, wait=True)


def _dispatch(f, pos, gates, tabs, n_cls, gap_tabs, gap_cls, n_rows, n_tiles, tm):
    D = f.shape[1]
    ls = 2 * tm + N_EXPERTS * RUN_ALIGN
    width = D + LANES
    grid_spec = pltpu.PrefetchScalarGridSpec(
        num_scalar_prefetch=7,
        grid=(n_tiles,),
        in_specs=[pl.BlockSpec((tm, D), lambda t, *_: (t, 0)), pl.BlockSpec((2, tm), lambda t, *_: (0, t)),
                  pl.BlockSpec((2, tm), lambda t, *_: (0, t))],
        out_specs=pl.BlockSpec(memory_space=pl.ANY),
        scratch_shapes=[pltpu.VMEM((2, ls, width), BF16), pltpu.VMEM((EXPERT_BLOCK, width), BF16),
                        pltpu.SemaphoreType.DMA((3,))],
    )
    return pl.pallas_call(
        functools.partial(_dispatch_kernel, n_tiles=n_tiles, n_cls=n_cls, gap_cls=gap_cls),
        grid_spec=grid_spec,
        out_shape=jax.ShapeDtypeStruct((n_rows, width), BF16),
        compiler_params=_cparams("arbitrary"),
        name="moe_dispatch",
    )(*tabs, *gap_tabs[:3], f, pos, gates)


def _expert_kernel(be_ref, bc_ref, bs_ref, x_ref, wg_ref, wu_ref, wd_ref, y_ref):
    i = pl.program_id(0)
    D = y_ref.shape[1]

    @pl.when(bc_ref[i] > 0)
    def _():
        x = x_ref[:, :D]
        gate = jnp.dot(x, wg_ref[0], preferred_element_type=F32)
        up = jnp.dot(x, wu_ref[0], preferred_element_type=F32)
        hid = (gate * jax.nn.sigmoid(gate) * up).astype(BF16)
        pieces = x_ref[:, D:].astype(F32)
        route_gate = pieces[:, 0:1] + pieces[:, 1:2]
        y_ref[...] = (jnp.dot(hid, wd_ref[0], preferred_element_type=F32) * route_gate).astype(BF16)

    @pl.when(bc_ref[i] == 0)
    def _():
        y_ref[...] = jnp.zeros_like(y_ref)


def _experts(buf, block_expert, block_count, block_src, wg, wu, wd):
    n_rows = buf.shape[0]
    D = wg.shape[1]
    nb = n_rows // EXPERT_BLOCK
    grid_spec = pltpu.PrefetchScalarGridSpec(
        num_scalar_prefetch=3,
        grid=(nb,),
        in_specs=[pl.BlockSpec((EXPERT_BLOCK, buf.shape[1]), lambda i, be, bc, bs: (bs[i], 0)),
                  pl.BlockSpec((1,) + wg.shape[1:], lambda i, be, bc, bs: (be[i], 0, 0)),
                  pl.BlockSpec((1,) + wu.shape[1:], lambda i, be, bc, bs: (be[i], 0, 0)),
                  pl.BlockSpec((1,) + wd.shape[1:], lambda i, be, bc, bs: (be[i], 0, 0))],
        out_specs=pl.BlockSpec((EXPERT_BLOCK, D), lambda i, be, bc, bs: (i, 0)),
    )
    return pl.pallas_call(
        _expert_kernel,
        grid_spec=grid_spec,
        out_shape=jax.ShapeDtypeStruct((n_rows, D), BF16),
        compiler_params=_cparams("arbitrary"),
        name="moe_experts",
    )(block_expert, block_count, block_src, buf, wg, wu, wd)


def _combine_kernel(src_tab, dst_tab, cnt_tab, tot_tab, y_hbm, x_ref, pos_ref, gate_ref, fin_ref, xo_ref,
                    srt_ref, sems, *, final, n_tiles, n_cls):
    t = pl.program_id(0)
    slot = t % 2
    tm = x_ref.shape[0]
    ls = srt_ref.shape[1]

    tabs = (src_tab, dst_tab, cnt_tab, tot_tab)

    def copier(buf_slot):
        return lambda lo, go, rows: pltpu.make_async_copy(
            y_hbm.at[pl.ds(go, rows)], srt_ref.at[buf_slot, pl.ds(lo, rows)], sems.at[buf_slot])

    @pl.when(t == 0)
    def _():
        srt_ref[...] = jnp.zeros_like(srt_ref)
        _start_pieces(tabs, t, n_cls, copier(slot))

    @pl.when(t + 1 < n_tiles)
    def _():
        _start_pieces(tabs, t + 1, n_cls, copier(1 - slot))

    _await_pieces(tabs, t, ls, copier(slot))
    j = lax.broadcasted_iota(I32, (tm, ls), 1)
    pick = jnp.where((pos_ref[:, 0:1] == j) | (pos_ref[:, 1:2] == j), 1.0, 0.0).astype(BF16)
    y = jnp.dot(pick, srt_ref[slot], preferred_element_type=F32)
    xn = x_ref[...] + gate_ref[0, 0] * y
    if final:
        xn = xn * lax.rsqrt(jnp.mean(xn * xn, axis=-1, keepdims=True) + RMS_EPS) * fin_ref[...]
    xo_ref[...] = xn


def _combine(y_buf, pos_t, tabs, n_cls, xs, mod, final_gain, n_tiles, tm, tiles_per_seq, final):
    N, D = xs.shape
    nmod = mod.shape[1]
    ls = 2 * tm + N_EXPERTS * RUN_ALIGN
    out_rows = n_tiles * tm if final else N
    kwargs = {} if final else {"input_output_aliases": {5: 0}}
    grid_spec = pltpu.PrefetchScalarGridSpec(
        num_scalar_prefetch=4,
        grid=(n_tiles,),
        in_specs=[pl.BlockSpec(memory_space=pl.ANY),
                  pl.BlockSpec((tm, D), lambda t, *_: (t, 0)),
                  pl.BlockSpec((tm, 2), lambda t, *_: (t, 0)),
                  pl.BlockSpec((1, 1, 1, D), lambda t, *_: (5, jnp.minimum(t // tiles_per_seq, nmod - 1), 0, 0)),
                  pl.BlockSpec((1, D), lambda t, *_: (0, 0))],
        out_specs=pl.BlockSpec((tm, D), lambda t, *_: (t, 0)),
        scratch_shapes=[pltpu.VMEM((2, ls, D), BF16), pltpu.SemaphoreType.DMA((2,))],
    )
    return pl.pallas_call(
        functools.partial(_combine_kernel, final=final, n_tiles=n_tiles, n_cls=n_cls),
        grid_spec=grid_spec,
        out_shape=jax.ShapeDtypeStruct((out_rows, D), F32),
        compiler_params=_cparams("arbitrary"),
        name="moe_combine_final" if final else "moe_combine",
        **kwargs,
    )(*tabs, y_buf, xs, pos_t, mod, final_gain.reshape(1, D))


def _moe(f, routing, xs, mod, wg, wu, wd, final_gain, n_tok, tm, tiles_per_seq, final):
    nt = n_tok // tm
    gates, pos, cnt = routing
    gates, pos, cnt = gates[:, :n_tok], pos[:, :n_tok], cnt[:nt]
    n = cnt[:, :, 0].astype(I32)
    rlen = (n + RUN_ALIGN - 1) // RUN_ALIGN * RUN_ALIGN
    lo = jnp.cumsum(rlen, axis=1) - rlen
    region = jnp.sum(rlen, axis=0)
    region_pad = (region + EXPERT_BLOCK - 1) // EXPERT_BLOCK * EXPERT_BLOCK
    pends = jnp.cumsum(region_pad)
    pstarts = pends - region_pad
    go = pstarts[None, :] + jnp.cumsum(rlen, axis=0) - rlen
    n_blocks = -(-(2 * n_tok + nt * N_EXPERTS * RUN_ALIGN) // EXPERT_BLOCK) + N_EXPERTS
    blk0 = jnp.arange(n_blocks, dtype=I32) * EXPERT_BLOCK
    block_expert = jnp.minimum(jnp.sum((blk0[:, None] >= pends[None, :]).astype(I32), axis=1), N_EXPERTS - 1)
    block_used = (blk0 < (pstarts + region)[block_expert]).astype(I32)
    tabs, n_cls = _chunk_tables(lo, go, rlen, tm)
    gap_tabs, gap_cls = _chunk_tables(jnp.zeros((1, N_EXPERTS), I32), (pstarts + region)[None, :],
                                      (region_pad - region)[None, :], EXPERT_BLOCK)
    buf = _dispatch(f, pos, gates, tabs, n_cls, gap_tabs, gap_cls, n_blocks * EXPERT_BLOCK, nt, tm)
    last_used = jnp.max(jnp.where(block_used > 0, jnp.arange(n_blocks, dtype=I32), 0))
    block_src = jnp.where(block_used > 0, jnp.arange(n_blocks, dtype=I32), last_used)
    y_buf = _experts(buf, block_expert.astype(I32), block_used, block_src, wg, wu, wd)
    return _combine(y_buf, pos.T, tabs, n_cls, xs, mod, final_gain, nt, tm, tiles_per_seq, final)


def _rope_tables(S, tm):
    rows = S // GRID_W
    row = jnp.repeat(jnp.arange(rows, dtype=F32), GRID_W)
    col = jnp.tile(jnp.arange(GRID_W, dtype=F32), rows)
    half = ROPE_PAIR
    inv_freq = ROPE_THETA ** (-jnp.arange(half, dtype=F32) / half)
    ang_r = row[:, None] * inv_freq[None, :]
    ang_c = col[:, None] * inv_freq[None, :]
    zeros = jnp.zeros_like(ang_r)
    cos = jnp.concatenate([jnp.cos(ang_r)] * 2 + [jnp.cos(ang_c)] * 2, axis=1)
    s1 = jnp.concatenate([-jnp.sin(ang_r), zeros, -jnp.sin(ang_c), zeros], axis=1)
    s2 = jnp.concatenate([zeros, jnp.sin(ang_r), zeros, jnp.sin(ang_c)], axis=1)
    def finish(tab, fill):
        tab = jnp.tile(tab, (1, LANES // ATTN_HEAD_DIM))
        return jnp.concatenate([tab, jnp.full((tm, LANES), fill, F32)], axis=0)
    return finish(cos, 1.0), finish(s1, 0.0), finish(s2, 0.0)


def kernel(x, c, ctx, c_ctx, ada_w, ada_b, norm_mix_g, norm_ffn_g, final_g, attn_w_qkv, attn_w_o, attn_sinks,
           gla_w_in, gla_w_a1, gla_w_a2, gla_b_a, gla_norm_g, gla_w_o, router_w, router_b,
           moe_w_gate, moe_w_up, moe_w_down):
    B, S, D = x.shape
    C = ctx.shape[1]
    depth = ada_w.shape[0]
    tm = TOKEN_TILE
    assert S % tm == 0 and (B * C) % tm == 0 and S % ATTN_BLOCK == 0 and C % ATTN_BLOCK == 0
    assert (B * S) % C == 0 and S % GLA_CHUNK == 0 and C % GLA_CHUNK == 0
    n_lat = B * S
    n_lat_tiles = n_lat // tm
    tiles_per_seq = S // tm

    rpad = -(-(B + 1) // 8) * 8
    cc = jnp.zeros((rpad, D), F32).at[:B].set(c).at[B].set(c_ctx)
    mods = _ada_table(cc, ada_w, ada_b)
    mods = mods[:, :B + 1].reshape(depth, B + 1, 6, 1, D).transpose(0, 2, 1, 3, 4)

    xs = (x.reshape(n_lat, D), ctx.reshape(B * C, D))
    ptm = PROJ_TILE if S % PROJ_TILE == 0 and (B * C) % PROJ_TILE == 0 else tm
    rope = _rope_tables(S, ptm)
    q_dim = ATTN_KV_HEADS * ATTN_GROUP * ATTN_HEAD_DIM
    kv_dim = ATTN_KV_HEADS * ATTN_HEAD_DIM
    kd = GLA_HEADS * GLA_KEY_DIM
    vd = GLA_HEADS * GLA_VAL_DIM

    def dup_heads(w):
        w = w.reshape(D, ATTN_KV_HEADS, 1, ATTN_HEAD_DIM)
        return jnp.broadcast_to(w, (D, ATTN_KV_HEADS, LANES // ATTN_HEAD_DIM, ATTN_HEAD_DIM)).reshape(D, -1)

    for i in range(depth):
        last = i == depth - 1
        mod = mods[i]
        j = i // 2
        if i % 2 == 0:
            wqkv = attn_w_qkv[j]
            w = dup_heads(wqkv[:, q_dim:q_dim + kv_dim]).astype(BF16)
            wqvt = jnp.concatenate([wqkv[:, :q_dim], wqkv[:, q_dim + kv_dim:]], axis=1).T.astype(BF16)
            q, k, vt = _qkv_proj(xs, norm_mix_g[i], mod, 0, w, wqvt, rope, ptm, n_lat // ptm, S // ptm)
            sink_row = jnp.repeat(attn_sinks[j].astype(F32).reshape(ATTN_KV_HEADS, ATTN_GROUP) * LOG2_E,
                                  ATTN_BLOCK, axis=1)
            o = _attention(q, k, vt, sink_row[:, None, :], B, S, C, not last)
            xs, f, *routing = _post_mixer(o, attn_w_o[j].astype(BF16), xs, mod, norm_ffn_g[i], router_w, router_b,
                                          tm, n_lat_tiles, tiles_per_seq)
        else:
            a1 = jnp.zeros((D, LANES), F32).at[:, :2 * GLA_GATE_RANK].set(
                jnp.concatenate([gla_w_a1[j, 0], gla_w_a1[j, 1]], axis=1))
            w = jnp.concatenate([gla_w_in[j], a1], axis=1).astype(BF16)
            w2 = jnp.zeros((LANES, 2 * kd), F32)
            w2 = w2.at[:GLA_GATE_RANK, :kd].set(gla_w_a2[j, 0]).at[GLA_GATE_RANK:2 * GLA_GATE_RANK, kd:].set(gla_w_a2[j, 1])
            ba = gla_b_a[j].reshape(1, 2 * kd).astype(F32)
            qk, v, og, la, chunk_tot = _gla_in_proj(xs, norm_mix_g[i], mod, 0, w, w2.astype(BF16), ba, ptm,
                                                    n_lat // ptm, S // ptm)
            o_fwd, o_bwd = _gla_scan(qk, v, la, chunk_tot, B, S, C)
            xs, f, *routing = _post_mixer(o_fwd, gla_w_o[j].astype(BF16), xs, mod, norm_ffn_g[i], router_w, router_b,
                                          tm, n_lat_tiles, tiles_per_seq,
                                          gla_extra=(o_bwd, og, gla_norm_g[j].reshape(1, GLA_VAL_DIM).astype(F32)))
        wg, wu = moe_w_gate[i].astype(BF16), moe_w_up[i].astype(BF16)
        wd = moe_w_down[i].astype(BF16)
        n_tok = n_lat if last else n_lat + B * C
        xs = _moe(f, routing, xs, mod, wg, wu, wd, final_g, n_tok, tm, tiles_per_seq, last)
    return xs.reshape(B, S, D)
```

```python
import functools

import numpy as np
import jax
import jax.numpy as jnp
from jax import lax
from jax.experimental import pallas as pl
from jax.experimental.pallas import tpu as pltpu

F32 = jnp.float32
BF16 = jnp.bfloat16
I32 = jnp.int32

LANES = 128
VMEM_LIMIT_BYTES = 56 * 1024 * 1024

RMS_EPS = 1e-6
GRID_W = 64
ROPE_THETA = 10000.0
ATTN_HEAD_DIM = 64
ROPE_PAIR = ATTN_HEAD_DIM // 4
ATTN_KV_HEADS = 4
ATTN_GROUP = 4
ATTN_BLOCK = 128
ATTN_STEP_BLOCKS = 4
LOG2_E = 1.4426950408889634
ATTN_Q_SCALE = ATTN_HEAD_DIM ** -0.5 * LOG2_E
GLA_HEADS = 4
GLA_KEY_DIM = 128
GLA_VAL_DIM = 256
GLA_GATE_RANK = 16
GLA_GATE_NORM = 16.0
GLA_CHUNK = 128
GLA_STAT_ROWS = 64
GLA_BOUNDED_TOTAL = 40.0
N_EXPERTS = 16
N_GROUPS = 4
EXPERTS_PER_GROUP = 4
EXPERT_BLOCK = 1024
RUN_ALIGN = 16
TOKEN_TILE = 512
PROJ_TILE = 1024


def _cparams(*sem):
    return pltpu.CompilerParams(dimension_semantics=sem, vmem_limit_bytes=VMEM_LIMIT_BYTES)


def _norm_mod(x, gain, shift, scale):
    h = x * lax.rsqrt(jnp.mean(x * x, axis=-1, keepdims=True) + RMS_EPS) * gain
    return h * (1.0 + scale) + shift


def _ada_kernel(c_ref, w_ref, b_ref, o_ref):
    c = c_ref[...]
    s = (c * jax.nn.sigmoid(c)).astype(BF16)
    o_ref[0] = jnp.dot(s, w_ref[0].astype(BF16), preferred_element_type=F32) + b_ref[0]


def _ada_table(cc, ada_w, ada_b):
    L, D, D6 = ada_w.shape
    R = cc.shape[0]
    tn = 1536
    return pl.pallas_call(
        _ada_kernel,
        grid=(L, D6 // tn),
        in_specs=[pl.BlockSpec((R, D), lambda l, j: (0, 0)),
                  pl.BlockSpec((1, D, tn), lambda l, j: (l, 0, j)),
                  pl.BlockSpec((1, 1, tn), lambda l, j: (l, 0, j))],
        out_specs=pl.BlockSpec((1, R, tn), lambda l, j: (l, 0, j)),
        out_shape=jax.ShapeDtypeStruct((L, R, D6), F32),
        compiler_params=_cparams("parallel", "parallel"),
        name="ada_table",
    )(cc, ada_w, ada_b.reshape(L, 1, D6))


def _stream_tile(x_refs, n_lat_tiles):
    if len(x_refs) == 1:
        return x_refs[0][...]
    return jnp.where(pl.program_id(0) < n_lat_tiles, x_refs[0][...], x_refs[1][...])


def _stream_args(xs, tm, n_lat_tiles):
    if isinstance(xs, tuple):
        D = xs[0].shape[1]
        return list(xs), [pl.BlockSpec((tm, D), lambda t: (jnp.minimum(t, n_lat_tiles - 1), 0)),
                          pl.BlockSpec((tm, D), lambda t: (jnp.maximum(t - n_lat_tiles, 0), 0))]
    return [xs], [pl.BlockSpec((tm, xs.shape[1]), lambda t: (t, 0))]


def _qkv_kernel(*refs, n_x, n_lat_tiles):
    x_refs = refs[:n_x]
    (g_ref, sh_ref, sc_ref, w_ref, wqvt_ref, cos_ref, s1_ref, s2_ref, cost_ref, s1t_ref, s2t_ref,
     qt_ref, k_ref, vt_ref) = refs[n_x:]
    h = _norm_mod(_stream_tile(x_refs, n_lat_tiles), g_ref[...], sh_ref[0, 0], sc_ref[0, 0]).astype(BF16)
    z = jnp.dot(h, w_ref[...], preferred_element_type=F32)
    cos, s1, s2 = cos_ref[...], s1_ref[...], s2_ref[...]
    for j in range(k_ref.shape[1] // LANES):
        zc = z[:, j * LANES:(j + 1) * LANES]
        r = zc * cos + pltpu.roll(zc, LANES - ROPE_PAIR, 1) * s1 + pltpu.roll(zc, ROPE_PAIR, 1) * s2
        k_ref[:, j * LANES:(j + 1) * LANES] = r.astype(BF16)
    zt = lax.dot_general(wqvt_ref[...], h, (((1,), (1,)), ((), ())), preferred_element_type=F32)
    qd = qt_ref.shape[1]
    cost, s1t, s2t = cost_ref[...], s1t_ref[...], s2t_ref[...]
    n_slabs = qt_ref.shape[0]
    for j in range(qd // LANES):
        zc = zt[j * LANES:(j + 1) * LANES]
        below = jnp.concatenate([zc[ROPE_PAIR:], zc[:ROPE_PAIR]], axis=0)
        above = jnp.concatenate([zc[LANES - ROPE_PAIR:], zc[:LANES - ROPE_PAIR]], axis=0)
        r = ((zc * cost + below * s1t + above * s2t) * ATTN_Q_SCALE).astype(BF16)
        for c in range(n_slabs):
            qt_ref[c, j * LANES:(j + 1) * LANES, :] = r[:, c * ATTN_BLOCK:(c + 1) * ATTN_BLOCK]
    vt = zt[qd:].astype(BF16)
    for c in range(n_slabs):
        vt_ref[c] = vt[:, c * ATTN_BLOCK:(c + 1) * ATTN_BLOCK]


def _gla_in_kernel(x_ref, g_ref, sh_ref, sc_ref, w_ref, w2_ref, ba_ref, qk_ref, v_ref, og_ref, la_ref, tot_ref):
    h = _norm_mod(x_ref[...], g_ref[...], sh_ref[0, 0], sc_ref[0, 0]).astype(BF16)
    kd = GLA_HEADS * GLA_KEY_DIM
    vd = GLA_HEADS * GLA_VAL_DIM
    a1 = jnp.dot(h, w_ref[:, 2 * kd + 2 * vd:], preferred_element_type=F32).astype(BF16)
    pre = jnp.dot(a1, w2_ref[...], preferred_element_type=F32) + ba_ref[...]
    z = jnp.dot(h, w_ref[:, :2 * kd + 2 * vd], preferred_element_type=F32)
    la = (jnp.minimum(pre, 0.0) - jnp.log1p(jnp.exp(-jnp.abs(pre)))) * (1.0 / GLA_GATE_NORM)
    la_ref[...] = la
    qk_ref[:, :kd] = (z[:, :kd] * (GLA_KEY_DIM ** -0.5)).astype(BF16)
    qk_ref[:, kd:] = z[:, kd:2 * kd].astype(BF16)
    v_ref[...] = z[:, 2 * kd:2 * kd + vd].astype(BF16)
    og_ref[...] = z[:, 2 * kd + vd:2 * kd + 2 * vd].astype(BF16)
    nc = la.shape[0] // GLA_STAT_ROWS
    tot = jnp.sum(la.reshape(nc, GLA_STAT_ROWS, la.shape[1]), axis=1)
    lane = lax.broadcasted_iota(I32, (nc, LANES), 1)
    acc = jnp.zeros((nc, LANES), F32)
    for hd in range(2 * GLA_HEADS):
        worst = jnp.min(tot[:, hd * GLA_KEY_DIM:(hd + 1) * GLA_KEY_DIM], axis=1, keepdims=True)
        acc = jnp.where(lane == hd, worst, acc)
    tot_ref[...] = acc


def _tile_specs(D, tm, n_lat_tiles, tiles_per_seq, n_mod_rows):
    def mod_idx(which):
        return lambda t: (which, jnp.minimum(t // tiles_per_seq, n_mod_rows - 1), 0, 0)
    return mod_idx, [pl.BlockSpec((tm, D), lambda t: (t, 0)),
                     pl.BlockSpec((1, D), lambda t: (0, 0))]


def _qkv_proj(xs, gain, mod, which, w, wqvt, rope, tm, n_lat_tiles, tiles_per_seq):
    x_args, x_specs = _stream_args(xs, tm, n_lat_tiles)
    N = sum(a.shape[0] for a in x_args)
    D = x_args[0].shape[1]
    nmod = mod.shape[1]
    mod_idx, specs = _tile_specs(D, tm, n_lat_tiles, tiles_per_seq, nmod)
    qd = ATTN_KV_HEADS * ATTN_GROUP * ATTN_HEAD_DIM
    kd = ATTN_KV_HEADS * LANES
    vd = wqvt.shape[0] - qd
    rope_idx = lambda t: jnp.where(t < n_lat_tiles, t % tiles_per_seq, tiles_per_seq)
    rows = pl.BlockSpec((tm, LANES), lambda t: (rope_idx(t), 0))
    cols = pl.BlockSpec((LANES, tm), lambda t: (0, rope_idx(t)))
    in_specs = x_specs + specs[1:] + [
        pl.BlockSpec((1, 1, 1, D), mod_idx(which)),
        pl.BlockSpec((1, 1, 1, D), mod_idx(which + 1)),
        pl.BlockSpec(w.shape, lambda t: (0, 0)),
        pl.BlockSpec(wqvt.shape, lambda t: (0, 0)),
        rows, rows, rows, cols, cols, cols,
    ]
    slabs = lambda width: (pl.BlockSpec((tm // ATTN_BLOCK, width, ATTN_BLOCK), lambda t: (t, 0, 0)),
                           jax.ShapeDtypeStruct((N // ATTN_BLOCK, width, ATTN_BLOCK), BF16))
    outs = [slabs(qd), (pl.BlockSpec((tm, kd), lambda t: (t, 0)), jax.ShapeDtypeStruct((N, kd), BF16)), slabs(vd)]
    return pl.pallas_call(
        functools.partial(_qkv_kernel, n_x=len(x_args), n_lat_tiles=n_lat_tiles),
        grid=(N // tm,),
        in_specs=in_specs,
        out_specs=[o[0] for o in outs],
        out_shape=[o[1] for o in outs],
        compiler_params=_cparams("parallel"),
        name="attn_qkv_proj",
    )(*x_args, gain.reshape(1, D), mod, mod, w, wqvt, *rope, *[tab.T for tab in rope])


def _gla_in_proj(xs, gain, mod, which, w, w2, ba, tm, n_lat_tiles, tiles_per_seq):
    N, D = xs.shape
    nmod = mod.shape[1]
    mod_idx, specs = _tile_specs(D, tm, n_lat_tiles, tiles_per_seq, nmod)
    kd = GLA_HEADS * GLA_KEY_DIM
    vd = GLA_HEADS * GLA_VAL_DIM
    in_specs = specs + [
        pl.BlockSpec((1, 1, 1, D), mod_idx(which)),
        pl.BlockSpec((1, 1, 1, D), mod_idx(which + 1)),
        pl.BlockSpec(w.shape, lambda t: (0, 0)),
        pl.BlockSpec(w2.shape, lambda t: (0, 0)),
        pl.BlockSpec((1, 2 * kd), lambda t: (0, 0)),
    ]
    row = lambda width: pl.BlockSpec((tm, width), lambda t: (t, 0))
    return pl.pallas_call(
        _gla_in_kernel,
        grid=(N // tm,),
        in_specs=in_specs,
        out_specs=[row(2 * kd), row(vd), row(vd), row(2 * kd),
                   pl.BlockSpec((tm // GLA_STAT_ROWS, LANES), lambda t: (t, 0))],
        out_shape=[jax.ShapeDtypeStruct((N, 2 * kd), BF16),
                   jax.ShapeDtypeStruct((N, vd), BF16),
                   jax.ShapeDtypeStruct((N, vd), BF16),
                   jax.ShapeDtypeStruct((N, 2 * kd), F32),
                   jax.ShapeDtypeStruct((N // GLA_STAT_ROWS, LANES), F32)],
        compiler_params=_cparams("parallel"),
        name="gla_in_proj",
    )(xs, gain.reshape(1, D), mod, mod, w, w2, ba)


def _attn_kernel(*refs, window, nq, nb):
    tq = ATTN_BLOCK
    if window:
        q_ref, kp, km, kn, kx, vp, vm, vn, vx, sink_ref, tri_ref, o_ref = refs
        k_blocks = ([lambda ks: kp[:, ks]] + [lambda ks, i=i: km[i * tq:(i + 1) * tq, ks] for i in range(nb)]
                    + [lambda ks: kn[:, ks]])
        v_blocks = ([lambda hs: vp[0, hs, :]] + [lambda hs, i=i: vm[i, hs, :] for i in range(nb)]
                    + [lambda hs: vn[0, hs, :]])
    else:
        q_ref, kx, vx, sink_ref, _, o_ref = refs
    if window:
        bias = []
        for i in range(nb):
            j = pl.program_id(1) * nb + i
            bias_prev = jnp.where(j > 0, tri_ref[0], -jnp.inf)
            bias_next = jnp.where(j < nq - 1, tri_ref[1], -jnp.inf)
            bias.append((jnp.concatenate([bias_prev] * ATTN_GROUP, axis=1),
                         jnp.concatenate([bias_next] * ATTN_GROUP, axis=1)))

    def scores(i, kh):
        ks = slice(kh * LANES, (kh + 1) * LANES)
        kk = jnp.concatenate([blk(ks) for blk in k_blocks[i:i + 3]] + [kx[:, ks]], axis=0) if window else kx[:, ks]
        zero = jnp.zeros((LANES - ATTN_HEAD_DIM, tq), BF16)
        heads = [kh * ATTN_GROUP + g for g in range(ATTN_GROUP)]
        qs = jnp.concatenate(
            [jnp.concatenate([q_ref[i, h * ATTN_HEAD_DIM:(h + 1) * ATTN_HEAD_DIM, :], zero], axis=0) for h in heads],
            axis=1)
        return jnp.dot(kk, qs, preferred_element_type=F32)

    def softmax(s, i, kh):
        if window:
            s = jnp.concatenate([s[:tq] + bias[i][0], s[tq:2 * tq], s[2 * tq:3 * tq] + bias[i][1], s[3 * tq:]], axis=0)
        sink = sink_ref[kh]
        m = jnp.maximum(jnp.max(s, axis=0, keepdims=True), sink)
        p = jnp.exp2(s - m)
        l = jnp.sum(p, axis=0, keepdims=True) + jnp.exp2(sink - m)
        return p.astype(BF16), l

    def values(p, l, i, kh):
        hs = slice(kh * ATTN_HEAD_DIM, (kh + 1) * ATTN_HEAD_DIM)
        v_ctx = [vx[c, hs, :] for c in range(vx.shape[0])]
        vv = jnp.concatenate(([blk(hs) for blk in v_blocks[i:i + 3]] if window else []) + v_ctx, axis=1)
        o = jnp.dot(vv, p, preferred_element_type=F32) / l
        for g in range(ATTN_GROUP):
            h = kh * ATTN_GROUP + g
            o_ref[i, h * ATTN_HEAD_DIM:(h + 1) * ATTN_HEAD_DIM, :] = o[:, g * tq:(g + 1) * tq].astype(BF16)

    units = [(i, kh) for i in range(nb) for kh in range(ATTN_KV_HEADS)]
    s_next = scores(*units[0])
    pending = None
    for n, unit in enumerate(units):
        s_cur = s_next
        if n + 1 < len(units):
            s_next = scores(*units[n + 1])
        if pending is not None:
            values(*pending)
        pending = softmax(s_cur, *unit) + unit
    values(*pending)


def _attention(qt, k, vt, sink_row, B, S, C, need_ctx):
    N = k.shape[0]
    qd = qt.shape[1]
    kd = k.shape[1]
    vd = vt.shape[1]
    tq = ATTN_BLOCK
    nq = S // tq
    nb = ATTN_STEP_BLOCKS
    nbc = C // tq
    assert nq % nb == 0 and C == nbc * tq
    steps = nq // nb
    ctx_blk0 = (B * S) // C
    mid = lambda b, j: (b * steps + j, 0)
    prev = lambda b, j: (b * nq + jnp.maximum(j * nb - 1, 0), 0)
    nxt = lambda b, j: (b * nq + jnp.minimum(j * nb + nb, nq - 1), 0)
    cmap = lambda b, j: (ctx_blk0 + b, 0)
    kctx = pl.BlockSpec((C, kd), cmap)
    slab = lambda n, m, width=vd: pl.BlockSpec((n, width, tq), lambda b, j: (m(b, j)[0], 0, 0))
    vctx = slab(nbc, cmap)
    sink_spec = pl.BlockSpec(sink_row.shape, lambda b, j: (0, 0, 0))
    key = np.arange(tq)[:, None]
    qry = np.arange(tq)[None, :]
    tri = jnp.asarray(np.stack([np.where(key >= qry, 0.0, -np.inf), np.where(key <= qry, 0.0, -np.inf)]), F32)
    o_lat = pl.pallas_call(
        functools.partial(_attn_kernel, window=True, nq=nq, nb=nb),
        grid=(B, steps),
        in_specs=[slab(nb, mid, qd),
                  pl.BlockSpec((tq, kd), prev), pl.BlockSpec((nb * tq, kd), mid), pl.BlockSpec((tq, kd), nxt), kctx,
                  slab(1, prev), slab(nb, mid), slab(1, nxt), vctx, sink_spec,
                  pl.BlockSpec(tri.shape, lambda b, j: (0, 0, 0))],
        out_specs=pl.BlockSpec((nb, qd, tq), lambda b, j: (mid(b, j)[0], 0, 0)),
        out_shape=jax.ShapeDtypeStruct((N // tq, qd, tq), BF16),
        compiler_params=_cparams("parallel", "parallel"),
        name="attn_window",
    )(qt, k, k, k, k, vt, vt, vt, vt, sink_row, tri)
    if not need_ctx:
        return o_lat
    return pl.pallas_call(
        functools.partial(_attn_kernel, window=False, nq=nbc, nb=nbc),
        grid=(B, 1),
        in_specs=[slab(nbc, cmap, qd), kctx, vctx, sink_spec,
                  pl.BlockSpec(memory_space=pl.ANY)],
        out_specs=pl.BlockSpec((nbc, qd, tq), lambda b, j: (cmap(b, j)[0], 0, 0)),
        out_shape=jax.ShapeDtypeStruct((N // tq, qd, tq), BF16),
        input_output_aliases={4: 0},
        compiler_params=_cparams("parallel", "parallel"),
        name="attn_context",
    )(qt, k, vt, sink_row, o_lat)


def _gla_constants(C):
    levels = []
    m = 1
    while m < C:
        levels.append(m)
        m *= 2
    t = np.arange(C)[:, None]
    u = np.arange(C)[None, :]
    secs = [(u <= t), (u > t)]
    masks = []
    for m in levels:
        base = (t // (2 * m)) * (2 * m)
        ref = base + m - 1
        second = t >= base + m
        secs.append(np.where(second, (u > ref) & (u <= t), (u > t) & (u <= ref)))
        masks.append((t // (2 * m) == u // (2 * m)) & second & (u < (u // (2 * m)) * (2 * m) + m))
    masks.append(t == u)
    masks.append(u <= t)
    mf = np.concatenate([s.astype(np.float32) for s in secs], axis=0)
    kf = np.stack([mk.astype(np.float32) for mk in masks], axis=0)
    mb = np.concatenate([s.astype(np.float32)[::-1, ::-1] for s in secs], axis=0)
    kb = np.stack([mk.astype(np.float32)[::-1, ::-1] for mk in masks], axis=0)
    return np.stack([mf, mb]), np.stack([kf, kb]), len(levels)


def _gla_chunk(q, k, v, g, st_ref, mm_ref, mk_ref, d, nl, bounded):
    C, dk = q.shape
    nt = (((1,), (1,)), ((), ()))
    g_hi = g.astype(BF16)
    g_lo = (g - g_hi.astype(F32)).astype(BF16)
    mmat = mm_ref[d, :2 * C] if bounded else mm_ref[d]
    e2 = jnp.dot(mmat, jnp.concatenate([g_hi, g_lo], axis=1), preferred_element_type=F32)
    ee = e2[:, :dk] + e2[:, dk:]
    ex = jnp.exp(ee)
    qf, kf = q.astype(F32), k.astype(F32)
    qe = (qf * ex[:C]).astype(BF16)
    ke = (kf * ex[C:2 * C]).astype(BF16)
    st = st_ref[...]
    o = lax.dot_general(qe, st.astype(BF16), nt, preferred_element_type=F32)
    if bounded:
        ki = (kf * jnp.exp(-ee[:C])).astype(BF16)
        a = mk_ref[d, nl + 1] * lax.dot_general(qe, ki, nt, preferred_element_type=F32)
    else:
        a = mk_ref[d, nl] * lax.dot_general(q, k, nt, preferred_element_type=F32)
        for i in range(nl):
            xl = ex[(2 + i) * C:(3 + i) * C]
            ql = (qf * xl).astype(BF16)
            kl = (kf * xl).astype(BF16)
            a = a + mk_ref[d, i] * lax.dot_general(ql, kl, nt, preferred_element_type=F32)
    o = o + jnp.dot(a.astype(BF16), v, preferred_element_type=F32)
    decay = jnp.exp(jnp.sum(g, axis=0, keepdims=True))
    st_ref[...] = st * decay + lax.dot_general(v, ke, (((0,), (0,)), ((), ())), preferred_element_type=F32)
    return o


def _gla_kernel(okf_ref, okb_ref, qf, kf, vf, lf, qb, kb, vb, lb, mm_ref, mk_ref, of_ref, ob_ref, st_ref,
                *, nl, seg_chunks, lat_segs, ctx_seg0):
    C = GLA_CHUNK
    H, DK, DV = GLA_HEADS, GLA_KEY_DIM, GLA_VAL_DIM
    b, j = pl.program_id(0), pl.program_id(1)

    @pl.when(j == 0)
    def _():
        st_ref[...] = jnp.zeros_like(st_ref)

    seg_f = jnp.where(j == 0, ctx_seg0 + b, b * lat_segs + j - 1)
    seg_b = jnp.where(j == 0, ctx_seg0 + b, b * lat_segs + lat_segs - j)

    nt = (((1,), (1,)), ((), ()))
    kw = H * DK

    def rows_of(i):
        cf, cb = i, seg_chunks - 1 - i
        return slice(cf * C, (cf + 1) * C), slice(cb * C, (cb + 1) * C)

    def advance_general(i):
        rf, rb = rows_of(i)
        for h in range(H):
            ks, vs = slice(h * DK, (h + 1) * DK), slice(h * DV, (h + 1) * DV)
            o = _gla_chunk(qf[rf, ks], kf[rf, ks], vf[rf, vs], lf[rf, ks], st_ref.at[h],
                           mm_ref, mk_ref, 0, nl, False)
            of_ref[rf, vs] = o.astype(BF16)
            o = _gla_chunk(qb[rb, ks], kb[rb, ks], vb[rb, vs], lb[rb, ks], st_ref.at[H + h],
                           mm_ref, mk_ref, 1, nl, False)
            ob_ref[rb, vs] = o.astype(BF16)

    def bounded_front(i):
        rf, rb = rows_of(i)
        sides = ((qf, kf, vf, lf, rf, of_ref, 0), (qb, kb, vb, lb, rb, ob_ref, 1))
        pre = []
        for q_r, k_r, v_r, l_r, rows, o_r, d in sides:
            g = l_r[rows, :]
            g_hi = g.astype(BF16)
            g_lo = (g - g_hi.astype(F32)).astype(BF16)
            e2 = jnp.dot(mm_ref[d, :2 * C], jnp.concatenate([g_hi, g_lo], axis=1),
                         preferred_element_type=F32)
            pre.append((e2[:, :kw] + e2[:, kw:], g))
        units = []
        for (ee, g), (q_r, k_r, v_r, l_r, rows, o_r, d) in zip(pre, sides):
            ex = jnp.exp(ee)
            kf32 = k_r[rows, :].astype(F32)
            qe = (q_r[rows, :].astype(F32) * ex[:C]).astype(BF16)
            ke = (kf32 * ex[C:]).astype(BF16)
            ki = (kf32 * jnp.exp(-ee[:C])).astype(BF16)
            decay = jnp.exp(jnp.sum(g, axis=0, keepdims=True))
            for h in range(H):
                ks = slice(h * DK, (h + 1) * DK)
                units.append([d, h, qe[:, ks], ke[:, ks], ki[:, ks], decay[:, ks], v_r, rows, o_r])
        for unit in units:
            d, h, qe, ke, ki = unit[:5]
            score = lax.dot_general(qe, ki, nt, preferred_element_type=F32)
            unit.append((mk_ref[d, nl + 1] * score).astype(BF16))
        return units

    def bounded_back(units):
        inter = [lax.dot_general(u[2], st_ref[u[0] * H + u[1]].astype(BF16), nt, preferred_element_type=F32)
                 for u in units]
        for n, (d, h, qe, ke, ki, decay, v_r, rows, o_r, a) in enumerate(units):
            vs = slice(h * DV, (h + 1) * DV)
            v = v_r[rows, vs]
            o = inter[n] + jnp.dot(a, v, preferred_element_type=F32)
            upd = lax.dot_general(v, ke, (((0,), (0,)), ((), ())), preferred_element_type=F32)
            o_r[rows, vs] = o.astype(BF16)
            st_ref[d * H + h] = st_ref[d * H + h] * decay + upd

    bounded = None
    for i in range(seg_chunks):
        ok = (okf_ref[seg_f * seg_chunks + i] != 0) & (okb_ref[seg_b * seg_chunks + seg_chunks - 1 - i] != 0)
        bounded = ok if bounded is None else bounded & ok

    @pl.when(bounded)
    def _():
        fronts = [bounded_front(i) for i in range(seg_chunks)]
        for units in fronts:
            bounded_back(units)

    @pl.when(jnp.logical_not(bounded))
    def _():
        for i in range(seg_chunks):
            advance_general(i)


def _gla_scan(qk, v, la, chunk_tot, B, S, C):
    N = qk.shape[0]
    H, DK, DV = GLA_HEADS, GLA_KEY_DIM, GLA_VAL_DIM
    seg = C
    assert S % seg == 0 and seg % GLA_CHUNK == 0
    lat_segs = S // seg
    ctx_seg0 = (B * S) // seg
    mm, mk, nl = _gla_constants(GLA_CHUNK)
    mm = jnp.asarray(mm, BF16)
    mk = jnp.asarray(mk, F32)
    chunk_tot = jnp.sum(chunk_tot.reshape(-1, GLA_CHUNK // GLA_STAT_ROWS, LANES), axis=1)
    ok = chunk_tot[:, :2 * H] >= -GLA_BOUNDED_TOTAL
    okf = jnp.all(ok[:, :H], axis=1).astype(I32)
    okb = jnp.all(ok[:, H:], axis=1).astype(I32)
    fwd = lambda col: (lambda b, j, *_: (jnp.where(j == 0, ctx_seg0 + b, b * lat_segs + j - 1), col))
    bwd = lambda col: (lambda b, j, *_: (jnp.where(j == 0, ctx_seg0 + b, b * lat_segs + lat_segs - j), col))
    kw, vw = H * DK, H * DV
    in_specs = [
        pl.BlockSpec((seg, kw), fwd(0)), pl.BlockSpec((seg, kw), fwd(1)), pl.BlockSpec((seg, vw), fwd(0)),
        pl.BlockSpec((seg, kw), fwd(0)),
        pl.BlockSpec((seg, kw), bwd(0)), pl.BlockSpec((seg, kw), bwd(1)), pl.BlockSpec((seg, vw), bwd(0)),
        pl.BlockSpec((seg, kw), bwd(1)),
        pl.BlockSpec(mm.shape, lambda b, j, *_: (0, 0, 0)),
        pl.BlockSpec(mk.shape, lambda b, j, *_: (0, 0, 0, 0)),
    ]
    grid_spec = pltpu.PrefetchScalarGridSpec(
        num_scalar_prefetch=2,
        grid=(B, lat_segs + 1),
        in_specs=in_specs,
        out_specs=[pl.BlockSpec((seg, vw), fwd(0)), pl.BlockSpec((seg, vw), bwd(0))],
        scratch_shapes=[pltpu.VMEM((2 * H, DV, DK), F32)],
    )
    return pl.pallas_call(
        functools.partial(_gla_kernel, nl=nl, seg_chunks=seg // GLA_CHUNK, lat_segs=lat_segs, ctx_seg0=ctx_seg0),
        grid_spec=grid_spec,
        out_shape=[jax.ShapeDtypeStruct((N, vw), BF16), jax.ShapeDtypeStruct((N, vw), BF16)],
        compiler_params=_cparams("parallel", "arbitrary"),
        name="gla_scan",
    )(okf, okb, qk, qk, v, la, qk, qk, v, la, mm, mk)


def _post_kernel(*refs, gla, n_x, n_lat_tiles):
    x_refs, refs = refs[:n_x], refs[n_x:]
    route_refs = refs[-3:]
    route_in = refs[-8:-5]
    refs = refs[:-8] + refs[-5:-3]
    x_tile = _stream_tile(x_refs, n_lat_tiles)
    if gla:
        o_ref, ob_ref, og_ref, ng_ref, w_ref, gate_ref, fg_ref, fsh_ref, fsc_ref, xo_ref, f_ref = refs
        o = o_ref[...].astype(F32) + ob_ref[...].astype(F32)
        g = og_ref[...].astype(F32)
        parts = []
        for h in range(GLA_HEADS):
            oh = o[:, h * GLA_VAL_DIM:(h + 1) * GLA_VAL_DIM]
            parts.append(oh * lax.rsqrt(jnp.mean(oh * oh, axis=-1, keepdims=True) + RMS_EPS) * ng_ref[...])
        mix = (jnp.concatenate(parts, axis=1) * (g * jax.nn.sigmoid(g))).astype(BF16)
        y = jnp.dot(mix, w_ref[...], preferred_element_type=F32)
    else:
        o_ref, w_ref, gate_ref, fg_ref, fsh_ref, fsc_ref, xo_ref, f_ref = refs
        o_t = jnp.concatenate([o_ref[c] for c in range(o_ref.shape[0])], axis=1)
        y = lax.dot_general(o_t, w_ref[...], (((0,), (0,)), ((), ())), preferred_element_type=F32)
    xn = x_tile + gate_ref[0, 0] * y
    xo_ref[...] = xn
    f = _norm_mod(xn, fg_ref[...], fsh_ref[0, 0], fsc_ref[0, 0])
    f_ref[...] = f.astype(BF16)
    _route_tile(f, *route_in, *route_refs)


def _post_mixer(o, w_o, xs, mod, ffn_gain, router_w, router_b, tm, n_lat_tiles, tiles_per_seq, gla_extra=None):
    args, specs = _stream_args(xs, tm, n_lat_tiles)
    n_x = len(args)
    N = sum(a.shape[0] for a in args)
    D = args[0].shape[1]
    r_args, r_in, r_out, r_shapes = _route_io(router_w, router_b, N, D, tm)
    nmod = mod.shape[1]
    mod_idx = lambda which: (lambda t: (which, jnp.minimum(t // tiles_per_seq, nmod - 1), 0, 0))
    row = lambda width: pl.BlockSpec((tm, width), lambda t: (t, 0))
    const = lambda a: pl.BlockSpec(a.shape, lambda t: (0,) * a.ndim)
    if gla_extra is None:
        args, specs = args + [o], specs + [pl.BlockSpec((tm // o.shape[2],) + o.shape[1:], lambda t: (t, 0, 0))]
    else:
        o_bwd, og, ng = gla_extra
        args += [o, o_bwd, og, ng]
        specs += [row(o.shape[1]), row(o_bwd.shape[1]), row(og.shape[1]), const(ng)]
    args += [w_o, mod, ffn_gain.reshape(1, D), mod, mod]
    specs += [const(w_o), pl.BlockSpec((1, 1, 1, D), mod_idx(2)), pl.BlockSpec((1, D), lambda t: (0, 0)),
              pl.BlockSpec((1, 1, 1, D), mod_idx(3)), pl.BlockSpec((1, 1, 1, D), mod_idx(4))]
    args += r_args
    specs += r_in
    return pl.pallas_call(
        functools.partial(_post_kernel, gla=gla_extra is not None, n_x=n_x, n_lat_tiles=n_lat_tiles),
        grid=(N // tm,),
        in_specs=specs,
        out_specs=[row(D), row(D)] + r_out,
        out_shape=[jax.ShapeDtypeStruct((N, D), F32), jax.ShapeDtypeStruct((N, D), BF16)] + r_shapes,
        input_output_aliases={0: 0} if n_x == 1 else {},
        compiler_params=_cparams("parallel"),
        name="post_mixer_gla" if gla_extra is not None else "post_mixer_attn",
    )(*args)


def _route_tile(f, rw_ref, rb_ref, tri_ref, gate_ref, pos_ref, cnt_ref):
    tm = f.shape[0]
    fh = f.astype(BF16)
    fl = (f - fh.astype(F32)).astype(BF16)
    parts = jnp.dot(jnp.concatenate([fh, fl], axis=0), rw_ref[...], preferred_element_type=F32)
    logits = (parts[:tm, :LANES] + parts[:tm, LANES:]) + (parts[tm:, :LANES] + parts[tm:, LANES:])
    lt = logits.T[:N_EXPERTS]
    scores = jax.nn.sigmoid(lt)
    sel = scores + rb_ref[...]
    srow = [sel[e:e + 1] for e in range(N_EXPERTS)]
    prow = [scores[e:e + 1] for e in range(N_EXPERTS)]
    gscore = []
    for g in range(N_GROUPS):
        a, b, c, d = srow[4 * g:4 * g + 4]
        hi1, lo1, hi2, lo2 = jnp.maximum(a, b), jnp.minimum(a, b), jnp.maximum(c, d), jnp.minimum(c, d)
        gscore.append(jnp.maximum(hi1, hi2) + jnp.maximum(jnp.minimum(hi1, hi2), jnp.maximum(lo1, lo2)))
    best, grp = gscore[0], jnp.zeros_like(gscore[0], dtype=I32)
    for g in range(1, N_GROUPS):
        better = gscore[g] > best
        grp = jnp.where(better, g, grp)
        best = jnp.where(better, gscore[g], best)
    s_in, p_in = [], []
    for k in range(EXPERTS_PER_GROUP):
        sv, pv = srow[k], prow[k]
        for g in range(1, N_GROUPS):
            sv = jnp.where(grp == g, srow[4 * g + k], sv)
            pv = jnp.where(grp == g, prow[4 * g + k], pv)
        s_in.append(sv)
        p_in.append(pv)
    i1, v1, g1 = jnp.zeros_like(grp), s_in[0], p_in[0]
    for k in range(1, EXPERTS_PER_GROUP):
        better = s_in[k] > v1
        i1 = jnp.where(better, k, i1)
        g1 = jnp.where(better, p_in[k], g1)
        v1 = jnp.where(better, s_in[k], v1)
    i2, v2, g2 = jnp.zeros_like(grp), jnp.full_like(v1, -jnp.inf), jnp.zeros_like(v1)
    for k in range(EXPERTS_PER_GROUP):
        better = (i1 != k) & (s_in[k] > v2)
        i2 = jnp.where(better, k, i2)
        g2 = jnp.where(better, p_in[k], g2)
        v2 = jnp.where(better, s_in[k], v2)
    e1 = grp * EXPERTS_PER_GROUP + i1
    e2 = grp * EXPERTS_PER_GROUP + i2
    tot = g1 + g2
    gate_ref[0:1, :] = g1 / tot
    gate_ref[1:2, :] = g2 / tot
    eid = lax.broadcasted_iota(I32, scores.shape, 0)
    hot1 = eid == e1
    hot2 = eid == e2
    onehot = jnp.where(hot1 | hot2, 1.0, 0.0).astype(BF16)
    cum = jnp.dot(onehot, tri_ref[...], preferred_element_type=F32)
    count = cum[:, cum.shape[1] - 1:]
    cnt_ref[0] = jnp.broadcast_to(count, cnt_ref.shape[1:])
    run_len = jnp.floor((count + (RUN_ALIGN - 1)) * (1.0 / RUN_ALIGN)) * RUN_ALIGN
    ends = jnp.broadcast_to(run_len, (N_EXPERTS, LANES))
    row = lax.broadcasted_iota(I32, (N_EXPERTS, LANES), 0)
    step = 1
    while step < N_EXPERTS:
        ends = ends + jnp.where(row >= step, pltpu.roll(ends, step, 0), 0.0)
        step *= 2
    slot = cum + (ends[:, 0:1] - run_len - 1.0)
    pos_ref[0:1, :] = jnp.sum(jnp.where(hot1, slot, 0.0), axis=0, keepdims=True).astype(I32)
    pos_ref[1:2, :] = jnp.sum(jnp.where(hot2, slot, 0.0), axis=0, keepdims=True).astype(I32)


def _route_io(router_w, router_b, N, D, tm):
    rw = jnp.zeros((D, LANES), F32).at[:, :N_EXPERTS].set(router_w.astype(F32))
    rw_hi = rw.astype(BF16)
    rw = jnp.concatenate([rw_hi, (rw - rw_hi.astype(F32)).astype(BF16)], axis=1)
    rb = router_b.astype(F32).reshape(N_EXPERTS, 1)
    tri = jnp.asarray(np.triu(np.ones((tm, tm), np.float32)), BF16)
    in_specs = [pl.BlockSpec((D, 2 * LANES), lambda t: (0, 0)), pl.BlockSpec((N_EXPERTS, 1), lambda t: (0, 0)),
                pl.BlockSpec((tm, tm), lambda t: (0, 0))]
    lane_row = lambda dt: (pl.BlockSpec((2, tm), lambda t: (0, t)), jax.ShapeDtypeStruct((2, N), dt))
    outs = [lane_row(F32), lane_row(I32),
            (pl.BlockSpec((1, N_EXPERTS, LANES), lambda t: (t, 0, 0)),
             jax.ShapeDtypeStruct((N // tm, N_EXPERTS, LANES), F32))]
    return [rw, rb, tri], in_specs, [o[0] for o in outs], [o[1] for o in outs]


def _chunk_tables(lo, go, rlen, tm):
    n_cls = (tm // RUN_ALIGN).bit_length()
    units = rlen // RUN_ALIGN
    cls = jnp.arange(n_cls, dtype=I32)
    flag = (units[:, :, None] >> cls) & 1
    rows = flag * (RUN_ALIGN << cls)
    above = jnp.cumsum(rows[..., ::-1], axis=-1)[..., ::-1] - rows
    src = lo[:, :, None] + above
    dst = go[:, :, None] + above
    slot = jnp.cumsum(flag, axis=1) - 1
    hit = (flag[:, None] == 1) & (slot[:, None] == jnp.arange(N_EXPERTS, dtype=I32)[None, :, None, None])
    compact = lambda a: jnp.sum(jnp.where(hit, a[:, None], 0), axis=2).transpose(0, 2, 1)
    flat = lambda a: a.reshape(-1).astype(I32)
    return (flat(compact(src)), flat(compact(dst)), flat(jnp.sum(flag, axis=1)), flat(jnp.sum(units, axis=1))), n_cls


def _start_pieces(tabs, tile, n_cls, make_copy, wait=False):
    src_ref, dst_ref, cnt_ref = tabs[:3]
    for b in range(n_cls):
        base = (tile * n_cls + b) * N_EXPERTS

        def body(i, carry, base=base, rows=RUN_ALIGN << b):
            copy = make_copy(pl.multiple_of(src_ref[base + i], RUN_ALIGN),
                             pl.multiple_of(dst_ref[base + i], RUN_ALIGN), rows)
            copy.wait() if wait else copy.start()
            return carry

        lax.fori_loop(0, cnt_ref[tile * n_cls + b], body, 0)


def _await_pieces(tabs, tile, max_rows, make_copy):
    units = tabs[3][tile]
    for b in range((max_rows // RUN_ALIGN).bit_length()):
        @pl.when(((units >> b) & 1) != 0)
        def _(rows=RUN_ALIGN << b):
            make_copy(0, 0, rows).wait()


def _dispatch_kernel(src_tab, dst_tab, cnt_tab, tot_tab, gap_src, gap_dst, gap_cnt, f_ref, pos_ref, gates_ref,
                     buf_hbm, srt_ref, zero_ref, sems, *, n_tiles, n_cls, gap_cls):
    t = pl.program_id(0)
    gaps = (gap_src, gap_dst, gap_cnt)
    gap_copy = lambda lo, go, rows: pltpu.make_async_copy(
        zero_ref.at[pl.ds(lo, rows)], buf_hbm.at[pl.ds(go, rows)], sems.at[2])

    @pl.when(t == 0)
    def _():
        zero_ref[...] = jnp.zeros_like(zero_ref)
        _start_pieces(gaps, 0, gap_cls, gap_copy)
    slot = t % 2
    tm, D = f_ref.shape
    ls = srt_ref.shape[1]
    j = lax.broadcasted_iota(I32, (ls, tm), 0)
    hit0 = pos_ref[0:1, :] == j
    hit1 = pos_ref[1:2, :] == j
    perm = jnp.where(hit0 | hit1, 1.0, 0.0).astype(BF16)
    srt_ref[slot, :, :D] = jnp.dot(perm, f_ref[...], preferred_element_type=F32).astype(BF16)
    picked = jnp.where(hit0, gates_ref[0:1, :], jnp.where(hit1, gates_ref[1:2, :], 0.0))
    g = jnp.sum(picked, axis=1, keepdims=True)
    g_hi = g.astype(BF16).astype(F32)
    g_lo = (g - g_hi).astype(BF16).astype(F32)
    lane = lax.broadcasted_iota(I32, (ls, LANES), 1)
    srt_ref[slot, :, D:] = jnp.where(lane == 0, g_hi, jnp.where(lane == 1, g_lo, 0.0)).astype(BF16)

    tabs = (src_tab, dst_tab, cnt_tab, tot_tab)

    def copier(buf_slot):
        return lambda lo, go, rows: pltpu.make_async_copy(
            srt_ref.at[buf_slot, pl.ds(lo, rows)], buf_hbm.at[pl.ds(go, rows)], sems.at[buf_slot])

    _start_pieces(tabs, t, n_cls, copier(slot))

    @pl.when(t > 0)
    def _():
        _await_pieces(tabs, t - 1, ls, copier(1 - slot))

    @pl.when(t == n_tiles - 1)
    def _():
        _await_pieces(tabs, t, ls, copier(slot))
        _start_pieces(gaps, 0, gap_cls, gap_copy, wait=True)


def _dispatch(f, pos, gates, tabs, n_cls, gap_tabs, gap_cls, n_rows, n_tiles, tm):
    D = f.shape[1]
    ls = 2 * tm + N_EXPERTS * RUN_ALIGN
    width = D + LANES
    grid_spec = pltpu.PrefetchScalarGridSpec(
        num_scalar_prefetch=7,
        grid=(n_tiles,),
        in_specs=[pl.BlockSpec((tm, D), lambda t, *_: (t, 0)), pl.BlockSpec((2, tm), lambda t, *_: (0, t)),
                  pl.BlockSpec((2, tm), lambda t, *_: (0, t))],
        out_specs=pl.BlockSpec(memory_space=pl.ANY),
        scratch_shapes=[pltpu.VMEM((2, ls, width), BF16), pltpu.VMEM((EXPERT_BLOCK, width), BF16),
                        pltpu.SemaphoreType.DMA((3,))],
    )
    return pl.pallas_call(
        functools.partial(_dispatch_kernel, n_tiles=n_tiles, n_cls=n_cls, gap_cls=gap_cls),
        grid_spec=grid_spec,
        out_shape=jax.ShapeDtypeStruct((n_rows, width), BF16),
        compiler_params=_cparams("arbitrary"),
        name="moe_dispatch",
    )(*tabs, *gap_tabs[:3], f, pos, gates)


def _expert_kernel(be_ref, bc_ref, bs_ref, x_ref, wg_ref, wu_ref, wd_ref, y_ref, wg_bf, wu_bf, wd_bf):
    i = pl.program_id(0)
    D = y_ref.shape[1]

    @pl.when((i == 0) | (be_ref[i] != be_ref[jnp.maximum(i - 1, 0)]))
    def _():
        wg_bf[...] = wg_ref[0, 0].astype(BF16)
        wu_bf[...] = wu_ref[0, 0].astype(BF16)
        wd_bf[...] = wd_ref[0, 0].astype(BF16)

    @pl.when(bc_ref[i] > 0)
    def _():
        x = x_ref[:, :D]
        gate = jnp.dot(x, wg_bf[...], preferred_element_type=F32)
        up = jnp.dot(x, wu_bf[...], preferred_element_type=F32)
        hid = (gate * jax.nn.sigmoid(gate) * up).astype(BF16)
        pieces = x_ref[:, D:].astype(F32)
        route_gate = pieces[:, 0:1] + pieces[:, 1:2]
        y_ref[...] = (jnp.dot(hid, wd_bf[...], preferred_element_type=F32) * route_gate).astype(BF16)

    @pl.when(bc_ref[i] == 0)
    def _():
        y_ref[...] = jnp.zeros_like(y_ref)


def _experts(buf, block_expert, block_count, block_src, wg, wu, wd, layer):
    n_rows = buf.shape[0]
    D = wg.shape[2]
    nb = n_rows // EXPERT_BLOCK
    wspec = lambda w: pl.BlockSpec((1, 1) + w.shape[2:], lambda i, be, bc, bs: (layer, be[i], 0, 0))
    grid_spec = pltpu.PrefetchScalarGridSpec(
        num_scalar_prefetch=3,
        grid=(nb,),
        in_specs=[pl.BlockSpec((EXPERT_BLOCK, buf.shape[1]), lambda i, be, bc, bs: (bs[i], 0)),
                  wspec(wg), wspec(wu), wspec(wd)],
        out_specs=pl.BlockSpec((EXPERT_BLOCK, D), lambda i, be, bc, bs: (i, 0)),
        scratch_shapes=[pltpu.VMEM(wg.shape[2:], BF16), pltpu.VMEM(wu.shape[2:], BF16),
                        pltpu.VMEM(wd.shape[2:], BF16)],
    )
    return pl.pallas_call(
        _expert_kernel,
        grid_spec=grid_spec,
        out_shape=jax.ShapeDtypeStruct((n_rows, D), BF16),
        compiler_params=_cparams("arbitrary"),
        name="moe_experts",
    )(block_expert, block_count, block_src, buf, wg, wu, wd)


def _combine_kernel(src_tab, dst_tab, cnt_tab, tot_tab, y_hbm, x_ref, pos_ref, gate_ref, fin_ref, xo_ref,
                    srt_ref, sems, *, final, n_tiles, n_cls):
    t = pl.program_id(0)
    slot = t % 2
    tm = x_ref.shape[0]
    ls = srt_ref.shape[1]

    tabs = (src_tab, dst_tab, cnt_tab, tot_tab)

    def copier(buf_slot):
        return lambda lo, go, rows: pltpu.make_async_copy(
            y_hbm.at[pl.ds(go, rows)], srt_ref.at[buf_slot, pl.ds(lo, rows)], sems.at[buf_slot])

    @pl.when(t == 0)
    def _():
        srt_ref[...] = jnp.zeros_like(srt_ref)
        _start_pieces(tabs, t, n_cls, copier(slot))

    @pl.when(t + 1 < n_tiles)
    def _():
        _start_pieces(tabs, t + 1, n_cls, copier(1 - slot))

    _await_pieces(tabs, t, ls, copier(slot))
    j = lax.broadcasted_iota(I32, (tm, ls), 1)
    pick = jnp.where((pos_ref[:, 0:1] == j) | (pos_ref[:, 1:2] == j), 1.0, 0.0).astype(BF16)
    y = jnp.dot(pick, srt_ref[slot], preferred_element_type=F32)
    xn = x_ref[...] + gate_ref[0, 0] * y
    if final:
        xn = xn * lax.rsqrt(jnp.mean(xn * xn, axis=-1, keepdims=True) + RMS_EPS) * fin_ref[...]
    xo_ref[...] = xn


def _combine(y_buf, pos_t, tabs, n_cls, xs, mod, final_gain, n_tiles, tm, tiles_per_seq, final):
    N, D = xs.shape
    nmod = mod.shape[1]
    ls = 2 * tm + N_EXPERTS * RUN_ALIGN
    out_rows = n_tiles * tm if final else N
    kwargs = {} if final else {"input_output_aliases": {5: 0}}
    grid_spec = pltpu.PrefetchScalarGridSpec(
        num_scalar_prefetch=4,
        grid=(n_tiles,),
        in_specs=[pl.BlockSpec(memory_space=pl.ANY),
                  pl.BlockSpec((tm, D), lambda t, *_: (t, 0)),
                  pl.BlockSpec((tm, 2), lambda t, *_: (t, 0)),
                  pl.BlockSpec((1, 1, 1, D), lambda t, *_: (5, jnp.minimum(t // tiles_per_seq, nmod - 1), 0, 0)),
                  pl.BlockSpec((1, D), lambda t, *_: (0, 0))],
        out_specs=pl.BlockSpec((tm, D), lambda t, *_: (t, 0)),
        scratch_shapes=[pltpu.VMEM((2, ls, D), BF16), pltpu.SemaphoreType.DMA((2,))],
    )
    return pl.pallas_call(
        functools.partial(_combine_kernel, final=final, n_tiles=n_tiles, n_cls=n_cls),
        grid_spec=grid_spec,
        out_shape=jax.ShapeDtypeStruct((out_rows, D), F32),
        compiler_params=_cparams("arbitrary"),
        name="moe_combine_final" if final else "moe_combine",
        **kwargs,
    )(*tabs, y_buf, xs, pos_t, mod, final_gain.reshape(1, D))


def _moe(f, routing, xs, mod, wg, wu, wd, layer, final_gain, n_tok, tm, tiles_per_seq, final):
    nt = n_tok // tm
    gates, pos, cnt = routing
    gates, pos, cnt = gates[:, :n_tok], pos[:, :n_tok], cnt[:nt]
    n = cnt[:, :, 0].astype(I32)
    rlen = (n + RUN_ALIGN - 1) // RUN_ALIGN * RUN_ALIGN
    lo = jnp.cumsum(rlen, axis=1) - rlen
    region = jnp.sum(rlen, axis=0)
    region_pad = (region + EXPERT_BLOCK - 1) // EXPERT_BLOCK * EXPERT_BLOCK
    pends = jnp.cumsum(region_pad)
    pstarts = pends - region_pad
    go = pstarts[None, :] + jnp.cumsum(rlen, axis=0) - rlen
    n_blocks = -(-(2 * n_tok + nt * N_EXPERTS * RUN_ALIGN) // EXPERT_BLOCK) + N_EXPERTS
    blk0 = jnp.arange(n_blocks, dtype=I32) * EXPERT_BLOCK
    block_expert = jnp.minimum(jnp.sum((blk0[:, None] >= pends[None, :]).astype(I32), axis=1), N_EXPERTS - 1)
    block_used = (blk0 < (pstarts + region)[block_expert]).astype(I32)
    tabs, n_cls = _chunk_tables(lo, go, rlen, tm)
    gap_tabs, gap_cls = _chunk_tables(jnp.zeros((1, N_EXPERTS), I32), (pstarts + region)[None, :],
                                      (region_pad - region)[None, :], EXPERT_BLOCK)
    buf = _dispatch(f, pos, gates, tabs, n_cls, gap_tabs, gap_cls, n_blocks * EXPERT_BLOCK, nt, tm)
    last_used = jnp.max(jnp.where(block_used > 0, jnp.arange(n_blocks, dtype=I32), 0))
    block_src = jnp.where(block_used > 0, jnp.arange(n_blocks, dtype=I32), last_used)
    y_buf = _experts(buf, block_expert.astype(I32), block_used, block_src, wg, wu, wd, layer)
    return _combine(y_buf, pos.T, tabs, n_cls, xs, mod, final_gain, nt, tm, tiles_per_seq, final)


def _rope_tables(S, tm):
    rows = S // GRID_W
    row = jnp.repeat(jnp.arange(rows, dtype=F32), GRID_W)
    col = jnp.tile(jnp.arange(GRID_W, dtype=F32), rows)
    half = ROPE_PAIR
    inv_freq = ROPE_THETA ** (-jnp.arange(half, dtype=F32) / half)
    ang_r = row[:, None] * inv_freq[None, :]
    ang_c = col[:, None] * inv_freq[None, :]
    zeros = jnp.zeros_like(ang_r)
    cos = jnp.concatenate([jnp.cos(ang_r)] * 2 + [jnp.cos(ang_c)] * 2, axis=1)
    s1 = jnp.concatenate([-jnp.sin(ang_r), zeros, -jnp.sin(ang_c), zeros], axis=1)
    s2 = jnp.concatenate([zeros, jnp.sin(ang_r), zeros, jnp.sin(ang_c)], axis=1)
    def finish(tab, fill):
        tab = jnp.tile(tab, (1, LANES // ATTN_HEAD_DIM))
        return jnp.concatenate([tab, jnp.full((tm, LANES), fill, F32)], axis=0)
    return finish(cos, 1.0), finish(s1, 0.0), finish(s2, 0.0)


def kernel(x, c, ctx, c_ctx, ada_w, ada_b, norm_mix_g, norm_ffn_g, final_g, attn_w_qkv, attn_w_o, attn_sinks,
           gla_w_in, gla_w_a1, gla_w_a2, gla_b_a, gla_norm_g, gla_w_o, router_w, router_b,
           moe_w_gate, moe_w_up, moe_w_down):
    B, S, D = x.shape
    C = ctx.shape[1]
    depth = ada_w.shape[0]
    tm = TOKEN_TILE
    assert S % tm == 0 and (B * C) % tm == 0 and S % ATTN_BLOCK == 0 and C % ATTN_BLOCK == 0
    assert (B * S) % C == 0 and S % GLA_CHUNK == 0 and C % GLA_CHUNK == 0
    n_lat = B * S
    n_lat_tiles = n_lat // tm
    tiles_per_seq = S // tm

    rpad = -(-(B + 1) // 8) * 8
    cc = jnp.zeros((rpad, D), F32).at[:B].set(c).at[B].set(c_ctx)
    mods = _ada_table(cc, ada_w, ada_b)
    mods = mods[:, :B + 1].reshape(depth, B + 1, 6, 1, D).transpose(0, 2, 1, 3, 4)

    xs = (x.reshape(n_lat, D), ctx.reshape(B * C, D))
    ptm = PROJ_TILE if S % PROJ_TILE == 0 and (B * C) % PROJ_TILE == 0 else tm
    rope = _rope_tables(S, ptm)
    q_dim = ATTN_KV_HEADS * ATTN_GROUP * ATTN_HEAD_DIM
    kv_dim = ATTN_KV_HEADS * ATTN_HEAD_DIM
    kd = GLA_HEADS * GLA_KEY_DIM
    vd = GLA_HEADS * GLA_VAL_DIM

    def dup_heads(w):
        w = w.reshape(D, ATTN_KV_HEADS, 1, ATTN_HEAD_DIM)
        return jnp.broadcast_to(w, (D, ATTN_KV_HEADS, LANES // ATTN_HEAD_DIM, ATTN_HEAD_DIM)).reshape(D, -1)

    for i in range(depth):
        last = i == depth - 1
        mod = mods[i]
        j = i // 2
        if i % 2 == 0:
            wqkv = attn_w_qkv[j]
            w = dup_heads(wqkv[:, q_dim:q_dim + kv_dim]).astype(BF16)
            wqvt = jnp.concatenate([wqkv[:, :q_dim], wqkv[:, q_dim + kv_dim:]], axis=1).T.astype(BF16)
            q, k, vt = _qkv_proj(xs, norm_mix_g[i], mod, 0, w, wqvt, rope, ptm, n_lat // ptm, S // ptm)
            sink_row = jnp.repeat(attn_sinks[j].astype(F32).reshape(ATTN_KV_HEADS, ATTN_GROUP) * LOG2_E,
                                  ATTN_BLOCK, axis=1)
            o = _attention(q, k, vt, sink_row[:, None, :], B, S, C, not last)
            xs, f, *routing = _post_mixer(o, attn_w_o[j].astype(BF16), xs, mod, norm_ffn_g[i], router_w, router_b,
                                          tm, n_lat_tiles, tiles_per_seq)
        else:
            a1 = jnp.zeros((D, LANES), F32).at[:, :2 * GLA_GATE_RANK].set(
                jnp.concatenate([gla_w_a1[j, 0], gla_w_a1[j, 1]], axis=1))
            w = jnp.concatenate([gla_w_in[j], a1], axis=1).astype(BF16)
            w2 = jnp.zeros((LANES, 2 * kd), F32)
            w2 = w2.at[:GLA_GATE_RANK, :kd].set(gla_w_a2[j, 0]).at[GLA_GATE_RANK:2 * GLA_GATE_RANK, kd:].set(gla_w_a2[j, 1])
            ba = gla_b_a[j].reshape(1, 2 * kd).astype(F32)
            qk, v, og, la, chunk_tot = _gla_in_proj(xs, norm_mix_g[i], mod, 0, w, w2.astype(BF16), ba, ptm,
                                                    n_lat // ptm, S // ptm)
            o_fwd, o_bwd = _gla_scan(qk, v, la, chunk_tot, B, S, C)
            xs, f, *routing = _post_mixer(o_fwd, gla_w_o[j].astype(BF16), xs, mod, norm_ffn_g[i], router_w, router_b,
                                          tm, n_lat_tiles, tiles_per_seq,
                                          gla_extra=(o_bwd, og, gla_norm_g[j].reshape(1, GLA_VAL_DIM).astype(F32)))
        n_tok = n_lat if last else n_lat + B * C
        xs = _moe(f, routing, xs, mod, moe_w_gate, moe_w_up, moe_w_down, i, final_g, n_tok, tm, tiles_per_seq, last)
    return xs.reshape(B, S, D)
```

```python
import functools

import numpy as np
import jax
import jax.numpy as jnp
from jax import lax
from jax.experimental import pallas as pl
from jax.experimental.pallas import tpu as pltpu

F32 = jnp.float32
BF16 = jnp.bfloat16
I32 = jnp.int32

LANES = 128
VMEM_LIMIT_BYTES = 56 * 1024 * 1024

RMS_EPS = 1e-6
GRID_W = 64
ROPE_THETA = 10000.0
ATTN_HEAD_DIM = 64
ROPE_PAIR = ATTN_HEAD_DIM // 4
ATTN_KV_HEADS = 4
ATTN_GROUP = 4
ATTN_BLOCK = 128
ATTN_STEP_BLOCKS = 8
LOG2_E = 1.4426950408889634
ATTN_Q_SCALE = ATTN_HEAD_DIM ** -0.5 * LOG2_E
GLA_HEADS = 4
GLA_KEY_DIM = 128
GLA_VAL_DIM = 256
GLA_GATE_RANK = 16
GLA_GATE_NORM = 16.0
GLA_CHUNK = 128
GLA_STAT_ROWS = 64
GLA_BOUNDED_TOTAL = 40.0
N_EXPERTS = 16
N_GROUPS = 4
EXPERTS_PER_GROUP = 4
EXPERT_BLOCK = 1024
RUN_ALIGN = 16
TOKEN_TILE = 512
PROJ_TILE = 1024


def _cparams(*sem):
    return pltpu.CompilerParams(dimension_semantics=sem, vmem_limit_bytes=VMEM_LIMIT_BYTES)


def _norm_mod(x, gain, shift, scale):
    h = x * lax.rsqrt(jnp.mean(x * x, axis=-1, keepdims=True) + RMS_EPS) * gain
    return h * (1.0 + scale) + shift


def _ada_kernel(c_ref, w_ref, b_ref, o_ref):
    c = c_ref[...]
    s = (c * jax.nn.sigmoid(c)).astype(BF16)
    o_ref[0] = jnp.dot(s, w_ref[0].astype(BF16), preferred_element_type=F32) + b_ref[0]


def _ada_table(cc, ada_w, ada_b):
    L, D, D6 = ada_w.shape
    R = cc.shape[0]
    tn = 1536
    return pl.pallas_call(
        _ada_kernel,
        grid=(L, D6 // tn),
        in_specs=[pl.BlockSpec((R, D), lambda l, j: (0, 0)),
                  pl.BlockSpec((1, D, tn), lambda l, j: (l, 0, j)),
                  pl.BlockSpec((1, 1, tn), lambda l, j: (l, 0, j))],
        out_specs=pl.BlockSpec((1, R, tn), lambda l, j: (l, 0, j)),
        out_shape=jax.ShapeDtypeStruct((L, R, D6), F32),
        compiler_params=_cparams("parallel", "parallel"),
        name="ada_table",
    )(cc, ada_w, ada_b.reshape(L, 1, D6))


def _stream_tile(x_refs, n_lat_tiles):
    if len(x_refs) == 1:
        return x_refs[0][...]
    return jnp.where(pl.program_id(0) < n_lat_tiles, x_refs[0][...], x_refs[1][...])


def _stream_args(xs, tm, n_lat_tiles):
    if isinstance(xs, tuple):
        D = xs[0].shape[1]
        return list(xs), [pl.BlockSpec((tm, D), lambda t: (jnp.minimum(t, n_lat_tiles - 1), 0)),
                          pl.BlockSpec((tm, D), lambda t: (jnp.maximum(t - n_lat_tiles, 0), 0))]
    return [xs], [pl.BlockSpec((tm, xs.shape[1]), lambda t: (t, 0))]


def _qkv_kernel(*refs, n_x, n_lat_tiles):
    x_refs = refs[:n_x]
    (g_ref, sh_ref, sc_ref, w_ref, wqvt_ref, cos_ref, s1_ref, s2_ref, cost_ref, s1t_ref, s2t_ref,
     qt_ref, k_ref, vt_ref) = refs[n_x:]
    h = _norm_mod(_stream_tile(x_refs, n_lat_tiles), g_ref[...], sh_ref[0, 0], sc_ref[0, 0]).astype(BF16)
    z = jnp.dot(h, w_ref[...], preferred_element_type=F32)
    cos, s1, s2 = cos_ref[...], s1_ref[...], s2_ref[...]
    for j in range(k_ref.shape[1] // LANES):
        zc = z[:, j * LANES:(j + 1) * LANES]
        r = zc * cos + pltpu.roll(zc, LANES - ROPE_PAIR, 1) * s1 + pltpu.roll(zc, ROPE_PAIR, 1) * s2
        k_ref[:, j * LANES:(j + 1) * LANES] = r.astype(BF16)
    zt = lax.dot_general(wqvt_ref[...], h, (((1,), (1,)), ((), ())), preferred_element_type=F32)
    qd = qt_ref.shape[1]
    cost, s1t, s2t = cost_ref[...], s1t_ref[...], s2t_ref[...]
    n_slabs = qt_ref.shape[0]
    for j in range(qd // LANES):
        zc = zt[j * LANES:(j + 1) * LANES]
        below = jnp.concatenate([zc[ROPE_PAIR:], zc[:ROPE_PAIR]], axis=0)
        above = jnp.concatenate([zc[LANES - ROPE_PAIR:], zc[:LANES - ROPE_PAIR]], axis=0)
        r = ((zc * cost + below * s1t + above * s2t) * ATTN_Q_SCALE).astype(BF16)
        for c in range(n_slabs):
            qt_ref[c, j * LANES:(j + 1) * LANES, :] = r[:, c * ATTN_BLOCK:(c + 1) * ATTN_BLOCK]
    vt = zt[qd:].astype(BF16)
    for c in range(n_slabs):
        vt_ref[c] = vt[:, c * ATTN_BLOCK:(c + 1) * ATTN_BLOCK]


def _gla_in_kernel(x_ref, g_ref, sh_ref, sc_ref, w_ref, w2_ref, ba_ref, qk_ref, v_ref, og_ref, la_ref, tot_ref):
    h = _norm_mod(x_ref[...], g_ref[...], sh_ref[0, 0], sc_ref[0, 0]).astype(BF16)
    kd = GLA_HEADS * GLA_KEY_DIM
    vd = GLA_HEADS * GLA_VAL_DIM
    a1 = jnp.dot(h, w_ref[:, 2 * kd + 2 * vd:], preferred_element_type=F32).astype(BF16)
    pre = jnp.dot(a1, w2_ref[...], preferred_element_type=F32) + ba_ref[...]
    z = jnp.dot(h, w_ref[:, :2 * kd + 2 * vd], preferred_element_type=F32)
    la = (jnp.minimum(pre, 0.0) - jnp.log1p(jnp.exp(-jnp.abs(pre)))) * (1.0 / GLA_GATE_NORM)
    la_ref[...] = la
    qk_ref[:, :kd] = (z[:, :kd] * (GLA_KEY_DIM ** -0.5)).astype(BF16)
    qk_ref[:, kd:] = z[:, kd:2 * kd].astype(BF16)
    v_ref[...] = z[:, 2 * kd:2 * kd + vd].astype(BF16)
    og_ref[...] = z[:, 2 * kd + vd:2 * kd + 2 * vd].astype(BF16)
    nc = la.shape[0] // GLA_STAT_ROWS
    tot = jnp.sum(la.reshape(nc, GLA_STAT_ROWS, la.shape[1]), axis=1)
    lane = lax.broadcasted_iota(I32, (nc, LANES), 1)
    acc = jnp.zeros((nc, LANES), F32)
    for hd in range(2 * GLA_HEADS):
        worst = jnp.min(tot[:, hd * GLA_KEY_DIM:(hd + 1) * GLA_KEY_DIM], axis=1, keepdims=True)
        acc = jnp.where(lane == hd, worst, acc)
    tot_ref[...] = acc


def _tile_specs(D, tm, n_lat_tiles, tiles_per_seq, n_mod_rows):
    def mod_idx(which):
        return lambda t: (which, jnp.minimum(t // tiles_per_seq, n_mod_rows - 1), 0, 0)
    return mod_idx, [pl.BlockSpec((tm, D), lambda t: (t, 0)),
                     pl.BlockSpec((1, D), lambda t: (0, 0))]


def _qkv_proj(xs, gain, mod, which, w, wqvt, rope, tm, n_lat_tiles, tiles_per_seq):
    x_args, x_specs = _stream_args(xs, tm, n_lat_tiles)
    N = sum(a.shape[0] for a in x_args)
    D = x_args[0].shape[1]
    nmod = mod.shape[1]
    mod_idx, specs = _tile_specs(D, tm, n_lat_tiles, tiles_per_seq, nmod)
    qd = ATTN_KV_HEADS * ATTN_GROUP * ATTN_HEAD_DIM
    kd = ATTN_KV_HEADS * LANES
    vd = wqvt.shape[0] - qd
    rope_idx = lambda t: jnp.where(t < n_lat_tiles, t % tiles_per_seq, tiles_per_seq)
    rows = pl.BlockSpec((tm, LANES), lambda t: (rope_idx(t), 0))
    cols = pl.BlockSpec((LANES, tm), lambda t: (0, rope_idx(t)))
    in_specs = x_specs + specs[1:] + [
        pl.BlockSpec((1, 1, 1, D), mod_idx(which)),
        pl.BlockSpec((1, 1, 1, D), mod_idx(which + 1)),
        pl.BlockSpec(w.shape, lambda t: (0, 0)),
        pl.BlockSpec(wqvt.shape, lambda t: (0, 0)),
        rows, rows, rows, cols, cols, cols,
    ]
    slabs = lambda width: (pl.BlockSpec((tm // ATTN_BLOCK, width, ATTN_BLOCK), lambda t: (t, 0, 0)),
                           jax.ShapeDtypeStruct((N // ATTN_BLOCK, width, ATTN_BLOCK), BF16))
    outs = [slabs(qd), (pl.BlockSpec((tm, kd), lambda t: (t, 0)), jax.ShapeDtypeStruct((N, kd), BF16)), slabs(vd)]
    return pl.pallas_call(
        functools.partial(_qkv_kernel, n_x=len(x_args), n_lat_tiles=n_lat_tiles),
        grid=(N // tm,),
        in_specs=in_specs,
        out_specs=[o[0] for o in outs],
        out_shape=[o[1] for o in outs],
        compiler_params=_cparams("parallel"),
        name="attn_qkv_proj",
    )(*x_args, gain.reshape(1, D), mod, mod, w, wqvt, *rope, *[tab.T for tab in rope])


def _gla_in_proj(xs, gain, mod, which, w, w2, ba, tm, n_lat_tiles, tiles_per_seq):
    N, D = xs.shape
    nmod = mod.shape[1]
    mod_idx, specs = _tile_specs(D, tm, n_lat_tiles, tiles_per_seq, nmod)
    kd = GLA_HEADS * GLA_KEY_DIM
    vd = GLA_HEADS * GLA_VAL_DIM
    in_specs = specs + [
        pl.BlockSpec((1, 1, 1, D), mod_idx(which)),
        pl.BlockSpec((1, 1, 1, D), mod_idx(which + 1)),
        pl.BlockSpec(w.shape, lambda t: (0, 0)),
        pl.BlockSpec(w2.shape, lambda t: (0, 0)),
        pl.BlockSpec((1, 2 * kd), lambda t: (0, 0)),
    ]
    row = lambda width: pl.BlockSpec((tm, width), lambda t: (t, 0))
    return pl.pallas_call(
        _gla_in_kernel,
        grid=(N // tm,),
        in_specs=in_specs,
        out_specs=[row(2 * kd), row(vd), row(vd), row(2 * kd),
                   pl.BlockSpec((tm // GLA_STAT_ROWS, LANES), lambda t: (t, 0))],
        out_shape=[jax.ShapeDtypeStruct((N, 2 * kd), BF16),
                   jax.ShapeDtypeStruct((N, vd), BF16),
                   jax.ShapeDtypeStruct((N, vd), BF16),
                   jax.ShapeDtypeStruct((N, 2 * kd), F32),
                   jax.ShapeDtypeStruct((N // GLA_STAT_ROWS, LANES), F32)],
        compiler_params=_cparams("parallel"),
        name="gla_in_proj",
    )(xs, gain.reshape(1, D), mod, mod, w, w2, ba)


def _attn_kernel(*refs, window, nq, nb):
    tq = ATTN_BLOCK
    if window:
        q_ref, kp, km, kn, kx, vp, vm, vn, vx, sink_ref, tri_ref, o_ref = refs
        k_blocks = ([lambda ks: kp[:, ks]] + [lambda ks, i=i: km[i * tq:(i + 1) * tq, ks] for i in range(nb)]
                    + [lambda ks: kn[:, ks]])
        v_blocks = ([lambda hs: vp[0, hs, :]] + [lambda hs, i=i: vm[i, hs, :] for i in range(nb)]
                    + [lambda hs: vn[0, hs, :]])
    else:
        q_ref, kx, vx, sink_ref, _, o_ref = refs
    if window:
        bias = []
        for i in range(nb):
            j = pl.program_id(1) * nb + i
            bias_prev = jnp.where(j > 0, tri_ref[0], -jnp.inf)
            bias_next = jnp.where(j < nq - 1, tri_ref[1], -jnp.inf)
            bias.append((jnp.concatenate([bias_prev] * ATTN_GROUP, axis=1),
                         jnp.concatenate([bias_next] * ATTN_GROUP, axis=1)))

    def scores(i, kh):
        ks = slice(kh * LANES, (kh + 1) * LANES)
        kk = jnp.concatenate([blk(ks) for blk in k_blocks[i:i + 3]] + [kx[:, ks]], axis=0) if window else kx[:, ks]
        zero = jnp.zeros((LANES - ATTN_HEAD_DIM, tq), BF16)
        heads = [kh * ATTN_GROUP + g for g in range(ATTN_GROUP)]
        qs = jnp.concatenate(
            [jnp.concatenate([q_ref[i, h * ATTN_HEAD_DIM:(h + 1) * ATTN_HEAD_DIM, :], zero], axis=0) for h in heads],
            axis=1)
        return jnp.dot(kk, qs, preferred_element_type=F32)

    def softmax(s, i, kh):
        if window:
            s = jnp.concatenate([s[:tq] + bias[i][0], s[tq:2 * tq], s[2 * tq:3 * tq] + bias[i][1], s[3 * tq:]], axis=0)
        sink = sink_ref[kh]
        m = jnp.maximum(jnp.max(s, axis=0, keepdims=True), sink)
        p = jnp.exp2(s - m)
        l = jnp.sum(p, axis=0, keepdims=True) + jnp.exp2(sink - m)
        return p.astype(BF16), l

    def values(p, l, i, kh):
        hs = slice(kh * ATTN_HEAD_DIM, (kh + 1) * ATTN_HEAD_DIM)
        v_ctx = [vx[c, hs, :] for c in range(vx.shape[0])]
        vv = jnp.concatenate(([blk(hs) for blk in v_blocks[i:i + 3]] if window else []) + v_ctx, axis=1)
        o = jnp.dot(vv, p, preferred_element_type=F32) / l
        for g in range(ATTN_GROUP):
            h = kh * ATTN_GROUP + g
            o_ref[i, h * ATTN_HEAD_DIM:(h + 1) * ATTN_HEAD_DIM, :] = o[:, g * tq:(g + 1) * tq].astype(BF16)

    units = [(i, kh) for i in range(nb) for kh in range(ATTN_KV_HEADS)]
    s_next = scores(*units[0])
    pending = None
    for n, unit in enumerate(units):
        s_cur = s_next
        if n + 1 < len(units):
            s_next = scores(*units[n + 1])
        if pending is not None:
            values(*pending)
        pending = softmax(s_cur, *unit) + unit
    values(*pending)


def _attention(qt, k, vt, sink_row, B, S, C, need_ctx):
    N = k.shape[0]
    qd = qt.shape[1]
    kd = k.shape[1]
    vd = vt.shape[1]
    tq = ATTN_BLOCK
    nq = S // tq
    nb = min(ATTN_STEP_BLOCKS, nq)
    nbc = C // tq
    assert nq % nb == 0 and C == nbc * tq
    steps = nq // nb
    ctx_blk0 = (B * S) // C
    mid = lambda b, j: (b * steps + j, 0)
    prev = lambda b, j: (b * nq + jnp.maximum(j * nb - 1, 0), 0)
    nxt = lambda b, j: (b * nq + jnp.minimum(j * nb + nb, nq - 1), 0)
    cmap = lambda b, j: (ctx_blk0 + b, 0)
    kctx = pl.BlockSpec((C, kd), cmap)
    slab = lambda n, m, width=vd: pl.BlockSpec((n, width, tq), lambda b, j: (m(b, j)[0], 0, 0))
    vctx = slab(nbc, cmap)
    sink_spec = pl.BlockSpec(sink_row.shape, lambda b, j: (0, 0, 0))
    key = np.arange(tq)[:, None]
    qry = np.arange(tq)[None, :]
    tri = jnp.asarray(np.stack([np.where(key >= qry, 0.0, -np.inf), np.where(key <= qry, 0.0, -np.inf)]), F32)
    o_lat = pl.pallas_call(
        functools.partial(_attn_kernel, window=True, nq=nq, nb=nb),
        grid=(B, steps),
        in_specs=[slab(nb, mid, qd),
                  pl.BlockSpec((tq, kd), prev), pl.BlockSpec((nb * tq, kd), mid), pl.BlockSpec((tq, kd), nxt), kctx,
                  slab(1, prev), slab(nb, mid), slab(1, nxt), vctx, sink_spec,
                  pl.BlockSpec(tri.shape, lambda b, j: (0, 0, 0))],
        out_specs=pl.BlockSpec((nb, qd, tq), lambda b, j: (mid(b, j)[0], 0, 0)),
        out_shape=jax.ShapeDtypeStruct((N // tq, qd, tq), BF16),
        compiler_params=_cparams("parallel", "parallel"),
        name="attn_window",
    )(qt, k, k, k, k, vt, vt, vt, vt, sink_row, tri)
    if not need_ctx:
        return o_lat
    return pl.pallas_call(
        functools.partial(_attn_kernel, window=False, nq=nbc, nb=nbc),
        grid=(B, 1),
        in_specs=[slab(nbc, cmap, qd), kctx, vctx, sink_spec,
                  pl.BlockSpec(memory_space=pl.ANY)],
        out_specs=pl.BlockSpec((nbc, qd, tq), lambda b, j: (cmap(b, j)[0], 0, 0)),
        out_shape=jax.ShapeDtypeStruct((N // tq, qd, tq), BF16),
        input_output_aliases={4: 0},
        compiler_params=_cparams("parallel", "parallel"),
        name="attn_context",
    )(qt, k, vt, sink_row, o_lat)


def _gla_constants(C):
    levels = []
    m = 1
    while m < C:
        levels.append(m)
        m *= 2
    t = np.arange(C)[:, None]
    u = np.arange(C)[None, :]
    secs = [(u <= t), (u > t)]
    masks = []
    for m in levels:
        base = (t // (2 * m)) * (2 * m)
        ref = base + m - 1
        second = t >= base + m
        secs.append(np.where(second, (u > ref) & (u <= t), (u > t) & (u <= ref)))
        masks.append((t // (2 * m) == u // (2 * m)) & second & (u < (u // (2 * m)) * (2 * m) + m))
    masks.append(t == u)
    masks.append(u <= t)
    mf = np.concatenate([s.astype(np.float32) for s in secs], axis=0)
    kf = np.stack([mk.astype(np.float32) for mk in masks], axis=0)
    mb = np.concatenate([s.astype(np.float32)[::-1, ::-1] for s in secs], axis=0)
    kb = np.stack([mk.astype(np.float32)[::-1, ::-1] for mk in masks], axis=0)
    return np.stack([mf, mb]), np.stack([kf, kb]), len(levels)


def _gla_chunk(q, k, v, g, st_ref, mm_ref, mk_ref, d, nl, bounded):
    C, dk = q.shape
    nt = (((1,), (1,)), ((), ()))
    g_hi = g.astype(BF16)
    g_lo = (g - g_hi.astype(F32)).astype(BF16)
    mmat = mm_ref[d, :2 * C] if bounded else mm_ref[d]
    e2 = jnp.dot(mmat, jnp.concatenate([g_hi, g_lo], axis=1), preferred_element_type=F32)
    ee = e2[:, :dk] + e2[:, dk:]
    ex = jnp.exp(ee)
    qf, kf = q.astype(F32), k.astype(F32)
    qe = (qf * ex[:C]).astype(BF16)
    ke = (kf * ex[C:2 * C]).astype(BF16)
    st = st_ref[...]
    o = lax.dot_general(qe, st.astype(BF16), nt, preferred_element_type=F32)
    if bounded:
        ki = (kf * jnp.exp(-ee[:C])).astype(BF16)
        a = mk_ref[d, nl + 1] * lax.dot_general(qe, ki, nt, preferred_element_type=F32)
    else:
        a = mk_ref[d, nl] * lax.dot_general(q, k, nt, preferred_element_type=F32)
        for i in range(nl):
            xl = ex[(2 + i) * C:(3 + i) * C]
            ql = (qf * xl).astype(BF16)
            kl = (kf * xl).astype(BF16)
            a = a + mk_ref[d, i] * lax.dot_general(ql, kl, nt, preferred_element_type=F32)
    o = o + jnp.dot(a.astype(BF16), v, preferred_element_type=F32)
    decay = jnp.exp(jnp.sum(g, axis=0, keepdims=True))
    st_ref[...] = st * decay + lax.dot_general(v, ke, (((0,), (0,)), ((), ())), preferred_element_type=F32)
    return o


def _gla_kernel(okf_ref, okb_ref, qf, kf, vf, lf, qb, kb, vb, lb, mm_ref, mk_ref, of_ref, ob_ref, st_ref,
                *, nl, seg_chunks, lat_segs, ctx_seg0):
    C = GLA_CHUNK
    H, DK, DV = GLA_HEADS, GLA_KEY_DIM, GLA_VAL_DIM
    b, j = pl.program_id(0), pl.program_id(1)

    @pl.when(j == 0)
    def _():
        st_ref[...] = jnp.zeros_like(st_ref)

    seg_f = jnp.where(j == 0, ctx_seg0 + b, b * lat_segs + j - 1)
    seg_b = jnp.where(j == 0, ctx_seg0 + b, b * lat_segs + lat_segs - j)

    nt = (((1,), (1,)), ((), ()))
    kw = H * DK

    def rows_of(i):
        cf, cb = i, seg_chunks - 1 - i
        return slice(cf * C, (cf + 1) * C), slice(cb * C, (cb + 1) * C)

    def advance_general(i):
        rf, rb = rows_of(i)
        for h in range(H):
            ks, vs = slice(h * DK, (h + 1) * DK), slice(h * DV, (h + 1) * DV)
            o = _gla_chunk(qf[rf, ks], kf[rf, ks], vf[rf, vs], lf[rf, ks], st_ref.at[h],
                           mm_ref, mk_ref, 0, nl, False)
            of_ref[rf, vs] = o.astype(BF16)
            o = _gla_chunk(qb[rb, ks], kb[rb, ks], vb[rb, vs], lb[rb, ks], st_ref.at[H + h],
                           mm_ref, mk_ref, 1, nl, False)
            ob_ref[rb, vs] = o.astype(BF16)

    def bounded_front(i):
        rf, rb = rows_of(i)
        sides = ((qf, kf, vf, lf, rf, of_ref, 0), (qb, kb, vb, lb, rb, ob_ref, 1))
        pre = []
        for q_r, k_r, v_r, l_r, rows, o_r, d in sides:
            g = l_r[rows, :]
            g_hi = g.astype(BF16)
            g_lo = (g - g_hi.astype(F32)).astype(BF16)
            e2 = jnp.dot(mm_ref[d, :2 * C], jnp.concatenate([g_hi, g_lo], axis=1),
                         preferred_element_type=F32)
            pre.append((e2[:, :kw] + e2[:, kw:], g))
        units = []
        for (ee, g), (q_r, k_r, v_r, l_r, rows, o_r, d) in zip(pre, sides):
            ex = jnp.exp(ee)
            kf32 = k_r[rows, :].astype(F32)
            qe = (q_r[rows, :].astype(F32) * ex[:C]).astype(BF16)
            ke = (kf32 * ex[C:]).astype(BF16)
            ki = (kf32 * jnp.exp(-ee[:C])).astype(BF16)
            decay = jnp.exp(jnp.sum(g, axis=0, keepdims=True))
            for h in range(H):
                ks = slice(h * DK, (h + 1) * DK)
                units.append([d, h, qe[:, ks], ke[:, ks], ki[:, ks], decay[:, ks], v_r, rows, o_r])
        for unit in units:
            d, h, qe, ke, ki = unit[:5]
            score = lax.dot_general(qe, ki, nt, preferred_element_type=F32)
            unit.append((mk_ref[d, nl + 1] * score).astype(BF16))
        return units

    def bounded_back(units):
        inter = [lax.dot_general(u[2], st_ref[u[0] * H + u[1]].astype(BF16), nt, preferred_element_type=F32)
                 for u in units]
        for n, (d, h, qe, ke, ki, decay, v_r, rows, o_r, a) in enumerate(units):
            vs = slice(h * DV, (h + 1) * DV)
            v = v_r[rows, vs]
            o = inter[n] + jnp.dot(a, v, preferred_element_type=F32)
            upd = lax.dot_general(v, ke, (((0,), (0,)), ((), ())), preferred_element_type=F32)
            o_r[rows, vs] = o.astype(BF16)
            st_ref[d * H + h] = st_ref[d * H + h] * decay + upd

    bounded = None
    for i in range(seg_chunks):
        ok = (okf_ref[seg_f * seg_chunks + i] != 0) & (okb_ref[seg_b * seg_chunks + seg_chunks - 1 - i] != 0)
        bounded = ok if bounded is None else bounded & ok

    @pl.when(bounded)
    def _():
        fronts = [bounded_front(i) for i in range(seg_chunks)]
        for units in fronts:
            bounded_back(units)

    @pl.when(jnp.logical_not(bounded))
    def _():
        for i in range(seg_chunks):
            advance_general(i)


def _gla_scan(qk, v, la, chunk_tot, B, S, C):
    N = qk.shape[0]
    H, DK, DV = GLA_HEADS, GLA_KEY_DIM, GLA_VAL_DIM
    seg = C
    assert S % seg == 0 and seg % GLA_CHUNK == 0
    lat_segs = S // seg
    ctx_seg0 = (B * S) // seg
    mm, mk, nl = _gla_constants(GLA_CHUNK)
    mm = jnp.asarray(mm, BF16)
    mk = jnp.asarray(mk, F32)
    chunk_tot = jnp.sum(chunk_tot.reshape(-1, GLA_CHUNK // GLA_STAT_ROWS, LANES), axis=1)
    ok = chunk_tot[:, :2 * H] >= -GLA_BOUNDED_TOTAL
    okf = jnp.all(ok[:, :H], axis=1).astype(I32)
    okb = jnp.all(ok[:, H:], axis=1).astype(I32)
    fwd = lambda col: (lambda b, j, *_: (jnp.where(j == 0, ctx_seg0 + b, b * lat_segs + j - 1), col))
    bwd = lambda col: (lambda b, j, *_: (jnp.where(j == 0, ctx_seg0 + b, b * lat_segs + lat_segs - j), col))
    kw, vw = H * DK, H * DV
    in_specs = [
        pl.BlockSpec((seg, kw), fwd(0)), pl.BlockSpec((seg, kw), fwd(1)), pl.BlockSpec((seg, vw), fwd(0)),
        pl.BlockSpec((seg, kw), fwd(0)),
        pl.BlockSpec((seg, kw), bwd(0)), pl.BlockSpec((seg, kw), bwd(1)), pl.BlockSpec((seg, vw), bwd(0)),
        pl.BlockSpec((seg, kw), bwd(1)),
        pl.BlockSpec(mm.shape, lambda b, j, *_: (0, 0, 0)),
        pl.BlockSpec(mk.shape, lambda b, j, *_: (0, 0, 0, 0)),
    ]
    grid_spec = pltpu.PrefetchScalarGridSpec(
        num_scalar_prefetch=2,
        grid=(B, lat_segs + 1),
        in_specs=in_specs,
        out_specs=[pl.BlockSpec((seg, vw), fwd(0)), pl.BlockSpec((seg, vw), bwd(0))],
        scratch_shapes=[pltpu.VMEM((2 * H, DV, DK), F32)],
    )
    return pl.pallas_call(
        functools.partial(_gla_kernel, nl=nl, seg_chunks=seg // GLA_CHUNK, lat_segs=lat_segs, ctx_seg0=ctx_seg0),
        grid_spec=grid_spec,
        out_shape=[jax.ShapeDtypeStruct((N, vw), BF16), jax.ShapeDtypeStruct((N, vw), BF16)],
        compiler_params=_cparams("parallel", "arbitrary"),
        name="gla_scan",
    )(okf, okb, qk, qk, v, la, qk, qk, v, la, mm, mk)


def _post_kernel(*refs, gla, n_x, n_lat_tiles):
    x_refs, refs = refs[:n_x], refs[n_x:]
    route_refs = refs[-3:]
    route_in = refs[-8:-5]
    refs = refs[:-8] + refs[-5:-3]
    x_tile = _stream_tile(x_refs, n_lat_tiles)
    if gla:
        o_ref, ob_ref, og_ref, ng_ref, w_ref, gate_ref, fg_ref, fsh_ref, fsc_ref, xo_ref, f_ref = refs
        o = o_ref[...].astype(F32) + ob_ref[...].astype(F32)
        g = og_ref[...].astype(F32)
        parts = []
        for h in range(GLA_HEADS):
            oh = o[:, h * GLA_VAL_DIM:(h + 1) * GLA_VAL_DIM]
            parts.append(oh * lax.rsqrt(jnp.mean(oh * oh, axis=-1, keepdims=True) + RMS_EPS) * ng_ref[...])
        mix = (jnp.concatenate(parts, axis=1) * (g * jax.nn.sigmoid(g))).astype(BF16)
        y = jnp.dot(mix, w_ref[...], preferred_element_type=F32)
    else:
        o_ref, w_ref, gate_ref, fg_ref, fsh_ref, fsc_ref, xo_ref, f_ref = refs
        o_t = jnp.concatenate([o_ref[c] for c in range(o_ref.shape[0])], axis=1)
        y = lax.dot_general(o_t, w_ref[...], (((0,), (0,)), ((), ())), preferred_element_type=F32)
    xn = x_tile + gate_ref[0, 0] * y
    xo_ref[...] = xn
    f = _norm_mod(xn, fg_ref[...], fsh_ref[0, 0], fsc_ref[0, 0])
    f_ref[...] = f.astype(BF16)
    _route_tile(f, *route_in, *route_refs)


def _post_mixer(o, w_o, xs, mod, ffn_gain, router_w, router_b, tm, n_lat_tiles, tiles_per_seq, gla_extra=None,
                n_tiles=None):
    args, specs = _stream_args(xs, tm, n_lat_tiles)
    n_x = len(args)
    N = sum(a.shape[0] for a in args)
    D = args[0].shape[1]
    r_args, r_in, r_out, r_shapes = _route_io(router_w, router_b, N, D, tm)
    nmod = mod.shape[1]
    mod_idx = lambda which: (lambda t: (which, jnp.minimum(t // tiles_per_seq, nmod - 1), 0, 0))
    row = lambda width: pl.BlockSpec((tm, width), lambda t: (t, 0))
    const = lambda a: pl.BlockSpec(a.shape, lambda t: (0,) * a.ndim)
    if gla_extra is None:
        args, specs = args + [o], specs + [pl.BlockSpec((tm // o.shape[2],) + o.shape[1:], lambda t: (t, 0, 0))]
    else:
        o_bwd, og, ng = gla_extra
        args += [o, o_bwd, og, ng]
        specs += [row(o.shape[1]), row(o_bwd.shape[1]), row(og.shape[1]), const(ng)]
    args += [w_o, mod, ffn_gain.reshape(1, D), mod, mod]
    specs += [const(w_o), pl.BlockSpec((1, 1, 1, D), mod_idx(2)), pl.BlockSpec((1, D), lambda t: (0, 0)),
              pl.BlockSpec((1, 1, 1, D), mod_idx(3)), pl.BlockSpec((1, 1, 1, D), mod_idx(4))]
    args += r_args
    specs += r_in
    return pl.pallas_call(
        functools.partial(_post_kernel, gla=gla_extra is not None, n_x=n_x, n_lat_tiles=n_lat_tiles),
        grid=(N // tm if n_tiles is None else n_tiles,),
        in_specs=specs,
        out_specs=[row(D), row(D)] + r_out,
        out_shape=[jax.ShapeDtypeStruct((N, D), F32), jax.ShapeDtypeStruct((N, D), BF16)] + r_shapes,
        input_output_aliases={0: 0} if n_x == 1 else {},
        compiler_params=_cparams("parallel"),
        name="post_mixer_gla" if gla_extra is not None else "post_mixer_attn",
    )(*args)


def _route_tile(f, rw_ref, rb_ref, tri_ref, gate_ref, pos_ref, cnt_ref):
    tm = f.shape[0]
    fh = f.astype(BF16)
    fl = (f - fh.astype(F32)).astype(BF16)
    parts = jnp.dot(jnp.concatenate([fh, fl], axis=0), rw_ref[...], preferred_element_type=F32)
    logits = (parts[:tm, :LANES] + parts[:tm, LANES:]) + (parts[tm:, :LANES] + parts[tm:, LANES:])
    lt = logits.T[:N_EXPERTS]
    scores = jax.nn.sigmoid(lt)
    sel = scores + rb_ref[...]
    srow = [sel[e:e + 1] for e in range(N_EXPERTS)]
    prow = [scores[e:e + 1] for e in range(N_EXPERTS)]
    gscore = []
    for g in range(N_GROUPS):
        a, b, c, d = srow[4 * g:4 * g + 4]
        hi1, lo1, hi2, lo2 = jnp.maximum(a, b), jnp.minimum(a, b), jnp.maximum(c, d), jnp.minimum(c, d)
        gscore.append(jnp.maximum(hi1, hi2) + jnp.maximum(jnp.minimum(hi1, hi2), jnp.maximum(lo1, lo2)))
    best, grp = gscore[0], jnp.zeros_like(gscore[0], dtype=I32)
    for g in range(1, N_GROUPS):
        better = gscore[g] > best
        grp = jnp.where(better, g, grp)
        best = jnp.where(better, gscore[g], best)
    s_in, p_in = [], []
    for k in range(EXPERTS_PER_GROUP):
        sv, pv = srow[k], prow[k]
        for g in range(1, N_GROUPS):
            sv = jnp.where(grp == g, srow[4 * g + k], sv)
            pv = jnp.where(grp == g, prow[4 * g + k], pv)
        s_in.append(sv)
        p_in.append(pv)
    i1, v1, g1 = jnp.zeros_like(grp), s_in[0], p_in[0]
    for k in range(1, EXPERTS_PER_GROUP):
        better = s_in[k] > v1
        i1 = jnp.where(better, k, i1)
        g1 = jnp.where(better, p_in[k], g1)
        v1 = jnp.where(better, s_in[k], v1)
    i2, v2, g2 = jnp.zeros_like(grp), jnp.full_like(v1, -jnp.inf), jnp.zeros_like(v1)
    for k in range(EXPERTS_PER_GROUP):
        better = (i1 != k) & (s_in[k] > v2)
        i2 = jnp.where(better, k, i2)
        g2 = jnp.where(better, p_in[k], g2)
        v2 = jnp.where(better, s_in[k], v2)
    e1 = grp * EXPERTS_PER_GROUP + i1
    e2 = grp * EXPERTS_PER_GROUP + i2
    tot = g1 + g2
    gate_ref[0:1, :] = g1 / tot
    gate_ref[1:2, :] = g2 / tot
    eid = lax.broadcasted_iota(I32, scores.shape, 0)
    hot1 = eid == e1
    hot2 = eid == e2
    onehot = jnp.where(hot1 | hot2, 1.0, 0.0).astype(BF16)
    cum = jnp.dot(onehot, tri_ref[...], preferred_element_type=F32)
    count = cum[:, cum.shape[1] - 1:]
    cnt_ref[0] = jnp.broadcast_to(count, cnt_ref.shape[1:])
    run_len = jnp.floor((count + (RUN_ALIGN - 1)) * (1.0 / RUN_ALIGN)) * RUN_ALIGN
    ends = jnp.broadcast_to(run_len, (N_EXPERTS, LANES))
    row = lax.broadcasted_iota(I32, (N_EXPERTS, LANES), 0)
    step = 1
    while step < N_EXPERTS:
        ends = ends + jnp.where(row >= step, pltpu.roll(ends, step, 0), 0.0)
        step *= 2
    slot = cum + (ends[:, 0:1] - run_len - 1.0)
    pos_ref[0:1, :] = jnp.sum(jnp.where(hot1, slot, 0.0), axis=0, keepdims=True).astype(I32)
    pos_ref[1:2, :] = jnp.sum(jnp.where(hot2, slot, 0.0), axis=0, keepdims=True).astype(I32)


def _route_io(router_w, router_b, N, D, tm):
    rw = jnp.zeros((D, LANES), F32).at[:, :N_EXPERTS].set(router_w.astype(F32))
    rw_hi = rw.astype(BF16)
    rw = jnp.concatenate([rw_hi, (rw - rw_hi.astype(F32)).astype(BF16)], axis=1)
    rb = router_b.astype(F32).reshape(N_EXPERTS, 1)
    tri = jnp.asarray(np.triu(np.ones((tm, tm), np.float32)), BF16)
    in_specs = [pl.BlockSpec((D, 2 * LANES), lambda t: (0, 0)), pl.BlockSpec((N_EXPERTS, 1), lambda t: (0, 0)),
                pl.BlockSpec((tm, tm), lambda t: (0, 0))]
    lane_row = lambda dt: (pl.BlockSpec((2, tm), lambda t: (0, t)), jax.ShapeDtypeStruct((2, N), dt))
    outs = [lane_row(F32), lane_row(I32),
            (pl.BlockSpec((1, N_EXPERTS, LANES), lambda t: (t, 0, 0)),
             jax.ShapeDtypeStruct((N // tm, N_EXPERTS, LANES), F32))]
    return [rw, rb, tri], in_specs, [o[0] for o in outs], [o[1] for o in outs]


def _chunk_tables(lo, go, rlen, tm):
    n_cls = (tm // RUN_ALIGN).bit_length()
    units = rlen // RUN_ALIGN
    cls = jnp.arange(n_cls, dtype=I32)
    flag = (units[:, :, None] >> cls) & 1
    rows = flag * (RUN_ALIGN << cls)
    above = jnp.cumsum(rows[..., ::-1], axis=-1)[..., ::-1] - rows
    src = lo[:, :, None] + above
    dst = go[:, :, None] + above
    slot = jnp.cumsum(flag, axis=1) - 1
    hit = (flag[:, None] == 1) & (slot[:, None] == jnp.arange(N_EXPERTS, dtype=I32)[None, :, None, None])
    compact = lambda a: jnp.sum(jnp.where(hit, a[:, None], 0), axis=2).transpose(0, 2, 1)
    flat = lambda a: a.reshape(-1).astype(I32)
    return (flat(compact(src)), flat(compact(dst)), flat(jnp.sum(flag, axis=1)), flat(jnp.sum(units, axis=1))), n_cls


def _start_pieces(tabs, tile, n_cls, make_copy, wait=False):
    src_ref, dst_ref, cnt_ref = tabs[:3]
    for b in range(n_cls):
        base = (tile * n_cls + b) * N_EXPERTS

        def body(i, carry, base=base, rows=RUN_ALIGN << b):
            copy = make_copy(pl.multiple_of(src_ref[base + i], RUN_ALIGN),
                             pl.multiple_of(dst_ref[base + i], RUN_ALIGN), rows)
            copy.wait() if wait else copy.start()
            return carry

        lax.fori_loop(0, cnt_ref[tile * n_cls + b], body, 0)


def _await_pieces(tabs, tile, max_rows, make_copy):
    units = tabs[3][tile]
    for b in range((max_rows // RUN_ALIGN).bit_length()):
        @pl.when(((units >> b) & 1) != 0)
        def _(rows=RUN_ALIGN << b):
            make_copy(0, 0, rows).wait()


def _dispatch_kernel(src_tab, dst_tab, cnt_tab, tot_tab, gap_src, gap_dst, gap_cnt, f_ref, pos_ref, gates_ref,
                     buf_hbm, srt_ref, zero_ref, sems, *, n_tiles, n_cls, gap_cls):
    t = pl.program_id(0)
    gaps = (gap_src, gap_dst, gap_cnt)
    gap_copy = lambda lo, go, rows: pltpu.make_async_copy(
        zero_ref.at[pl.ds(lo, rows)], buf_hbm.at[pl.ds(go, rows)], sems.at[2])

    @pl.when(t == 0)
    def _():
        zero_ref[...] = jnp.zeros_like(zero_ref)
        _start_pieces(gaps, 0, gap_cls, gap_copy)
    slot = t % 2
    tm, D = f_ref.shape
    ls = srt_ref.shape[1]
    j = lax.broadcasted_iota(I32, (ls, tm), 0)
    hit0 = pos_ref[0:1, :] == j
    hit1 = pos_ref[1:2, :] == j
    perm = jnp.where(hit0 | hit1, 1.0, 0.0).astype(BF16)
    srt_ref[slot, :, :D] = jnp.dot(perm, f_ref[...], preferred_element_type=F32).astype(BF16)
    picked = jnp.where(hit0, gates_ref[0:1, :], jnp.where(hit1, gates_ref[1:2, :], 0.0))
    g = jnp.sum(picked, axis=1, keepdims=True)
    g_hi = g.astype(BF16).astype(F32)
    g_lo = (g - g_hi).astype(BF16).astype(F32)
    lane = lax.broadcasted_iota(I32, (ls, LANES), 1)
    srt_ref[slot, :, D:] = jnp.where(lane == 0, g_hi, jnp.where(lane == 1, g_lo, 0.0)).astype(BF16)

    tabs = (src_tab, dst_tab, cnt_tab, tot_tab)

    def copier(buf_slot):
        return lambda lo, go, rows: pltpu.make_async_copy(
            srt_ref.at[buf_slot, pl.ds(lo, rows)], buf_hbm.at[pl.ds(go, rows)], sems.at[buf_slot])

    _start_pieces(tabs, t, n_cls, copier(slot))

    @pl.when(t > 0)
    def _():
        _await_pieces(tabs, t - 1, ls, copier(1 - slot))

    @pl.when(t == n_tiles - 1)
    def _():
        _await_pieces(tabs, t, ls, copier(slot))
        _start_pieces(gaps, 0, gap_cls, gap_copy, wait=True)


def _dispatch(f, pos, gates, tabs, n_cls, gap_tabs, gap_cls, n_rows, n_tiles, tm):
    D = f.shape[1]
    ls = 2 * tm + N_EXPERTS * RUN_ALIGN
    width = D + LANES
    grid_spec = pltpu.PrefetchScalarGridSpec(
        num_scalar_prefetch=7,
        grid=(n_tiles,),
        in_specs=[pl.BlockSpec((tm, D), lambda t, *_: (t, 0)), pl.BlockSpec((2, tm), lambda t, *_: (0, t)),
                  pl.BlockSpec((2, tm), lambda t, *_: (0, t))],
        out_specs=pl.BlockSpec(memory_space=pl.ANY),
        scratch_shapes=[pltpu.VMEM((2, ls, width), BF16), pltpu.VMEM((EXPERT_BLOCK, width), BF16),
                        pltpu.SemaphoreType.DMA((3,))],
    )
    return pl.pallas_call(
        functools.partial(_dispatch_kernel, n_tiles=n_tiles, n_cls=n_cls, gap_cls=gap_cls),
        grid_spec=grid_spec,
        out_shape=jax.ShapeDtypeStruct((n_rows, width), BF16),
        compiler_params=_cparams("arbitrary"),
        name="moe_dispatch",
    )(*tabs, *gap_tabs[:3], f, pos, gates)


def _expert_kernel(be_ref, bc_ref, bs_ref, x_ref, wg_ref, wu_ref, wd_ref, y_ref, wg_bf, wu_bf, wd_bf):
    i = pl.program_id(0)
    D = y_ref.shape[1]

    @pl.when((i == 0) | (be_ref[i] != be_ref[jnp.maximum(i - 1, 0)]))
    def _():
        wg_bf[...] = wg_ref[0, 0].astype(BF16)
        wu_bf[...] = wu_ref[0, 0].astype(BF16)
        wd_bf[...] = wd_ref[0, 0].astype(BF16)

    @pl.when(bc_ref[i] > 0)
    def _():
        x = x_ref[:, :D]
        gate = jnp.dot(x, wg_bf[...], preferred_element_type=F32)
        up = jnp.dot(x, wu_bf[...], preferred_element_type=F32)
        hid = (gate * jax.nn.sigmoid(gate) * up).astype(BF16)
        pieces = x_ref[:, D:].astype(F32)
        route_gate = pieces[:, 0:1] + pieces[:, 1:2]
        y_ref[...] = (jnp.dot(hid, wd_bf[...], preferred_element_type=F32) * route_gate).astype(BF16)

    @pl.when(bc_ref[i] == 0)
    def _():
        y_ref[...] = jnp.zeros_like(y_ref)


def _experts(buf, block_expert, block_count, block_src, wg, wu, wd, layer):
    n_rows = buf.shape[0]
    D = wg.shape[2]
    nb = n_rows // EXPERT_BLOCK
    wspec = lambda w: pl.BlockSpec((1, 1) + w.shape[2:], lambda i, be, bc, bs: (layer, be[i], 0, 0))
    grid_spec = pltpu.PrefetchScalarGridSpec(
        num_scalar_prefetch=3,
        grid=(nb,),
        in_specs=[pl.BlockSpec((EXPERT_BLOCK, buf.shape[1]), lambda i, be, bc, bs: (bs[i], 0)),
                  wspec(wg), wspec(wu), wspec(wd)],
        out_specs=pl.BlockSpec((EXPERT_BLOCK, D), lambda i, be, bc, bs: (i, 0)),
        scratch_shapes=[pltpu.VMEM(wg.shape[2:], BF16), pltpu.VMEM(wu.shape[2:], BF16),
                        pltpu.VMEM(wd.shape[2:], BF16)],
    )
    return pl.pallas_call(
        _expert_kernel,
        grid_spec=grid_spec,
        out_shape=jax.ShapeDtypeStruct((n_rows, D), BF16),
        compiler_params=_cparams("arbitrary"),
        name="moe_experts",
    )(block_expert, block_count, block_src, buf, wg, wu, wd)


def _combine_kernel(src_tab, dst_tab, cnt_tab, tot_tab, y_hbm, x_ref, pos_ref, gate_ref, fin_ref, xo_ref,
                    srt_ref, sems, *, final, n_tiles, n_cls):
    t = pl.program_id(0)
    slot = t % 2
    tm = x_ref.shape[0]
    ls = srt_ref.shape[1]

    tabs = (src_tab, dst_tab, cnt_tab, tot_tab)

    def copier(buf_slot):
        return lambda lo, go, rows: pltpu.make_async_copy(
            y_hbm.at[pl.ds(go, rows)], srt_ref.at[buf_slot, pl.ds(lo, rows)], sems.at[buf_slot])

    @pl.when(t == 0)
    def _():
        srt_ref[...] = jnp.zeros_like(srt_ref)
        _start_pieces(tabs, t, n_cls, copier(slot))

    @pl.when(t + 1 < n_tiles)
    def _():
        _start_pieces(tabs, t + 1, n_cls, copier(1 - slot))

    _await_pieces(tabs, t, ls, copier(slot))
    j = lax.broadcasted_iota(I32, (tm, ls), 1)
    pick = jnp.where((pos_ref[:, 0:1] == j) | (pos_ref[:, 1:2] == j), 1.0, 0.0).astype(BF16)
    y = jnp.dot(pick, srt_ref[slot], preferred_element_type=F32)
    xn = x_ref[...] + gate_ref[0, 0] * y
    if final:
        xn = xn * lax.rsqrt(jnp.mean(xn * xn, axis=-1, keepdims=True) + RMS_EPS) * fin_ref[...]
    xo_ref[...] = xn


def _combine(y_buf, pos_t, tabs, n_cls, xs, mod, final_gain, n_tiles, tm, tiles_per_seq, final):
    N, D = xs.shape
    nmod = mod.shape[1]
    ls = 2 * tm + N_EXPERTS * RUN_ALIGN
    out_rows = n_tiles * tm if final else N
    kwargs = {} if final else {"input_output_aliases": {5: 0}}
    grid_spec = pltpu.PrefetchScalarGridSpec(
        num_scalar_prefetch=4,
        grid=(n_tiles,),
        in_specs=[pl.BlockSpec(memory_space=pl.ANY),
                  pl.BlockSpec((tm, D), lambda t, *_: (t, 0)),
                  pl.BlockSpec((tm, 2), lambda t, *_: (t, 0)),
                  pl.BlockSpec((1, 1, 1, D), lambda t, *_: (5, jnp.minimum(t // tiles_per_seq, nmod - 1), 0, 0)),
                  pl.BlockSpec((1, D), lambda t, *_: (0, 0))],
        out_specs=pl.BlockSpec((tm, D), lambda t, *_: (t, 0)),
        scratch_shapes=[pltpu.VMEM((2, ls, D), BF16), pltpu.SemaphoreType.DMA((2,))],
    )
    return pl.pallas_call(
        functools.partial(_combine_kernel, final=final, n_tiles=n_tiles, n_cls=n_cls),
        grid_spec=grid_spec,
        out_shape=jax.ShapeDtypeStruct((out_rows, D), F32),
        compiler_params=_cparams("arbitrary"),
        name="moe_combine_final" if final else "moe_combine",
        **kwargs,
    )(*tabs, y_buf, xs, pos_t, mod, final_gain.reshape(1, D))


def _moe(f, routing, xs, mod, wg, wu, wd, layer, final_gain, n_tok, tm, tiles_per_seq, final):
    nt = n_tok // tm
    gates, pos, cnt = routing
    gates, pos, cnt = gates[:, :n_tok], pos[:, :n_tok], cnt[:nt]
    n = cnt[:, :, 0].astype(I32)
    rlen = (n + RUN_ALIGN - 1) // RUN_ALIGN * RUN_ALIGN
    lo = jnp.cumsum(rlen, axis=1) - rlen
    region = jnp.sum(rlen, axis=0)
    region_pad = (region + EXPERT_BLOCK - 1) // EXPERT_BLOCK * EXPERT_BLOCK
    pends = jnp.cumsum(region_pad)
    pstarts = pends - region_pad
    go = pstarts[None, :] + jnp.cumsum(rlen, axis=0) - rlen
    n_blocks = -(-(2 * n_tok + nt * N_EXPERTS * RUN_ALIGN) // EXPERT_BLOCK) + N_EXPERTS
    blk0 = jnp.arange(n_blocks, dtype=I32) * EXPERT_BLOCK
    block_expert = jnp.minimum(jnp.sum((blk0[:, None] >= pends[None, :]).astype(I32), axis=1), N_EXPERTS - 1)
    block_used = (blk0 < (pstarts + region)[block_expert]).astype(I32)
    tabs, n_cls = _chunk_tables(lo, go, rlen, tm)
    gap_tabs, gap_cls = _chunk_tables(jnp.zeros((1, N_EXPERTS), I32), (pstarts + region)[None, :],
                                      (region_pad - region)[None, :], EXPERT_BLOCK)
    buf = _dispatch(f, pos, gates, tabs, n_cls, gap_tabs, gap_cls, n_blocks * EXPERT_BLOCK, nt, tm)
    last_used = jnp.max(jnp.where(block_used > 0, jnp.arange(n_blocks, dtype=I32), 0))
    block_src = jnp.where(block_used > 0, jnp.arange(n_blocks, dtype=I32), last_used)
    y_buf = _experts(buf, block_expert.astype(I32), block_used, block_src, wg, wu, wd, layer)
    return _combine(y_buf, pos.T, tabs, n_cls, xs, mod, final_gain, nt, tm, tiles_per_seq, final)


def _rope_tables(S, tm):
    rows = S // GRID_W
    row = jnp.repeat(jnp.arange(rows, dtype=F32), GRID_W)
    col = jnp.tile(jnp.arange(GRID_W, dtype=F32), rows)
    half = ROPE_PAIR
    inv_freq = ROPE_THETA ** (-jnp.arange(half, dtype=F32) / half)
    ang_r = row[:, None] * inv_freq[None, :]
    ang_c = col[:, None] * inv_freq[None, :]
    zeros = jnp.zeros_like(ang_r)
    cos = jnp.concatenate([jnp.cos(ang_r)] * 2 + [jnp.cos(ang_c)] * 2, axis=1)
    s1 = jnp.concatenate([-jnp.sin(ang_r), zeros, -jnp.sin(ang_c), zeros], axis=1)
    s2 = jnp.concatenate([zeros, jnp.sin(ang_r), zeros, jnp.sin(ang_c)], axis=1)
    def finish(tab, fill):
        tab = jnp.tile(tab, (1, LANES // ATTN_HEAD_DIM))
        return jnp.concatenate([tab, jnp.full((tm, LANES), fill, F32)], axis=0)
    return finish(cos, 1.0), finish(s1, 0.0), finish(s2, 0.0)


def kernel(x, c, ctx, c_ctx, ada_w, ada_b, norm_mix_g, norm_ffn_g, final_g, attn_w_qkv, attn_w_o, attn_sinks,
           gla_w_in, gla_w_a1, gla_w_a2, gla_b_a, gla_norm_g, gla_w_o, router_w, router_b,
           moe_w_gate, moe_w_up, moe_w_down):
    B, S, D = x.shape
    C = ctx.shape[1]
    depth = ada_w.shape[0]
    tm = TOKEN_TILE
    assert S % tm == 0 and (B * C) % tm == 0 and S % ATTN_BLOCK == 0 and C % ATTN_BLOCK == 0
    assert (B * S) % C == 0 and S % GLA_CHUNK == 0 and C % GLA_CHUNK == 0
    n_lat = B * S
    n_lat_tiles = n_lat // tm
    tiles_per_seq = S // tm

    rpad = -(-(B + 1) // 8) * 8
    cc = jnp.zeros((rpad, D), F32).at[:B].set(c).at[B].set(c_ctx)
    mods = _ada_table(cc, ada_w, ada_b)
    mods = mods[:, :B + 1].reshape(depth, B + 1, 6, 1, D).transpose(0, 2, 1, 3, 4)

    xs = (x.reshape(n_lat, D), ctx.reshape(B * C, D))
    ptm = PROJ_TILE if S % PROJ_TILE == 0 and (B * C) % PROJ_TILE == 0 else tm
    rope = _rope_tables(S, ptm)
    q_dim = ATTN_KV_HEADS * ATTN_GROUP * ATTN_HEAD_DIM
    kv_dim = ATTN_KV_HEADS * ATTN_HEAD_DIM
    kd = GLA_HEADS * GLA_KEY_DIM
    vd = GLA_HEADS * GLA_VAL_DIM

    def dup_heads(w):
        w = w.reshape(D, ATTN_KV_HEADS, 1, ATTN_HEAD_DIM)
        return jnp.broadcast_to(w, (D, ATTN_KV_HEADS, LANES // ATTN_HEAD_DIM, ATTN_HEAD_DIM)).reshape(D, -1)

    for i in range(depth):
        last = i == depth - 1
        mod = mods[i]
        j = i // 2
        if i % 2 == 0:
            wqkv = attn_w_qkv[j]
            w = dup_heads(wqkv[:, q_dim:q_dim + kv_dim]).astype(BF16)
            wqvt = jnp.concatenate([wqkv[:, :q_dim], wqkv[:, q_dim + kv_dim:]], axis=1).T.astype(BF16)
            q, k, vt = _qkv_proj(xs, norm_mix_g[i], mod, 0, w, wqvt, rope, ptm, n_lat // ptm, S // ptm)
            sink_row = jnp.repeat(attn_sinks[j].astype(F32).reshape(ATTN_KV_HEADS, ATTN_GROUP) * LOG2_E,
                                  ATTN_BLOCK, axis=1)
            o = _attention(q, k, vt, sink_row[:, None, :], B, S, C, not last)
            xs, f, *routing = _post_mixer(o, attn_w_o[j].astype(BF16), xs, mod, norm_ffn_g[i], router_w, router_b,
                                          tm, n_lat_tiles, tiles_per_seq)
        else:
            a1 = jnp.zeros((D, LANES), F32).at[:, :2 * GLA_GATE_RANK].set(
                jnp.concatenate([gla_w_a1[j, 0], gla_w_a1[j, 1]], axis=1))
            w = jnp.concatenate([gla_w_in[j], a1], axis=1).astype(BF16)
            w2 = jnp.zeros((LANES, 2 * kd), F32)
            w2 = w2.at[:GLA_GATE_RANK, :kd].set(gla_w_a2[j, 0]).at[GLA_GATE_RANK:2 * GLA_GATE_RANK, kd:].set(gla_w_a2[j, 1])
            ba = gla_b_a[j].reshape(1, 2 * kd).astype(F32)
            qk, v, og, la, chunk_tot = _gla_in_proj(xs, norm_mix_g[i], mod, 0, w, w2.astype(BF16), ba, ptm,
                                                    n_lat // ptm, S // ptm)
            o_fwd, o_bwd = _gla_scan(qk, v, la, chunk_tot, B, S, C)
            xs, f, *routing = _post_mixer(o_fwd, gla_w_o[j].astype(BF16), xs, mod, norm_ffn_g[i], router_w, router_b,
                                          tm, n_lat_tiles, tiles_per_seq,
                                          gla_extra=(o_bwd, og, gla_norm_g[j].reshape(1, GLA_VAL_DIM).astype(F32)),
                                          n_tiles=n_lat_tiles if last else None)
        n_tok = n_lat if last else n_lat + B * C
        xs = _moe(f, routing, xs, mod, moe_w_gate, moe_w_up, moe_w_down, i, final_g, n_tok, tm, tiles_per_seq, last)
    return xs.reshape(B, S, D)
```
